```python
import math
import jax
import jax.numpy as jnp
from jax import lax
import numpy as np

D_MODEL = 2048
BATCH = 2
SEQ = 4096
DEPTH = 2

N_MEM = 256
MIX_A = D_MODEL // 4
MIX_B = D_MODEL // 4
MIX_C = D_MODEL // 4
MIX_D = D_MODEL - MIX_A - MIX_B - MIX_C
MIX_WIDTH = MIX_A + MIX_B + MIX_C + MIX_D

A_HEADS = 4
A_VDIM = MIX_A // A_HEADS
A_QKDIM = A_VDIM // 2
B_HDIM = 64
B_HEADS = MIX_B // B_HDIM
B_DECAY_LORA = 64
B_AAA_LORA = 64
B_GATE_LORA = 128
LN_X_EPS = 64e-5
C_HEADS = 4
C_HDIM = MIX_C // C_HEADS
C_PATTERNS = ((128, 1), (512, 4), (2048, 16))
D_GSIZE = 16
D_GROUPS = MIX_D // D_GSIZE
D_STATE = 64

IN_A = 3 * MIX_A
IN_B = 3 * MIX_B + 2 * B_DECAY_LORA + 2 * B_AAA_LORA + B_GATE_LORA
IN_C = 3 * MIX_C
IN_D = MIX_D
IN_TOTAL = IN_A + IN_B + IN_C + IN_D

XA_HEADS = 4
XA_HDIM = D_MODEL // XA_HEADS

N_GROUPS = 4
EXPERTS_PER_GROUP = 8
N_EXPERTS = N_GROUPS * EXPERTS_PER_GROUP
TOP_K = 2
D_EXPERT = D_MODEL // 4
MOE_BLOCK = 128

ROPE_THETA = 10000.0
Q_BLOCK = 128
RMS_EPS = 1e-6
NEG_INF = -1e30

kernel_name = 'hybrid_parallel_heads_encoder'

F32 = jnp.float32


def rms_norm(x, g, eps=RMS_EPS):
    xf = x.astype(F32)
    y = xf * lax.rsqrt(jnp.mean(xf * xf, axis=-1, keepdims=True) + eps)
    return (y * g.astype(F32)).astype(x.dtype)


def rope_tables(seq, dim):
    inv = 1.0 / (ROPE_THETA ** (jnp.arange(0, dim, 2, dtype=F32) / dim))
    ang = jnp.arange(seq, dtype=F32)[:, None] * inv[None, :]
    return jnp.cos(ang), jnp.sin(ang)


def apply_rope(x, cos, sin):
    xf = x.astype(F32)
    x1, x2 = jnp.split(xf, 2, axis=-1)
    return jnp.concatenate([x1 * cos - x2 * sin, x2 * cos + x1 * sin], axis=-1).astype(x.dtype)


def diff_attention(z, lam_vecs, subln_g, layer_idx):
    b, s, _ = z.shape
    q, k, v = jnp.split(z, 3, axis=-1)
    q = q.reshape(b, s, A_HEADS, 2, A_QKDIM).transpose(0, 2, 3, 1, 4)
    k = k.reshape(b, s, A_HEADS, 2, A_QKDIM).transpose(0, 2, 3, 1, 4)
    v = v.reshape(b, s, A_HEADS, A_VDIM).transpose(0, 2, 1, 3)
    cos, sin = rope_tables(s, A_QKDIM)
    q = apply_rope(q, cos, sin)
    k = apply_rope(k, cos, sin)
    lam_init = 0.8 - 0.6 * math.exp(-0.3 * layer_idx)
    lf = lam_vecs.astype(F32)
    lam = jnp.exp(jnp.sum(lf[0] * lf[1])) - jnp.exp(jnp.sum(lf[2] * lf[3])) + lam_init
    scale = A_QKDIM ** -0.5
    nb = s // Q_BLOCK
    qb = q.reshape(b, A_HEADS, 2, nb, Q_BLOCK, A_QKDIM).transpose(3, 0, 1, 2, 4, 5)

    def block(qi):
        sc = jnp.einsum('bhcqd,bhckd->bhcqk', qi, k).astype(F32) * scale
        p = jax.nn.softmax(sc, axis=-1)
        attn = p[:, :, 0] - lam * p[:, :, 1]
        return jnp.einsum('bhqk,bhkd->bhqd', attn.astype(v.dtype), v)

    o = lax.map(block, qb)
    o = o.transpose(1, 0, 3, 2, 4).reshape(b, s, A_HEADS, A_VDIM)
    o = rms_norm(o.astype(F32), subln_g) * (1.0 - lam_init)
    return o.reshape(b, s, MIX_A)


def token_shift_centred(z, mu):
    zp = jnp.pad(z, ((0, 0), (1, 0), (0, 0)))[:, :-1]
    zn = jnp.pad(z, ((0, 0), (0, 1), (0, 0)))[:, 1:]
    return z + mu[0] * (zp - z) + mu[1] * (zn - z)


def rwkv7_bidir(z, mu, w0, w2, a0, a2, g2, k_k, k_a, r_k, lnx_w, lnx_b):
    b, s, _ = z.shape
    z = token_shift_centred(z.astype(F32), mu.astype(F32))
    cuts = [MIX_B, 2 * MIX_B, 3 * MIX_B, 3 * MIX_B + 2 * B_DECAY_LORA,
            3 * MIX_B + 2 * B_DECAY_LORA + 2 * B_AAA_LORA]
    r, k, v, wd, ad, gd = jnp.split(z, cuts, axis=-1)
    wd = wd.reshape(b, s, 2, B_DECAY_LORA)
    ad = ad.reshape(b, s, 2, B_AAA_LORA)
    logw = -jax.nn.softplus(-(w0 + jnp.einsum('bsdr,drc->bsdc', jnp.tanh(wd), w2))) - 0.5
    decay = jnp.exp(-jnp.exp(logw))
    a = jax.nn.sigmoid(a0 + jnp.einsum('bsdr,drc->bsdc', ad, a2))
    g = jax.nn.sigmoid(gd) @ g2

    def heads(t):
        return t.reshape(t.shape[:-1] + (B_HEADS, B_HDIM))

    kk = heads(k * k_k)
    kk = kk * lax.rsqrt(jnp.sum(kk * kk, axis=-1, keepdims=True) + 1e-12)
    kmod = k[:, :, None, :] * (1.0 + (a - 1.0) * k_a)
    r_h, v_h = heads(r), heads(v)
    a_h, w_h, k_h = heads(a), heads(decay), heads(kmod)

    def dirs(t):
        t = jnp.broadcast_to(t, (b, s, 2, B_HEADS, B_HDIM))
        t = jnp.stack([t[:, :, 0], jnp.flip(t[:, :, 1], axis=1)], axis=0)
        return jnp.moveaxis(t, 2, 0)

    xs = (dirs(r_h[:, :, None]), dirs(w_h), dirs(k_h), dirs(v_h[:, :, None]),
          dirs(-kk[:, :, None]), dirs(kk[:, :, None] * a_h))

    def step(state, inp):
        rt, wt, kt, vt, at, bt = inp
        sa = jnp.einsum('dbhij,dbhj->dbhi', state, at)
        state = state * wt[..., None, :] + sa[..., :, None] * bt[..., None, :] + vt[..., :, None] * kt[..., None, :]
        return state, jnp.einsum('dbhij,dbhj->dbhi', state, rt)

    s0 = jnp.zeros((2, b, B_HEADS, B_HDIM, B_HDIM), F32)
    _, ys = lax.scan(step, s0, xs)
    y = jnp.moveaxis(ys[:, 0] + jnp.flip(ys[:, 1], axis=0), 0, 1)
    mean = jnp.mean(y, axis=-1, keepdims=True)
    var = jnp.mean(jnp.square(y - mean), axis=-1, keepdims=True)
    y = ((y - mean) * lax.rsqrt(var + LN_X_EPS)).reshape(b, s, MIX_B) * lnx_w + lnx_b
    k_bonus = heads(0.5 * (kmod[:, :, 0] + kmod[:, :, 1]))
    bonus = jnp.sum(r_h * k_bonus * r_k, axis=-1, keepdims=True) * v_h
    return (y + bonus.reshape(b, s, MIX_B)) * g


def dilated_attention(z):
    b, s, _ = z.shape
    q, k, v = jnp.split(z, 3, axis=-1)
    q = q.reshape(b, s, C_HEADS, C_HDIM).transpose(0, 2, 1, 3)
    k = k.reshape(b, s, C_HEADS, C_HDIM).transpose(0, 2, 1, 3)
    v = v.reshape(b, s, C_HEADS, C_HDIM).transpose(0, 2, 1, 3)
    cos, sin = rope_tables(s, C_HDIM)
    q = apply_rope(q, cos, sin)
    k = apply_rope(k, cos, sin)
    offs = jnp.asarray(np.stack([np.arange(-(w // (2 * d)), w // (2 * d) + 1) * d
                                 for w, d in C_PATTERNS]), jnp.int32)
    scale = C_HDIM ** -0.5
    nb = s // Q_BLOCK
    qb = q.reshape(b, C_HEADS, nb, Q_BLOCK, C_HDIM).transpose(2, 0, 1, 3, 4)
    starts = jnp.arange(nb, dtype=jnp.int32) * Q_BLOCK

    def block(args):
        qi, st = args
        pos = st + jnp.arange(Q_BLOCK, dtype=jnp.int32)
        idx = pos[None, :, None] + offs[:, None, :]
        valid = (idx >= 0) & (idx < s)
        idx = jnp.clip(idx, 0, s - 1)
        kg = k[:, :, idx]
        vg = v[:, :, idx]
        sc = jnp.einsum('bhqd,bhpqkd->bhpqk', qi, kg).astype(F32) * scale
        sc = jnp.where(valid, sc, NEG_INF)
        m = jnp.max(sc, axis=-1, keepdims=True)
        e = jnp.exp(sc - m)
        den = jnp.sum(e, axis=-1, keepdims=True)
        o = jnp.einsum('bhpqk,bhpqkd->bhpqd', (e / den).astype(v.dtype), vg).astype(F32)
        lse = (m + jnp.log(den))[..., 0]
        wgt = jax.nn.softmax(lse, axis=2)
        return jnp.einsum('bhpq,bhpqd->bhqd', wgt, o)

    o = lax.map(block, (qb, starts))
    return o.transpose(1, 0, 3, 2, 4).reshape(b, s, MIX_C)


def cmul(ar, ai, br, bi):
    return ar * br - ai * bi, ar * bi + ai * br


def s5_combine(e1, e2):
    a1r, a1i, x1r, x1i = e1
    a2r, a2i, x2r, x2i = e2
    ar, ai = cmul(a2r, a2i, a1r, a1i)
    xr, xi = cmul(a2r, a2i, x1r, x1i)
    return ar, ai, xr + x2r, xi + x2i


def s5_bidir(u, a_re, a_im, log_dt, b_re, b_im, c_re, c_im, d_skip, glu_w, glu_b):
    b, s, _ = u.shape
    uf = u.astype(F32)
    ug = uf.reshape(b, s, D_GROUPS, D_GSIZE)
    b_re, b_im = b_re.astype(F32), b_im.astype(F32)
    x_re = jnp.zeros((b, s, D_GROUPS, D_STATE), F32)
    x_im = jnp.zeros((b, s, D_GROUPS, D_STATE), F32)
    for direction in range(2):
        lr = jnp.minimum(a_re[direction].astype(F32), -1e-4)
        li = a_im[direction].astype(F32)
        dt = jnp.exp(log_dt[direction].astype(F32))[:, None]
        mag = jnp.exp(dt * lr)
        abr, abi = mag * jnp.cos(dt * li), mag * jnp.sin(dt * li)
        den = lr * lr + li * li
        fr, fi = cmul(abr - 1.0, abi, lr / den, -li / den)
        bbr, bbi = cmul(fr[..., None], fi[..., None], b_re, b_im)
        bur = jnp.einsum('gnc,bsgc->bsgn', bbr, ug)
        bui = jnp.einsum('gnc,bsgc->bsgn', bbi, ug)
        ar = jnp.broadcast_to(abr, bur.shape)
        ai = jnp.broadcast_to(abi, bur.shape)
        _, _, xr, xi = lax.associative_scan(s5_combine, (ar, ai, bur, bui),
                                            reverse=(direction == 1), axis=1)
        x_re = x_re + xr
        x_im = x_im + xi
    y = (jnp.einsum('gcn,bsgn->bsgc', c_re.astype(F32), x_re)
         - jnp.einsum('gcn,bsgn->bsgc', c_im.astype(F32), x_im))
    y = y.reshape(b, s, MIX_D) + d_skip * uf
    gl = jax.nn.gelu(y)
    return gl * jax.nn.sigmoid(gl @ glu_w + glu_b)


def cross_attention(h, memn, wq, wkv, wo):
    b, s, _ = h.shape
    m = memn.shape[1]
    q = (h @ wq).reshape(b, s, XA_HEADS, XA_HDIM)
    k, v = jnp.split(memn @ wkv, 2, axis=-1)
    k = k.reshape(b, m, XA_HEADS, XA_HDIM)
    v = v.reshape(b, m, XA_HEADS, XA_HDIM)
    sc = jnp.einsum('bshd,bmhd->bhsm', q, k).astype(F32) * (XA_HDIM ** -0.5)
    p = jax.nn.softmax(sc, axis=-1)
    o = jnp.einsum('bhsm,bmhd->bshd', p.astype(v.dtype), v).reshape(b, s, D_MODEL)
    return o @ wo


def hier_moe(h, wr_group, wr_expert, w1, w3, w2):
    b, s, d = h.shape
    t = b * s
    xt = h.reshape(t, d)
    gp = jax.nn.softmax((xt @ wr_group).astype(F32), axis=-1)
    g_idx = jnp.argmax(gp, axis=-1)
    g_w = jnp.take_along_axis(gp, g_idx[:, None], axis=-1)
    el = (xt @ wr_expert).astype(F32).reshape(t, N_GROUPS, EXPERTS_PER_GROUP)
    el = jnp.take_along_axis(el, g_idx[:, None, None], axis=1)[:, 0]
    ep = jax.nn.softmax(el, axis=-1)
    top_w, top_i = lax.top_k(ep, TOP_K)
    top_w = top_w / jnp.sum(top_w, axis=-1, keepdims=True)
    eid = (g_idx[:, None] * EXPERTS_PER_GROUP + top_i).reshape(-1).astype(jnp.int32)
    gate = (g_w * top_w).reshape(-1)
    tok = jnp.repeat(jnp.arange(t, dtype=jnp.int32), TOP_K)
    n_assign = t * TOP_K
    order = jnp.argsort(eid)
    se = eid[order]
    counts = jnp.bincount(eid, length=N_EXPERTS)
    padded = ((counts + MOE_BLOCK - 1) // MOE_BLOCK) * MOE_BLOCK
    pad_end = jnp.cumsum(padded)
    pad_start = pad_end - padded
    start = jnp.cumsum(counts) - counts
    dest = pad_start[se] + (jnp.arange(n_assign, dtype=jnp.int32) - start[se])
    cap = n_assign + N_EXPERTS * MOE_BLOCK
    nb = cap // MOE_BLOCK
    slot_tok = jnp.zeros((cap,), jnp.int32).at[dest].set(tok[order])
    slot_gate = jnp.zeros((cap,), F32).at[dest].set(gate[order])
    blk_exp = jnp.minimum(jnp.searchsorted(pad_end, jnp.arange(nb) * MOE_BLOCK, side='right'),
                          N_EXPERTS - 1)
    xs = xt[slot_tok].reshape(nb, MOE_BLOCK, d)

    def run(args):
        xb, e = args
        return (jax.nn.silu(xb @ w1[e]) * (xb @ w3[e])) @ w2[e]

    ys = lax.map(run, (xs, blk_exp)).reshape(cap, d)
    out = jnp.zeros((t, d), F32).at[slot_tok].add(ys.astype(F32) * slot_gate[:, None])
    return out.reshape(b, s, d).astype(h.dtype)


def setup_inputs(seed: int = 0) -> dict:
    key = jax.random.key(seed)
    keys = jax.random.split(key, 64)
    counter = [0]

    def nxt():
        counter[0] += 1
        return keys[counter[0] - 1]

    def nrm(shape, scale):
        return jax.random.normal(nxt(), shape, F32) * scale

    def unif(shape, lo, hi):
        return jax.random.uniform(nxt(), shape, F32, lo, hi)

    def gain(shape):
        return 1.0 + nrm(shape, 0.05)

    L, D = DEPTH, D_MODEL
    state_n = jnp.arange(D_STATE, dtype=F32)
    return {
        'x': nrm((BATCH, SEQ, D), 1.0),
        'mem': nrm((BATCH, N_MEM, D), 1.0),
        'norm_mix': gain((L, D)),
        'w_in': nrm((L, D, IN_TOTAL), D ** -0.5),
        'w_out': nrm((L, MIX_WIDTH, D), MIX_WIDTH ** -0.5),
        'diff_lambda': nrm((L, 4, A_QKDIM), 0.1),
        'diff_subln': gain((L, A_VDIM)),
        'rwkv_mu': unif((L, 2, IN_B), 0.0, 0.5),
        'rwkv_w0': unif((L, 2, MIX_B), -6.0, -1.0),
        'rwkv_w2': nrm((L, 2, B_DECAY_LORA, MIX_B), 0.1),
        'rwkv_a0': nrm((L, 2, MIX_B), 0.1),
        'rwkv_a2': nrm((L, 2, B_AAA_LORA, MIX_B), 0.1),
        'rwkv_g2': nrm((L, B_GATE_LORA, MIX_B), B_GATE_LORA ** -0.5),
        'rwkv_kk': 0.85 + nrm((L, MIX_B), 0.05),
        'rwkv_ka': gain((L, MIX_B)),
        'rwkv_rk': nrm((L, B_HEADS, B_HDIM), 0.1),
        'rwkv_lnx_w': gain((L, MIX_B)),
        'rwkv_lnx_b': nrm((L, MIX_B), 0.01),
        's5_a_re': -0.5 + nrm((L, 2, D_GROUPS, D_STATE), 0.02),
        's5_a_im': math.pi * state_n + nrm((L, 2, D_GROUPS, D_STATE), 0.02),
        's5_log_dt': unif((L, 2, D_GROUPS), math.log(0.001), math.log(0.1)),
        's5_b_re': nrm((L, D_GROUPS, D_STATE, D_GSIZE), (2 * D_GSIZE) ** -0.5),
        's5_b_im': nrm((L, D_GROUPS, D_STATE, D_GSIZE), (2 * D_GSIZE) ** -0.5),
        's5_c_re': nrm((L, D_GROUPS, D_GSIZE, D_STATE), (2 * D_STATE) ** -0.5),
        's5_c_im': nrm((L, D_GROUPS, D_GSIZE, D_STATE), (2 * D_STATE) ** -0.5),
        's5_d': nrm((L, MIX_D), 1.0),
        's5_glu_w': nrm((L, MIX_D, MIX_D), MIX_D ** -0.5),
        's5_glu_b': nrm((L, MIX_D), 0.01),
        'mix_out_norm': gain((L, 2, MIX_C)),
        'norm_cross': gain((L, D)),
        'norm_mem': gain((L, D)),
        'xa_wq': nrm((L, D, D), D ** -0.5),
        'xa_wkv': nrm((L, D, 2 * D), D ** -0.5),
        'xa_wo': nrm((L, D, D), D ** -0.5),
        'norm_moe': gain((L, D)),
        'router_group': nrm((L, D, N_GROUPS), D ** -0.5),
        'router_expert': nrm((L, D, N_EXPERTS), D ** -0.5),
        'moe_w1': nrm((L, N_EXPERTS, D, D_EXPERT), D ** -0.5),
        'moe_w3': nrm((L, N_EXPERTS, D, D_EXPERT), D ** -0.5),
        'moe_w2': nrm((L, N_EXPERTS, D_EXPERT, D), D_EXPERT ** -0.5),
        'norm_final': gain((D,)),
    }


def reference(x, mem, norm_mix, w_in, w_out, diff_lambda, diff_subln, rwkv_mu, rwkv_w0, rwkv_w2,
              rwkv_a0, rwkv_a2, rwkv_g2, rwkv_kk, rwkv_ka, rwkv_rk, rwkv_lnx_w, rwkv_lnx_b,
              s5_a_re, s5_a_im, s5_log_dt, s5_b_re, s5_b_im, s5_c_re, s5_c_im, s5_d, s5_glu_w,
              s5_glu_b, mix_out_norm, norm_cross, norm_mem, xa_wq, xa_wkv, xa_wo, norm_moe,
              router_group, router_expert, moe_w1, moe_w3, moe_w2, norm_final):
    h = x
    for l in range(DEPTH):
        z = rms_norm(h, norm_mix[l]) @ w_in[l]
        za, zb, zc, zd = jnp.split(z, [IN_A, IN_A + IN_B, IN_A + IN_B + IN_C], axis=-1)
        oa = diff_attention(za, diff_lambda[l], diff_subln[l], l)
        ob = rwkv7_bidir(zb, rwkv_mu[l], rwkv_w0[l], rwkv_w2[l], rwkv_a0[l], rwkv_a2[l], rwkv_g2[l],
                         rwkv_kk[l], rwkv_ka[l], rwkv_rk[l], rwkv_lnx_w[l], rwkv_lnx_b[l])
        oc = rms_norm(dilated_attention(zc).astype(F32), mix_out_norm[l, 0])
        od = rms_norm(s5_bidir(zd, s5_a_re[l], s5_a_im[l], s5_log_dt[l], s5_b_re[l], s5_b_im[l],
                               s5_c_re[l], s5_c_im[l], s5_d[l], s5_glu_w[l], s5_glu_b[l]),
                      mix_out_norm[l, 1])
        mix = jnp.concatenate([oa, ob, oc, od], axis=-1).astype(h.dtype)
        h = h + mix @ w_out[l]
        h = h + cross_attention(rms_norm(h, norm_cross[l]), rms_norm(mem, norm_mem[l]),
                                xa_wq[l], xa_wkv[l], xa_wo[l])
        h = h + hier_moe(rms_norm(h, norm_moe[l]), router_group[l], router_expert[l],
                         moe_w1[l], moe_w3[l], moe_w2[l])
    return rms_norm(h, norm_final)
```

```python
import functools
import math

import numpy as np
import jax
import jax.numpy as jnp
from jax import lax
from jax.experimental import pallas as pl
from jax.experimental.pallas import tpu as pltpu

F32 = jnp.float32
BF16 = jnp.bfloat16

D_MODEL = 2048
MIX = D_MODEL // 4
A_HEADS = 4
A_VDIM = MIX // A_HEADS
A_QKDIM = A_VDIM // 2
B_HDIM = 64
B_HEADS = MIX // B_HDIM
B_LORA = 64
B_GATE_LORA = 128
LN_X_EPS = 64e-5
C_HEADS = 4
C_HDIM = MIX // C_HEADS
C_PATTERNS = ((128, 1), (512, 4), (2048, 16))
D_GSIZE = 16
D_GROUPS = MIX // D_GSIZE
D_STATE = 64
IN_A = 3 * MIX
IN_B = 3 * MIX + 4 * B_LORA + B_GATE_LORA
IN_B_PAD = 2048
IN_C = 3 * MIX
IN_D = MIX
XA_HEADS = 4
XA_HDIM = D_MODEL // XA_HEADS
N_GROUPS = 4
EXPERTS_PER_GROUP = 8
N_EXPERTS = N_GROUPS * EXPERTS_PER_GROUP
TOP_K = 2
D_EXPERT = D_MODEL // 4
ROPE_THETA = 10000.0
RMS_EPS = 1e-6
NEG_INF = -1e30

Z_B = 0
Z_A = IN_B_PAD
Z_C = Z_A + IN_A
Z_D = Z_C + IN_C
Z_W = Z_D + IN_D

LANES = 128
VMEM_LIMIT = 56 * 1024 * 1024


def _cparams(sem, vmem=VMEM_LIMIT):
    return pltpu.CompilerParams(dimension_semantics=sem, vmem_limit_bytes=vmem)


def _rms(x, g):
    return x * lax.rsqrt(jnp.mean(x * x, axis=-1, keepdims=True) + RMS_EPS) * g


def _dot(a, b):
    return jnp.dot(a, b, preferred_element_type=F32)


def _dot_t(a, b):
    return lax.dot_general(a, b, (((1,), (1,)), ((), ())), preferred_element_type=F32)


def _split3(x):
    hi = x.astype(BF16)
    r1 = x - hi.astype(F32)
    mid = r1.astype(BF16)
    lo = (r1 - mid.astype(F32)).astype(BF16)
    return hi, mid, lo


def _segsum(x, ones):
    hi, mid, lo = _split3(x)
    return _dot(hi, ones) + _dot(mid, ones) + _dot(lo, ones)


def _norm_matmul_kernel(x_ref, g_ref, w_ref, o_ref, xn_ref):
    @pl.when(pl.program_id(1) == 0)
    def _():
        xn_ref[...] = _rms(x_ref[...], g_ref[...]).astype(BF16)

    o_ref[...] = _dot(xn_ref[...], w_ref[...]).astype(o_ref.dtype)


def norm_matmul(x, g, w, *, tm, tn, out_dtype):
    m, k = x.shape
    n = w.shape[1]
    return pl.pallas_call(
        _norm_matmul_kernel,
        grid=(m // tm, n // tn),
        in_specs=[pl.BlockSpec((tm, k), lambda i, j: (i, 0)),
                  pl.BlockSpec((1, k), lambda i, j: (0, 0)),
                  pl.BlockSpec((k, tn), lambda i, j: (0, j))],
        out_specs=pl.BlockSpec((tm, tn), lambda i, j: (i, j)),
        out_shape=jax.ShapeDtypeStruct((m, n), out_dtype),
        scratch_shapes=[pltpu.VMEM((tm, k), BF16)],
        compiler_params=_cparams(("parallel", "arbitrary")),
        name="norm_matmul",
    )(x, g.reshape(1, k), w)


def _rope_tables(seq, dim, width):
    inv = 1.0 / (ROPE_THETA ** (jnp.arange(0, dim, 2, dtype=F32) / dim))
    ang = jnp.arange(seq, dtype=F32)[:, None] * inv[None, :]
    cos, sin = jnp.cos(ang), jnp.sin(ang)
    cos = jnp.concatenate([cos, cos], axis=-1)
    sin = jnp.concatenate([-sin, sin], axis=-1)
    reps = width // dim
    return jnp.tile(cos, (1, reps)), jnp.tile(sin, (1, reps))


def _rope_kernel(q_ref, k_ref, v_ref, cos_ref, sin_ref, qo_ref, ko_ref, vo_ref, *, half, scale):
    cos = cos_ref[...]
    sin = sin_ref[...]
    width = cos.shape[1]
    lane = lax.broadcasted_iota(jnp.int32, cos.shape, 1)
    first = (lane % (2 * half)) < half

    def rot(x):
        ahead = pltpu.roll(x, width - half, axis=1)
        behind = pltpu.roll(x, half, axis=1)
        return x * cos + jnp.where(first, ahead, behind) * sin

    qo_ref[...] = (rot(q_ref[...]) * scale).astype(BF16)
    ko_ref[...] = rot(k_ref[...]).astype(BF16)
    vo_ref[...] = v_ref[...].astype(BF16)


def rope_qkv(z, col0, head_dim, *, tm):
    b, s, _ = z.shape
    cos, sin = _rope_tables(s, head_dim, MIX)
    cb = col0 // MIX
    zspec = lambda off: pl.BlockSpec((None, tm, MIX), lambda bi, i, off=off: (bi, i, cb + off))
    tspec = pl.BlockSpec((tm, MIX), lambda bi, i: (i, 0))
    ospec = pl.BlockSpec((None, tm, MIX), lambda bi, i: (bi, i, 0))
    oshape = jax.ShapeDtypeStruct((b, s, MIX), BF16)
    return pl.pallas_call(
        functools.partial(_rope_kernel, half=head_dim // 2, scale=head_dim ** -0.5),
        grid=(b, s // tm),
        in_specs=[zspec(0), zspec(1), zspec(2), tspec, tspec],
        out_specs=[ospec, ospec, ospec],
        out_shape=[oshape, oshape, oshape],
        compiler_params=_cparams(("parallel", "parallel")),
        name="rope_qkv",
    )(z, z, z, cos, sin)


def _diff_attn_kernel(lam_ref, g_ref, q_ref, k_ref, v_ref, o_ref, *, lam_init):
    lv = lam_ref[...]
    lam = (jnp.exp(jnp.sum(lv[0:1] * lv[1:2], axis=-1, keepdims=True))
           - jnp.exp(jnp.sum(lv[2:3] * lv[3:4], axis=-1, keepdims=True)) + lam_init)
    q = q_ref[...]
    k = k_ref[...]
    lane = lax.broadcasted_iota(jnp.int32, q.shape, 1)
    zero = jnp.zeros_like(q)
    s1 = _dot_t(jnp.where(lane < A_QKDIM, q, zero), k)
    s2 = _dot_t(jnp.where(lane >= A_QKDIM, q, zero), k)
    e1 = jnp.exp(s1 - jnp.max(s1, axis=-1, keepdims=True))
    e2 = jnp.exp(s2 - jnp.max(s2, axis=-1, keepdims=True))
    w1 = 1.0 / jnp.sum(e1, axis=-1, keepdims=True)
    w2 = lam / jnp.sum(e2, axis=-1, keepdims=True)
    attn = (e1 * w1 - e2 * w2).astype(BF16)
    o = _dot(attn, v_ref[...])
    o_ref[...] = _rms(o, g_ref[...]) * (1.0 - lam_init)


def diff_attention(q, k, v, lam_vecs, subln_g, layer_idx, *, tq):
    b, s, _ = q.shape
    lam_init = 0.8 - 0.6 * math.exp(-0.3 * layer_idx)
    qspec = pl.BlockSpec((None, tq, A_VDIM), lambda bi, h, i: (bi, i, h))
    kspec = pl.BlockSpec((None, s, A_VDIM), lambda bi, h, i: (bi, 0, h))
    return pl.pallas_call(
        functools.partial(_diff_attn_kernel, lam_init=lam_init),
        grid=(b, A_HEADS, s // tq),
        in_specs=[pl.BlockSpec((4, A_QKDIM), lambda bi, h, i: (0, 0)),
                  pl.BlockSpec((1, A_VDIM), lambda bi, h, i: (0, 0)),
                  qspec, kspec, kspec],
        out_specs=qspec,
        out_shape=jax.ShapeDtypeStruct((b, s, MIX), F32),
        compiler_params=_cparams(("parallel", "parallel", "arbitrary")),
        name="diff_attention",
    )(lam_vecs, subln_g.reshape(1, A_VDIM), q, k, v)


C_REACH = max(w // 2 for w, _ in C_PATTERNS)


def _dilated_bias_table(tq, window):
    n_delta = (window - tq) // tq + 1
    i = np.arange(tq)[None, :, None]
    j = np.arange(window)[None, None, :]
    n = np.arange(n_delta)[:, None, None]
    d = j - i - n * tq
    count = np.zeros(d.shape, np.int32)
    for w, dil in C_PATTERNS:
        count += ((np.abs(d) <= w // 2) & (d % dil == 0)).astype(np.int32)
    bias = np.where(count > 0, np.log(np.maximum(count, 1)), NEG_INF)
    return jnp.asarray(bias, F32)


def _dilated_attn_kernel(bias_ref, q_ref, k_ref, v_ref, o_ref, *, tq, window, seq):
    start = pl.program_id(2) * tq
    ws = pl.multiple_of(jnp.clip(start - C_REACH, 0, seq - window), tq)
    kw = k_ref[pl.ds(ws, window), :]
    vw = v_ref[pl.ds(ws, window), :]
    s = _dot_t(q_ref[...], kw) + bias_ref[...]
    e = jnp.exp(s - jnp.max(s, axis=-1, keepdims=True))
    den = jnp.sum(e, axis=-1, keepdims=True)
    o_ref[...] = _dot(e.astype(BF16), vw) / den


def dilated_attention(q, k, v, *, tq=128):
    b, s, _ = q.shape
    window = tq + 2 * C_REACH
    assert s >= window and s % tq == 0 and C_REACH % tq == 0
    bias = _dilated_bias_table(tq, window)

    def bias_map(bi, h, i):
        start = i * tq
        ws = jnp.clip(start - C_REACH, 0, s - window)
        return ((start - ws) // tq, 0, 0)

    qspec = pl.BlockSpec((None, tq, C_HDIM), lambda bi, h, i: (bi, i, h))
    kspec = pl.BlockSpec((None, s, C_HDIM), lambda bi, h, i: (bi, 0, h))
    return pl.pallas_call(
        functools.partial(_dilated_attn_kernel, tq=tq, window=window, seq=s),
        grid=(b, C_HEADS, s // tq),
        in_specs=[pl.BlockSpec((None, tq, window), bias_map), qspec, kspec, kspec],
        out_specs=qspec,
        out_shape=jax.ShapeDtypeStruct((b, s, MIX), F32),
        compiler_params=_cparams(("parallel", "parallel", "arbitrary")),
        name="dilated_attention",
    )(bias, q, k, v)


S5_BLOCKS = 4
S5_GPB = D_GROUPS // S5_BLOCKS
S5_CH = S5_GPB * D_GSIZE
S5_ST = S5_GPB * D_STATE


def _s5_scan_kernel(u_ref, bm_ref, cm_ref, pr_ref, pi_ref, y_ref, cr_ref, ci_ref, *, tc, reverse):
    @pl.when(pl.program_id(2) == 0)
    def _():
        cr_ref[...] = jnp.zeros_like(cr_ref)
        ci_ref[...] = jnp.zeros_like(ci_ref)

    bu = _dot(u_ref[...].astype(BF16), bm_ref[...])
    xr = bu[:, :S5_ST]
    xi = bu[:, S5_ST:]
    row = lax.broadcasted_iota(jnp.int32, xr.shape, 0)
    n_levels = tc.bit_length() - 1
    for lvl in range(n_levels):
        sh = 1 << lvl
        prow = (tc - sh) if reverse else (sh - 1)
        ar = pr_ref[prow:prow + 1, :]
        ai = pi_ref[prow:prow + 1, :]
        if reverse:
            keep = row < (tc - sh)
            sr = jnp.where(keep, pltpu.roll(xr, tc - sh, axis=0), 0.0)
            si = jnp.where(keep, pltpu.roll(xi, tc - sh, axis=0), 0.0)
        else:
            keep = row >= sh
            sr = jnp.where(keep, pltpu.roll(xr, sh, axis=0), 0.0)
            si = jnp.where(keep, pltpu.roll(xi, sh, axis=0), 0.0)
        xr, xi = xr + ar * sr - ai * si, xi + ar * si + ai * sr
    pr = pr_ref[...]
    pi = pi_ref[...]
    cr = cr_ref[...]
    ci = ci_ref[...]
    xr, xi = xr + pr * cr - pi * ci, xi + pr * ci + pi * cr
    last = 0 if reverse else tc - 1
    cr_ref[...] = xr[last:last + 1, :]
    ci_ref[...] = xi[last:last + 1, :]
    y_ref[...] = _dot(jnp.concatenate([xr, xi], axis=1).astype(BF16), cm_ref[...])


def _s5_prepare(direction, a_re, a_im, log_dt, b_re, b_im, c_re, c_im, tc):
    lr = jnp.minimum(a_re[direction].astype(F32), -1e-4)
    li = a_im[direction].astype(F32)
    dt = jnp.exp(log_dt[direction].astype(F32))[:, None]
    mag = jnp.exp(dt * lr)
    abr, abi = mag * jnp.cos(dt * li), mag * jnp.sin(dt * li)
    den = lr * lr + li * li
    qr, qi = lr / den, -li / den
    fr = (abr - 1.0) * qr - abi * qi
    fi = (abr - 1.0) * qi + abi * qr
    b_re, b_im = b_re.astype(F32), b_im.astype(F32)
    bbr = fr[..., None] * b_re - fi[..., None] * b_im
    bbi = fr[..., None] * b_im + fi[..., None] * b_re
    eye = jnp.eye(S5_GPB, dtype=F32)

    def in_block(m):
        m = m.reshape(S5_BLOCKS, S5_GPB, D_STATE, D_GSIZE)
        return jnp.einsum('kgnc,gh->kgchn', m, eye).reshape(S5_BLOCKS, S5_CH, S5_ST)

    def out_block(m):
        m = m.reshape(S5_BLOCKS, S5_GPB, D_GSIZE, D_STATE)
        return jnp.einsum('kgcn,gh->kgnhc', m, eye).reshape(S5_BLOCKS, S5_ST, S5_CH)

    bmat = jnp.concatenate([in_block(bbr), in_block(bbi)], axis=2).astype(BF16)
    cmat = jnp.concatenate([out_block(c_re.astype(F32)), -out_block(c_im.astype(F32))],
                           axis=1).astype(BF16)
    steps = jnp.arange(1, tc + 1, dtype=F32)[:, None, None]
    pmag = jnp.exp(steps * (dt * lr)[None])
    ang = steps * (dt * li)[None]
    p_re = (pmag * jnp.cos(ang)).reshape(tc, D_GROUPS * D_STATE)
    p_im = (pmag * jnp.sin(ang)).reshape(tc, D_GROUPS * D_STATE)
    if direction == 1:
        p_re, p_im = p_re[::-1], p_im[::-1]
    return bmat, cmat, p_re, p_im


def s5_scan(z, direction, params, *, tc):
    b, s, _ = z.shape
    nch = s // tc
    bmat, cmat, p_re, p_im = _s5_prepare(direction, *params, tc)
    reverse = direction == 1
    tmap = (lambda c: nch - 1 - c) if reverse else (lambda c: c)
    ub = Z_D // S5_CH
    return pl.pallas_call(
        functools.partial(_s5_scan_kernel, tc=tc, reverse=reverse),
        grid=(b, S5_BLOCKS, nch),
        in_specs=[pl.BlockSpec((None, tc, S5_CH), lambda bi, k, c: (bi, tmap(c), ub + k)),
                  pl.BlockSpec((None, S5_CH, 2 * S5_ST), lambda bi, k, c: (k, 0, 0)),
                  pl.BlockSpec((None, 2 * S5_ST, S5_CH), lambda bi, k, c: (k, 0, 0)),
                  pl.BlockSpec((tc, S5_ST), lambda bi, k, c: (0, k)),
                  pl.BlockSpec((tc, S5_ST), lambda bi, k, c: (0, k))],
        out_specs=pl.BlockSpec((None, tc, S5_CH), lambda bi, k, c: (bi, tmap(c), k)),
        out_shape=jax.ShapeDtypeStruct((b, s, MIX), F32),
        scratch_shapes=[pltpu.VMEM((1, S5_ST), F32), pltpu.VMEM((1, S5_ST), F32)],
        compiler_params=_cparams(("parallel", "parallel", "arbitrary")),
        name="s5_scan",
    )(z, bmat, cmat, p_re, p_im)


def _s5_out_kernel(yf_ref, yb_ref, u_ref, d_ref, w_ref, b_ref, o_ref):
    y = yf_ref[...] + yb_ref[...] + d_ref[...] * u_ref[...]
    gl = jax.nn.gelu(y)
    gate = jax.nn.sigmoid(_dot(gl.astype(BF16), w_ref[...]) + b_ref[...])
    o_ref[...] = gl * gate


def s5_output(yf, yb, z, d_skip, glu_w, glu_b, *, tm):
    b, s, _ = z.shape
    yspec = pl.BlockSpec((None, tm, MIX), lambda bi, i: (bi, i, 0))
    vspec = pl.BlockSpec((1, MIX), lambda bi, i: (0, 0))
    return pl.pallas_call(
        _s5_out_kernel,
        grid=(b, s // tm),
        in_specs=[yspec, yspec,
                  pl.BlockSpec((None, tm, MIX), lambda bi, i: (bi, i, Z_D // MIX)),
                  vspec, pl.BlockSpec((MIX, MIX), lambda bi, i: (0, 0)), vspec],
        out_specs=yspec,
        out_shape=jax.ShapeDtypeStruct((b, s, MIX), F32),
        compiler_params=_cparams(("parallel", "parallel")),
        name="s5_output",
    )(yf, yb, z, d_skip.reshape(1, MIX), glu_w.astype(BF16), glu_b.reshape(1, MIX))


def s5_bidir(z, a_re, a_im, log_dt, b_re, b_im, c_re, c_im, d_skip, glu_w, glu_b, *, tc=256, tm=512):
    params = (a_re, a_im, log_dt, b_re, b_im, c_re, c_im)
    yf = s5_scan(z, 0, params, tc=tc)
    yb = s5_scan(z, 1, params, tc=tc)
    return s5_output(yf, yb, z, d_skip, glu_w, glu_b, tm=tm)


def _softplus(y):
    return jnp.maximum(y, 0.0) + jnp.log(1.0 + jnp.exp(-jnp.abs(y)))


def _head_ones(width):
    seg = np.arange(width) // B_HDIM
    return jnp.asarray(seg[:, None] == seg[None, :], BF16)


def _rwkv_pre_kernel(z_ref, zp_ref, zn_ref, mu_ref, w0_ref, a0_ref, w2_ref, a2_ref, g2_ref,
                     kk_ref, ka_ref, rk_ref, ones_ref,
                     r_o, a_o, w0_o, w1_o, k0_o, k1_o, b0_o, b1_o, vt_o, g_o, bonus_o,
                     *, tm, n_tiles):
    i = pl.program_id(1)
    z = z_ref[...]
    row = lax.broadcasted_iota(jnp.int32, z.shape, 0)
    prev_row = jnp.where(i > 0, zp_ref[7:8, :], 0.0)
    next_row = jnp.where(i < n_tiles - 1, zn_ref[0:1, :], 0.0)
    zp = jnp.where(row == 0, prev_row, pltpu.roll(z, 1, axis=0))
    zn = jnp.where(row == tm - 1, next_row, pltpu.roll(z, tm - 1, axis=0))
    xs = z + mu_ref[0:1, :] * (zp - z) + mu_ref[1:2, :] * (zn - z)
    r = xs[:, 0:MIX]
    k = xs[:, MIX:2 * MIX]
    v = xs[:, 2 * MIX:3 * MIX]
    c0 = 3 * MIX
    wd = xs[:, c0:c0 + 2 * B_LORA]
    ad = xs[:, c0 + 2 * B_LORA:c0 + 4 * B_LORA]
    gd = xs[:, c0 + 4 * B_LORA:c0 + 4 * B_LORA + B_GATE_LORA]
    lw = _dot(jnp.tanh(wd).astype(BF16), w2_ref[...])
    la = _dot(ad.astype(BF16), a2_ref[...])
    g_o[...] = _dot(jax.nn.sigmoid(gd).astype(BF16), g2_ref[...])
    ones = ones_ref[...]
    kk = k * kk_ref[...]
    kk = kk * lax.rsqrt(_segsum(kk * kk, ones) + 1e-12)
    ka = ka_ref[...]
    ksum = jnp.zeros_like(k)
    for d, (w_o, k_o, b_o) in enumerate(((w0_o, k0_o, b0_o), (w1_o, k1_o, b1_o))):
        cols = slice(d * MIX, (d + 1) * MIX)
        logw = -_softplus(-(w0_ref[d:d + 1, :] + lw[:, cols])) - 0.5
        w_o[...] = jnp.exp(-jnp.exp(logw))
        a = jax.nn.sigmoid(a0_ref[d:d + 1, :] + la[:, cols])
        kmod = k * (1.0 + (a - 1.0) * ka)
        k_o[...] = kmod
        b_o[...] = kk * a
        ksum = ksum + kmod
    r_o[...] = r
    a_o[...] = -kk
    bonus_o[...] = _segsum(r * (0.5 * ksum) * rk_ref[...], ones) * v
    vt_o[...] = v.T


def _block_diag2(m):
    z = jnp.zeros_like(m[0])
    return jnp.concatenate([jnp.concatenate([m[0], z], axis=1),
                            jnp.concatenate([z, m[1]], axis=1)], axis=0)


def rwkv_pre(z, mu, w0, w2, a0, a2, g2, k_k, k_a, r_k, *, tm):
    b, s, _ = z.shape
    n_tiles = s // tm
    mu_p = jnp.pad(mu.astype(F32), ((0, 0), (0, IN_B_PAD - IN_B)))
    row_spec = pl.BlockSpec((None, tm, MIX), lambda bi, i: (bi, i, 0))
    vec = lambda n, w: pl.BlockSpec((n, w), lambda bi, i: (0, 0))
    rows = jax.ShapeDtypeStruct((b, s, MIX), F32)
    hb = tm // 8
    return pl.pallas_call(
        functools.partial(_rwkv_pre_kernel, tm=tm, n_tiles=n_tiles),
        grid=(b, n_tiles),
        in_specs=[pl.BlockSpec((None, tm, IN_B_PAD), lambda bi, i: (bi, i, 0)),
                  pl.BlockSpec((None, 8, IN_B_PAD), lambda bi, i: (bi, jnp.maximum(i * hb - 1, 0), 0)),
                  pl.BlockSpec((None, 8, IN_B_PAD), lambda bi, i: (bi, jnp.minimum((i + 1) * hb, s // 8 - 1), 0)),
                  vec(2, IN_B_PAD), vec(2, MIX), vec(2, MIX),
                  vec(2 * B_LORA, 2 * MIX), vec(2 * B_LORA, 2 * MIX), vec(B_GATE_LORA, MIX),
                  vec(1, MIX), vec(1, MIX), vec(1, MIX), vec(MIX, MIX)],
        out_specs=[row_spec] * 8 + [pl.BlockSpec((None, MIX, tm), lambda bi, i: (bi, 0, i)),
                                    row_spec, row_spec],
        out_shape=[rows] * 8 + [jax.ShapeDtypeStruct((b, MIX, s), F32), rows, rows],
        compiler_params=_cparams(("parallel", "parallel")),
        name="rwkv_pre",
    )(z, z, z, mu_p, w0.astype(F32), a0.astype(F32),
      _block_diag2(w2).astype(BF16), _block_diag2(a2).astype(BF16), g2.astype(BF16),
      k_k.reshape(1, MIX), k_a.reshape(1, MIX), r_k.reshape(1, MIX), _head_ones(MIX))


RW_PAIRS = B_HEADS // 2


def _rwkv_scan_kernel(a_ref, r_ref, w_ref, k_ref, b_ref, vt_ref, yt_ref,
                      s_ref, vsub_ref, ysub_ref, yacc_ref, *, nb, tc, sub, reverse):
    c = pl.program_id(0)

    @pl.when(c == 0)
    def _():
        s_ref[...] = jnp.zeros_like(s_ref)

    yacc_ref[...] = jnp.zeros_like(yacc_ref)
    ysub_ref[...] = jnp.zeros_like(ysub_ref)
    n_sub = tc // sub
    left = lax.broadcasted_iota(jnp.int32, (B_HDIM, LANES), 1) < B_HDIM
    lane_t = lax.broadcasted_iota(jnp.int32, (MIX, tc), 1)

    def sub_body(si, carry):
        sidx = (n_sub - 1 - si) if reverse else si
        base = pl.multiple_of(sidx * sub, sub)
        for bi in range(nb):
            vsub_ref[bi] = pltpu.roll(vt_ref[bi], (tc - base) % tc, axis=1)
        for gg in range(sub // 8):
            g8 = (sub // 8 - 1 - gg) if reverse else gg
            rows = pl.ds(pl.multiple_of(base + 8 * g8, 8), 8)
            for bi in range(nb):
                for p in range(RW_PAIRS):
                    cols = slice(p * LANES, (p + 1) * LANES)
                    lo = slice(p * LANES, p * LANES + B_HDIM)
                    hi = slice(p * LANES + B_HDIM, (p + 1) * LANES)
                    q = bi * RW_PAIRS + p
                    a8, w8, b8 = a_ref[bi, rows, cols], w_ref[bi, rows, cols], b_ref[bi, rows, cols]
                    k8, r8 = k_ref[bi, rows, cols], r_ref[bi, rows, cols]
                    st = s_ref[q]
                    for jj in range(8):
                        j = (7 - jj) if reverse else jj
                        u = 8 * g8 + j
                        pa = st * a8[j:j + 1, :]
                        sa = jnp.where(left,
                                       jnp.sum(jnp.where(left, pa, 0.0), axis=1, keepdims=True),
                                       jnp.sum(jnp.where(left, 0.0, pa), axis=1, keepdims=True))
                        vc = jnp.where(left, vsub_ref[bi, lo, u:u + 1], vsub_ref[bi, hi, u:u + 1])
                        st = st * w8[j:j + 1, :] + sa * b8[j:j + 1, :] + vc * k8[j:j + 1, :]
                        pr = st * r8[j:j + 1, :]
                        ysub_ref[bi, lo, u:u + 1] = jnp.sum(jnp.where(left, pr, 0.0), axis=1, keepdims=True)
                        ysub_ref[bi, hi, u:u + 1] = jnp.sum(jnp.where(left, 0.0, pr), axis=1, keepdims=True)
                    s_ref[q] = st
        for bi in range(nb):
            placed = pltpu.roll(ysub_ref[bi], base, axis=1)
            mine = (lane_t >= base) & (lane_t < base + sub)
            yacc_ref[bi] = jnp.where(mine, placed, yacc_ref[bi])
        return carry

    lax.fori_loop(0, n_sub, sub_body, 0)
    yt_ref[...] = yacc_ref[...]


def rwkv_scan(a, r, w, k, bvec, vt, *, reverse, tc=128, sub=32):
    nb, s, _ = a.shape
    nch = s // tc
    tmap = (lambda c: nch - 1 - c) if reverse else (lambda c: c)
    row_spec = pl.BlockSpec((nb, tc, MIX), lambda c: (0, tmap(c), 0))
    col_spec = pl.BlockSpec((nb, MIX, tc), lambda c: (0, 0, tmap(c)))
    return pl.pallas_call(
        functools.partial(_rwkv_scan_kernel, nb=nb, tc=tc, sub=sub, reverse=reverse),
        grid=(nch,),
        in_specs=[row_spec] * 5 + [col_spec],
        out_specs=col_spec,
        out_shape=jax.ShapeDtypeStruct((nb, MIX, s), F32),
        scratch_shapes=[pltpu.VMEM((nb * RW_PAIRS, B_HDIM, LANES), F32),
                        pltpu.VMEM((nb, MIX, tc), F32),
                        pltpu.VMEM((nb, MIX, tc), F32),
                        pltpu.VMEM((nb, MIX, tc), F32)],
        compiler_params=_cparams(("arbitrary",)),
        name="rwkv_scan",
    )(a, r, w, k, bvec, vt)


def _rwkv_post_kernel(yf_ref, yb_ref, bonus_ref, g_ref, lw_ref, lb_ref, ones_ref, o_ref):
    y = (yf_ref[...] + yb_ref[...]).T
    ones = ones_ref[...]
    mean = _segsum(y, ones) * (1.0 / B_HDIM)
    yc = y - mean
    var = _segsum(yc * yc, ones) * (1.0 / B_HDIM)
    yn = yc * lax.rsqrt(var + LN_X_EPS) * lw_ref[...] + lb_ref[...]
    o_ref[...] = (yn + bonus_ref[...]) * g_ref[...]


def rwkv_post(ytf, ytb, bonus, g, lnx_w, lnx_b, *, tm):
    b, _, s = ytf.shape
    col_spec = pl.BlockSpec((None, MIX, tm), lambda bi, i: (bi, 0, i))
    row_spec = pl.BlockSpec((None, tm, MIX), lambda bi, i: (bi, i, 0))
    vec = pl.BlockSpec((1, MIX), lambda bi, i: (0, 0))
    return pl.pallas_call(
        _rwkv_post_kernel,
        grid=(b, s // tm),
        in_specs=[col_spec, col_spec, row_spec, row_spec, vec, vec,
                  pl.BlockSpec((MIX, MIX), lambda bi, i: (0, 0))],
        out_specs=row_spec,
        out_shape=jax.ShapeDtypeStruct((b, s, MIX), F32),
        compiler_params=_cparams(("parallel", "parallel")),
        name="rwkv_post",
    )(ytf, ytb, bonus, g, lnx_w.reshape(1, MIX), lnx_b.reshape(1, MIX), _head_ones(MIX))


def rwkv7_bidir(z, mu, w0, w2, a0, a2, g2, k_k, k_a, r_k, lnx_w, lnx_b, *, tm=256, tc=128, sub=32):
    r, a, wf, wb, kf, kb, bf, bb, vt, g, bonus = rwkv_pre(z, mu, w0, w2, a0, a2, g2, k_k, k_a, r_k, tm=tm)
    ytf = rwkv_scan(a, r, wf, kf, bf, vt, reverse=False, tc=tc, sub=sub)
    ytb = rwkv_scan(a, r, wb, kb, bb, vt, reverse=True, tc=tc, sub=sub)
    return rwkv_post(ytf, ytb, bonus, g, lnx_w, lnx_b, tm=tm)


def _mix_out_kernel(h_ref, oa_ref, ob_ref, oc_ref, od_ref, gc_ref, gd_ref, w_ref, o_ref, mix_ref):
    @pl.when(pl.program_id(1) == 0)
    def _():
        mix_ref[...] = jnp.concatenate(
            [oa_ref[...], ob_ref[...], _rms(oc_ref[...], gc_ref[...]), _rms(od_ref[...], gd_ref[...])],
            axis=1).astype(BF16)

    o_ref[...] = h_ref[...] + _dot(mix_ref[...], w_ref[...])


def mix_out(h, oa, ob, oc, od, gc, gd, w_out, *, tm, tn):
    m, d = h.shape
    mspec = pl.BlockSpec((tm, MIX), lambda i, j: (i, 0))
    vspec = pl.BlockSpec((1, MIX), lambda i, j: (0, 0))
    hspec = pl.BlockSpec((tm, tn), lambda i, j: (i, j))
    return pl.pallas_call(
        _mix_out_kernel,
        grid=(m // tm, d // tn),
        in_specs=[hspec, mspec, mspec, mspec, mspec, vspec, vspec,
                  pl.BlockSpec((4 * MIX, tn), lambda i, j: (0, j))],
        out_specs=hspec,
        out_shape=jax.ShapeDtypeStruct((m, d), F32),
        scratch_shapes=[pltpu.VMEM((tm, 4 * MIX), BF16)],
        compiler_params=_cparams(("parallel", "arbitrary")),
        name="mix_out",
    )(h, oa, ob, oc, od, gc.reshape(1, MIX), gd.reshape(1, MIX), w_out)


def _cross_attn_kernel(h_ref, q_ref, kv_ref, wo_ref, o_ref):
    q = q_ref[...]
    outs = []
    for hd in range(XA_HEADS):
        cols = slice(hd * XA_HDIM, (hd + 1) * XA_HDIM)
        kh = kv_ref[:, cols]
        vh = kv_ref[:, D_MODEL + hd * XA_HDIM:D_MODEL + (hd + 1) * XA_HDIM]
        s = _dot_t(q[:, cols], kh) * (XA_HDIM ** -0.5)
        e = jnp.exp(s - jnp.max(s, axis=-1, keepdims=True))
        p = e / jnp.sum(e, axis=-1, keepdims=True)
        outs.append(_dot(p.astype(BF16), vh).astype(BF16))
    o = jnp.concatenate(outs, axis=1)
    o_ref[...] = h_ref[...] + _dot(o, wo_ref[...])


def cross_attention(h, q, kv, wo, *, tq):
    b, s, d = h.shape
    n_mem = kv.shape[1]
    hspec = pl.BlockSpec((None, tq, d), lambda bi, i: (bi, i, 0))
    return pl.pallas_call(
        _cross_attn_kernel,
        grid=(b, s // tq),
        in_specs=[hspec, hspec,
                  pl.BlockSpec((None, n_mem, 2 * d), lambda bi, i: (bi, 0, 0)),
                  pl.BlockSpec((d, d), lambda bi, i: (0, 0))],
        out_specs=hspec,
        out_shape=jax.ShapeDtypeStruct((b, s, d), F32),
        compiler_params=_cparams(("parallel", "parallel")),
        name="cross_attention",
    )(h, q, kv, wo)


ROUTER_W = LANES
MOE_TB = 256


def _router_kernel(h_ref, g_ref, wr_ref, xn_ref, eid_ref, gate_ref):
    xn = _rms(h_ref[...], g_ref[...])
    xn_ref[...] = xn
    logits = jnp.dot(xn, wr_ref[...], preferred_element_type=F32, precision=lax.Precision.HIGHEST)
    lane = lax.broadcasted_iota(jnp.int32, logits.shape, 1)
    big = jnp.int32(ROUTER_W)

    def first_max(mask):
        m = jnp.max(jnp.where(mask, logits, NEG_INF), axis=-1, keepdims=True)
        idx = jnp.min(jnp.where(mask & (logits == m), lane, big), axis=-1, keepdims=True)
        return m, idx

    gmask = lane < N_GROUPS
    gmax, gidx = first_max(gmask)
    g_w = 1.0 / jnp.sum(jnp.where(gmask, jnp.exp(logits - gmax), 0.0), axis=-1, keepdims=True)
    e0 = N_GROUPS + gidx * EXPERTS_PER_GROUP
    emask = (lane >= e0) & (lane < e0 + EXPERTS_PER_GROUP)
    m1, i1 = first_max(emask)
    m2, i2 = first_max(emask & (lane != i1))
    e2 = jnp.exp(m2 - m1)
    w1 = 1.0 / (1.0 + e2)
    w2 = e2 / (1.0 + e2)
    eid_ref[...] = jnp.where(lane == 0, i1 - N_GROUPS, jnp.where(lane == 1, i2 - N_GROUPS, 0))
    gate_ref[...] = jnp.where(lane == 0, g_w * w1, jnp.where(lane == 1, g_w * w2, 0.0))


def moe_router(h, g, wr_group, wr_expert, *, tm):
    m, d = h.shape
    wr = jnp.concatenate([wr_group, wr_expert], axis=1).astype(F32)
    wr = jnp.pad(wr, ((0, 0), (0, ROUTER_W - wr.shape[1])))
    hspec = pl.BlockSpec((tm, d), lambda i: (i, 0))
    lspec = pl.BlockSpec((tm, ROUTER_W), lambda i: (i, 0))
    return pl.pallas_call(
        _router_kernel,
        grid=(m // tm,),
        in_specs=[hspec, pl.BlockSpec((1, d), lambda i: (0, 0)),
                  pl.BlockSpec((d, ROUTER_W), lambda i: (0, 0))],
        out_specs=[hspec, lspec, lspec],
        out_shape=[jax.ShapeDtypeStruct((m, d), F32),
                   jax.ShapeDtypeStruct((m, ROUTER_W), jnp.int32),
                   jax.ShapeDtypeStruct((m, ROUTER_W), F32)],
        compiler_params=_cparams(("parallel",)),
        name="moe_router",
    )(h, g.reshape(1, d), wr)


def _moe_expert_kernel(tok_ref, exp_ref, nvb_ref, x_hbm, gate_ref, w1_ref, w3_ref, w2_ref, o_ref,
                       xbuf, sem, *, tb):
    i = pl.program_id(0)

    def row_copy(r, tok):
        return pltpu.make_async_copy(x_hbm.at[pl.ds(tok, 1)], xbuf.at[pl.ds(r, 1)], sem)

    @pl.when(i < nvb_ref[0])
    def _():
        def issue(r, c):
            row_copy(r, tok_ref[i * tb + r]).start()
            return c

        lax.fori_loop(0, tb, issue, 0)

        def drain(r, c):
            row_copy(r, 0).wait()
            return c

        lax.fori_loop(0, tb, drain, 0)
        x = xbuf[...].astype(BF16)
        h1 = _dot(x, w1_ref[...].astype(BF16))
        h3 = _dot(x, w3_ref[...].astype(BF16))
        act = (jax.nn.silu(h1) * h3).astype(BF16)
        o_ref[...] = _dot(act, w2_ref[...].astype(BF16)) * gate_ref[...]

    @pl.when(i >= nvb_ref[0])
    def _():
        o_ref[...] = jnp.zeros_like(o_ref)


def moe_experts(xn, slot_tok, slot_gate, blk_exp, n_valid, w1, w3, w2, *, tb):
    cap = slot_tok.shape[0]
    d = xn.shape[1]
    nblk = cap // tb
    wspec = lambda shape: pl.BlockSpec((None,) + shape, lambda i, tok, ex, nv: (ex[i], 0, 0))
    grid_spec = pltpu.PrefetchScalarGridSpec(
        num_scalar_prefetch=3,
        grid=(nblk,),
        in_specs=[pl.BlockSpec(memory_space=pl.ANY),
                  pl.BlockSpec((tb, 1), lambda i, tok, ex, nv: (i, 0)),
                  wspec((d, D_EXPERT)), wspec((d, D_EXPERT)), wspec((D_EXPERT, d))],
        out_specs=pl.BlockSpec((tb, d), lambda i, tok, ex, nv: (i, 0)),
        scratch_shapes=[pltpu.VMEM((tb, d), F32), pltpu.SemaphoreType.DMA(())],
    )
    return pl.pallas_call(
        functools.partial(_moe_expert_kernel, tb=tb),
        grid_spec=grid_spec,
        out_shape=jax.ShapeDtypeStruct((cap, d), F32),
        compiler_params=_cparams(("arbitrary",)),
        name="moe_experts",
    )(slot_tok, blk_exp, n_valid, xn, slot_gate.reshape(cap, 1), w1, w3, w2)


def _moe_combine_kernel(pos_ref, ys_hbm, h_ref, o_ref, buf, sem, *, tm):
    i = pl.program_id(0)

    def row_copy(r, src):
        return pltpu.make_async_copy(ys_hbm.at[pl.ds(src, 1)], buf.at[pl.ds(r, 1)], sem)

    def issue(r, c):
        row_copy(r, pos_ref[i * 2 * tm + r]).start()
        return c

    lax.fori_loop(0, 2 * tm, issue, 0)

    def drain(r, c):
        row_copy(r, 0).wait()
        return c

    lax.fori_loop(0, 2 * tm, drain, 0)
    o_ref[...] = h_ref[...] + buf[0:tm, :] + buf[tm:2 * tm, :]


def moe_combine(h, ys, pos, *, tm):
    m, d = h.shape
    hspec = pl.BlockSpec((tm, d), lambda i, p: (i, 0))
    grid_spec = pltpu.PrefetchScalarGridSpec(
        num_scalar_prefetch=1,
        grid=(m // tm,),
        in_specs=[pl.BlockSpec(memory_space=pl.ANY), hspec],
        out_specs=hspec,
        scratch_shapes=[pltpu.VMEM((2 * tm, d), F32), pltpu.SemaphoreType.DMA(())],
    )
    return pl.pallas_call(
        functools.partial(_moe_combine_kernel, tm=tm),
        grid_spec=grid_spec,
        out_shape=jax.ShapeDtypeStruct((m, d), F32),
        compiler_params=_cparams(("arbitrary",)),
        name="moe_combine",
    )(pos, ys, h)


def _moe_slots(eid, gate, tb):
    t = eid.shape[0]
    n_assign = t * TOP_K
    eid = eid.reshape(-1)
    gate = gate.reshape(-1)
    tok = jnp.repeat(jnp.arange(t, dtype=jnp.int32), TOP_K)
    order = jnp.argsort(eid)
    se = eid[order]
    counts = jnp.bincount(eid, length=N_EXPERTS)
    padded = ((counts + tb - 1) // tb) * tb
    pad_end = jnp.cumsum(padded)
    pad_start = pad_end - padded
    start = jnp.cumsum(counts) - counts
    dest = (pad_start[se] + (jnp.arange(n_assign, dtype=jnp.int32) - start[se])).astype(jnp.int32)
    cap = n_assign + N_EXPERTS * tb
    nblk = cap // tb
    slot_tok = jnp.zeros((cap,), jnp.int32).at[dest].set(tok[order])
    slot_gate = jnp.zeros((cap,), F32).at[dest].set(gate[order])
    n_valid = (pad_end[-1] // tb).astype(jnp.int32)
    blk = jnp.minimum(jnp.arange(nblk, dtype=jnp.int32), n_valid - 1) * tb
    blk_exp = jnp.minimum(jnp.searchsorted(pad_end, blk, side='right'), N_EXPERTS - 1).astype(jnp.int32)
    pos = jnp.zeros((n_assign,), jnp.int32).at[order].set(dest).reshape(t, TOP_K)
    return slot_tok, slot_gate, blk_exp, n_valid.reshape(1), pos


def hier_moe(h, g, wr_group, wr_expert, w1, w3, w2, *, tm_router=512, tm_combine=128, tb=MOE_TB):
    m, d = h.shape
    xn, eid, gate = moe_router(h, g, wr_group, wr_expert, tm=tm_router)
    slot_tok, slot_gate, blk_exp, n_valid, pos = _moe_slots(eid[:, :TOP_K], gate[:, :TOP_K], tb)
    ys = moe_experts(xn, slot_tok, slot_gate, blk_exp, n_valid, w1, w3, w2, tb=tb)
    pos = pos.reshape(m // tm_combine, tm_combine, TOP_K).transpose(0, 2, 1).reshape(-1)
    return moe_combine(h, ys, pos, tm=tm_combine)


def _final_norm_kernel(h_ref, g_ref, o_ref):
    o_ref[...] = _rms(h_ref[...], g_ref[...])


def final_norm(h, g, *, tm):
    m, d = h.shape
    hspec = pl.BlockSpec((tm, d), lambda i: (i, 0))
    return pl.pallas_call(
        _final_norm_kernel,
        grid=(m // tm,),
        in_specs=[hspec, pl.BlockSpec((1, d), lambda i: (0, 0))],
        out_specs=hspec,
        out_shape=jax.ShapeDtypeStruct((m, d), F32),
        compiler_params=_cparams(("parallel",)),
        name="final_norm",
    )(h, g.reshape(1, d))


def _w_in_layout(w_in):
    wa = w_in[:, :IN_A]
    wb = w_in[:, IN_A:IN_A + IN_B]
    wc = w_in[:, IN_A + IN_B:IN_A + IN_B + IN_C]
    wd = w_in[:, IN_A + IN_B + IN_C:]
    wb = jnp.pad(wb, ((0, 0), (0, IN_B_PAD - IN_B)))
    return jnp.concatenate([wb, wa, wc, wd], axis=1).astype(BF16)


def kernel(x, mem, norm_mix, w_in, w_out, diff_lambda, diff_subln, rwkv_mu, rwkv_w0, rwkv_w2,
           rwkv_a0, rwkv_a2, rwkv_g2, rwkv_kk, rwkv_ka, rwkv_rk, rwkv_lnx_w, rwkv_lnx_b,
           s5_a_re, s5_a_im, s5_log_dt, s5_b_re, s5_b_im, s5_c_re, s5_c_im, s5_d, s5_glu_w,
           s5_glu_b, mix_out_norm, norm_cross, norm_mem, xa_wq, xa_wkv, xa_wo, norm_moe,
           router_group, router_expert, moe_w1, moe_w3, moe_w2, norm_final):
    b, s, d = x.shape
    m = b * s
    n_mem = mem.shape[1]
    depth = w_in.shape[0]
    h = x.reshape(m, d)
    mem2 = mem.reshape(b * n_mem, d)
    for l in range(depth):
        z = norm_matmul(h, norm_mix[l], _w_in_layout(w_in[l]), tm=1024, tn=512, out_dtype=F32)
        z = z.reshape(b, s, Z_W)
        qa, ka, va = rope_qkv(z, Z_A, A_QKDIM, tm=512)
        oa = diff_attention(qa, ka, va, diff_lambda[l], diff_subln[l], l, tq=128)
        ob = rwkv7_bidir(z, rwkv_mu[l], rwkv_w0[l], rwkv_w2[l], rwkv_a0[l], rwkv_a2[l], rwkv_g2[l],
                         rwkv_kk[l], rwkv_ka[l], rwkv_rk[l], rwkv_lnx_w[l], rwkv_lnx_b[l])
        qc, kc, vc = rope_qkv(z, Z_C, C_HDIM, tm=512)
        oc = dilated_attention(qc, kc, vc)
        od = s5_bidir(z, s5_a_re[l], s5_a_im[l], s5_log_dt[l], s5_b_re[l], s5_b_im[l],
                      s5_c_re[l], s5_c_im[l], s5_d[l], s5_glu_w[l], s5_glu_b[l])
        h = mix_out(h, oa.reshape(m, MIX), ob.reshape(m, MIX), oc.reshape(m, MIX), od.reshape(m, MIX),
                    mix_out_norm[l, 0], mix_out_norm[l, 1], w_out[l].astype(BF16), tm=512, tn=512)
        q = norm_matmul(h, norm_cross[l], xa_wq[l].astype(BF16), tm=1024, tn=512, out_dtype=BF16)
        kv = norm_matmul(mem2, norm_mem[l], xa_wkv[l].astype(BF16), tm=b * n_mem, tn=512, out_dtype=BF16)
        h = cross_attention(h.reshape(b, s, d), q.reshape(b, s, d), kv.reshape(b, n_mem, 2 * d),
                            xa_wo[l].astype(BF16), tq=256).reshape(m, d)
        h = hier_moe(h, norm_moe[l], router_group[l], router_expert[l], moe_w1[l], moe_w3[l], moe_w2[l])
    return final_norm(h, norm_final, tm=512).reshape(b, s, d)
```

```python
import functools
import math

import numpy as np
import jax
import jax.numpy as jnp
from jax import lax
from jax.experimental import pallas as pl
from jax.experimental.pallas import tpu as pltpu

F32 = jnp.float32
BF16 = jnp.bfloat16

D_MODEL = 2048
MIX = D_MODEL // 4
A_HEADS = 4
A_VDIM = MIX // A_HEADS
A_QKDIM = A_VDIM // 2
B_HDIM = 64
B_HEADS = MIX // B_HDIM
B_LORA = 64
B_GATE_LORA = 128
LN_X_EPS = 64e-5
C_HEADS = 4
C_HDIM = MIX // C_HEADS
C_PATTERNS = ((128, 1), (512, 4), (2048, 16))
D_GSIZE = 16
D_GROUPS = MIX // D_GSIZE
D_STATE = 64
IN_A = 3 * MIX
IN_B = 3 * MIX + 4 * B_LORA + B_GATE_LORA
IN_B_PAD = 2048
IN_C = 3 * MIX
IN_D = MIX
XA_HEADS = 4
XA_HDIM = D_MODEL // XA_HEADS
N_GROUPS = 4
EXPERTS_PER_GROUP = 8
N_EXPERTS = N_GROUPS * EXPERTS_PER_GROUP
TOP_K = 2
D_EXPERT = D_MODEL // 4
ROPE_THETA = 10000.0
RMS_EPS = 1e-6
NEG_INF = -1e30

Z_B = 0
Z_A = IN_B_PAD
Z_C = Z_A + IN_A
Z_D = Z_C + IN_C
Z_W = Z_D + IN_D

LANES = 128
VMEM_LIMIT = 56 * 1024 * 1024


def _cparams(sem, vmem=VMEM_LIMIT):
    return pltpu.CompilerParams(dimension_semantics=sem, vmem_limit_bytes=vmem)


def _rms(x, g):
    return x * lax.rsqrt(jnp.mean(x * x, axis=-1, keepdims=True) + RMS_EPS) * g


def _dot(a, b):
    return jnp.dot(a, b, preferred_element_type=F32)


def _dot_t(a, b):
    return lax.dot_general(a, b, (((1,), (1,)), ((), ())), preferred_element_type=F32)


def _split3(x):
    hi = x.astype(BF16)
    r1 = x - hi.astype(F32)
    mid = r1.astype(BF16)
    lo = (r1 - mid.astype(F32)).astype(BF16)
    return hi, mid, lo


def _segsum(x, ones):
    hi, mid, lo = _split3(x)
    return _dot(hi, ones) + _dot(mid, ones) + _dot(lo, ones)


def _norm_matmul_kernel(x_ref, g_ref, w_ref, o_ref, xn_ref):
    @pl.when(pl.program_id(1) == 0)
    def _():
        xn_ref[...] = _rms(x_ref[...], g_ref[...]).astype(BF16)

    o_ref[...] = _dot(xn_ref[...], w_ref[...]).astype(o_ref.dtype)


def norm_matmul(x, g, w, *, tm, tn, out_dtype):
    m, k = x.shape
    n = w.shape[1]
    return pl.pallas_call(
        _norm_matmul_kernel,
        grid=(m // tm, n // tn),
        in_specs=[pl.BlockSpec((tm, k), lambda i, j: (i, 0)),
                  pl.BlockSpec((1, k), lambda i, j: (0, 0)),
                  pl.BlockSpec((k, tn), lambda i, j: (0, j))],
        out_specs=pl.BlockSpec((tm, tn), lambda i, j: (i, j)),
        out_shape=jax.ShapeDtypeStruct((m, n), out_dtype),
        scratch_shapes=[pltpu.VMEM((tm, k), BF16)],
        compiler_params=_cparams(("parallel", "arbitrary")),
        name="norm_matmul",
    )(x, g.reshape(1, k), w)


def _rope_tables(seq, dim, width):
    inv = 1.0 / (ROPE_THETA ** (jnp.arange(0, dim, 2, dtype=F32) / dim))
    ang = jnp.arange(seq, dtype=F32)[:, None] * inv[None, :]
    cos, sin = jnp.cos(ang), jnp.sin(ang)
    cos = jnp.concatenate([cos, cos], axis=-1)
    sin = jnp.concatenate([-sin, sin], axis=-1)
    reps = width // dim
    return jnp.tile(cos, (1, reps)), jnp.tile(sin, (1, reps))


def _rope_kernel(q_ref, k_ref, v_ref, cos_ref, sin_ref, qo_ref, ko_ref, vo_ref, *, half, scale):
    cos = cos_ref[...]
    sin = sin_ref[...]
    width = cos.shape[1]
    lane = lax.broadcasted_iota(jnp.int32, cos.shape, 1)
    first = (lane % (2 * half)) < half

    def rot(x):
        ahead = pltpu.roll(x, width - half, axis=1)
        behind = pltpu.roll(x, half, axis=1)
        return x * cos + jnp.where(first, ahead, behind) * sin

    qo_ref[...] = (rot(q_ref[...]) * scale).astype(BF16)
    ko_ref[...] = rot(k_ref[...]).astype(BF16)
    vo_ref[...] = v_ref[...].astype(BF16)


def rope_qkv(z, col0, head_dim, *, tm):
    b, s, _ = z.shape
    cos, sin = _rope_tables(s, head_dim, MIX)
    cb = col0 // MIX
    zspec = lambda off: pl.BlockSpec((None, tm, MIX), lambda bi, i, off=off: (bi, i, cb + off))
    tspec = pl.BlockSpec((tm, MIX), lambda bi, i: (i, 0))
    ospec = pl.BlockSpec((None, tm, MIX), lambda bi, i: (bi, i, 0))
    oshape = jax.ShapeDtypeStruct((b, s, MIX), BF16)
    return pl.pallas_call(
        functools.partial(_rope_kernel, half=head_dim // 2, scale=head_dim ** -0.5),
        grid=(b, s // tm),
        in_specs=[zspec(0), zspec(1), zspec(2), tspec, tspec],
        out_specs=[ospec, ospec, ospec],
        out_shape=[oshape, oshape, oshape],
        compiler_params=_cparams(("parallel", "parallel")),
        name="rope_qkv",
    )(z, z, z, cos, sin)


def _diff_attn_kernel(lam_ref, g_ref, q_ref, k_ref, v_ref, o_ref, *, lam_init):
    lv = lam_ref[...]
    lam = (jnp.exp(jnp.sum(lv[0:1] * lv[1:2], axis=-1, keepdims=True))
           - jnp.exp(jnp.sum(lv[2:3] * lv[3:4], axis=-1, keepdims=True)) + lam_init)
    q = q_ref[...]
    k = k_ref[...]
    lane = lax.broadcasted_iota(jnp.int32, q.shape, 1)
    zero = jnp.zeros_like(q)
    s1 = _dot_t(jnp.where(lane < A_QKDIM, q, zero), k)
    s2 = _dot_t(jnp.where(lane >= A_QKDIM, q, zero), k)
    e1 = jnp.exp(s1 - jnp.max(s1, axis=-1, keepdims=True))
    e2 = jnp.exp(s2 - jnp.max(s2, axis=-1, keepdims=True))
    w1 = 1.0 / jnp.sum(e1, axis=-1, keepdims=True)
    w2 = lam / jnp.sum(e2, axis=-1, keepdims=True)
    attn = (e1 * w1 - e2 * w2).astype(BF16)
    o = _dot(attn, v_ref[...])
    o_ref[...] = _rms(o, g_ref[...]) * (1.0 - lam_init)


def diff_attention(q, k, v, lam_vecs, subln_g, layer_idx, *, tq):
    b, s, _ = q.shape
    lam_init = 0.8 - 0.6 * math.exp(-0.3 * layer_idx)
    qspec = pl.BlockSpec((None, tq, A_VDIM), lambda bi, h, i: (bi, i, h))
    kspec = pl.BlockSpec((None, s, A_VDIM), lambda bi, h, i: (bi, 0, h))
    return pl.pallas_call(
        functools.partial(_diff_attn_kernel, lam_init=lam_init),
        grid=(b, A_HEADS, s // tq),
        in_specs=[pl.BlockSpec((4, A_QKDIM), lambda bi, h, i: (0, 0)),
                  pl.BlockSpec((1, A_VDIM), lambda bi, h, i: (0, 0)),
                  qspec, kspec, kspec],
        out_specs=qspec,
        out_shape=jax.ShapeDtypeStruct((b, s, MIX), F32),
        compiler_params=_cparams(("parallel", "parallel", "arbitrary")),
        name="diff_attention",
    )(lam_vecs, subln_g.reshape(1, A_VDIM), q, k, v)


C_REACH = max(w // 2 for w, _ in C_PATTERNS)


def _dilated_bias_table(tq, window):
    n_delta = (window - tq) // tq + 1
    i = np.arange(tq)[None, :, None]
    j = np.arange(window)[None, None, :]
    n = np.arange(n_delta)[:, None, None]
    d = j - i - n * tq
    count = np.zeros(d.shape, np.int32)
    for w, dil in C_PATTERNS:
        count += ((np.abs(d) <= w // 2) & (d % dil == 0)).astype(np.int32)
    bias = np.where(count > 0, np.log(np.maximum(count, 1)), NEG_INF)
    return jnp.asarray(bias, F32)


def _dilated_attn_kernel(bias_ref, q_ref, k_ref, v_ref, o_ref, *, tq, window, seq):
    start = pl.program_id(2) * tq
    ws = pl.multiple_of(jnp.clip(start - C_REACH, 0, seq - window), tq)
    kw = k_ref[pl.ds(ws, window), :]
    vw = v_ref[pl.ds(ws, window), :]
    s = _dot_t(q_ref[...], kw) + bias_ref[...]
    e = jnp.exp(s - jnp.max(s, axis=-1, keepdims=True))
    den = jnp.sum(e, axis=-1, keepdims=True)
    o_ref[...] = _dot(e.astype(BF16), vw) / den


def dilated_attention(q, k, v, *, tq=128):
    b, s, _ = q.shape
    window = tq + 2 * C_REACH
    assert s >= window and s % tq == 0 and C_REACH % tq == 0
    bias = _dilated_bias_table(tq, window)

    def bias_map(bi, h, i):
        start = i * tq
        ws = jnp.clip(start - C_REACH, 0, s - window)
        return ((start - ws) // tq, 0, 0)

    qspec = pl.BlockSpec((None, tq, C_HDIM), lambda bi, h, i: (bi, i, h))
    kspec = pl.BlockSpec((None, s, C_HDIM), lambda bi, h, i: (bi, 0, h))
    return pl.pallas_call(
        functools.partial(_dilated_attn_kernel, tq=tq, window=window, seq=s),
        grid=(b, C_HEADS, s // tq),
        in_specs=[pl.BlockSpec((None, tq, window), bias_map), qspec, kspec, kspec],
        out_specs=qspec,
        out_shape=jax.ShapeDtypeStruct((b, s, MIX), F32),
        compiler_params=_cparams(("parallel", "parallel", "arbitrary")),
        name="dilated_attention",
    )(bias, q, k, v)


S5_BLOCKS = 4
S5_GPB = D_GROUPS // S5_BLOCKS
S5_CH = S5_GPB * D_GSIZE
S5_ST = S5_GPB * D_STATE


def _s5_scan_kernel(u_ref, bm_ref, cm_ref, pr_ref, pi_ref, y_ref, cr_ref, ci_ref, *, tc, reverse):
    @pl.when(pl.program_id(2) == 0)
    def _():
        cr_ref[...] = jnp.zeros_like(cr_ref)
        ci_ref[...] = jnp.zeros_like(ci_ref)

    bu = _dot(u_ref[...].astype(BF16), bm_ref[...])
    xr = bu[:, :S5_ST]
    xi = bu[:, S5_ST:]
    row = lax.broadcasted_iota(jnp.int32, xr.shape, 0)
    n_levels = tc.bit_length() - 1
    for lvl in range(n_levels):
        sh = 1 << lvl
        prow = (tc - sh) if reverse else (sh - 1)
        ar = pr_ref[prow:prow + 1, :]
        ai = pi_ref[prow:prow + 1, :]
        if reverse:
            keep = row < (tc - sh)
            sr = jnp.where(keep, pltpu.roll(xr, tc - sh, axis=0), 0.0)
            si = jnp.where(keep, pltpu.roll(xi, tc - sh, axis=0), 0.0)
        else:
            keep = row >= sh
            sr = jnp.where(keep, pltpu.roll(xr, sh, axis=0), 0.0)
            si = jnp.where(keep, pltpu.roll(xi, sh, axis=0), 0.0)
        xr, xi = xr + ar * sr - ai * si, xi + ar * si + ai * sr
    pr = pr_ref[...]
    pi = pi_ref[...]
    cr = cr_ref[...]
    ci = ci_ref[...]
    xr, xi = xr + pr * cr - pi * ci, xi + pr * ci + pi * cr
    last = 0 if reverse else tc - 1
    cr_ref[...] = xr[last:last + 1, :]
    ci_ref[...] = xi[last:last + 1, :]
    y_ref[...] = _dot(jnp.concatenate([xr, xi], axis=1).astype(BF16), cm_ref[...])


def _s5_prepare(direction, a_re, a_im, log_dt, b_re, b_im, c_re, c_im, tc):
    lr = jnp.minimum(a_re[direction].astype(F32), -1e-4)
    li = a_im[direction].astype(F32)
    dt = jnp.exp(log_dt[direction].astype(F32))[:, None]
    mag = jnp.exp(dt * lr)
    abr, abi = mag * jnp.cos(dt * li), mag * jnp.sin(dt * li)
    den = lr * lr + li * li
    qr, qi = lr / den, -li / den
    fr = (abr - 1.0) * qr - abi * qi
    fi = (abr - 1.0) * qi + abi * qr
    b_re, b_im = b_re.astype(F32), b_im.astype(F32)
    bbr = fr[..., None] * b_re - fi[..., None] * b_im
    bbi = fr[..., None] * b_im + fi[..., None] * b_re
    eye = jnp.eye(S5_GPB, dtype=F32)

    def in_block(m):
        m = m.reshape(S5_BLOCKS, S5_GPB, D_STATE, D_GSIZE)
        return jnp.einsum('kgnc,gh->kgchn', m, eye).reshape(S5_BLOCKS, S5_CH, S5_ST)

    def out_block(m):
        m = m.reshape(S5_BLOCKS, S5_GPB, D_GSIZE, D_STATE)
        return jnp.einsum('kgcn,gh->kgnhc', m, eye).reshape(S5_BLOCKS, S5_ST, S5_CH)

    bmat = jnp.concatenate([in_block(bbr), in_block(bbi)], axis=2).astype(BF16)
    cmat = jnp.concatenate([out_block(c_re.astype(F32)), -out_block(c_im.astype(F32))],
                           axis=1).astype(BF16)
    steps = jnp.arange(1, tc + 1, dtype=F32)[:, None, None]
    pmag = jnp.exp(steps * (dt * lr)[None])
    ang = steps * (dt * li)[None]
    p_re = (pmag * jnp.cos(ang)).reshape(tc, D_GROUPS * D_STATE)
    p_im = (pmag * jnp.sin(ang)).reshape(tc, D_GROUPS * D_STATE)
    if direction == 1:
        p_re, p_im = p_re[::-1], p_im[::-1]
    return bmat, cmat, p_re, p_im


def s5_scan(z, direction, params, *, tc):
    b, s, _ = z.shape
    nch = s // tc
    bmat, cmat, p_re, p_im = _s5_prepare(direction, *params, tc)
    reverse = direction == 1
    tmap = (lambda c: nch - 1 - c) if reverse else (lambda c: c)
    ub = Z_D // S5_CH
    return pl.pallas_call(
        functools.partial(_s5_scan_kernel, tc=tc, reverse=reverse),
        grid=(b, S5_BLOCKS, nch),
        in_specs=[pl.BlockSpec((None, tc, S5_CH), lambda bi, k, c: (bi, tmap(c), ub + k)),
                  pl.BlockSpec((None, S5_CH, 2 * S5_ST), lambda bi, k, c: (k, 0, 0)),
                  pl.BlockSpec((None, 2 * S5_ST, S5_CH), lambda bi, k, c: (k, 0, 0)),
                  pl.BlockSpec((tc, S5_ST), lambda bi, k, c: (0, k)),
                  pl.BlockSpec((tc, S5_ST), lambda bi, k, c: (0, k))],
        out_specs=pl.BlockSpec((None, tc, S5_CH), lambda bi, k, c: (bi, tmap(c), k)),
        out_shape=jax.ShapeDtypeStruct((b, s, MIX), F32),
        scratch_shapes=[pltpu.VMEM((1, S5_ST), F32), pltpu.VMEM((1, S5_ST), F32)],
        compiler_params=_cparams(("parallel", "parallel", "arbitrary")),
        name="s5_scan",
    )(z, bmat, cmat, p_re, p_im)


def _s5_out_kernel(yf_ref, yb_ref, u_ref, d_ref, w_ref, b_ref, o_ref):
    y = yf_ref[...] + yb_ref[...] + d_ref[...] * u_ref[...]
    gl = jax.nn.gelu(y)
    gate = jax.nn.sigmoid(_dot(gl.astype(BF16), w_ref[...]) + b_ref[...])
    o_ref[...] = gl * gate


def s5_output(yf, yb, z, d_skip, glu_w, glu_b, *, tm):
    b, s, _ = z.shape
    yspec = pl.BlockSpec((None, tm, MIX), lambda bi, i: (bi, i, 0))
    vspec = pl.BlockSpec((1, MIX), lambda bi, i: (0, 0))
    return pl.pallas_call(
        _s5_out_kernel,
        grid=(b, s // tm),
        in_specs=[yspec, yspec,
                  pl.BlockSpec((None, tm, MIX), lambda bi, i: (bi, i, Z_D // MIX)),
                  vspec, pl.BlockSpec((MIX, MIX), lambda bi, i: (0, 0)), vspec],
        out_specs=yspec,
        out_shape=jax.ShapeDtypeStruct((b, s, MIX), F32),
        compiler_params=_cparams(("parallel", "parallel")),
        name="s5_output",
    )(yf, yb, z, d_skip.reshape(1, MIX), glu_w.astype(BF16), glu_b.reshape(1, MIX))


def s5_bidir(z, a_re, a_im, log_dt, b_re, b_im, c_re, c_im, d_skip, glu_w, glu_b, *, tc=256, tm=512):
    params = (a_re, a_im, log_dt, b_re, b_im, c_re, c_im)
    yf = s5_scan(z, 0, params, tc=tc)
    yb = s5_scan(z, 1, params, tc=tc)
    return s5_output(yf, yb, z, d_skip, glu_w, glu_b, tm=tm)


def _softplus(y):
    return jnp.maximum(y, 0.0) + jnp.log(1.0 + jnp.exp(-jnp.abs(y)))


def _head_ones(width):
    seg = np.arange(width) // B_HDIM
    return jnp.asarray(seg[:, None] == seg[None, :], BF16)


def _rwkv_pre_kernel(z_ref, zp_ref, zn_ref, mu_ref, w0_ref, a0_ref, w2_ref, a2_ref, g2_ref,
                     kk_ref, ka_ref, rk_ref, ones_ref,
                     r_o, a_o, w0_o, w1_o, k0_o, k1_o, b0_o, b1_o, v_o, g_o, bonus_o,
                     *, tm, n_tiles):
    i = pl.program_id(1)
    z = z_ref[...]
    row = lax.broadcasted_iota(jnp.int32, z.shape, 0)
    prev_row = jnp.where(i > 0, zp_ref[7:8, :], 0.0)
    next_row = jnp.where(i < n_tiles - 1, zn_ref[0:1, :], 0.0)
    zp = jnp.where(row == 0, prev_row, pltpu.roll(z, 1, axis=0))
    zn = jnp.where(row == tm - 1, next_row, pltpu.roll(z, tm - 1, axis=0))
    xs = z + mu_ref[0:1, :] * (zp - z) + mu_ref[1:2, :] * (zn - z)
    r = xs[:, 0:MIX]
    k = xs[:, MIX:2 * MIX]
    v = xs[:, 2 * MIX:3 * MIX]
    c0 = 3 * MIX
    wd = xs[:, c0:c0 + 2 * B_LORA]
    ad = xs[:, c0 + 2 * B_LORA:c0 + 4 * B_LORA]
    gd = xs[:, c0 + 4 * B_LORA:c0 + 4 * B_LORA + B_GATE_LORA]
    lw = _dot(jnp.tanh(wd).astype(BF16), w2_ref[...])
    la = _dot(ad.astype(BF16), a2_ref[...])
    g_o[...] = _dot(jax.nn.sigmoid(gd).astype(BF16), g2_ref[...])
    ones = ones_ref[...]
    kk = k * kk_ref[...]
    kk = kk * lax.rsqrt(_segsum(kk * kk, ones) + 1e-12)
    ka = ka_ref[...]
    ksum = jnp.zeros_like(k)
    for d, (w_o, k_o, b_o) in enumerate(((w0_o, k0_o, b0_o), (w1_o, k1_o, b1_o))):
        cols = slice(d * MIX, (d + 1) * MIX)
        logw = -_softplus(-(w0_ref[d:d + 1, :] + lw[:, cols])) - 0.5
        w_o[...] = -jnp.exp(logw)
        a = jax.nn.sigmoid(a0_ref[d:d + 1, :] + la[:, cols])
        kmod = k * (1.0 + (a - 1.0) * ka)
        k_o[...] = kmod
        b_o[...] = kk * a
        ksum = ksum + kmod
    r_o[...] = r
    a_o[...] = -kk
    bonus_o[...] = _segsum(r * (0.5 * ksum) * rk_ref[...], ones) * v
    v_o[...] = v


def _block_diag2(m):
    z = jnp.zeros_like(m[0])
    return jnp.concatenate([jnp.concatenate([m[0], z], axis=1),
                            jnp.concatenate([z, m[1]], axis=1)], axis=0)


def rwkv_pre(z, mu, w0, w2, a0, a2, g2, k_k, k_a, r_k, *, tm):
    b, s, _ = z.shape
    n_tiles = s // tm
    mu_p = jnp.pad(mu.astype(F32), ((0, 0), (0, IN_B_PAD - IN_B)))
    row_spec = pl.BlockSpec((None, tm, MIX), lambda bi, i: (bi, i, 0))
    vec = lambda n, w: pl.BlockSpec((n, w), lambda bi, i: (0, 0))
    rows = jax.ShapeDtypeStruct((b, s, MIX), F32)
    hb = tm // 8
    return pl.pallas_call(
        functools.partial(_rwkv_pre_kernel, tm=tm, n_tiles=n_tiles),
        grid=(b, n_tiles),
        in_specs=[pl.BlockSpec((None, tm, IN_B_PAD), lambda bi, i: (bi, i, 0)),
                  pl.BlockSpec((None, 8, IN_B_PAD), lambda bi, i: (bi, jnp.maximum(i * hb - 1, 0), 0)),
                  pl.BlockSpec((None, 8, IN_B_PAD), lambda bi, i: (bi, jnp.minimum((i + 1) * hb, s // 8 - 1), 0)),
                  vec(2, IN_B_PAD), vec(2, MIX), vec(2, MIX),
                  vec(2 * B_LORA, 2 * MIX), vec(2 * B_LORA, 2 * MIX), vec(B_GATE_LORA, MIX),
                  vec(1, MIX), vec(1, MIX), vec(1, MIX), vec(MIX, MIX)],
        out_specs=[row_spec] * 11,
        out_shape=[rows] * 11,
        compiler_params=_cparams(("parallel", "parallel")),
        name="rwkv_pre",
    )(z, z, z, mu_p, w0.astype(F32), a0.astype(F32),
      _block_diag2(w2).astype(BF16), _block_diag2(a2).astype(BF16), g2.astype(BF16),
      k_k.reshape(1, MIX), k_a.reshape(1, MIX), r_k.reshape(1, MIX), _head_ones(MIX))


def _rwkv_scan_kernel(a_ref, r_ref, lw_ref, k_ref, b_ref, v_ref, tri_ref, y_ref, h_ref, *, tc, reverse):
    c = pl.program_id(1)

    @pl.when(c == 0)
    def _():
        h_ref[...] = jnp.zeros_like(h_ref)

    lw = lw_ref[...]
    tri = tri_ref[...]
    hi, mid, lo = _split3(lw)
    cl = _dot(tri, hi) + _dot(tri, mid) + _dot(tri, lo)
    cle = cl - lw
    last = 0 if reverse else tc - 1
    tot = cl[last:last + 1, :]
    einv = jnp.exp(-cl)
    etot = jnp.exp(tot - cl)
    a_t = a_ref[...] * jnp.exp(cle)
    r_t = r_ref[...] * jnp.exp(cl)
    b_all = b_ref[...]
    k_all = k_ref[...]
    b_t = b_all * einv
    k_t = k_all * einv
    b_h = b_all * etot
    k_h = k_all * etot
    g_tot = jnp.exp(tot)
    v_all = v_ref[...]

    row = lax.broadcasted_iota(jnp.int32, (tc, tc), 0)
    col = lax.broadcasted_iota(jnp.int32, (tc, tc), 1)
    eye = (row == col).astype(F32)
    row2 = lax.broadcasted_iota(jnp.int32, (tc, 2 * tc), 0)
    col2 = lax.broadcasted_iota(jnp.int32, (tc, 2 * tc), 1) & (tc - 1)
    before = (col2 > row2) if reverse else (col2 < row2)
    upto = (col2 >= row2) if reverse else (col2 <= row2)
    zeros_tv = jnp.zeros((tc, B_HDIM), BF16)
    n_sq = tc.bit_length() - 1

    heads = range(B_HEADS)
    hsl = [slice(hd * B_HDIM, (hd + 1) * B_HDIM) for hd in heads]
    tdot = lambda x, y: lax.dot_general(x, y, (((0,), (0,)), ((), ())), preferred_element_type=F32)
    at = [a_t[:, s].astype(BF16) for s in hsl]
    rt = [r_t[:, s].astype(BF16) for s in hsl]
    bk = [jnp.concatenate([b_t[:, s], k_t[:, s]], axis=0).astype(BF16) for s in hsl]
    bkh = [jnp.concatenate([b_h[:, s], k_h[:, s]], axis=0).astype(BF16) for s in hsl]
    vv = [v_all[:, s].astype(BF16) for s in hsl]
    ga = [jnp.where(before, _dot_t(at[h], bk[h]), 0.0) for h in heads]
    gr = [jnp.where(upto, _dot_t(rt[h], bk[h]), 0.0).astype(BF16) for h in heads]
    lkv = [_dot(ga[h].astype(BF16), jnp.concatenate([zeros_tv, vv[h]], axis=0)) for h in heads]
    lp = [ga[h][:, :tc] for h in heads]
    pinv = [eye + lp[h] for h in heads]
    for _ in range(n_sq - 1):
        lpb = [x.astype(BF16) for x in lp]
        lp = [_dot(x, x) for x in lpb]
        pinv = [pinv[h] + _dot(lp[h].astype(BF16), pinv[h].astype(BF16)) for h in heads]
    pinv = [x.astype(BF16) for x in pinv]
    ah = [_dot(pinv[h], at[h]).astype(BF16) for h in heads]
    u0 = [_dot(pinv[h], lkv[h].astype(BF16)).astype(BF16) for h in heads]
    uv = [jnp.concatenate([u0[h], vv[h]], axis=0) for h in heads]
    rh = [r_t[:, hsl[h]] + _dot(gr[h], jnp.concatenate([ah[h], zeros_tv], axis=0)) for h in heads]
    y0 = [_dot(gr[h], uv[h]) for h in heads]
    phi = [eye * g_tot[:, hsl[h]] + tdot(bkh[h][:tc], ah[h]) for h in heads]
    hadd = [tdot(bkh[h], uv[h]) for h in heads]
    h0 = [h_ref[h].astype(BF16) for h in heads]
    y_ref[...] = jnp.concatenate([_dot(rh[h].astype(BF16), h0[h]) + y0[h] for h in heads], axis=1)
    for h in heads:
        h_ref[h] = _dot(phi[h].astype(BF16), h0[h]) + hadd[h]


def rwkv_scan(a, r, lw, k, bvec, v, *, reverse, tc=64):
    nb, s, _ = a.shape
    nch = s // tc
    tmap = (lambda c: nch - 1 - c) if reverse else (lambda c: c)
    row_spec = pl.BlockSpec((None, tc, MIX), lambda bi, c: (bi, tmap(c), 0))
    t_idx = np.arange(tc)
    tri = (t_idx[None, :] >= t_idx[:, None]) if reverse else (t_idx[None, :] <= t_idx[:, None])
    return pl.pallas_call(
        functools.partial(_rwkv_scan_kernel, tc=tc, reverse=reverse),
        grid=(nb, nch),
        in_specs=[row_spec] * 6 + [pl.BlockSpec((tc, tc), lambda bi, c: (0, 0))],
        out_specs=row_spec,
        out_shape=jax.ShapeDtypeStruct((nb, s, MIX), F32),
        scratch_shapes=[pltpu.VMEM((B_HEADS, B_HDIM, B_HDIM), F32)],
        compiler_params=_cparams(("parallel", "arbitrary")),
        name="rwkv_scan",
    )(a, r, lw, k, bvec, v, jnp.asarray(tri, BF16))


def _rwkv_post_kernel(yf_ref, yb_ref, bonus_ref, g_ref, lw_ref, lb_ref, ones_ref, o_ref):
    y = yf_ref[...] + yb_ref[...]
    ones = ones_ref[...]
    mean = _segsum(y, ones) * (1.0 / B_HDIM)
    yc = y - mean
    var = _segsum(yc * yc, ones) * (1.0 / B_HDIM)
    yn = yc * lax.rsqrt(var + LN_X_EPS) * lw_ref[...] + lb_ref[...]
    o_ref[...] = (yn + bonus_ref[...]) * g_ref[...]


def rwkv_post(yf, yb, bonus, g, lnx_w, lnx_b, *, tm):
    b, s, _ = yf.shape
    row_spec = pl.BlockSpec((None, tm, MIX), lambda bi, i: (bi, i, 0))
    vec = pl.BlockSpec((1, MIX), lambda bi, i: (0, 0))
    return pl.pallas_call(
        _rwkv_post_kernel,
        grid=(b, s // tm),
        in_specs=[row_spec, row_spec, row_spec, row_spec, vec, vec,
                  pl.BlockSpec((MIX, MIX), lambda bi, i: (0, 0))],
        out_specs=row_spec,
        out_shape=jax.ShapeDtypeStruct((b, s, MIX), F32),
        compiler_params=_cparams(("parallel", "parallel")),
        name="rwkv_post",
    )(yf, yb, bonus, g, lnx_w.reshape(1, MIX), lnx_b.reshape(1, MIX), _head_ones(MIX))


def rwkv7_bidir(z, mu, w0, w2, a0, a2, g2, k_k, k_a, r_k, lnx_w, lnx_b, *, tm=256, tc=64):
    r, a, wf, wb, kf, kb, bf, bb, v, g, bonus = rwkv_pre(z, mu, w0, w2, a0, a2, g2, k_k, k_a, r_k, tm=tm)
    yf = rwkv_scan(a, r, wf, kf, bf, v, reverse=False, tc=tc)
    yb = rwkv_scan(a, r, wb, kb, bb, v, reverse=True, tc=tc)
    return rwkv_post(yf, yb, bonus, g, lnx_w, lnx_b, tm=tm)


def _mix_out_kernel(h_ref, oa_ref, ob_ref, oc_ref, od_ref, gc_ref, gd_ref, w_ref, o_ref, mix_ref):
    @pl.when(pl.program_id(1) == 0)
    def _():
        mix_ref[...] = jnp.concatenate(
            [oa_ref[...], ob_ref[...], _rms(oc_ref[...], gc_ref[...]), _rms(od_ref[...], gd_ref[...])],
            axis=1).astype(BF16)

    o_ref[...] = h_ref[...] + _dot(mix_ref[...], w_ref[...])


def mix_out(h, oa, ob, oc, od, gc, gd, w_out, *, tm, tn):
    m, d = h.shape
    mspec = pl.BlockSpec((tm, MIX), lambda i, j: (i, 0))
    vspec = pl.BlockSpec((1, MIX), lambda i, j: (0, 0))
    hspec = pl.BlockSpec((tm, tn), lambda i, j: (i, j))
    return pl.pallas_call(
        _mix_out_kernel,
        grid=(m // tm, d // tn),
        in_specs=[hspec, mspec, mspec, mspec, mspec, vspec, vspec,
                  pl.BlockSpec((4 * MIX, tn), lambda i, j: (0, j))],
        out_specs=hspec,
        out_shape=jax.ShapeDtypeStruct((m, d), F32),
        scratch_shapes=[pltpu.VMEM((tm, 4 * MIX), BF16)],
        compiler_params=_cparams(("parallel", "arbitrary")),
        name="mix_out",
    )(h, oa, ob, oc, od, gc.reshape(1, MIX), gd.reshape(1, MIX), w_out)


def _cross_attn_kernel(h_ref, q_ref, kv_ref, wo_ref, o_ref):
    q = q_ref[...]
    outs = []
    for hd in range(XA_HEADS):
        cols = slice(hd * XA_HDIM, (hd + 1) * XA_HDIM)
        kh = kv_ref[:, cols]
        vh = kv_ref[:, D_MODEL + hd * XA_HDIM:D_MODEL + (hd + 1) * XA_HDIM]
        s = _dot_t(q[:, cols], kh) * (XA_HDIM ** -0.5)
        e = jnp.exp(s - jnp.max(s, axis=-1, keepdims=True))
        p = e / jnp.sum(e, axis=-1, keepdims=True)
        outs.append(_dot(p.astype(BF16), vh).astype(BF16))
    o = jnp.concatenate(outs, axis=1)
    o_ref[...] = h_ref[...] + _dot(o, wo_ref[...])


def cross_attention(h, q, kv, wo, *, tq):
    b, s, d = h.shape
    n_mem = kv.shape[1]
    hspec = pl.BlockSpec((None, tq, d), lambda bi, i: (bi, i, 0))
    return pl.pallas_call(
        _cross_attn_kernel,
        grid=(b, s // tq),
        in_specs=[hspec, hspec,
                  pl.BlockSpec((None, n_mem, 2 * d), lambda bi, i: (bi, 0, 0)),
                  pl.BlockSpec((d, d), lambda bi, i: (0, 0))],
        out_specs=hspec,
        out_shape=jax.ShapeDtypeStruct((b, s, d), F32),
        compiler_params=_cparams(("parallel", "parallel")),
        name="cross_attention",
    )(h, q, kv, wo)


ROUTER_W = LANES
MOE_TB = 256


def _router_kernel(h_ref, g_ref, wr_ref, xn_ref, eid_ref, gate_ref):
    xn = _rms(h_ref[...], g_ref[...])
    xn_ref[...] = xn
    logits = jnp.dot(xn, wr_ref[...], preferred_element_type=F32, precision=lax.Precision.HIGHEST)
    lane = lax.broadcasted_iota(jnp.int32, logits.shape, 1)
    big = jnp.int32(ROUTER_W)

    def first_max(mask):
        m = jnp.max(jnp.where(mask, logits, NEG_INF), axis=-1, keepdims=True)
        idx = jnp.min(jnp.where(mask & (logits == m), lane, big), axis=-1, keepdims=True)
        return m, idx

    gmask = lane < N_GROUPS
    gmax, gidx = first_max(gmask)
    g_w = 1.0 / jnp.sum(jnp.where(gmask, jnp.exp(logits - gmax), 0.0), axis=-1, keepdims=True)
    e0 = N_GROUPS + gidx * EXPERTS_PER_GROUP
    emask = (lane >= e0) & (lane < e0 + EXPERTS_PER_GROUP)
    m1, i1 = first_max(emask)
    m2, i2 = first_max(emask & (lane != i1))
    e2 = jnp.exp(m2 - m1)
    w1 = 1.0 / (1.0 + e2)
    w2 = e2 / (1.0 + e2)
    eid_ref[...] = jnp.where(lane == 0, i1 - N_GROUPS, jnp.where(lane == 1, i2 - N_GROUPS, 0))
    gate_ref[...] = jnp.where(lane == 0, g_w * w1, jnp.where(lane == 1, g_w * w2, 0.0))


def moe_router(h, g, wr_group, wr_expert, *, tm):
    m, d = h.shape
    wr = jnp.concatenate([wr_group, wr_expert], axis=1).astype(F32)
    wr = jnp.pad(wr, ((0, 0), (0, ROUTER_W - wr.shape[1])))
    hspec = pl.BlockSpec((tm, d), lambda i: (i, 0))
    lspec = pl.BlockSpec((tm, ROUTER_W), lambda i: (i, 0))
    return pl.pallas_call(
        _router_kernel,
        grid=(m // tm,),
        in_specs=[hspec, pl.BlockSpec((1, d), lambda i: (0, 0)),
                  pl.BlockSpec((d, ROUTER_W), lambda i: (0, 0))],
        out_specs=[hspec, lspec, lspec],
        out_shape=[jax.ShapeDtypeStruct((m, d), F32),
                   jax.ShapeDtypeStruct((m, ROUTER_W), jnp.int32),
                   jax.ShapeDtypeStruct((m, ROUTER_W), F32)],
        compiler_params=_cparams(("parallel",)),
        name="moe_router",
    )(h, g.reshape(1, d), wr)


def _moe_expert_kernel(tok_ref, exp_ref, nvb_ref, x_hbm, gate_ref, w1_ref, w3_ref, w2_ref, o_ref,
                       xbuf, sem, *, tb):
    i = pl.program_id(0)

    def row_copy(r, tok):
        return pltpu.make_async_copy(x_hbm.at[pl.ds(tok, 1)], xbuf.at[pl.ds(r, 1)], sem)

    @pl.when(i < nvb_ref[0])
    def _():
        def issue(r, c):
            row_copy(r, tok_ref[i * tb + r]).start()
            return c

        lax.fori_loop(0, tb, issue, 0)

        def drain(r, c):
            row_copy(r, 0).wait()
            return c

        lax.fori_loop(0, tb, drain, 0)
        x = xbuf[...].astype(BF16)
        h1 = _dot(x, w1_ref[...].astype(BF16))
        h3 = _dot(x, w3_ref[...].astype(BF16))
        act = (jax.nn.silu(h1) * h3).astype(BF16)
        o_ref[...] = _dot(act, w2_ref[...].astype(BF16)) * gate_ref[...]

    @pl.when(i >= nvb_ref[0])
    def _():
        o_ref[...] = jnp.zeros_like(o_ref)


def moe_experts(xn, slot_tok, slot_gate, blk_exp, n_valid, w1, w3, w2, *, tb):
    cap = slot_tok.shape[0]
    d = xn.shape[1]
    nblk = cap // tb
    wspec = lambda shape: pl.BlockSpec((None,) + shape, lambda i, tok, ex, nv: (ex[i], 0, 0))
    grid_spec = pltpu.PrefetchScalarGridSpec(
        num_scalar_prefetch=3,
        grid=(nblk,),
        in_specs=[pl.BlockSpec(memory_space=pl.ANY),
                  pl.BlockSpec((tb, 1), lambda i, tok, ex, nv: (i, 0)),
                  wspec((d, D_EXPERT)), wspec((d, D_EXPERT)), wspec((D_EXPERT, d))],
        out_specs=pl.BlockSpec((tb, d), lambda i, tok, ex, nv: (i, 0)),
        scratch_shapes=[pltpu.VMEM((tb, d), F32), pltpu.SemaphoreType.DMA(())],
    )
    return pl.pallas_call(
        functools.partial(_moe_expert_kernel, tb=tb),
        grid_spec=grid_spec,
        out_shape=jax.ShapeDtypeStruct((cap, d), F32),
        compiler_params=_cparams(("arbitrary",)),
        name="moe_experts",
    )(slot_tok, blk_exp, n_valid, xn, slot_gate.reshape(cap, 1), w1, w3, w2)


def _moe_combine_kernel(pos_ref, ys_hbm, h_ref, o_ref, buf, sem, *, tm):
    i = pl.program_id(0)

    def row_copy(r, src):
        return pltpu.make_async_copy(ys_hbm.at[pl.ds(src, 1)], buf.at[pl.ds(r, 1)], sem)

    def issue(r, c):
        row_copy(r, pos_ref[i * 2 * tm + r]).start()
        return c

    lax.fori_loop(0, 2 * tm, issue, 0)

    def drain(r, c):
        row_copy(r, 0).wait()
        return c

    lax.fori_loop(0, 2 * tm, drain, 0)
    o_ref[...] = h_ref[...] + buf[0:tm, :] + buf[tm:2 * tm, :]


def moe_combine(h, ys, pos, *, tm):
    m, d = h.shape
    hspec = pl.BlockSpec((tm, d), lambda i, p: (i, 0))
    grid_spec = pltpu.PrefetchScalarGridSpec(
        num_scalar_prefetch=1,
        grid=(m // tm,),
        in_specs=[pl.BlockSpec(memory_space=pl.ANY), hspec],
        out_specs=hspec,
        scratch_shapes=[pltpu.VMEM((2 * tm, d), F32), pltpu.SemaphoreType.DMA(())],
    )
    return pl.pallas_call(
        functools.partial(_moe_combine_kernel, tm=tm),
        grid_spec=grid_spec,
        out_shape=jax.ShapeDtypeStruct((m, d), F32),
        compiler_params=_cparams(("arbitrary",)),
        name="moe_combine",
    )(pos, ys, h)


def _moe_slots(eid, gate, tb):
    t = eid.shape[0]
    n_assign = t * TOP_K
    eid = eid.reshape(-1)
    gate = gate.reshape(-1)
    tok = jnp.repeat(jnp.arange(t, dtype=jnp.int32), TOP_K)
    order = jnp.argsort(eid)
    se = eid[order]
    counts = jnp.bincount(eid, length=N_EXPERTS)
    padded = ((counts + tb - 1) // tb) * tb
    pad_end = jnp.cumsum(padded)
    pad_start = pad_end - padded
    start = jnp.cumsum(counts) - counts
    dest = (pad_start[se] + (jnp.arange(n_assign, dtype=jnp.int32) - start[se])).astype(jnp.int32)
    cap = n_assign + N_EXPERTS * tb
    nblk = cap // tb
    slot_tok = jnp.zeros((cap,), jnp.int32).at[dest].set(tok[order])
    slot_gate = jnp.zeros((cap,), F32).at[dest].set(gate[order])
    n_valid = (pad_end[-1] // tb).astype(jnp.int32)
    blk = jnp.minimum(jnp.arange(nblk, dtype=jnp.int32), n_valid - 1) * tb
    blk_exp = jnp.minimum(jnp.searchsorted(pad_end, blk, side='right'), N_EXPERTS - 1).astype(jnp.int32)
    pos = jnp.zeros((n_assign,), jnp.int32).at[order].set(dest).reshape(t, TOP_K)
    return slot_tok, slot_gate, blk_exp, n_valid.reshape(1), pos


def hier_moe(h, g, wr_group, wr_expert, w1, w3, w2, *, tm_router=512, tm_combine=128, tb=MOE_TB):
    m, d = h.shape
    xn, eid, gate = moe_router(h, g, wr_group, wr_expert, tm=tm_router)
    slot_tok, slot_gate, blk_exp, n_valid, pos = _moe_slots(eid[:, :TOP_K], gate[:, :TOP_K], tb)
    ys = moe_experts(xn, slot_tok, slot_gate, blk_exp, n_valid, w1, w3, w2, tb=tb)
    pos = pos.reshape(m // tm_combine, tm_combine, TOP_K).transpose(0, 2, 1).reshape(-1)
    return moe_combine(h, ys, pos, tm=tm_combine)


def _final_norm_kernel(h_ref, g_ref, o_ref):
    o_ref[...] = _rms(h_ref[...], g_ref[...])


def final_norm(h, g, *, tm):
    m, d = h.shape
    hspec = pl.BlockSpec((tm, d), lambda i: (i, 0))
    return pl.pallas_call(
        _final_norm_kernel,
        grid=(m // tm,),
        in_specs=[hspec, pl.BlockSpec((1, d), lambda i: (0, 0))],
        out_specs=hspec,
        out_shape=jax.ShapeDtypeStruct((m, d), F32),
        compiler_params=_cparams(("parallel",)),
        name="final_norm",
    )(h, g.reshape(1, d))


def _w_in_layout(w_in):
    wa = w_in[:, :IN_A]
    wb = w_in[:, IN_A:IN_A + IN_B]
    wc = w_in[:, IN_A + IN_B:IN_A + IN_B + IN_C]
    wd = w_in[:, IN_A + IN_B + IN_C:]
    wb = jnp.pad(wb, ((0, 0), (0, IN_B_PAD - IN_B)))
    return jnp.concatenate([wb, wa, wc, wd], axis=1).astype(BF16)


def kernel(x, mem, norm_mix, w_in, w_out, diff_lambda, diff_subln, rwkv_mu, rwkv_w0, rwkv_w2,
           rwkv_a0, rwkv_a2, rwkv_g2, rwkv_kk, rwkv_ka, rwkv_rk, rwkv_lnx_w, rwkv_lnx_b,
           s5_a_re, s5_a_im, s5_log_dt, s5_b_re, s5_b_im, s5_c_re, s5_c_im, s5_d, s5_glu_w,
           s5_glu_b, mix_out_norm, norm_cross, norm_mem, xa_wq, xa_wkv, xa_wo, norm_moe,
           router_group, router_expert, moe_w1, moe_w3, moe_w2, norm_final):
    b, s, d = x.shape
    m = b * s
    n_mem = mem.shape[1]
    depth = w_in.shape[0]
    h = x.reshape(m, d)
    mem2 = mem.reshape(b * n_mem, d)
    for l in range(depth):
        z = norm_matmul(h, norm_mix[l], _w_in_layout(w_in[l]), tm=1024, tn=512, out_dtype=F32)
        z = z.reshape(b, s, Z_W)
        qa, ka, va = rope_qkv(z, Z_A, A_QKDIM, tm=512)
        oa = diff_attention(qa, ka, va, diff_lambda[l], diff_subln[l], l, tq=128)
        ob = rwkv7_bidir(z, rwkv_mu[l], rwkv_w0[l], rwkv_w2[l], rwkv_a0[l], rwkv_a2[l], rwkv_g2[l],
                         rwkv_kk[l], rwkv_ka[l], rwkv_rk[l], rwkv_lnx_w[l], rwkv_lnx_b[l])
        qc, kc, vc = rope_qkv(z, Z_C, C_HDIM, tm=512)
        oc = dilated_attention(qc, kc, vc)
        od = s5_bidir(z, s5_a_re[l], s5_a_im[l], s5_log_dt[l], s5_b_re[l], s5_b_im[l],
                      s5_c_re[l], s5_c_im[l], s5_d[l], s5_glu_w[l], s5_glu_b[l])
        h = mix_out(h, oa.reshape(m, MIX), ob.reshape(m, MIX), oc.reshape(m, MIX), od.reshape(m, MIX),
                    mix_out_norm[l, 0], mix_out_norm[l, 1], w_out[l].astype(BF16), tm=512, tn=512)
        q = norm_matmul(h, norm_cross[l], xa_wq[l].astype(BF16), tm=1024, tn=512, out_dtype=BF16)
        kv = norm_matmul(mem2, norm_mem[l], xa_wkv[l].astype(BF16), tm=b * n_mem, tn=512, out_dtype=BF16)
        h = cross_attention(h.reshape(b, s, d), q.reshape(b, s, d), kv.reshape(b, n_mem, 2 * d),
                            xa_wo[l].astype(BF16), tq=256).reshape(m, d)
        h = hier_moe(h, norm_moe[l], router_group[l], router_expert[l], moe_w1[l], moe_w3[l], moe_w2[l])
    return final_norm(h, norm_final, tm=512).reshape(b, s, d)
```

```python
import functools
import math

import numpy as np
import jax
import jax.numpy as jnp
from jax import lax
from jax.experimental import pallas as pl
from jax.experimental.pallas import tpu as pltpu

F32 = jnp.float32
BF16 = jnp.bfloat16

D_MODEL = 2048
MIX = D_MODEL // 4
A_HEADS = 4
A_VDIM = MIX // A_HEADS
A_QKDIM = A_VDIM // 2
B_HDIM = 64
B_HEADS = MIX // B_HDIM
B_LORA = 64
B_GATE_LORA = 128
LN_X_EPS = 64e-5
C_HEADS = 4
C_HDIM = MIX // C_HEADS
C_PATTERNS = ((128, 1), (512, 4), (2048, 16))
D_GSIZE = 16
D_GROUPS = MIX // D_GSIZE
D_STATE = 64
IN_A = 3 * MIX
IN_B = 3 * MIX + 4 * B_LORA + B_GATE_LORA
IN_B_PAD = 2048
IN_C = 3 * MIX
IN_D = MIX
XA_HEADS = 4
XA_HDIM = D_MODEL // XA_HEADS
N_GROUPS = 4
EXPERTS_PER_GROUP = 8
N_EXPERTS = N_GROUPS * EXPERTS_PER_GROUP
TOP_K = 2
D_EXPERT = D_MODEL // 4
ROPE_THETA = 10000.0
RMS_EPS = 1e-6
NEG_INF = -1e30

Z_B = 0
Z_A = IN_B_PAD
Z_C = Z_A + IN_A
Z_D = Z_C + IN_C
Z_W = Z_D + IN_D

LANES = 128
VMEM_LIMIT = 56 * 1024 * 1024


def _cparams(sem, vmem=VMEM_LIMIT):
    return pltpu.CompilerParams(dimension_semantics=sem, vmem_limit_bytes=vmem)


def _rms(x, g):
    return x * lax.rsqrt(jnp.mean(x * x, axis=-1, keepdims=True) + RMS_EPS) * g


def _dot(a, b):
    return jnp.dot(a, b, preferred_element_type=F32)


def _dot_t(a, b):
    return lax.dot_general(a, b, (((1,), (1,)), ((), ())), preferred_element_type=F32)


def _split3(x):
    hi = x.astype(BF16)
    r1 = x - hi.astype(F32)
    mid = r1.astype(BF16)
    lo = (r1 - mid.astype(F32)).astype(BF16)
    return hi, mid, lo


def _segsum(x, ones):
    hi, mid, lo = _split3(x)
    return _dot(hi, ones) + _dot(mid, ones) + _dot(lo, ones)


def _norm_matmul_kernel(x_ref, g_ref, w_ref, o_ref, xn_ref):
    @pl.when(pl.program_id(1) == 0)
    def _():
        xn_ref[...] = _rms(x_ref[...], g_ref[...]).astype(BF16)

    o_ref[...] = _dot(xn_ref[...], w_ref[...]).astype(o_ref.dtype)


def norm_matmul(x, g, w, *, tm, tn, out_dtype):
    m, k = x.shape
    n = w.shape[1]
    return pl.pallas_call(
        _norm_matmul_kernel,
        grid=(m // tm, n // tn),
        in_specs=[pl.BlockSpec((tm, k), lambda i, j: (i, 0)),
                  pl.BlockSpec((1, k), lambda i, j: (0, 0)),
                  pl.BlockSpec((k, tn), lambda i, j: (0, j))],
        out_specs=pl.BlockSpec((tm, tn), lambda i, j: (i, j)),
        out_shape=jax.ShapeDtypeStruct((m, n), out_dtype),
        scratch_shapes=[pltpu.VMEM((tm, k), BF16)],
        compiler_params=_cparams(("parallel", "arbitrary")),
        name="norm_matmul",
    )(x, g.reshape(1, k), w)


def _rope_tables(seq, dim, width):
    inv = 1.0 / (ROPE_THETA ** (jnp.arange(0, dim, 2, dtype=F32) / dim))
    ang = jnp.arange(seq, dtype=F32)[:, None] * inv[None, :]
    cos, sin = jnp.cos(ang), jnp.sin(ang)
    cos = jnp.concatenate([cos, cos], axis=-1)
    sin = jnp.concatenate([-sin, sin], axis=-1)
    reps = width // dim
    return jnp.tile(cos, (1, reps)), jnp.tile(sin, (1, reps))


def _rope_kernel(q_ref, k_ref, v_ref, cos_ref, sin_ref, qo_ref, ko_ref, vo_ref, *, half, scale, ones_cols):
    cos = cos_ref[...]
    sin = sin_ref[...]
    width = cos.shape[1]
    lane = lax.broadcasted_iota(jnp.int32, cos.shape, 1)
    first = (lane % (2 * half)) < half

    def rot(x):
        ahead = pltpu.roll(x, width - half, axis=1)
        behind = pltpu.roll(x, half, axis=1)
        return x * cos + jnp.where(first, ahead, behind) * sin

    qo_ref[...] = (rot(q_ref[...]) * scale).astype(BF16)
    ko_ref[...] = rot(k_ref[...]).astype(BF16)
    v = v_ref[...].astype(BF16)
    if ones_cols is None:
        vo_ref[...] = v
    else:
        ones = jnp.ones((v.shape[0], ones_cols), BF16)
        parts = []
        for hd in range(v.shape[1] // ones_cols):
            parts += [v[:, hd * ones_cols:(hd + 1) * ones_cols], ones]
        vo_ref[...] = jnp.concatenate(parts, axis=1)


def rope_qkv(z, col0, head_dim, *, tm, ones_cols=None, tables=None):
    b, s, _ = z.shape
    cos, sin = tables if tables is not None else _rope_tables(s, head_dim, MIX)
    cb = col0 // MIX
    zspec = lambda off: pl.BlockSpec((None, tm, MIX), lambda bi, i, off=off: (bi, i, cb + off))
    tspec = pl.BlockSpec((tm, MIX), lambda bi, i: (i, 0))
    ospec = pl.BlockSpec((None, tm, MIX), lambda bi, i: (bi, i, 0))
    oshape = jax.ShapeDtypeStruct((b, s, MIX), BF16)
    if ones_cols is None:
        vspec, vshape = ospec, oshape
    else:
        vspec = pl.BlockSpec((None, tm, 2 * MIX), lambda bi, i: (bi, i, 0))
        vshape = jax.ShapeDtypeStruct((b, s, 2 * MIX), BF16)
    return pl.pallas_call(
        functools.partial(_rope_kernel, half=head_dim // 2, scale=head_dim ** -0.5, ones_cols=ones_cols),
        grid=(b, s // tm),
        in_specs=[zspec(0), zspec(1), zspec(2), tspec, tspec],
        out_specs=[ospec, ospec, vspec],
        out_shape=[oshape, oshape, vshape],
        compiler_params=_cparams(("parallel", "parallel")),
        name="rope_qkv",
    )(z, z, z, cos, sin)


def _diff_attn_kernel(lam_ref, g_ref, q_ref, k_ref, v1_ref, o_ref, *, lam_init):
    lv = lam_ref[...]
    lam = (jnp.exp(jnp.sum(lv[0:1] * lv[1:2], axis=-1, keepdims=True))
           - jnp.exp(jnp.sum(lv[2:3] * lv[3:4], axis=-1, keepdims=True)) + lam_init)
    q = q_ref[...]
    k = k_ref[...]
    v1 = v1_ref[...]
    lane = lax.broadcasted_iota(jnp.int32, q.shape, 1)
    zero = jnp.zeros_like(q)

    def branch(qm):
        s = _dot_t(qm, k)
        p = jnp.exp((s - jnp.max(s, axis=-1, keepdims=True)).astype(BF16))
        acc = _dot(p, v1)
        return acc[:, :A_VDIM] / acc[:, A_VDIM:]

    o = branch(jnp.where(lane < A_QKDIM, q, zero)) - lam * branch(jnp.where(lane >= A_QKDIM, q, zero))
    o_ref[...] = _rms(o, g_ref[...]) * (1.0 - lam_init)


def diff_attention(q, k, v1, lam_vecs, subln_g, layer_idx, *, tq):
    b, s, _ = q.shape
    lam_init = 0.8 - 0.6 * math.exp(-0.3 * layer_idx)
    qspec = pl.BlockSpec((None, tq, A_VDIM), lambda bi, h, i: (bi, i, h))
    return pl.pallas_call(
        functools.partial(_diff_attn_kernel, lam_init=lam_init),
        grid=(b, A_HEADS, s // tq),
        in_specs=[pl.BlockSpec((4, A_QKDIM), lambda bi, h, i: (0, 0)),
                  pl.BlockSpec((1, A_VDIM), lambda bi, h, i: (0, 0)),
                  qspec,
                  pl.BlockSpec((None, s, A_VDIM), lambda bi, h, i: (bi, 0, h)),
                  pl.BlockSpec((None, s, 2 * A_VDIM), lambda bi, h, i: (bi, 0, h))],
        out_specs=qspec,
        out_shape=jax.ShapeDtypeStruct((b, s, MIX), F32),
        compiler_params=_cparams(("parallel", "parallel", "arbitrary")),
        name="diff_attention",
    )(lam_vecs, subln_g.reshape(1, A_VDIM), q, k, v1)


C_REACH = max(w // 2 for w, _ in C_PATTERNS)


def _dilated_bias_table(tq, window):
    n_delta = (window - tq) // tq + 1
    i = np.arange(tq)[None, :, None]
    j = np.arange(window)[None, None, :]
    n = np.arange(n_delta)[:, None, None]
    d = j - i - n * tq
    count = np.zeros(d.shape, np.int32)
    for w, dil in C_PATTERNS:
        count += ((np.abs(d) <= w // 2) & (d % dil == 0)).astype(np.int32)
    bias = np.where(count > 0, np.log(np.maximum(count, 1)), NEG_INF)
    return jnp.asarray(bias, F32)


def _dilated_attn_kernel(bias_ref, q_ref, k_ref, v_ref, o_ref, *, tq, window, seq):
    start = pl.program_id(2) * tq
    ws = pl.multiple_of(jnp.clip(start - C_REACH, 0, seq - window), tq)
    kw = k_ref[pl.ds(ws, window), :]
    vw = v_ref[pl.ds(ws, window), :]
    s = _dot_t(q_ref[...], kw) + bias_ref[...]
    e = jnp.exp(s - jnp.max(s, axis=-1, keepdims=True))
    den = jnp.sum(e, axis=-1, keepdims=True)
    o_ref[...] = _dot(e.astype(BF16), vw) / den


def dilated_attention(q, k, v, *, tq=128):
    b, s, _ = q.shape
    window = tq + 2 * C_REACH
    assert s >= window and s % tq == 0 and C_REACH % tq == 0
    bias = _dilated_bias_table(tq, window)

    def bias_map(bi, h, i):
        start = i * tq
        ws = jnp.clip(start - C_REACH, 0, s - window)
        return ((start - ws) // tq, 0, 0)

    qspec = pl.BlockSpec((None, tq, C_HDIM), lambda bi, h, i: (bi, i, h))
    kspec = pl.BlockSpec((None, s, C_HDIM), lambda bi, h, i: (bi, 0, h))
    return pl.pallas_call(
        functools.partial(_dilated_attn_kernel, tq=tq, window=window, seq=s),
        grid=(b, C_HEADS, s // tq),
        in_specs=[pl.BlockSpec((None, tq, window), bias_map), qspec, kspec, kspec],
        out_specs=qspec,
        out_shape=jax.ShapeDtypeStruct((b, s, MIX), F32),
        compiler_params=_cparams(("parallel", "parallel", "arbitrary")),
        name="dilated_attention",
    )(bias, q, k, v)


S5_BLOCKS = 4
S5_GPB = D_GROUPS // S5_BLOCKS
S5_CH = S5_GPB * D_GSIZE
S5_ST = S5_GPB * D_STATE
S5_ROWS = 8


def _s5_scan_kernel(u_ref, bm_ref, cm_ref, pr_ref, pi_ref, y_ref, cr_ref, ci_ref, *, tc, reverse):
    @pl.when(pl.program_id(2) == 0)
    def _():
        cr_ref[...] = jnp.zeros_like(cr_ref)
        ci_ref[...] = jnp.zeros_like(ci_ref)

    bu = _dot(u_ref[...].astype(BF16), bm_ref[...])
    xr = bu[:, :S5_ST]
    xi = bu[:, S5_ST:]
    row = lax.broadcasted_iota(jnp.int32, xr.shape, 0) & (S5_ROWS - 1)
    for lvl in range(S5_ROWS.bit_length() - 1):
        sh = 1 << lvl
        prow = (S5_ROWS - sh) if reverse else (sh - 1)
        ar = pr_ref[prow:prow + 1, :]
        ai = pi_ref[prow:prow + 1, :]
        if reverse:
            keep = row < (S5_ROWS - sh)
            sr = jnp.where(keep, pltpu.roll(xr, tc - sh, axis=0), 0.0)
            si = jnp.where(keep, pltpu.roll(xi, tc - sh, axis=0), 0.0)
        else:
            keep = row >= sh
            sr = jnp.where(keep, pltpu.roll(xr, sh, axis=0), 0.0)
            si = jnp.where(keep, pltpu.roll(xi, sh, axis=0), 0.0)
        xr, xi = xr + ar * sr - ai * si, xi + ar * si + ai * sr
    pr = pr_ref[...]
    pi = pi_ref[...]
    cr = cr_ref[...]
    ci = ci_ref[...]
    n_groups = tc // S5_ROWS
    last = 0 if reverse else S5_ROWS - 1
    out_r = [None] * n_groups
    out_i = [None] * n_groups
    for gi in (range(n_groups - 1, -1, -1) if reverse else range(n_groups)):
        rows = slice(gi * S5_ROWS, (gi + 1) * S5_ROWS)
        gr = xr[rows] + pr * cr - pi * ci
        gim = xi[rows] + pr * ci + pi * cr
        cr = gr[last:last + 1, :]
        ci = gim[last:last + 1, :]
        out_r[gi] = gr
        out_i[gi] = gim
    cr_ref[...] = cr
    ci_ref[...] = ci
    x = jnp.concatenate([jnp.concatenate(out_r, axis=0), jnp.concatenate(out_i, axis=0)], axis=1)
    y_ref[...] = _dot(x.astype(BF16), cm_ref[...])


def _s5_prepare(direction, a_re, a_im, log_dt, b_re, b_im, c_re, c_im):
    lr = jnp.minimum(a_re[direction].astype(F32), -1e-4)
    li = a_im[direction].astype(F32)
    dt = jnp.exp(log_dt[direction].astype(F32))[:, None]
    mag = jnp.exp(dt * lr)
    abr, abi = mag * jnp.cos(dt * li), mag * jnp.sin(dt * li)
    den = lr * lr + li * li
    qr, qi = lr / den, -li / den
    fr = (abr - 1.0) * qr - abi * qi
    fi = (abr - 1.0) * qi + abi * qr
    b_re, b_im = b_re.astype(F32), b_im.astype(F32)
    bbr = fr[..., None] * b_re - fi[..., None] * b_im
    bbi = fr[..., None] * b_im + fi[..., None] * b_re
    eye = jnp.eye(S5_GPB, dtype=F32)

    def in_block(m):
        m = m.reshape(S5_BLOCKS, S5_GPB, D_STATE, D_GSIZE)
        return jnp.einsum('kgnc,gh->kgchn', m, eye).reshape(S5_BLOCKS, S5_CH, S5_ST)

    def out_block(m):
        m = m.reshape(S5_BLOCKS, S5_GPB, D_GSIZE, D_STATE)
        return jnp.einsum('kgcn,gh->kgnhc', m, eye).reshape(S5_BLOCKS, S5_ST, S5_CH)

    bmat = jnp.concatenate([in_block(bbr), in_block(bbi)], axis=2).astype(BF16)
    cmat = jnp.concatenate([out_block(c_re.astype(F32)), -out_block(c_im.astype(F32))],
                           axis=1).astype(BF16)
    steps = jnp.arange(1, S5_ROWS + 1, dtype=F32)[:, None, None]
    pmag = jnp.exp(steps * (dt * lr)[None])
    ang = steps * (dt * li)[None]
    p_re = (pmag * jnp.cos(ang)).reshape(S5_ROWS, D_GROUPS * D_STATE)
    p_im = (pmag * jnp.sin(ang)).reshape(S5_ROWS, D_GROUPS * D_STATE)
    if direction == 1:
        p_re, p_im = p_re[::-1], p_im[::-1]
    return bmat, cmat, p_re, p_im


def s5_scan(z, direction, params, *, tc):
    b, s, _ = z.shape
    nch = s // tc
    bmat, cmat, p_re, p_im = _s5_prepare(direction, *params)
    reverse = direction == 1
    tmap = (lambda c: nch - 1 - c) if reverse else (lambda c: c)
    ub = Z_D // S5_CH
    return pl.pallas_call(
        functools.partial(_s5_scan_kernel, tc=tc, reverse=reverse),
        grid=(b, S5_BLOCKS, nch),
        in_specs=[pl.BlockSpec((None, tc, S5_CH), lambda bi, k, c: (bi, tmap(c), ub + k)),
                  pl.BlockSpec((None, S5_CH, 2 * S5_ST), lambda bi, k, c: (k, 0, 0)),
                  pl.BlockSpec((None, 2 * S5_ST, S5_CH), lambda bi, k, c: (k, 0, 0)),
                  pl.BlockSpec((S5_ROWS, S5_ST), lambda bi, k, c: (0, k)),
                  pl.BlockSpec((S5_ROWS, S5_ST), lambda bi, k, c: (0, k))],
        out_specs=pl.BlockSpec((None, tc, S5_CH), lambda bi, k, c: (bi, tmap(c), k)),
        out_shape=jax.ShapeDtypeStruct((b, s, MIX), F32),
        scratch_shapes=[pltpu.VMEM((1, S5_ST), F32), pltpu.VMEM((1, S5_ST), F32)],
        compiler_params=_cparams(("parallel", "parallel", "arbitrary")),
        name="s5_scan",
    )(z, bmat, cmat, p_re, p_im)


def _s5_out_kernel(yf_ref, yb_ref, u_ref, d_ref, w_ref, b_ref, o_ref):
    y = yf_ref[...] + yb_ref[...] + d_ref[...] * u_ref[...]
    gl = jax.nn.gelu(y)
    gate = jax.nn.sigmoid(_dot(gl.astype(BF16), w_ref[...]) + b_ref[...])
    o_ref[...] = gl * gate


def s5_output(yf, yb, z, d_skip, glu_w, glu_b, *, tm):
    b, s, _ = z.shape
    yspec = pl.BlockSpec((None, tm, MIX), lambda bi, i: (bi, i, 0))
    vspec = pl.BlockSpec((1, MIX), lambda bi, i: (0, 0))
    return pl.pallas_call(
        _s5_out_kernel,
        grid=(b, s // tm),
        in_specs=[yspec, yspec,
                  pl.BlockSpec((None, tm, MIX), lambda bi, i: (bi, i, Z_D // MIX)),
                  vspec, pl.BlockSpec((MIX, MIX), lambda bi, i: (0, 0)), vspec],
        out_specs=yspec,
        out_shape=jax.ShapeDtypeStruct((b, s, MIX), F32),
        compiler_params=_cparams(("parallel", "parallel")),
        name="s5_output",
    )(yf, yb, z, d_skip.reshape(1, MIX), glu_w.astype(BF16), glu_b.reshape(1, MIX))


def s5_bidir(z, a_re, a_im, log_dt, b_re, b_im, c_re, c_im, d_skip, glu_w, glu_b, *, tc=512, tm=512):
    params = (a_re, a_im, log_dt, b_re, b_im, c_re, c_im)
    yf = s5_scan(z, 0, params, tc=tc)
    yb = s5_scan(z, 1, params, tc=tc)
    return s5_output(yf, yb, z, d_skip, glu_w, glu_b, tm=tm)


def _softplus(y):
    return jnp.maximum(y, 0.0) + jnp.log(1.0 + jnp.exp(-jnp.abs(y)))


def _head_ones(width):
    seg = np.arange(width) // B_HDIM
    return jnp.asarray(seg[:, None] == seg[None, :], BF16)


def _rwkv_pre_kernel(z_ref, zp_ref, zn_ref, mu_ref, w0_ref, a0_ref, w2_ref, a2_ref, g2_ref,
                     kk_ref, ka_ref, rk_ref, ones_ref,
                     r_o, a_o, w0_o, w1_o, k0_o, k1_o, b0_o, b1_o, v_o, g_o, bonus_o,
                     *, tm, n_tiles):
    i = pl.program_id(1)
    z = z_ref[...]
    row = lax.broadcasted_iota(jnp.int32, z.shape, 0)
    prev_row = jnp.where(i > 0, zp_ref[7:8, :], 0.0)
    next_row = jnp.where(i < n_tiles - 1, zn_ref[0:1, :], 0.0)
    zp = jnp.where(row == 0, prev_row, pltpu.roll(z, 1, axis=0))
    zn = jnp.where(row == tm - 1, next_row, pltpu.roll(z, tm - 1, axis=0))
    xs = z + mu_ref[0:1, :] * (zp - z) + mu_ref[1:2, :] * (zn - z)
    r = xs[:, 0:MIX]
    k = xs[:, MIX:2 * MIX]
    v = xs[:, 2 * MIX:3 * MIX]
    c0 = 3 * MIX
    wd = xs[:, c0:c0 + 2 * B_LORA]
    ad = xs[:, c0 + 2 * B_LORA:c0 + 4 * B_LORA]
    gd = xs[:, c0 + 4 * B_LORA:c0 + 4 * B_LORA + B_GATE_LORA]
    lw = _dot(jnp.tanh(wd).astype(BF16), w2_ref[...])
    la = _dot(ad.astype(BF16), a2_ref[...])
    g_o[...] = _dot(jax.nn.sigmoid(gd).astype(BF16), g2_ref[...])
    ones = ones_ref[...]
    kk = k * kk_ref[...]
    kk = kk * lax.rsqrt(_segsum(kk * kk, ones) + 1e-12)
    ka = ka_ref[...]
    ksum = jnp.zeros_like(k)
    for d, (w_o, k_o, b_o) in enumerate(((w0_o, k0_o, b0_o), (w1_o, k1_o, b1_o))):
        cols = slice(d * MIX, (d + 1) * MIX)
        logw = -_softplus(-(w0_ref[d:d + 1, :] + lw[:, cols])) - 0.5
        w_o[...] = -jnp.exp(logw)
        a = jax.nn.sigmoid(a0_ref[d:d + 1, :] + la[:, cols])
        kmod = k * (1.0 + (a - 1.0) * ka)
        k_o[...] = kmod
        b_o[...] = kk * a
        ksum = ksum + kmod
    r_o[...] = r
    a_o[...] = -kk
    bonus_o[...] = _segsum(r * (0.5 * ksum) * rk_ref[...], ones) * v
    v_o[...] = v


def _block_diag2(m):
    z = jnp.zeros_like(m[0])
    return jnp.concatenate([jnp.concatenate([m[0], z], axis=1),
                            jnp.concatenate([z, m[1]], axis=1)], axis=0)


def rwkv_pre(z, mu, w0, w2, a0, a2, g2, k_k, k_a, r_k, *, tm):
    b, s, _ = z.shape
    n_tiles = s // tm
    mu_p = jnp.pad(mu.astype(F32), ((0, 0), (0, IN_B_PAD - IN_B)))
    row_spec = pl.BlockSpec((None, tm, MIX), lambda bi, i: (bi, i, 0))
    vec = lambda n, w: pl.BlockSpec((n, w), lambda bi, i: (0, 0))
    rows = jax.ShapeDtypeStruct((b, s, MIX), F32)
    hb = tm // 8
    return pl.pallas_call(
        functools.partial(_rwkv_pre_kernel, tm=tm, n_tiles=n_tiles),
        grid=(b, n_tiles),
        in_specs=[pl.BlockSpec((None, tm, IN_B_PAD), lambda bi, i: (bi, i, 0)),
                  pl.BlockSpec((None, 8, IN_B_PAD), lambda bi, i: (bi, jnp.maximum(i * hb - 1, 0), 0)),
                  pl.BlockSpec((None, 8, IN_B_PAD), lambda bi, i: (bi, jnp.minimum((i + 1) * hb, s // 8 - 1), 0)),
                  vec(2, IN_B_PAD), vec(2, MIX), vec(2, MIX),
                  vec(2 * B_LORA, 2 * MIX), vec(2 * B_LORA, 2 * MIX), vec(B_GATE_LORA, MIX),
                  vec(1, MIX), vec(1, MIX), vec(1, MIX), vec(MIX, MIX)],
        out_specs=[row_spec] * 11,
        out_shape=[rows] * 11,
        compiler_params=_cparams(("parallel", "parallel")),
        name="rwkv_pre",
    )(z, z, z, mu_p, w0.astype(F32), a0.astype(F32),
      _block_diag2(w2).astype(BF16), _block_diag2(a2).astype(BF16), g2.astype(BF16),
      k_k.reshape(1, MIX), k_a.reshape(1, MIX), r_k.reshape(1, MIX), _head_ones(MIX))


def _rwkv_scan_kernel(a_ref, r_ref, lw_ref, k_ref, b_ref, v_ref, tri_ref, y_ref, h_ref, *, tc, reverse):
    c = pl.program_id(1)

    @pl.when(c == 0)
    def _():
        h_ref[...] = jnp.zeros_like(h_ref)

    lw = lw_ref[...]
    tri = tri_ref[...]
    hi, mid, lo = _split3(lw)
    cl = _dot(tri, hi) + _dot(tri, mid) + _dot(tri, lo)
    cle = cl - lw
    last = 0 if reverse else tc - 1
    tot = cl[last:last + 1, :]
    einv = jnp.exp(-cl)
    etot = jnp.exp(tot - cl)
    a_t = a_ref[...] * jnp.exp(cle)
    r_t = r_ref[...] * jnp.exp(cl)
    b_all = b_ref[...]
    k_all = k_ref[...]
    b_t = b_all * einv
    k_t = k_all * einv
    b_h = b_all * etot
    k_h = k_all * etot
    g_tot = jnp.exp(tot)
    v_all = v_ref[...]

    row = lax.broadcasted_iota(jnp.int32, (tc, tc), 0)
    col = lax.broadcasted_iota(jnp.int32, (tc, tc), 1)
    eye = (row == col).astype(F32)
    row2 = lax.broadcasted_iota(jnp.int32, (tc, 2 * tc), 0)
    col2 = lax.broadcasted_iota(jnp.int32, (tc, 2 * tc), 1) & (tc - 1)
    before = (col2 > row2) if reverse else (col2 < row2)
    upto = (col2 >= row2) if reverse else (col2 <= row2)
    zeros_tv = jnp.zeros((tc, B_HDIM), BF16)
    n_sq = tc.bit_length() - 1

    heads = range(B_HEADS)
    hsl = [slice(hd * B_HDIM, (hd + 1) * B_HDIM) for hd in heads]
    tdot = lambda x, y: lax.dot_general(x, y, (((0,), (0,)), ((), ())), preferred_element_type=F32)
    at = [a_t[:, s].astype(BF16) for s in hsl]
    rt = [r_t[:, s].astype(BF16) for s in hsl]
    bk = [jnp.concatenate([b_t[:, s], k_t[:, s]], axis=0).astype(BF16) for s in hsl]
    bkh = [jnp.concatenate([b_h[:, s], k_h[:, s]], axis=0).astype(BF16) for s in hsl]
    vv = [v_all[:, s].astype(BF16) for s in hsl]
    ga = [jnp.where(before, _dot_t(at[h], bk[h]), 0.0) for h in heads]
    gr = [jnp.where(upto, _dot_t(rt[h], bk[h]), 0.0).astype(BF16) for h in heads]
    lkv = [_dot(ga[h].astype(BF16), jnp.concatenate([zeros_tv, vv[h]], axis=0)) for h in heads]
    lp = [ga[h][:, :tc] for h in heads]
    pinv = [eye + lp[h] for h in heads]
    for _ in range(n_sq - 1):
        lpb = [x.astype(BF16) for x in lp]
        lp = [_dot(x, x) for x in lpb]
        pinv = [pinv[h] + _dot(lp[h].astype(BF16), pinv[h].astype(BF16)) for h in heads]
    pinv = [x.astype(BF16) for x in pinv]
    ah = [_dot(pinv[h], at[h]).astype(BF16) for h in heads]
    u0 = [_dot(pinv[h], lkv[h].astype(BF16)).astype(BF16) for h in heads]
    uv = [jnp.concatenate([u0[h], vv[h]], axis=0) for h in heads]
    rh = [r_t[:, hsl[h]] + _dot(gr[h], jnp.concatenate([ah[h], zeros_tv], axis=0)) for h in heads]
    y0 = [_dot(gr[h], uv[h]) for h in heads]
    phi = [eye * g_tot[:, hsl[h]] + tdot(bkh[h][:tc], ah[h]) for h in heads]
    hadd = [tdot(bkh[h], uv[h]) for h in heads]
    h0 = [h_ref[h].astype(BF16) for h in heads]
    y_ref[...] = jnp.concatenate([_dot(rh[h].astype(BF16), h0[h]) + y0[h] for h in heads], axis=1)
    for h in heads:
        h_ref[h] = _dot(phi[h].astype(BF16), h0[h]) + hadd[h]


def rwkv_scan(a, r, lw, k, bvec, v, *, reverse, tc=64):
    nb, s, _ = a.shape
    nch = s // tc
    tmap = (lambda c: nch - 1 - c) if reverse else (lambda c: c)
    row_spec = pl.BlockSpec((None, tc, MIX), lambda bi, c: (bi, tmap(c), 0))
    t_idx = np.arange(tc)
    tri = (t_idx[None, :] >= t_idx[:, None]) if reverse else (t_idx[None, :] <= t_idx[:, None])
    return pl.pallas_call(
        functools.partial(_rwkv_scan_kernel, tc=tc, reverse=reverse),
        grid=(nb, nch),
        in_specs=[row_spec] * 6 + [pl.BlockSpec((tc, tc), lambda bi, c: (0, 0))],
        out_specs=row_spec,
        out_shape=jax.ShapeDtypeStruct((nb, s, MIX), F32),
        scratch_shapes=[pltpu.VMEM((B_HEADS, B_HDIM, B_HDIM), F32)],
        compiler_params=_cparams(("parallel", "arbitrary")),
        name="rwkv_scan",
    )(a, r, lw, k, bvec, v, jnp.asarray(tri, BF16))


def _rwkv_post_kernel(yf_ref, yb_ref, bonus_ref, g_ref, lw_ref, lb_ref, ones_ref, o_ref):
    y = yf_ref[...] + yb_ref[...]
    ones = ones_ref[...]
    mean = _segsum(y, ones) * (1.0 / B_HDIM)
    yc = y - mean
    var = _segsum(yc * yc, ones) * (1.0 / B_HDIM)
    yn = yc * lax.rsqrt(var + LN_X_EPS) * lw_ref[...] + lb_ref[...]
    o_ref[...] = (yn + bonus_ref[...]) * g_ref[...]


def rwkv_post(yf, yb, bonus, g, lnx_w, lnx_b, *, tm):
    b, s, _ = yf.shape
    row_spec = pl.BlockSpec((None, tm, MIX), lambda bi, i: (bi, i, 0))
    vec = pl.BlockSpec((1, MIX), lambda bi, i: (0, 0))
    return pl.pallas_call(
        _rwkv_post_kernel,
        grid=(b, s // tm),
        in_specs=[row_spec, row_spec, row_spec, row_spec, vec, vec,
                  pl.BlockSpec((MIX, MIX), lambda bi, i: (0, 0))],
        out_specs=row_spec,
        out_shape=jax.ShapeDtypeStruct((b, s, MIX), F32),
        compiler_params=_cparams(("parallel", "parallel")),
        name="rwkv_post",
    )(yf, yb, bonus, g, lnx_w.reshape(1, MIX), lnx_b.reshape(1, MIX), _head_ones(MIX))


def rwkv7_bidir(z, mu, w0, w2, a0, a2, g2, k_k, k_a, r_k, lnx_w, lnx_b, *, tm=256, tc=64):
    r, a, wf, wb, kf, kb, bf, bb, v, g, bonus = rwkv_pre(z, mu, w0, w2, a0, a2, g2, k_k, k_a, r_k, tm=tm)
    yf = rwkv_scan(a, r, wf, kf, bf, v, reverse=False, tc=tc)
    yb = rwkv_scan(a, r, wb, kb, bb, v, reverse=True, tc=tc)
    return rwkv_post(yf, yb, bonus, g, lnx_w, lnx_b, tm=tm)


def _mix_out_kernel(h_ref, oa_ref, ob_ref, oc_ref, od_ref, gc_ref, gd_ref, w_ref, o_ref, mix_ref):
    @pl.when(pl.program_id(1) == 0)
    def _():
        mix_ref[...] = jnp.concatenate(
            [oa_ref[...], ob_ref[...], _rms(oc_ref[...], gc_ref[...]), _rms(od_ref[...], gd_ref[...])],
            axis=1).astype(BF16)

    o_ref[...] = h_ref[...] + _dot(mix_ref[...], w_ref[...])


def mix_out(h, oa, ob, oc, od, gc, gd, w_out, *, tm, tn):
    m, d = h.shape
    mspec = pl.BlockSpec((tm, MIX), lambda i, j: (i, 0))
    vspec = pl.BlockSpec((1, MIX), lambda i, j: (0, 0))
    hspec = pl.BlockSpec((tm, tn), lambda i, j: (i, j))
    return pl.pallas_call(
        _mix_out_kernel,
        grid=(m // tm, d // tn),
        in_specs=[hspec, mspec, mspec, mspec, mspec, vspec, vspec,
                  pl.BlockSpec((4 * MIX, tn), lambda i, j: (0, j))],
        out_specs=hspec,
        out_shape=jax.ShapeDtypeStruct((m, d), F32),
        scratch_shapes=[pltpu.VMEM((tm, 4 * MIX), BF16)],
        compiler_params=_cparams(("parallel", "arbitrary")),
        name="mix_out",
    )(h, oa, ob, oc, od, gc.reshape(1, MIX), gd.reshape(1, MIX), w_out)


def _cross_attn_kernel(h_ref, q_ref, kv_ref, wo_ref, o_ref):
    q = q_ref[...]
    outs = []
    for hd in range(XA_HEADS):
        cols = slice(hd * XA_HDIM, (hd + 1) * XA_HDIM)
        kh = kv_ref[:, cols]
        vh = kv_ref[:, D_MODEL + hd * XA_HDIM:D_MODEL + (hd + 1) * XA_HDIM]
        s = _dot_t(q[:, cols], kh) * (XA_HDIM ** -0.5)
        e = jnp.exp(s - jnp.max(s, axis=-1, keepdims=True))
        p = e / jnp.sum(e, axis=-1, keepdims=True)
        outs.append(_dot(p.astype(BF16), vh).astype(BF16))
    o = jnp.concatenate(outs, axis=1)
    o_ref[...] = h_ref[...] + _dot(o, wo_ref[...])


def cross_attention(h, q, kv, wo, *, tq):
    b, s, d = h.shape
    n_mem = kv.shape[1]
    hspec = pl.BlockSpec((None, tq, d), lambda bi, i: (bi, i, 0))
    return pl.pallas_call(
        _cross_attn_kernel,
        grid=(b, s // tq),
        in_specs=[hspec, hspec,
                  pl.BlockSpec((None, n_mem, 2 * d), lambda bi, i: (bi, 0, 0)),
                  pl.BlockSpec((d, d), lambda bi, i: (0, 0))],
        out_specs=hspec,
        out_shape=jax.ShapeDtypeStruct((b, s, d), F32),
        compiler_params=_cparams(("parallel", "parallel")),
        name="cross_attention",
    )(h, q, kv, wo)


ROUTER_W = LANES
MOE_TB = 256


def _router_kernel(h_ref, g_ref, wr_ref, xn_ref, eid_ref, gate_ref):
    xn = _rms(h_ref[...], g_ref[...])
    xn_ref[...] = xn
    logits = jnp.dot(xn, wr_ref[...], preferred_element_type=F32, precision=lax.Precision.HIGHEST)
    lane = lax.broadcasted_iota(jnp.int32, logits.shape, 1)
    big = jnp.int32(ROUTER_W)

    def first_max(mask):
        m = jnp.max(jnp.where(mask, logits, NEG_INF), axis=-1, keepdims=True)
        idx = jnp.min(jnp.where(mask & (logits == m), lane, big), axis=-1, keepdims=True)
        return m, idx

    gmask = lane < N_GROUPS
    gmax, gidx = first_max(gmask)
    g_w = 1.0 / jnp.sum(jnp.where(gmask, jnp.exp(logits - gmax), 0.0), axis=-1, keepdims=True)
    e0 = N_GROUPS + gidx * EXPERTS_PER_GROUP
    emask = (lane >= e0) & (lane < e0 + EXPERTS_PER_GROUP)
    m1, i1 = first_max(emask)
    m2, i2 = first_max(emask & (lane != i1))
    e2 = jnp.exp(m2 - m1)
    w1 = 1.0 / (1.0 + e2)
    w2 = e2 / (1.0 + e2)
    eid_ref[...] = jnp.where(lane == 0, i1 - N_GROUPS, jnp.where(lane == 1, i2 - N_GROUPS, 0))
    gate_ref[...] = jnp.where(lane == 0, g_w * w1, jnp.where(lane == 1, g_w * w2, 0.0))


def moe_router(h, g, wr_group, wr_expert, *, tm):
    m, d = h.shape
    wr = jnp.concatenate([wr_group, wr_expert], axis=1).astype(F32)
    wr = jnp.pad(wr, ((0, 0), (0, ROUTER_W - wr.shape[1])))
    hspec = pl.BlockSpec((tm, d), lambda i: (i, 0))
    lspec = pl.BlockSpec((tm, ROUTER_W), lambda i: (i, 0))
    return pl.pallas_call(
        _router_kernel,
        grid=(m // tm,),
        in_specs=[hspec, pl.BlockSpec((1, d), lambda i: (0, 0)),
                  pl.BlockSpec((d, ROUTER_W), lambda i: (0, 0))],
        out_specs=[hspec, lspec, lspec],
        out_shape=[jax.ShapeDtypeStruct((m, d), F32),
                   jax.ShapeDtypeStruct((m, ROUTER_W), jnp.int32),
                   jax.ShapeDtypeStruct((m, ROUTER_W), F32)],
        compiler_params=_cparams(("parallel",)),
        name="moe_router",
    )(h, g.reshape(1, d), wr)


GATHER_UNROLL = 8


def _start_row_gather(idx_ref, base, src_hbm, dst, sem, n_rows):
    def issue(r, c):
        pltpu.make_async_copy(src_hbm.at[pl.ds(idx_ref[base + r], 1)], dst.at[pl.ds(r, 1)], sem).start()
        return c

    lax.fori_loop(0, n_rows, issue, 0, unroll=GATHER_UNROLL)


def _wait_row_gather(src_hbm, dst, sem, n_rows):
    pltpu.make_async_copy(src_hbm.at[pl.ds(0, n_rows)], dst, sem).wait()


def _moe_expert_kernel(tok_ref, exp_ref, nvb_ref, x_hbm, gate_ref, w1_ref, w3_ref, w2_ref, o_ref,
                       xbuf, sem, *, tb):
    i = pl.program_id(0)
    nvb = nvb_ref[0]

    @pl.when(i == 0)
    def _():
        _start_row_gather(tok_ref, 0, x_hbm, xbuf.at[0], sem.at[0], tb)

    @pl.when(i + 1 < nvb)
    def _():
        nxt = (i + 1) % 2
        _start_row_gather(tok_ref, (i + 1) * tb, x_hbm, xbuf.at[nxt], sem.at[nxt], tb)

    @pl.when(i < nvb)
    def _():
        cur = i % 2
        _wait_row_gather(x_hbm, xbuf.at[cur], sem.at[cur], tb)
        x = xbuf[cur].astype(BF16)
        h1 = _dot(x, w1_ref[...].astype(BF16))
        h3 = _dot(x, w3_ref[...].astype(BF16))
        act = (jax.nn.silu(h1) * h3).astype(BF16)
        o_ref[...] = _dot(act, w2_ref[...].astype(BF16)) * gate_ref[...]

    @pl.when(i >= nvb)
    def _():
        o_ref[...] = jnp.zeros_like(o_ref)


def moe_experts(xn, slot_tok, slot_gate, blk_exp, n_valid, w1, w3, w2, layer, *, tb):
    cap = slot_tok.shape[0]
    d = xn.shape[1]
    nblk = cap // tb
    wspec = lambda shape: pl.BlockSpec((None, None) + shape, lambda i, tok, ex, nv: (layer, ex[i], 0, 0))
    grid_spec = pltpu.PrefetchScalarGridSpec(
        num_scalar_prefetch=3,
        grid=(nblk,),
        in_specs=[pl.BlockSpec(memory_space=pl.ANY),
                  pl.BlockSpec((tb, 1), lambda i, tok, ex, nv: (i, 0)),
                  wspec((d, D_EXPERT)), wspec((d, D_EXPERT)), wspec((D_EXPERT, d))],
        out_specs=pl.BlockSpec((tb, d), lambda i, tok, ex, nv: (i, 0)),
        scratch_shapes=[pltpu.VMEM((2, tb, d), F32), pltpu.SemaphoreType.DMA((2,))],
    )
    return pl.pallas_call(
        functools.partial(_moe_expert_kernel, tb=tb),
        grid_spec=grid_spec,
        out_shape=jax.ShapeDtypeStruct((cap, d), F32),
        compiler_params=_cparams(("arbitrary",)),
        name="moe_experts",
    )(slot_tok, blk_exp, n_valid, xn, slot_gate.reshape(cap, 1), w1, w3, w2)


def _moe_combine_kernel(pos_ref, ys_hbm, h_ref, o_ref, buf, sem, *, tm, n_tiles):
    i = pl.program_id(0)
    n = 2 * tm

    @pl.when(i == 0)
    def _():
        _start_row_gather(pos_ref, 0, ys_hbm, buf.at[0], sem.at[0], n)

    @pl.when(i + 1 < n_tiles)
    def _():
        nxt = (i + 1) % 2
        _start_row_gather(pos_ref, (i + 1) * n, ys_hbm, buf.at[nxt], sem.at[nxt], n)

    cur = i % 2
    _wait_row_gather(ys_hbm, buf.at[cur], sem.at[cur], n)
    o_ref[...] = h_ref[...] + buf[cur, 0:tm, :] + buf[cur, tm:n, :]


def moe_combine(h, ys, pos, *, tm):
    m, d = h.shape
    hspec = pl.BlockSpec((tm, d), lambda i, p: (i, 0))
    grid_spec = pltpu.PrefetchScalarGridSpec(
        num_scalar_prefetch=1,
        grid=(m // tm,),
        in_specs=[pl.BlockSpec(memory_space=pl.ANY), hspec],
        out_specs=hspec,
        scratch_shapes=[pltpu.VMEM((2, 2 * tm, d), F32), pltpu.SemaphoreType.DMA((2,))],
    )
    return pl.pallas_call(
        functools.partial(_moe_combine_kernel, tm=tm, n_tiles=m // tm),
        grid_spec=grid_spec,
        out_shape=jax.ShapeDtypeStruct((m, d), F32),
        compiler_params=_cparams(("arbitrary",)),
        name="moe_combine",
    )(pos, ys, h)


def _moe_slots(eid, gate, tb):
    t = eid.shape[0]
    n_assign = t * TOP_K
    eid = eid.reshape(-1)
    gate = gate.reshape(-1)
    order = jnp.argsort(eid).astype(jnp.int32)
    rank = jnp.argsort(order).astype(jnp.int32)
    se = eid[order]
    experts = jnp.arange(N_EXPERTS, dtype=jnp.int32)
    start = jnp.searchsorted(se, experts, side='left').astype(jnp.int32)
    counts = jnp.searchsorted(se, experts, side='right').astype(jnp.int32) - start
    padded = ((counts + tb - 1) // tb) * tb
    pad_end = jnp.cumsum(padded)
    pad_start = pad_end - padded
    cap = n_assign + N_EXPERTS * tb
    nblk = cap // tb
    n_valid = (pad_end[-1] // tb).astype(jnp.int32)
    blk = jnp.minimum(jnp.arange(nblk, dtype=jnp.int32), n_valid - 1) * tb
    blk_exp = jnp.minimum(jnp.searchsorted(pad_end, blk, side='right'), N_EXPERTS - 1).astype(jnp.int32)
    slot = jnp.arange(cap, dtype=jnp.int32)
    e_s = jnp.repeat(blk_exp, tb)
    off = slot - pad_start[e_s]
    live = (off < counts[e_s]) & (slot < pad_end[-1])
    src = order[jnp.clip(start[e_s] + off, 0, n_assign - 1)]
    slot_tok = jnp.where(live, src // TOP_K, 0).astype(jnp.int32)
    slot_gate = jnp.where(live, gate[src], 0.0)
    e_a = eid
    pos = (pad_start[e_a] + rank - start[e_a]).astype(jnp.int32).reshape(t, TOP_K)
    return slot_tok, slot_gate, blk_exp, n_valid.reshape(1), pos


def hier_moe(h, g, wr_group, wr_expert, w1, w3, w2, layer, *, tm_router=512, tm_combine=128, tb=MOE_TB):
    m, d = h.shape
    xn, eid, gate = moe_router(h, g, wr_group, wr_expert, tm=tm_router)
    slot_tok, slot_gate, blk_exp, n_valid, pos = _moe_slots(eid[:, :TOP_K], gate[:, :TOP_K], tb)
    ys = moe_experts(xn, slot_tok, slot_gate, blk_exp, n_valid, w1, w3, w2, layer, tb=tb)
    pos = pos.reshape(m // tm_combine, tm_combine, TOP_K).transpose(0, 2, 1).reshape(-1)
    return moe_combine(h, ys, pos, tm=tm_combine)


def _final_norm_kernel(h_ref, g_ref, o_ref):
    o_ref[...] = _rms(h_ref[...], g_ref[...])


def final_norm(h, g, *, tm):
    m, d = h.shape
    hspec = pl.BlockSpec((tm, d), lambda i: (i, 0))
    return pl.pallas_call(
        _final_norm_kernel,
        grid=(m // tm,),
        in_specs=[hspec, pl.BlockSpec((1, d), lambda i: (0, 0))],
        out_specs=hspec,
        out_shape=jax.ShapeDtypeStruct((m, d), F32),
        compiler_params=_cparams(("parallel",)),
        name="final_norm",
    )(h, g.reshape(1, d))


def _w_in_layout(w_in):
    wa = w_in[:, :IN_A]
    wb = w_in[:, IN_A:IN_A + IN_B]
    wc = w_in[:, IN_A + IN_B:IN_A + IN_B + IN_C]
    wd = w_in[:, IN_A + IN_B + IN_C:]
    wb = jnp.pad(wb, ((0, 0), (0, IN_B_PAD - IN_B)))
    return jnp.concatenate([wb, wa, wc, wd], axis=1).astype(BF16)


def kernel(x, mem, norm_mix, w_in, w_out, diff_lambda, diff_subln, rwkv_mu, rwkv_w0, rwkv_w2,
           rwkv_a0, rwkv_a2, rwkv_g2, rwkv_kk, rwkv_ka, rwkv_rk, rwkv_lnx_w, rwkv_lnx_b,
           s5_a_re, s5_a_im, s5_log_dt, s5_b_re, s5_b_im, s5_c_re, s5_c_im, s5_d, s5_glu_w,
           s5_glu_b, mix_out_norm, norm_cross, norm_mem, xa_wq, xa_wkv, xa_wo, norm_moe,
           router_group, router_expert, moe_w1, moe_w3, moe_w2, norm_final):
    b, s, d = x.shape
    m = b * s
    n_mem = mem.shape[1]
    depth = w_in.shape[0]
    h = x.reshape(m, d)
    mem2 = mem.reshape(b * n_mem, d)
    rope_a = _rope_tables(s, A_QKDIM, MIX)
    rope_c = _rope_tables(s, C_HDIM, MIX)
    for l in range(depth):
        z = norm_matmul(h, norm_mix[l], _w_in_layout(w_in[l]), tm=1024, tn=512, out_dtype=F32)
        z = z.reshape(b, s, Z_W)
        qa, ka, v1a = rope_qkv(z, Z_A, A_QKDIM, tm=512, ones_cols=A_VDIM, tables=rope_a)
        oa = diff_attention(qa, ka, v1a, diff_lambda[l], diff_subln[l], l, tq=256)
        ob = rwkv7_bidir(z, rwkv_mu[l], rwkv_w0[l], rwkv_w2[l], rwkv_a0[l], rwkv_a2[l], rwkv_g2[l],
                         rwkv_kk[l], rwkv_ka[l], rwkv_rk[l], rwkv_lnx_w[l], rwkv_lnx_b[l])
        qc, kc, vc = rope_qkv(z, Z_C, C_HDIM, tm=512, tables=rope_c)
        oc = dilated_attention(qc, kc, vc, tq=256)
        od = s5_bidir(z, s5_a_re[l], s5_a_im[l], s5_log_dt[l], s5_b_re[l], s5_b_im[l],
                      s5_c_re[l], s5_c_im[l], s5_d[l], s5_glu_w[l], s5_glu_b[l])
        h = mix_out(h, oa.reshape(m, MIX), ob.reshape(m, MIX), oc.reshape(m, MIX), od.reshape(m, MIX),
                    mix_out_norm[l, 0], mix_out_norm[l, 1], w_out[l].astype(BF16), tm=512, tn=512)
        q = norm_matmul(h, norm_cross[l], xa_wq[l].astype(BF16), tm=1024, tn=512, out_dtype=BF16)
        kv = norm_matmul(mem2, norm_mem[l], xa_wkv[l].astype(BF16), tm=b * n_mem, tn=512, out_dtype=BF16)
        h = cross_attention(h.reshape(b, s, d), q.reshape(b, s, d), kv.reshape(b, n_mem, 2 * d),
                            xa_wo[l].astype(BF16), tq=256).reshape(m, d)
        h = hier_moe(h, norm_moe[l], router_group[l], router_expert[l], moe_w1, moe_w3, moe_w2, l)
    return final_norm(h, norm_final, tm=512).reshape(b, s, d)
```

```python
import functools
import math

import numpy as np
import jax
import jax.numpy as jnp
from jax import lax
from jax.experimental import pallas as pl
from jax.experimental.pallas import tpu as pltpu

F32 = jnp.float32
BF16 = jnp.bfloat16

D_MODEL = 2048
MIX = D_MODEL // 4
A_HEADS = 4
A_VDIM = MIX // A_HEADS
A_QKDIM = A_VDIM // 2
B_HDIM = 64
B_HEADS = MIX // B_HDIM
B_LORA = 64
B_GATE_LORA = 128
LN_X_EPS = 64e-5
C_HEADS = 4
C_HDIM = MIX // C_HEADS
C_PATTERNS = ((128, 1), (512, 4), (2048, 16))
D_GSIZE = 16
D_GROUPS = MIX // D_GSIZE
D_STATE = 64
IN_A = 3 * MIX
IN_B = 3 * MIX + 4 * B_LORA + B_GATE_LORA
IN_B_PAD = 2048
IN_C = 3 * MIX
IN_D = MIX
XA_HEADS = 4
XA_HDIM = D_MODEL // XA_HEADS
N_GROUPS = 4
EXPERTS_PER_GROUP = 8
N_EXPERTS = N_GROUPS * EXPERTS_PER_GROUP
TOP_K = 2
D_EXPERT = D_MODEL // 4
ROPE_THETA = 10000.0
RMS_EPS = 1e-6
NEG_INF = -1e30

Z_B = 0
Z_A = IN_B_PAD
Z_C = Z_A + IN_A
Z_D = Z_C + IN_C
Z_W = Z_D + IN_D

LANES = 128
VMEM_LIMIT = 56 * 1024 * 1024


def _cparams(sem, vmem=VMEM_LIMIT):
    return pltpu.CompilerParams(dimension_semantics=sem, vmem_limit_bytes=vmem)


def _rms(x, g):
    return x * lax.rsqrt(jnp.mean(x * x, axis=-1, keepdims=True) + RMS_EPS) * g


def _dot(a, b):
    return jnp.dot(a, b, preferred_element_type=F32)


def _dot_t(a, b):
    return lax.dot_general(a, b, (((1,), (1,)), ((), ())), preferred_element_type=F32)


def _split3(x):
    hi = x.astype(BF16)
    r1 = x - hi.astype(F32)
    mid = r1.astype(BF16)
    lo = (r1 - mid.astype(F32)).astype(BF16)
    return hi, mid, lo


def _segsum(x, ones):
    hi, mid, lo = _split3(x)
    return _dot(hi, ones) + _dot(mid, ones) + _dot(lo, ones)


def _norm_matmul_kernel(x_ref, g_ref, w_ref, o_ref, xn_ref):
    @pl.when(pl.program_id(1) == 0)
    def _():
        xn_ref[...] = _rms(x_ref[...], g_ref[...]).astype(BF16)

    o_ref[...] = _dot(xn_ref[...], w_ref[...]).astype(o_ref.dtype)


def norm_matmul(x, g, w, *, tm, tn, out_dtype):
    m, k = x.shape
    n = w.shape[1]
    return pl.pallas_call(
        _norm_matmul_kernel,
        grid=(m // tm, n // tn),
        in_specs=[pl.BlockSpec((tm, k), lambda i, j: (i, 0)),
                  pl.BlockSpec((1, k), lambda i, j: (0, 0)),
                  pl.BlockSpec((k, tn), lambda i, j: (0, j))],
        out_specs=pl.BlockSpec((tm, tn), lambda i, j: (i, j)),
        out_shape=jax.ShapeDtypeStruct((m, n), out_dtype),
        scratch_shapes=[pltpu.VMEM((tm, k), BF16)],
        compiler_params=_cparams(("parallel", "arbitrary")),
        name="norm_matmul",
    )(x, g.reshape(1, k), w)


def _rope_tables(seq, dim, width):
    inv = 1.0 / (ROPE_THETA ** (jnp.arange(0, dim, 2, dtype=F32) / dim))
    ang = jnp.arange(seq, dtype=F32)[:, None] * inv[None, :]
    cos, sin = jnp.cos(ang), jnp.sin(ang)
    cos = jnp.concatenate([cos, cos], axis=-1)
    sin = jnp.concatenate([-sin, sin], axis=-1)
    reps = width // dim
    return jnp.tile(cos, (1, reps)), jnp.tile(sin, (1, reps))


def _rope_kernel(q_ref, k_ref, v_ref, cos_ref, sin_ref, qo_ref, ko_ref, vo_ref, *, half, scale, ones_cols):
    cos = cos_ref[...]
    sin = sin_ref[...]
    width = cos.shape[1]
    lane = lax.broadcasted_iota(jnp.int32, cos.shape, 1)
    first = (lane % (2 * half)) < half

    def rot(x):
        ahead = pltpu.roll(x, width - half, axis=1)
        behind = pltpu.roll(x, half, axis=1)
        return x * cos + jnp.where(first, ahead, behind) * sin

    qo_ref[...] = (rot(q_ref[...]) * scale).astype(BF16)
    ko_ref[...] = rot(k_ref[...]).astype(BF16)
    v = v_ref[...].astype(BF16)
    if ones_cols is None:
        vo_ref[...] = v
    else:
        ones = jnp.ones((v.shape[0], ones_cols), BF16)
        parts = []
        for hd in range(v.shape[1] // ones_cols):
            parts += [v[:, hd * ones_cols:(hd + 1) * ones_cols], ones]
        vo_ref[...] = jnp.concatenate(parts, axis=1)


def rope_qkv(z, col0, head_dim, *, tm, ones_cols=None, tables=None):
    b, s, _ = z.shape
    cos, sin = tables if tables is not None else _rope_tables(s, head_dim, MIX)
    cb = col0 // MIX
    zspec = lambda off: pl.BlockSpec((None, tm, MIX), lambda bi, i, off=off: (bi, i, cb + off))
    tspec = pl.BlockSpec((tm, MIX), lambda bi, i: (i, 0))
    ospec = pl.BlockSpec((None, tm, MIX), lambda bi, i: (bi, i, 0))
    oshape = jax.ShapeDtypeStruct((b, s, MIX), BF16)
    if ones_cols is None:
        vspec, vshape = ospec, oshape
    else:
        vspec = pl.BlockSpec((None, tm, 2 * MIX), lambda bi, i: (bi, i, 0))
        vshape = jax.ShapeDtypeStruct((b, s, 2 * MIX), BF16)
    return pl.pallas_call(
        functools.partial(_rope_kernel, half=head_dim // 2, scale=head_dim ** -0.5, ones_cols=ones_cols),
        grid=(b, s // tm),
        in_specs=[zspec(0), zspec(1), zspec(2), tspec, tspec],
        out_specs=[ospec, ospec, vspec],
        out_shape=[oshape, oshape, vshape],
        compiler_params=_cparams(("parallel", "parallel")),
        name="rope_qkv",
    )(z, z, z, cos, sin)


def _diff_attn_kernel(lam_ref, g_ref, q_ref, k_ref, v1_ref, o_ref, *, lam_init):
    lv = lam_ref[...]
    lam = (jnp.exp(jnp.sum(lv[0:1] * lv[1:2], axis=-1, keepdims=True))
           - jnp.exp(jnp.sum(lv[2:3] * lv[3:4], axis=-1, keepdims=True)) + lam_init)
    q = q_ref[...]
    k = k_ref[...]
    v1 = v1_ref[...]
    lane = lax.broadcasted_iota(jnp.int32, q.shape, 1)
    zero = jnp.zeros_like(q)

    def branch(qm):
        s = _dot_t(qm, k)
        p = jnp.exp((s - jnp.max(s, axis=-1, keepdims=True)).astype(BF16))
        acc = _dot(p, v1)
        return acc[:, :A_VDIM] / acc[:, A_VDIM:]

    o = branch(jnp.where(lane < A_QKDIM, q, zero)) - lam * branch(jnp.where(lane >= A_QKDIM, q, zero))
    o_ref[...] = _rms(o, g_ref[...]) * (1.0 - lam_init)


def diff_attention(q, k, v1, lam_vecs, subln_g, layer_idx, *, tq):
    b, s, _ = q.shape
    lam_init = 0.8 - 0.6 * math.exp(-0.3 * layer_idx)
    qspec = pl.BlockSpec((None, tq, A_VDIM), lambda bi, h, i: (bi, i, h))
    return pl.pallas_call(
        functools.partial(_diff_attn_kernel, lam_init=lam_init),
        grid=(b, A_HEADS, s // tq),
        in_specs=[pl.BlockSpec((4, A_QKDIM), lambda bi, h, i: (0, 0)),
                  pl.BlockSpec((1, A_VDIM), lambda bi, h, i: (0, 0)),
                  qspec,
                  pl.BlockSpec((None, s, A_VDIM), lambda bi, h, i: (bi, 0, h)),
                  pl.BlockSpec((None, s, 2 * A_VDIM), lambda bi, h, i: (bi, 0, h))],
        out_specs=qspec,
        out_shape=jax.ShapeDtypeStruct((b, s, MIX), F32),
        compiler_params=_cparams(("parallel", "parallel", "arbitrary")),
        name="diff_attention",
    )(lam_vecs, subln_g.reshape(1, A_VDIM), q, k, v1)


C_REACH = max(w // 2 for w, _ in C_PATTERNS)


def _dilated_bias_table(tq, window):
    n_delta = (window - tq) // tq + 1
    i = np.arange(tq)[None, :, None]
    j = np.arange(window)[None, None, :]
    n = np.arange(n_delta)[:, None, None]
    d = j - i - n * tq
    count = np.zeros(d.shape, np.int32)
    for w, dil in C_PATTERNS:
        count += ((np.abs(d) <= w // 2) & (d % dil == 0)).astype(np.int32)
    bias = np.where(count > 0, np.log(np.maximum(count, 1)), NEG_INF)
    return jnp.asarray(bias, F32)


def _dilated_attn_kernel(bias_ref, q_ref, k_ref, v_ref, o_ref, *, tq, window, seq):
    start = pl.program_id(2) * tq
    ws = pl.multiple_of(jnp.clip(start - C_REACH, 0, seq - window), tq)
    kw = k_ref[pl.ds(ws, window), :]
    vw = v_ref[pl.ds(ws, window), :]
    s = _dot_t(q_ref[...], kw) + bias_ref[...]
    e = jnp.exp(s - jnp.max(s, axis=-1, keepdims=True))
    den = jnp.sum(e, axis=-1, keepdims=True)
    o_ref[...] = _dot(e.astype(BF16), vw) / den


def dilated_attention(q, k, v, *, tq=128):
    b, s, _ = q.shape
    window = tq + 2 * C_REACH
    assert s >= window and s % tq == 0 and C_REACH % tq == 0
    bias = _dilated_bias_table(tq, window)

    def bias_map(bi, h, i):
        start = i * tq
        ws = jnp.clip(start - C_REACH, 0, s - window)
        return ((start - ws) // tq, 0, 0)

    qspec = pl.BlockSpec((None, tq, C_HDIM), lambda bi, h, i: (bi, i, h))
    kspec = pl.BlockSpec((None, s, C_HDIM), lambda bi, h, i: (bi, 0, h))
    return pl.pallas_call(
        functools.partial(_dilated_attn_kernel, tq=tq, window=window, seq=s),
        grid=(b, C_HEADS, s // tq),
        in_specs=[pl.BlockSpec((None, tq, window), bias_map), qspec, kspec, kspec],
        out_specs=qspec,
        out_shape=jax.ShapeDtypeStruct((b, s, MIX), F32),
        compiler_params=_cparams(("parallel", "parallel", "arbitrary")),
        name="dilated_attention",
    )(bias, q, k, v)


S5_BLOCKS = 4
S5_GPB = D_GROUPS // S5_BLOCKS
S5_CH = S5_GPB * D_GSIZE
S5_ST = S5_GPB * D_STATE
S5_ROWS = 8


def _s5_scan_kernel(u_ref, bm_ref, cm_ref, pr_ref, pi_ref, y_ref, cr_ref, ci_ref, *, tc, reverse):
    @pl.when(pl.program_id(2) == 0)
    def _():
        cr_ref[...] = jnp.zeros_like(cr_ref)
        ci_ref[...] = jnp.zeros_like(ci_ref)

    bu = _dot(u_ref[...].astype(BF16), bm_ref[...])
    xr = bu[:, :S5_ST]
    xi = bu[:, S5_ST:]
    row = lax.broadcasted_iota(jnp.int32, xr.shape, 0) & (S5_ROWS - 1)
    for lvl in range(S5_ROWS.bit_length() - 1):
        sh = 1 << lvl
        prow = (S5_ROWS - sh) if reverse else (sh - 1)
        ar = pr_ref[prow:prow + 1, :]
        ai = pi_ref[prow:prow + 1, :]
        if reverse:
            keep = row < (S5_ROWS - sh)
            sr = jnp.where(keep, pltpu.roll(xr, tc - sh, axis=0), 0.0)
            si = jnp.where(keep, pltpu.roll(xi, tc - sh, axis=0), 0.0)
        else:
            keep = row >= sh
            sr = jnp.where(keep, pltpu.roll(xr, sh, axis=0), 0.0)
            si = jnp.where(keep, pltpu.roll(xi, sh, axis=0), 0.0)
        xr, xi = xr + ar * sr - ai * si, xi + ar * si + ai * sr
    pr = pr_ref[...]
    pi = pi_ref[...]
    cr = cr_ref[...]
    ci = ci_ref[...]
    n_groups = tc // S5_ROWS
    last = 0 if reverse else S5_ROWS - 1
    out_r = [None] * n_groups
    out_i = [None] * n_groups
    for gi in (range(n_groups - 1, -1, -1) if reverse else range(n_groups)):
        rows = slice(gi * S5_ROWS, (gi + 1) * S5_ROWS)
        gr = xr[rows] + pr * cr - pi * ci
        gim = xi[rows] + pr * ci + pi * cr
        cr = gr[last:last + 1, :]
        ci = gim[last:last + 1, :]
        out_r[gi] = gr
        out_i[gi] = gim
    cr_ref[...] = cr
    ci_ref[...] = ci
    x = jnp.concatenate([jnp.concatenate(out_r, axis=0), jnp.concatenate(out_i, axis=0)], axis=1)
    y_ref[...] = _dot(x.astype(BF16), cm_ref[...])


def _s5_prepare(direction, a_re, a_im, log_dt, b_re, b_im, c_re, c_im):
    lr = jnp.minimum(a_re[direction].astype(F32), -1e-4)
    li = a_im[direction].astype(F32)
    dt = jnp.exp(log_dt[direction].astype(F32))[:, None]
    mag = jnp.exp(dt * lr)
    abr, abi = mag * jnp.cos(dt * li), mag * jnp.sin(dt * li)
    den = lr * lr + li * li
    qr, qi = lr / den, -li / den
    fr = (abr - 1.0) * qr - abi * qi
    fi = (abr - 1.0) * qi + abi * qr
    b_re, b_im = b_re.astype(F32), b_im.astype(F32)
    bbr = fr[..., None] * b_re - fi[..., None] * b_im
    bbi = fr[..., None] * b_im + fi[..., None] * b_re
    eye = jnp.eye(S5_GPB, dtype=F32)

    def in_block(m):
        m = m.reshape(S5_BLOCKS, S5_GPB, D_STATE, D_GSIZE)
        return jnp.einsum('kgnc,gh->kgchn', m, eye).reshape(S5_BLOCKS, S5_CH, S5_ST)

    def out_block(m):
        m = m.reshape(S5_BLOCKS, S5_GPB, D_GSIZE, D_STATE)
        return jnp.einsum('kgcn,gh->kgnhc', m, eye).reshape(S5_BLOCKS, S5_ST, S5_CH)

    bmat = jnp.concatenate([in_block(bbr), in_block(bbi)], axis=2).astype(BF16)
    cmat = jnp.concatenate([out_block(c_re.astype(F32)), -out_block(c_im.astype(F32))],
                           axis=1).astype(BF16)
    steps = jnp.arange(1, S5_ROWS + 1, dtype=F32)[:, None, None]
    pmag = jnp.exp(steps * (dt * lr)[None])
    ang = steps * (dt * li)[None]
    p_re = (pmag * jnp.cos(ang)).reshape(S5_ROWS, D_GROUPS * D_STATE)
    p_im = (pmag * jnp.sin(ang)).reshape(S5_ROWS, D_GROUPS * D_STATE)
    if direction == 1:
        p_re, p_im = p_re[::-1], p_im[::-1]
    return bmat, cmat, p_re, p_im


def s5_scan(z, direction, params, *, tc):
    b, s, _ = z.shape
    nch = s // tc
    bmat, cmat, p_re, p_im = _s5_prepare(direction, *params)
    reverse = direction == 1
    tmap = (lambda c: nch - 1 - c) if reverse else (lambda c: c)
    ub = Z_D // S5_CH
    return pl.pallas_call(
        functools.partial(_s5_scan_kernel, tc=tc, reverse=reverse),
        grid=(b, S5_BLOCKS, nch),
        in_specs=[pl.BlockSpec((None, tc, S5_CH), lambda bi, k, c: (bi, tmap(c), ub + k)),
                  pl.BlockSpec((None, S5_CH, 2 * S5_ST), lambda bi, k, c: (k, 0, 0)),
                  pl.BlockSpec((None, 2 * S5_ST, S5_CH), lambda bi, k, c: (k, 0, 0)),
                  pl.BlockSpec((S5_ROWS, S5_ST), lambda bi, k, c: (0, k)),
                  pl.BlockSpec((S5_ROWS, S5_ST), lambda bi, k, c: (0, k))],
        out_specs=pl.BlockSpec((None, tc, S5_CH), lambda bi, k, c: (bi, tmap(c), k)),
        out_shape=jax.ShapeDtypeStruct((b, s, MIX), F32),
        scratch_shapes=[pltpu.VMEM((1, S5_ST), F32), pltpu.VMEM((1, S5_ST), F32)],
        compiler_params=_cparams(("parallel", "parallel", "arbitrary")),
        name="s5_scan",
    )(z, bmat, cmat, p_re, p_im)


def _s5_out_kernel(yf_ref, yb_ref, u_ref, d_ref, w_ref, b_ref, o_ref):
    y = yf_ref[...] + yb_ref[...] + d_ref[...] * u_ref[...]
    gl = jax.nn.gelu(y)
    gate = jax.nn.sigmoid(_dot(gl.astype(BF16), w_ref[...]) + b_ref[...])
    o_ref[...] = gl * gate


def s5_output(yf, yb, z, d_skip, glu_w, glu_b, *, tm):
    b, s, _ = z.shape
    yspec = pl.BlockSpec((None, tm, MIX), lambda bi, i: (bi, i, 0))
    vspec = pl.BlockSpec((1, MIX), lambda bi, i: (0, 0))
    return pl.pallas_call(
        _s5_out_kernel,
        grid=(b, s // tm),
        in_specs=[yspec, yspec,
                  pl.BlockSpec((None, tm, MIX), lambda bi, i: (bi, i, Z_D // MIX)),
                  vspec, pl.BlockSpec((MIX, MIX), lambda bi, i: (0, 0)), vspec],
        out_specs=yspec,
        out_shape=jax.ShapeDtypeStruct((b, s, MIX), F32),
        compiler_params=_cparams(("parallel", "parallel")),
        name="s5_output",
    )(yf, yb, z, d_skip.reshape(1, MIX), glu_w.astype(BF16), glu_b.reshape(1, MIX))


def s5_bidir(z, a_re, a_im, log_dt, b_re, b_im, c_re, c_im, d_skip, glu_w, glu_b, *, tc=1024, tm=512):
    params = (a_re, a_im, log_dt, b_re, b_im, c_re, c_im)
    yf = s5_scan(z, 0, params, tc=tc)
    yb = s5_scan(z, 1, params, tc=tc)
    return s5_output(yf, yb, z, d_skip, glu_w, glu_b, tm=tm)


def _softplus(y):
    return jnp.maximum(y, 0.0) + jnp.log(1.0 + jnp.exp(-jnp.abs(y)))


def _head_ones(width):
    seg = np.arange(width) // B_HDIM
    return jnp.asarray(seg[:, None] == seg[None, :], BF16)


def _rwkv_pre_kernel(z_ref, zp_ref, zn_ref, mu_ref, w0_ref, a0_ref, w2_ref, a2_ref, g2_ref,
                     kk_ref, ka_ref, rk_ref, ones_ref,
                     r_o, a_o, w0_o, w1_o, k0_o, k1_o, b0_o, b1_o, v_o, g_o, bonus_o,
                     *, tm, n_tiles):
    i = pl.program_id(1)
    z = z_ref[...]
    row = lax.broadcasted_iota(jnp.int32, z.shape, 0)
    prev_row = jnp.where(i > 0, zp_ref[7:8, :], 0.0)
    next_row = jnp.where(i < n_tiles - 1, zn_ref[0:1, :], 0.0)
    zp = jnp.where(row == 0, prev_row, pltpu.roll(z, 1, axis=0))
    zn = jnp.where(row == tm - 1, next_row, pltpu.roll(z, tm - 1, axis=0))
    xs = z + mu_ref[0:1, :] * (zp - z) + mu_ref[1:2, :] * (zn - z)
    r = xs[:, 0:MIX]
    k = xs[:, MIX:2 * MIX]
    v = xs[:, 2 * MIX:3 * MIX]
    c0 = 3 * MIX
    wd = xs[:, c0:c0 + 2 * B_LORA]
    ad = xs[:, c0 + 2 * B_LORA:c0 + 4 * B_LORA]
    gd = xs[:, c0 + 4 * B_LORA:c0 + 4 * B_LORA + B_GATE_LORA]
    lw = _dot(jnp.tanh(wd).astype(BF16), w2_ref[...])
    la = _dot(ad.astype(BF16), a2_ref[...])
    g_o[...] = _dot(jax.nn.sigmoid(gd).astype(BF16), g2_ref[...])
    ones = ones_ref[...]
    kk = k * kk_ref[...]
    kk = kk * lax.rsqrt(_segsum(kk * kk, ones) + 1e-12)
    ka = ka_ref[...]
    ksum = jnp.zeros_like(k)
    for d, (w_o, k_o, b_o) in enumerate(((w0_o, k0_o, b0_o), (w1_o, k1_o, b1_o))):
        cols = slice(d * MIX, (d + 1) * MIX)
        logw = -_softplus(-(w0_ref[d:d + 1, :] + lw[:, cols])) - 0.5
        w_o[...] = -jnp.exp(logw)
        a = jax.nn.sigmoid(a0_ref[d:d + 1, :] + la[:, cols])
        kmod = k * (1.0 + (a - 1.0) * ka)
        k_o[...] = kmod
        b_o[...] = kk * a
        ksum = ksum + kmod
    r_o[...] = r
    a_o[...] = -kk
    bonus_o[...] = _segsum(r * (0.5 * ksum) * rk_ref[...], ones) * v
    v_o[...] = v


def _block_diag2(m):
    z = jnp.zeros_like(m[0])
    return jnp.concatenate([jnp.concatenate([m[0], z], axis=1),
                            jnp.concatenate([z, m[1]], axis=1)], axis=0)


def rwkv_pre(z, mu, w0, w2, a0, a2, g2, k_k, k_a, r_k, *, tm):
    b, s, _ = z.shape
    n_tiles = s // tm
    mu_p = jnp.pad(mu.astype(F32), ((0, 0), (0, IN_B_PAD - IN_B)))
    row_spec = pl.BlockSpec((None, tm, MIX), lambda bi, i: (bi, i, 0))
    vec = lambda n, w: pl.BlockSpec((n, w), lambda bi, i: (0, 0))
    rows = jax.ShapeDtypeStruct((b, s, MIX), F32)
    hb = tm // 8
    return pl.pallas_call(
        functools.partial(_rwkv_pre_kernel, tm=tm, n_tiles=n_tiles),
        grid=(b, n_tiles),
        in_specs=[pl.BlockSpec((None, tm, IN_B_PAD), lambda bi, i: (bi, i, 0)),
                  pl.BlockSpec((None, 8, IN_B_PAD), lambda bi, i: (bi, jnp.maximum(i * hb - 1, 0), 0)),
                  pl.BlockSpec((None, 8, IN_B_PAD), lambda bi, i: (bi, jnp.minimum((i + 1) * hb, s // 8 - 1), 0)),
                  vec(2, IN_B_PAD), vec(2, MIX), vec(2, MIX),
                  vec(2 * B_LORA, 2 * MIX), vec(2 * B_LORA, 2 * MIX), vec(B_GATE_LORA, MIX),
                  vec(1, MIX), vec(1, MIX), vec(1, MIX), vec(MIX, MIX)],
        out_specs=[row_spec] * 11,
        out_shape=[rows] * 11,
        compiler_params=_cparams(("parallel", "parallel")),
        name="rwkv_pre",
    )(z, z, z, mu_p, w0.astype(F32), a0.astype(F32),
      _block_diag2(w2).astype(BF16), _block_diag2(a2).astype(BF16), g2.astype(BF16),
      k_k.reshape(1, MIX), k_a.reshape(1, MIX), r_k.reshape(1, MIX), _head_ones(MIX))


def _rwkv_scan_kernel(af_ref, rf_ref, vf_ref, lwf_ref, kf_ref, bf_ref,
                      ab_ref, rb_ref, vb_ref, lwb_ref, kb_ref, bb_ref,
                      trif_ref, trib_ref, yf_ref, yb_ref, h_ref, *, tc, nb):
    @pl.when(pl.program_id(0) == 0)
    def _():
        h_ref[...] = jnp.zeros_like(h_ref)

    row = lax.broadcasted_iota(jnp.int32, (tc, tc), 0)
    col = lax.broadcasted_iota(jnp.int32, (tc, tc), 1)
    eye = (row == col).astype(F32)
    row2 = lax.broadcasted_iota(jnp.int32, (tc, 2 * tc), 0)
    col2 = lax.broadcasted_iota(jnp.int32, (tc, 2 * tc), 1) & (tc - 1)
    zeros_tv = jnp.zeros((tc, B_HDIM), BF16)
    n_sq = tc.bit_length() - 1
    tdot = lambda x, y: lax.dot_general(x, y, (((0,), (0,)), ((), ())), preferred_element_type=F32)

    seqs = []
    for reverse, (a_ref, r_ref, v_ref, lw_ref, k_ref, b_ref, tri_ref, y_ref) in (
            (False, (af_ref, rf_ref, vf_ref, lwf_ref, kf_ref, bf_ref, trif_ref, yf_ref)),
            (True, (ab_ref, rb_ref, vb_ref, lwb_ref, kb_ref, bb_ref, trib_ref, yb_ref))):
        tri = tri_ref[...]
        before = (col2 > row2) if reverse else (col2 < row2)
        upto = (col2 >= row2) if reverse else (col2 <= row2)
        last = 0 if reverse else tc - 1
        for bi in range(nb):
            lw = lw_ref[bi]
            hi, mid, lo = _split3(lw)
            cl = _dot(tri, hi) + _dot(tri, mid) + _dot(tri, lo)
            tot = cl[last:last + 1, :]
            einv = jnp.exp(-cl)
            etot = jnp.exp(tot - cl)
            b_all = b_ref[bi]
            k_all = k_ref[bi]
            seqs.append(dict(
                a_t=a_ref[bi] * jnp.exp(cl - lw), r_t=r_ref[bi] * jnp.exp(cl),
                b_t=b_all * einv, k_t=k_all * einv, b_h=b_all * etot, k_h=k_all * etot,
                g_tot=jnp.exp(tot), v=v_ref[bi], before=before, upto=upto, y_ref=y_ref, bi=bi))

    inst = [(sq, slice(hd * B_HDIM, (hd + 1) * B_HDIM)) for sq in seqs for hd in range(B_HEADS)]
    ids = range(len(inst))
    at = [sq['a_t'][:, s].astype(BF16) for sq, s in inst]
    rt = [sq['r_t'][:, s].astype(BF16) for sq, s in inst]
    bk = [jnp.concatenate([sq['b_t'][:, s], sq['k_t'][:, s]], axis=0).astype(BF16) for sq, s in inst]
    bkh = [jnp.concatenate([sq['b_h'][:, s], sq['k_h'][:, s]], axis=0).astype(BF16) for sq, s in inst]
    vv = [sq['v'][:, s].astype(BF16) for sq, s in inst]
    g = [_dot_t(jnp.concatenate([at[n], rt[n]], axis=0), bk[n]) for n in ids]
    ga = [jnp.where(inst[n][0]['before'], g[n][:tc], 0.0) for n in ids]
    gr = [jnp.where(inst[n][0]['upto'], g[n][tc:], 0.0).astype(BF16) for n in ids]
    lkv = [_dot(ga[n].astype(BF16), jnp.concatenate([zeros_tv, vv[n]], axis=0)) for n in ids]
    lb = [ga[n][:, :tc].astype(BF16) for n in ids]
    pinv = [eye + ga[n][:, :tc] for n in ids]
    lb = [_dot(x, x).astype(BF16) for x in lb]
    for _ in range(1, n_sq - 1):
        prod = [_dot(jnp.concatenate([pinv[n].astype(BF16), lb[n]], axis=0), lb[n]) for n in ids]
        pinv = [pinv[n] + prod[n][:tc] for n in ids]
        lb = [prod[n][tc:].astype(BF16) for n in ids]
    pinv = [(pinv[n] + _dot(pinv[n].astype(BF16), lb[n])).astype(BF16) for n in ids]
    ah = [_dot(pinv[n], at[n]).astype(BF16) for n in ids]
    u0 = [_dot(pinv[n], lkv[n].astype(BF16)).astype(BF16) for n in ids]
    rhs = [jnp.concatenate([jnp.concatenate([ah[n], u0[n]], axis=1),
                            jnp.concatenate([zeros_tv, vv[n]], axis=1)], axis=0) for n in ids]
    ry = [_dot(gr[n], rhs[n]) for n in ids]
    ph = [tdot(bkh[n], rhs[n]) for n in ids]
    rh = [inst[n][0]['r_t'][:, inst[n][1]] + ry[n][:, :B_HDIM] for n in ids]
    phi = [eye * inst[n][0]['g_tot'][:, inst[n][1]] + ph[n][:, :B_HDIM] for n in ids]
    h0 = [h_ref[n].astype(BF16) for n in ids]
    fin = [_dot(jnp.concatenate([rh[n], phi[n]], axis=0).astype(BF16), h0[n]) for n in ids]
    yo = [fin[n][:tc] + ry[n][:, B_HDIM:] for n in ids]
    for si, sq in enumerate(seqs):
        sq['y_ref'][sq['bi']] = jnp.concatenate(yo[si * B_HEADS:(si + 1) * B_HEADS], axis=1)
    for n in ids:
        h_ref[n] = fin[n][tc:] + ph[n][:, B_HDIM:]


def rwkv_scan(a, r, v, lwf, kf, bvf, lwb, kb, bvb, *, tc=64):
    nb, s, _ = a.shape
    nch = s // tc
    fwd = pl.BlockSpec((nb, tc, MIX), lambda c: (0, c, 0))
    bwd = pl.BlockSpec((nb, tc, MIX), lambda c: (0, nch - 1 - c, 0))
    tspec = pl.BlockSpec((tc, tc), lambda c: (0, 0))
    t_idx = np.arange(tc)
    tri_f = jnp.asarray(t_idx[None, :] <= t_idx[:, None], BF16)
    tri_b = jnp.asarray(t_idx[None, :] >= t_idx[:, None], BF16)
    out = jax.ShapeDtypeStruct((nb, s, MIX), F32)
    return pl.pallas_call(
        functools.partial(_rwkv_scan_kernel, tc=tc, nb=nb),
        grid=(nch,),
        in_specs=[fwd] * 6 + [bwd] * 6 + [tspec, tspec],
        out_specs=[fwd, bwd],
        out_shape=[out, out],
        scratch_shapes=[pltpu.VMEM((2 * nb * B_HEADS, B_HDIM, B_HDIM), F32)],
        compiler_params=_cparams(("arbitrary",)),
        name="rwkv_scan",
    )(a, r, v, lwf, kf, bvf, a, r, v, lwb, kb, bvb, tri_f, tri_b)


def _rwkv_post_kernel(yf_ref, yb_ref, bonus_ref, g_ref, lw_ref, lb_ref, ones_ref, o_ref):
    y = yf_ref[...] + yb_ref[...]
    ones = ones_ref[...]
    mean = _segsum(y, ones) * (1.0 / B_HDIM)
    yc = y - mean
    var = _segsum(yc * yc, ones) * (1.0 / B_HDIM)
    yn = yc * lax.rsqrt(var + LN_X_EPS) * lw_ref[...] + lb_ref[...]
    o_ref[...] = (yn + bonus_ref[...]) * g_ref[...]


def rwkv_post(yf, yb, bonus, g, lnx_w, lnx_b, *, tm):
    b, s, _ = yf.shape
    row_spec = pl.BlockSpec((None, tm, MIX), lambda bi, i: (bi, i, 0))
    vec = pl.BlockSpec((1, MIX), lambda bi, i: (0, 0))
    return pl.pallas_call(
        _rwkv_post_kernel,
        grid=(b, s // tm),
        in_specs=[row_spec, row_spec, row_spec, row_spec, vec, vec,
                  pl.BlockSpec((MIX, MIX), lambda bi, i: (0, 0))],
        out_specs=row_spec,
        out_shape=jax.ShapeDtypeStruct((b, s, MIX), F32),
        compiler_params=_cparams(("parallel", "parallel")),
        name="rwkv_post",
    )(yf, yb, bonus, g, lnx_w.reshape(1, MIX), lnx_b.reshape(1, MIX), _head_ones(MIX))


def rwkv7_bidir(z, mu, w0, w2, a0, a2, g2, k_k, k_a, r_k, lnx_w, lnx_b, *, tm=256, tc=64):
    r, a, wf, wb, kf, kb, bf, bb, v, g, bonus = rwkv_pre(z, mu, w0, w2, a0, a2, g2, k_k, k_a, r_k, tm=tm)
    yf, yb = rwkv_scan(a, r, v, wf, kf, bf, wb, kb, bb, tc=tc)
    return rwkv_post(yf, yb, bonus, g, lnx_w, lnx_b, tm=tm)


def _mix_out_kernel(h_ref, oa_ref, ob_ref, oc_ref, od_ref, gc_ref, gd_ref, w_ref, o_ref, mix_ref):
    @pl.when(pl.program_id(1) == 0)
    def _():
        mix_ref[...] = jnp.concatenate(
            [oa_ref[...], ob_ref[...], _rms(oc_ref[...], gc_ref[...]), _rms(od_ref[...], gd_ref[...])],
            axis=1).astype(BF16)

    o_ref[...] = h_ref[...] + _dot(mix_ref[...], w_ref[...])


def mix_out(h, oa, ob, oc, od, gc, gd, w_out, *, tm, tn):
    m, d = h.shape
    mspec = pl.BlockSpec((tm, MIX), lambda i, j: (i, 0))
    vspec = pl.BlockSpec((1, MIX), lambda i, j: (0, 0))
    hspec = pl.BlockSpec((tm, tn), lambda i, j: (i, j))
    return pl.pallas_call(
        _mix_out_kernel,
        grid=(m // tm, d // tn),
        in_specs=[hspec, mspec, mspec, mspec, mspec, vspec, vspec,
                  pl.BlockSpec((4 * MIX, tn), lambda i, j: (0, j))],
        out_specs=hspec,
        out_shape=jax.ShapeDtypeStruct((m, d), F32),
        scratch_shapes=[pltpu.VMEM((tm, 4 * MIX), BF16)],
        compiler_params=_cparams(("parallel", "arbitrary")),
        name="mix_out",
    )(h, oa, ob, oc, od, gc.reshape(1, MIX), gd.reshape(1, MIX), w_out)


def _cross_attn_kernel(h_ref, q_ref, kv_ref, wo_ref, o_ref):
    q = q_ref[...]
    outs = []
    for hd in range(XA_HEADS):
        cols = slice(hd * XA_HDIM, (hd + 1) * XA_HDIM)
        kh = kv_ref[:, cols]
        vh = kv_ref[:, D_MODEL + hd * XA_HDIM:D_MODEL + (hd + 1) * XA_HDIM]
        s = _dot_t(q[:, cols], kh) * (XA_HDIM ** -0.5)
        e = jnp.exp(s - jnp.max(s, axis=-1, keepdims=True))
        p = e / jnp.sum(e, axis=-1, keepdims=True)
        outs.append(_dot(p.astype(BF16), vh).astype(BF16))
    o = jnp.concatenate(outs, axis=1)
    o_ref[...] = h_ref[...] + _dot(o, wo_ref[...])


def cross_attention(h, q, kv, wo, *, tq):
    b, s, d = h.shape
    n_mem = kv.shape[1]
    hspec = pl.BlockSpec((None, tq, d), lambda bi, i: (bi, i, 0))
    return pl.pallas_call(
        _cross_attn_kernel,
        grid=(b, s // tq),
        in_specs=[hspec, hspec,
                  pl.BlockSpec((None, n_mem, 2 * d), lambda bi, i: (bi, 0, 0)),
                  pl.BlockSpec((d, d), lambda bi, i: (0, 0))],
        out_specs=hspec,
        out_shape=jax.ShapeDtypeStruct((b, s, d), F32),
        compiler_params=_cparams(("parallel", "parallel")),
        name="cross_attention",
    )(h, q, kv, wo)


ROUTER_W = LANES
MOE_TB = 256


def _router_kernel(h_ref, g_ref, wr_ref, xn_ref, eid_ref, gate_ref):
    xn = _rms(h_ref[...], g_ref[...])
    xn_ref[...] = xn
    logits = jnp.dot(xn, wr_ref[...], preferred_element_type=F32, precision=lax.Precision.HIGHEST)
    lane = lax.broadcasted_iota(jnp.int32, logits.shape, 1)
    big = jnp.int32(ROUTER_W)

    def first_max(mask):
        m = jnp.max(jnp.where(mask, logits, NEG_INF), axis=-1, keepdims=True)
        idx = jnp.min(jnp.where(mask & (logits == m), lane, big), axis=-1, keepdims=True)
        return m, idx

    gmask = lane < N_GROUPS
    gmax, gidx = first_max(gmask)
    g_w = 1.0 / jnp.sum(jnp.where(gmask, jnp.exp(logits - gmax), 0.0), axis=-1, keepdims=True)
    e0 = N_GROUPS + gidx * EXPERTS_PER_GROUP
    emask = (lane >= e0) & (lane < e0 + EXPERTS_PER_GROUP)
    m1, i1 = first_max(emask)
    m2, i2 = first_max(emask & (lane != i1))
    e2 = jnp.exp(m2 - m1)
    w1 = 1.0 / (1.0 + e2)
    w2 = e2 / (1.0 + e2)
    eid_ref[...] = jnp.where(lane == 0, i1 - N_GROUPS, jnp.where(lane == 1, i2 - N_GROUPS, 0))
    gate_ref[...] = jnp.where(lane == 0, g_w * w1, jnp.where(lane == 1, g_w * w2, 0.0))


def moe_router(h, g, wr_group, wr_expert, *, tm):
    m, d = h.shape
    wr = jnp.concatenate([wr_group, wr_expert], axis=1).astype(F32)
    wr = jnp.pad(wr, ((0, 0), (0, ROUTER_W - wr.shape[1])))
    hspec = pl.BlockSpec((tm, d), lambda i: (i, 0))
    lspec = pl.BlockSpec((tm, ROUTER_W), lambda i: (i, 0))
    return pl.pallas_call(
        _router_kernel,
        grid=(m // tm,),
        in_specs=[hspec, pl.BlockSpec((1, d), lambda i: (0, 0)),
                  pl.BlockSpec((d, ROUTER_W), lambda i: (0, 0))],
        out_specs=[hspec, lspec, lspec],
        out_shape=[jax.ShapeDtypeStruct((m, d), F32),
                   jax.ShapeDtypeStruct((m, ROUTER_W), jnp.int32),
                   jax.ShapeDtypeStruct((m, ROUTER_W), F32)],
        compiler_params=_cparams(("parallel",)),
        name="moe_router",
    )(h, g.reshape(1, d), wr)


GATHER_UNROLL = 8


def _start_row_gather(idx_ref, base, src_hbm, dst, sem, n_rows):
    def issue(half, c):
        for prio in range(2):
            r = 2 * half + prio
            pltpu.make_async_copy(src_hbm.at[pl.ds(idx_ref[base + r], 1)], dst.at[pl.ds(r, 1)],
                                  sem).start(priority=prio)
        return c

    lax.fori_loop(0, n_rows // 2, issue, 0, unroll=GATHER_UNROLL // 2)


def _wait_row_gather(src_hbm, dst, sem, n_rows):
    pltpu.make_async_copy(src_hbm.at[pl.ds(0, n_rows)], dst, sem).wait()


def _moe_expert_kernel(tok_ref, exp_ref, nvb_ref, x_hbm, gate_ref, w1_ref, w3_ref, w2_ref, o_ref,
                       xbuf, sem, *, tb):
    i = pl.program_id(0)
    nvb = nvb_ref[0]

    @pl.when(i == 0)
    def _():
        _start_row_gather(tok_ref, 0, x_hbm, xbuf.at[0], sem.at[0], tb)

    @pl.when(i + 1 < nvb)
    def _():
        nxt = (i + 1) % 2
        _start_row_gather(tok_ref, (i + 1) * tb, x_hbm, xbuf.at[nxt], sem.at[nxt], tb)

    @pl.when(i < nvb)
    def _():
        cur = i % 2
        _wait_row_gather(x_hbm, xbuf.at[cur], sem.at[cur], tb)
        x = xbuf[cur].astype(BF16)
        h1 = _dot(x, w1_ref[...].astype(BF16))
        h3 = _dot(x, w3_ref[...].astype(BF16))
        act = (jax.nn.silu(h1) * h3).astype(BF16)
        o_ref[...] = _dot(act, w2_ref[...].astype(BF16)) * gate_ref[...]

    @pl.when(i >= nvb)
    def _():
        o_ref[...] = jnp.zeros_like(o_ref)


def moe_experts(xn, slot_tok, slot_gate, blk_exp, n_valid, w1, w3, w2, layer, *, tb):
    cap = slot_tok.shape[0]
    d = xn.shape[1]
    nblk = cap // tb
    wspec = lambda shape: pl.BlockSpec((None, None) + shape, lambda i, tok, ex, nv: (layer, ex[i], 0, 0))
    grid_spec = pltpu.PrefetchScalarGridSpec(
        num_scalar_prefetch=3,
        grid=(nblk,),
        in_specs=[pl.BlockSpec(memory_space=pl.ANY),
                  pl.BlockSpec((tb, 1), lambda i, tok, ex, nv: (i, 0)),
                  wspec((d, D_EXPERT)), wspec((d, D_EXPERT)), wspec((D_EXPERT, d))],
        out_specs=pl.BlockSpec((tb, d), lambda i, tok, ex, nv: (i, 0)),
        scratch_shapes=[pltpu.VMEM((2, tb, d), F32), pltpu.SemaphoreType.DMA((2,))],
    )
    return pl.pallas_call(
        functools.partial(_moe_expert_kernel, tb=tb),
        grid_spec=grid_spec,
        out_shape=jax.ShapeDtypeStruct((cap, d), F32),
        compiler_params=_cparams(("arbitrary",)),
        name="moe_experts",
    )(slot_tok, blk_exp, n_valid, xn, slot_gate.reshape(cap, 1), w1, w3, w2)


def _moe_combine_kernel(pos_ref, ys_hbm, h_ref, o_ref, buf, sem, *, tm, n_tiles):
    i = pl.program_id(0)
    n = 2 * tm

    @pl.when(i == 0)
    def _():
        _start_row_gather(pos_ref, 0, ys_hbm, buf.at[0], sem.at[0], n)

    @pl.when(i + 1 < n_tiles)
    def _():
        nxt = (i + 1) % 2
        _start_row_gather(pos_ref, (i + 1) * n, ys_hbm, buf.at[nxt], sem.at[nxt], n)

    cur = i % 2
    _wait_row_gather(ys_hbm, buf.at[cur], sem.at[cur], n)
    o_ref[...] = h_ref[...] + buf[cur, 0:tm, :] + buf[cur, tm:n, :]


def moe_combine(h, ys, pos, *, tm):
    m, d = h.shape
    hspec = pl.BlockSpec((tm, d), lambda i, p: (i, 0))
    grid_spec = pltpu.PrefetchScalarGridSpec(
        num_scalar_prefetch=1,
        grid=(m // tm,),
        in_specs=[pl.BlockSpec(memory_space=pl.ANY), hspec],
        out_specs=hspec,
        scratch_shapes=[pltpu.VMEM((2, 2 * tm, d), F32), pltpu.SemaphoreType.DMA((2,))],
    )
    return pl.pallas_call(
        functools.partial(_moe_combine_kernel, tm=tm, n_tiles=m // tm),
        grid_spec=grid_spec,
        out_shape=jax.ShapeDtypeStruct((m, d), F32),
        compiler_params=_cparams(("arbitrary",)),
        name="moe_combine",
    )(pos, ys, h)


def _moe_slots(eid, gate, tb):
    t = eid.shape[0]
    n_assign = t * TOP_K
    eid = eid.reshape(-1)
    gate = gate.reshape(-1)
    order = jnp.argsort(eid).astype(jnp.int32)
    rank = jnp.argsort(order).astype(jnp.int32)
    se = eid[order]
    experts = jnp.arange(N_EXPERTS, dtype=jnp.int32)
    start = jnp.searchsorted(se, experts, side='left').astype(jnp.int32)
    counts = jnp.searchsorted(se, experts, side='right').astype(jnp.int32) - start
    padded = ((counts + tb - 1) // tb) * tb
    pad_end = jnp.cumsum(padded)
    pad_start = pad_end - padded
    cap = n_assign + N_EXPERTS * tb
    nblk = cap // tb
    n_valid = (pad_end[-1] // tb).astype(jnp.int32)
    blk = jnp.minimum(jnp.arange(nblk, dtype=jnp.int32), n_valid - 1) * tb
    blk_exp = jnp.minimum(jnp.searchsorted(pad_end, blk, side='right'), N_EXPERTS - 1).astype(jnp.int32)
    slot = jnp.arange(cap, dtype=jnp.int32)
    e_s = jnp.repeat(blk_exp, tb)
    off = slot - pad_start[e_s]
    live = (off < counts[e_s]) & (slot < pad_end[-1])
    src = order[jnp.clip(start[e_s] + off, 0, n_assign - 1)]
    slot_tok = jnp.where(live, src // TOP_K, 0).astype(jnp.int32)
    slot_gate = jnp.where(live, gate[src], 0.0)
    e_a = eid
    pos = (pad_start[e_a] + rank - start[e_a]).astype(jnp.int32).reshape(t, TOP_K)
    return slot_tok, slot_gate, blk_exp, n_valid.reshape(1), pos


def hier_moe(h, g, wr_group, wr_expert, w1, w3, w2, layer, *, tm_router=512, tm_combine=128, tb=MOE_TB):
    m, d = h.shape
    xn, eid, gate = moe_router(h, g, wr_group, wr_expert, tm=tm_router)
    slot_tok, slot_gate, blk_exp, n_valid, pos = _moe_slots(eid[:, :TOP_K], gate[:, :TOP_K], tb)
    ys = moe_experts(xn, slot_tok, slot_gate, blk_exp, n_valid, w1, w3, w2, layer, tb=tb)
    pos = pos.reshape(m // tm_combine, tm_combine, TOP_K).transpose(0, 2, 1).reshape(-1)
    return moe_combine(h, ys, pos, tm=tm_combine)


def _final_norm_kernel(h_ref, g_ref, o_ref):
    o_ref[...] = _rms(h_ref[...], g_ref[...])


def final_norm(h, g, *, tm):
    m, d = h.shape
    hspec = pl.BlockSpec((tm, d), lambda i: (i, 0))
    return pl.pallas_call(
        _final_norm_kernel,
        grid=(m // tm,),
        in_specs=[hspec, pl.BlockSpec((1, d), lambda i: (0, 0))],
        out_specs=hspec,
        out_shape=jax.ShapeDtypeStruct((m, d), F32),
        compiler_params=_cparams(("parallel",)),
        name="final_norm",
    )(h, g.reshape(1, d))


def _w_in_layout(w_in):
    wa = w_in[:, :IN_A]
    wb = w_in[:, IN_A:IN_A + IN_B]
    wc = w_in[:, IN_A + IN_B:IN_A + IN_B + IN_C]
    wd = w_in[:, IN_A + IN_B + IN_C:]
    wb = jnp.pad(wb, ((0, 0), (0, IN_B_PAD - IN_B)))
    return jnp.concatenate([wb, wa, wc, wd], axis=1).astype(BF16)


def kernel(x, mem, norm_mix, w_in, w_out, diff_lambda, diff_subln, rwkv_mu, rwkv_w0, rwkv_w2,
           rwkv_a0, rwkv_a2, rwkv_g2, rwkv_kk, rwkv_ka, rwkv_rk, rwkv_lnx_w, rwkv_lnx_b,
           s5_a_re, s5_a_im, s5_log_dt, s5_b_re, s5_b_im, s5_c_re, s5_c_im, s5_d, s5_glu_w,
           s5_glu_b, mix_out_norm, norm_cross, norm_mem, xa_wq, xa_wkv, xa_wo, norm_moe,
           router_group, router_expert, moe_w1, moe_w3, moe_w2, norm_final):
    b, s, d = x.shape
    m = b * s
    n_mem = mem.shape[1]
    depth = w_in.shape[0]
    h = x.reshape(m, d)
    mem2 = mem.reshape(b * n_mem, d)
    rope_a = _rope_tables(s, A_QKDIM, MIX)
    rope_c = _rope_tables(s, C_HDIM, MIX)
    for l in range(depth):
        z = norm_matmul(h, norm_mix[l], _w_in_layout(w_in[l]), tm=1024, tn=Z_W // 4, out_dtype=F32)
        z = z.reshape(b, s, Z_W)
        qa, ka, v1a = rope_qkv(z, Z_A, A_QKDIM, tm=512, ones_cols=A_VDIM, tables=rope_a)
        oa = diff_attention(qa, ka, v1a, diff_lambda[l], diff_subln[l], l, tq=512)
        ob = rwkv7_bidir(z, rwkv_mu[l], rwkv_w0[l], rwkv_w2[l], rwkv_a0[l], rwkv_a2[l], rwkv_g2[l],
                         rwkv_kk[l], rwkv_ka[l], rwkv_rk[l], rwkv_lnx_w[l], rwkv_lnx_b[l])
        qc, kc, vc = rope_qkv(z, Z_C, C_HDIM, tm=512, tables=rope_c)
        oc = dilated_attention(qc, kc, vc, tq=256)
        od = s5_bidir(z, s5_a_re[l], s5_a_im[l], s5_log_dt[l], s5_b_re[l], s5_b_im[l],
                      s5_c_re[l], s5_c_im[l], s5_d[l], s5_glu_w[l], s5_glu_b[l])
        h = mix_out(h, oa.reshape(m, MIX), ob.reshape(m, MIX), oc.reshape(m, MIX), od.reshape(m, MIX),
                    mix_out_norm[l, 0], mix_out_norm[l, 1], w_out[l].astype(BF16), tm=512, tn=512)
        q = norm_matmul(h, norm_cross[l], xa_wq[l].astype(BF16), tm=1024, tn=512, out_dtype=BF16)
        kv = norm_matmul(mem2, norm_mem[l], xa_wkv[l].astype(BF16), tm=b * n_mem, tn=512, out_dtype=BF16)
        h = cross_attention(h.reshape(b, s, d), q.reshape(b, s, d), kv.reshape(b, n_mem, 2 * d),
                            xa_wo[l].astype(BF16), tq=256).reshape(m, d)
        h = hier_moe(h, norm_moe[l], router_group[l], router_expert[l], moe_w1, moe_w3, moe_w2, l)
    return final_norm(h, norm_final, tm=512).reshape(b, s, d)
```

```python
import functools
import math

import numpy as np
import jax
import jax.numpy as jnp
from jax import lax
from jax.experimental import pallas as pl
from jax.experimental.pallas import tpu as pltpu

F32 = jnp.float32
BF16 = jnp.bfloat16

D_MODEL = 2048
MIX = D_MODEL // 4
A_HEADS = 4
A_VDIM = MIX // A_HEADS
A_QKDIM = A_VDIM // 2
B_HDIM = 64
B_HEADS = MIX // B_HDIM
B_LORA = 64
B_GATE_LORA = 128
LN_X_EPS = 64e-5
C_HEADS = 4
C_HDIM = MIX // C_HEADS
C_PATTERNS = ((128, 1), (512, 4), (2048, 16))
D_GSIZE = 16
D_GROUPS = MIX // D_GSIZE
D_STATE = 64
IN_A = 3 * MIX
IN_B = 3 * MIX + 4 * B_LORA + B_GATE_LORA
IN_B_PAD = 2048
IN_C = 3 * MIX
IN_D = MIX
XA_HEADS = 4
XA_HDIM = D_MODEL // XA_HEADS
N_GROUPS = 4
EXPERTS_PER_GROUP = 8
N_EXPERTS = N_GROUPS * EXPERTS_PER_GROUP
TOP_K = 2
D_EXPERT = D_MODEL // 4
ROPE_THETA = 10000.0
RMS_EPS = 1e-6
NEG_INF = -1e30

Z_B = 0
Z_A = IN_B_PAD
Z_C = Z_A + IN_A
Z_D = Z_C + IN_C
Z_W = Z_D + IN_D

LANES = 128
VMEM_LIMIT = 56 * 1024 * 1024


def _cparams(sem, vmem=VMEM_LIMIT):
    return pltpu.CompilerParams(dimension_semantics=sem, vmem_limit_bytes=vmem)


def _rms(x, g):
    return x * lax.rsqrt(jnp.mean(x * x, axis=-1, keepdims=True) + RMS_EPS) * g


def _dot(a, b):
    return jnp.dot(a, b, preferred_element_type=F32)


def _dot_t(a, b):
    return lax.dot_general(a, b, (((1,), (1,)), ((), ())), preferred_element_type=F32)


def _split3(x):
    hi = x.astype(BF16)
    r1 = x - hi.astype(F32)
    mid = r1.astype(BF16)
    lo = (r1 - mid.astype(F32)).astype(BF16)
    return hi, mid, lo


def _segsum(x, ones):
    hi, mid, lo = _split3(x)
    return _dot(hi, ones) + _dot(mid, ones) + _dot(lo, ones)


def _norm_matmul_kernel(x_ref, g_ref, w_ref, o_ref, xn_ref):
    @pl.when(pl.program_id(1) == 0)
    def _():
        xn_ref[...] = _rms(x_ref[...], g_ref[...]).astype(BF16)

    o_ref[...] = _dot(xn_ref[...], w_ref[...]).astype(o_ref.dtype)


def norm_matmul(x, g, w, *, tm, tn, out_dtype):
    m, k = x.shape
    n = w.shape[1]
    return pl.pallas_call(
        _norm_matmul_kernel,
        grid=(m // tm, n // tn),
        in_specs=[pl.BlockSpec((tm, k), lambda i, j: (i, 0)),
                  pl.BlockSpec((1, k), lambda i, j: (0, 0)),
                  pl.BlockSpec((k, tn), lambda i, j: (0, j))],
        out_specs=pl.BlockSpec((tm, tn), lambda i, j: (i, j)),
        out_shape=jax.ShapeDtypeStruct((m, n), out_dtype),
        scratch_shapes=[pltpu.VMEM((tm, k), BF16)],
        compiler_params=_cparams(("parallel", "arbitrary")),
        name="norm_matmul",
    )(x, g.reshape(1, k), w)


def _rope_tables(seq, dim, width):
    inv = 1.0 / (ROPE_THETA ** (jnp.arange(0, dim, 2, dtype=F32) / dim))
    ang = jnp.arange(seq, dtype=F32)[:, None] * inv[None, :]
    cos, sin = jnp.cos(ang), jnp.sin(ang)
    cos = jnp.concatenate([cos, cos], axis=-1)
    sin = jnp.concatenate([-sin, sin], axis=-1)
    reps = width // dim
    return jnp.tile(cos, (1, reps)), jnp.tile(sin, (1, reps))


def _rope_kernel(q_ref, k_ref, v_ref, cos_ref, sin_ref, qo_ref, ko_ref, vo_ref, *, half, scale, ones_cols):
    cos = cos_ref[...]
    sin = sin_ref[...]
    width = cos.shape[1]
    lane = lax.broadcasted_iota(jnp.int32, cos.shape, 1)
    first = (lane % (2 * half)) < half

    def rot(x):
        ahead = pltpu.roll(x, width - half, axis=1)
        behind = pltpu.roll(x, half, axis=1)
        return x * cos + jnp.where(first, ahead, behind) * sin

    qo_ref[...] = (rot(q_ref[...]) * scale).astype(BF16)
    ko_ref[...] = rot(k_ref[...]).astype(BF16)
    v = v_ref[...].astype(BF16)
    if ones_cols is None:
        vo_ref[...] = v
    else:
        ones = jnp.ones((v.shape[0], ones_cols), BF16)
        parts = []
        for hd in range(v.shape[1] // ones_cols):
            parts += [v[:, hd * ones_cols:(hd + 1) * ones_cols], ones]
        vo_ref[...] = jnp.concatenate(parts, axis=1)


def rope_qkv(z, col0, head_dim, *, tm, ones_cols=None, tables=None):
    b, s, _ = z.shape
    cos, sin = tables if tables is not None else _rope_tables(s, head_dim, MIX)
    cb = col0 // MIX
    zspec = lambda off: pl.BlockSpec((None, tm, MIX), lambda bi, i, off=off: (bi, i, cb + off))
    tspec = pl.BlockSpec((tm, MIX), lambda bi, i: (i, 0))
    ospec = pl.BlockSpec((None, tm, MIX), lambda bi, i: (bi, i, 0))
    oshape = jax.ShapeDtypeStruct((b, s, MIX), BF16)
    if ones_cols is None:
        vspec, vshape = ospec, oshape
    else:
        vspec = pl.BlockSpec((None, tm, 2 * MIX), lambda bi, i: (bi, i, 0))
        vshape = jax.ShapeDtypeStruct((b, s, 2 * MIX), BF16)
    return pl.pallas_call(
        functools.partial(_rope_kernel, half=head_dim // 2, scale=head_dim ** -0.5, ones_cols=ones_cols),
        grid=(b, s // tm),
        in_specs=[zspec(0), zspec(1), zspec(2), tspec, tspec],
        out_specs=[ospec, ospec, vspec],
        out_shape=[oshape, oshape, vshape],
        compiler_params=_cparams(("parallel", "parallel")),
        name="rope_qkv",
    )(z, z, z, cos, sin)


def _diff_attn_kernel(lam_ref, g_ref, q_ref, k_ref, v1_ref, o_ref, *, lam_init):
    lv = lam_ref[...]
    lam = (jnp.exp(jnp.sum(lv[0:1] * lv[1:2], axis=-1, keepdims=True))
           - jnp.exp(jnp.sum(lv[2:3] * lv[3:4], axis=-1, keepdims=True)) + lam_init)
    q = q_ref[...]
    k = k_ref[...]
    v1 = v1_ref[...]
    lane = lax.broadcasted_iota(jnp.int32, q.shape, 1)
    zero = jnp.zeros_like(q)

    def branch(qm):
        s = _dot_t(qm, k)
        p = jnp.exp((s - jnp.max(s, axis=-1, keepdims=True)).astype(BF16))
        acc = _dot(p, v1)
        return acc[:, :A_VDIM] / acc[:, A_VDIM:]

    o = branch(jnp.where(lane < A_QKDIM, q, zero)) - lam * branch(jnp.where(lane >= A_QKDIM, q, zero))
    o_ref[...] = _rms(o, g_ref[...]) * (1.0 - lam_init)


def diff_attention(q, k, v1, lam_vecs, subln_g, layer_idx, *, tq):
    b, s, _ = q.shape
    lam_init = 0.8 - 0.6 * math.exp(-0.3 * layer_idx)
    qspec = pl.BlockSpec((None, tq, A_VDIM), lambda bi, h, i: (bi, i, h))
    return pl.pallas_call(
        functools.partial(_diff_attn_kernel, lam_init=lam_init),
        grid=(b, A_HEADS, s // tq),
        in_specs=[pl.BlockSpec((4, A_QKDIM), lambda bi, h, i: (0, 0)),
                  pl.BlockSpec((1, A_VDIM), lambda bi, h, i: (0, 0)),
                  qspec,
                  pl.BlockSpec((None, s, A_VDIM), lambda bi, h, i: (bi, 0, h)),
                  pl.BlockSpec((None, s, 2 * A_VDIM), lambda bi, h, i: (bi, 0, h))],
        out_specs=qspec,
        out_shape=jax.ShapeDtypeStruct((b, s, MIX), F32),
        compiler_params=_cparams(("parallel", "parallel", "arbitrary")),
        name="diff_attention",
    )(lam_vecs, subln_g.reshape(1, A_VDIM), q, k, v1)


C_REACH = max(w // 2 for w, _ in C_PATTERNS)


def _dilated_bias_table(tq, window):
    n_delta = (window - tq) // tq + 1
    i = np.arange(tq)[None, :, None]
    j = np.arange(window)[None, None, :]
    n = np.arange(n_delta)[:, None, None]
    d = j - i - n * tq
    count = np.zeros(d.shape, np.int32)
    for w, dil in C_PATTERNS:
        count += ((np.abs(d) <= w // 2) & (d % dil == 0)).astype(np.int32)
    bias = np.where(count > 0, np.log(np.maximum(count, 1)), NEG_INF)
    return jnp.asarray(bias, F32)


def _dilated_attn_kernel(bias_ref, q_ref, k_ref, v_ref, o_ref, *, tq, window, seq):
    start = pl.program_id(2) * tq
    ws = pl.multiple_of(jnp.clip(start - C_REACH, 0, seq - window), tq)
    kw = k_ref[pl.ds(ws, window), :]
    vw = v_ref[pl.ds(ws, window), :]
    s = _dot_t(q_ref[...], kw) + bias_ref[...]
    e = jnp.exp(s - jnp.max(s, axis=-1, keepdims=True))
    den = jnp.sum(e, axis=-1, keepdims=True)
    o_ref[...] = _dot(e.astype(BF16), vw) / den


def dilated_attention(q, k, v, *, tq=128):
    b, s, _ = q.shape
    window = tq + 2 * C_REACH
    assert s >= window and s % tq == 0 and C_REACH % tq == 0
    bias = _dilated_bias_table(tq, window)

    def bias_map(bi, h, i):
        start = i * tq
        ws = jnp.clip(start - C_REACH, 0, s - window)
        return ((start - ws) // tq, 0, 0)

    qspec = pl.BlockSpec((None, tq, C_HDIM), lambda bi, h, i: (bi, i, h))
    kspec = pl.BlockSpec((None, s, C_HDIM), lambda bi, h, i: (bi, 0, h))
    return pl.pallas_call(
        functools.partial(_dilated_attn_kernel, tq=tq, window=window, seq=s),
        grid=(b, C_HEADS, s // tq),
        in_specs=[pl.BlockSpec((None, tq, window), bias_map), qspec, kspec, kspec],
        out_specs=qspec,
        out_shape=jax.ShapeDtypeStruct((b, s, MIX), F32),
        compiler_params=_cparams(("parallel", "parallel", "arbitrary")),
        name="dilated_attention",
    )(bias, q, k, v)


S5_BLOCKS = 4
S5_GPB = D_GROUPS // S5_BLOCKS
S5_CH = S5_GPB * D_GSIZE
S5_ST = S5_GPB * D_STATE
S5_ROWS = 8


def _s5_scan_kernel(u_ref, bm_ref, cm_ref, pr_ref, pi_ref, y_ref, cr_ref, ci_ref, *, tc, reverse):
    @pl.when(pl.program_id(2) == 0)
    def _():
        cr_ref[...] = jnp.zeros_like(cr_ref)
        ci_ref[...] = jnp.zeros_like(ci_ref)

    bu = _dot(u_ref[...].astype(BF16), bm_ref[...])
    xr = bu[:, :S5_ST]
    xi = bu[:, S5_ST:]
    row = lax.broadcasted_iota(jnp.int32, xr.shape, 0) & (S5_ROWS - 1)
    for lvl in range(S5_ROWS.bit_length() - 1):
        sh = 1 << lvl
        prow = (S5_ROWS - sh) if reverse else (sh - 1)
        ar = pr_ref[prow:prow + 1, :]
        ai = pi_ref[prow:prow + 1, :]
        if reverse:
            keep = row < (S5_ROWS - sh)
            sr = jnp.where(keep, pltpu.roll(xr, tc - sh, axis=0), 0.0)
            si = jnp.where(keep, pltpu.roll(xi, tc - sh, axis=0), 0.0)
        else:
            keep = row >= sh
            sr = jnp.where(keep, pltpu.roll(xr, sh, axis=0), 0.0)
            si = jnp.where(keep, pltpu.roll(xi, sh, axis=0), 0.0)
        xr, xi = xr + ar * sr - ai * si, xi + ar * si + ai * sr
    pr = pr_ref[...]
    pi = pi_ref[...]
    cr = cr_ref[...]
    ci = ci_ref[...]
    n_groups = tc // S5_ROWS
    last = 0 if reverse else S5_ROWS - 1
    out_r = [None] * n_groups
    out_i = [None] * n_groups
    for gi in (range(n_groups - 1, -1, -1) if reverse else range(n_groups)):
        rows = slice(gi * S5_ROWS, (gi + 1) * S5_ROWS)
        gr = xr[rows] + pr * cr - pi * ci
        gim = xi[rows] + pr * ci + pi * cr
        cr = gr[last:last + 1, :]
        ci = gim[last:last + 1, :]
        out_r[gi] = gr
        out_i[gi] = gim
    cr_ref[...] = cr
    ci_ref[...] = ci
    x = jnp.concatenate([jnp.concatenate(out_r, axis=0), jnp.concatenate(out_i, axis=0)], axis=1)
    y_ref[...] = _dot(x.astype(BF16), cm_ref[...])


def _s5_prepare(direction, a_re, a_im, log_dt, b_re, b_im, c_re, c_im):
    lr = jnp.minimum(a_re[direction].astype(F32), -1e-4)
    li = a_im[direction].astype(F32)
    dt = jnp.exp(log_dt[direction].astype(F32))[:, None]
    mag = jnp.exp(dt * lr)
    abr, abi = mag * jnp.cos(dt * li), mag * jnp.sin(dt * li)
    den = lr * lr + li * li
    qr, qi = lr / den, -li / den
    fr = (abr - 1.0) * qr - abi * qi
    fi = (abr - 1.0) * qi + abi * qr
    b_re, b_im = b_re.astype(F32), b_im.astype(F32)
    bbr = fr[..., None] * b_re - fi[..., None] * b_im
    bbi = fr[..., None] * b_im + fi[..., None] * b_re
    eye = jnp.eye(S5_GPB, dtype=F32)

    def in_block(m):
        m = m.reshape(S5_BLOCKS, S5_GPB, D_STATE, D_GSIZE)
        return jnp.einsum('kgnc,gh->kgchn', m, eye).reshape(S5_BLOCKS, S5_CH, S5_ST)

    def out_block(m):
        m = m.reshape(S5_BLOCKS, S5_GPB, D_GSIZE, D_STATE)
        return jnp.einsum('kgcn,gh->kgnhc', m, eye).reshape(S5_BLOCKS, S5_ST, S5_CH)

    bmat = jnp.concatenate([in_block(bbr), in_block(bbi)], axis=2).astype(BF16)
    cmat = jnp.concatenate([out_block(c_re.astype(F32)), -out_block(c_im.astype(F32))],
                           axis=1).astype(BF16)
    steps = jnp.arange(1, S5_ROWS + 1, dtype=F32)[:, None, None]
    pmag = jnp.exp(steps * (dt * lr)[None])
    ang = steps * (dt * li)[None]
    p_re = (pmag * jnp.cos(ang)).reshape(S5_ROWS, D_GROUPS * D_STATE)
    p_im = (pmag * jnp.sin(ang)).reshape(S5_ROWS, D_GROUPS * D_STATE)
    if direction == 1:
        p_re, p_im = p_re[::-1], p_im[::-1]
    return bmat, cmat, p_re, p_im


def s5_scan(z, direction, params, *, tc):
    b, s, _ = z.shape
    nch = s // tc
    bmat, cmat, p_re, p_im = _s5_prepare(direction, *params)
    reverse = direction == 1
    tmap = (lambda c: nch - 1 - c) if reverse else (lambda c: c)
    ub = Z_D // S5_CH
    return pl.pallas_call(
        functools.partial(_s5_scan_kernel, tc=tc, reverse=reverse),
        grid=(b, S5_BLOCKS, nch),
        in_specs=[pl.BlockSpec((None, tc, S5_CH), lambda bi, k, c: (bi, tmap(c), ub + k)),
                  pl.BlockSpec((None, S5_CH, 2 * S5_ST), lambda bi, k, c: (k, 0, 0)),
                  pl.BlockSpec((None, 2 * S5_ST, S5_CH), lambda bi, k, c: (k, 0, 0)),
                  pl.BlockSpec((S5_ROWS, S5_ST), lambda bi, k, c: (0, k)),
                  pl.BlockSpec((S5_ROWS, S5_ST), lambda bi, k, c: (0, k))],
        out_specs=pl.BlockSpec((None, tc, S5_CH), lambda bi, k, c: (bi, tmap(c), k)),
        out_shape=jax.ShapeDtypeStruct((b, s, MIX), F32),
        scratch_shapes=[pltpu.VMEM((1, S5_ST), F32), pltpu.VMEM((1, S5_ST), F32)],
        compiler_params=_cparams(("parallel", "parallel", "arbitrary")),
        name="s5_scan",
    )(z, bmat, cmat, p_re, p_im)


def _s5_out_kernel(yf_ref, yb_ref, u_ref, d_ref, w_ref, b_ref, o_ref):
    y = yf_ref[...] + yb_ref[...] + d_ref[...] * u_ref[...]
    gl = jax.nn.gelu(y)
    gate = jax.nn.sigmoid(_dot(gl.astype(BF16), w_ref[...]) + b_ref[...])
    o_ref[...] = gl * gate


def s5_output(yf, yb, z, d_skip, glu_w, glu_b, *, tm):
    b, s, _ = z.shape
    yspec = pl.BlockSpec((None, tm, MIX), lambda bi, i: (bi, i, 0))
    vspec = pl.BlockSpec((1, MIX), lambda bi, i: (0, 0))
    return pl.pallas_call(
        _s5_out_kernel,
        grid=(b, s // tm),
        in_specs=[yspec, yspec,
                  pl.BlockSpec((None, tm, MIX), lambda bi, i: (bi, i, Z_D // MIX)),
                  vspec, pl.BlockSpec((MIX, MIX), lambda bi, i: (0, 0)), vspec],
        out_specs=yspec,
        out_shape=jax.ShapeDtypeStruct((b, s, MIX), F32),
        compiler_params=_cparams(("parallel", "parallel")),
        name="s5_output",
    )(yf, yb, z, d_skip.reshape(1, MIX), glu_w.astype(BF16), glu_b.reshape(1, MIX))


def s5_bidir(z, a_re, a_im, log_dt, b_re, b_im, c_re, c_im, d_skip, glu_w, glu_b, *, tc=1024, tm=512):
    params = (a_re, a_im, log_dt, b_re, b_im, c_re, c_im)
    yf = s5_scan(z, 0, params, tc=tc)
    yb = s5_scan(z, 1, params, tc=tc)
    return s5_output(yf, yb, z, d_skip, glu_w, glu_b, tm=tm)


def _softplus(y):
    return jnp.maximum(y, 0.0) + jnp.log(1.0 + jnp.exp(-jnp.abs(y)))


def _head_ones(width):
    seg = np.arange(width) // B_HDIM
    return jnp.asarray(seg[:, None] == seg[None, :], BF16)


def _rwkv_pre_kernel(z_ref, zp_ref, zn_ref, mu_ref, w0_ref, a0_ref, w2_ref, a2_ref, g2_ref,
                     kk_ref, ka_ref, rk_ref, ones_ref,
                     r_o, a_o, w0_o, w1_o, k0_o, k1_o, b0_o, b1_o, v_o, g_o, bonus_o,
                     *, tm, n_tiles):
    i = pl.program_id(1)
    z = z_ref[...]
    row = lax.broadcasted_iota(jnp.int32, z.shape, 0)
    prev_row = jnp.where(i > 0, zp_ref[7:8, :], 0.0)
    next_row = jnp.where(i < n_tiles - 1, zn_ref[0:1, :], 0.0)
    zp = jnp.where(row == 0, prev_row, pltpu.roll(z, 1, axis=0))
    zn = jnp.where(row == tm - 1, next_row, pltpu.roll(z, tm - 1, axis=0))
    xs = z + mu_ref[0:1, :] * (zp - z) + mu_ref[1:2, :] * (zn - z)
    r = xs[:, 0:MIX]
    k = xs[:, MIX:2 * MIX]
    v = xs[:, 2 * MIX:3 * MIX]
    c0 = 3 * MIX
    wd = xs[:, c0:c0 + 2 * B_LORA]
    ad = xs[:, c0 + 2 * B_LORA:c0 + 4 * B_LORA]
    gd = xs[:, c0 + 4 * B_LORA:c0 + 4 * B_LORA + B_GATE_LORA]
    lw = _dot(jnp.tanh(wd).astype(BF16), w2_ref[...])
    la = _dot(ad.astype(BF16), a2_ref[...])
    g_o[...] = _dot(jax.nn.sigmoid(gd).astype(BF16), g2_ref[...])
    ones = ones_ref[...]
    kk = k * kk_ref[...]
    kk = kk * lax.rsqrt(_segsum(kk * kk, ones) + 1e-12)
    ka = ka_ref[...]
    ksum = jnp.zeros_like(k)
    for d, (w_o, k_o, b_o) in enumerate(((w0_o, k0_o, b0_o), (w1_o, k1_o, b1_o))):
        cols = slice(d * MIX, (d + 1) * MIX)
        logw = -_softplus(-(w0_ref[d:d + 1, :] + lw[:, cols])) - 0.5
        w_o[...] = -jnp.exp(logw)
        a = jax.nn.sigmoid(a0_ref[d:d + 1, :] + la[:, cols])
        kmod = k * (1.0 + (a - 1.0) * ka)
        k_o[...] = kmod
        b_o[...] = kk * a
        ksum = ksum + kmod
    r_o[...] = r
    a_o[...] = -kk
    bonus_o[...] = _segsum(r * (0.5 * ksum) * rk_ref[...], ones) * v
    v_o[...] = v


def _block_diag2(m):
    z = jnp.zeros_like(m[0])
    return jnp.concatenate([jnp.concatenate([m[0], z], axis=1),
                            jnp.concatenate([z, m[1]], axis=1)], axis=0)


def rwkv_pre(z, mu, w0, w2, a0, a2, g2, k_k, k_a, r_k, *, tm):
    b, s, _ = z.shape
    n_tiles = s // tm
    mu_p = jnp.pad(mu.astype(F32), ((0, 0), (0, IN_B_PAD - IN_B)))
    row_spec = pl.BlockSpec((None, tm, MIX), lambda bi, i: (bi, i, 0))
    vec = lambda n, w: pl.BlockSpec((n, w), lambda bi, i: (0, 0))
    rows = jax.ShapeDtypeStruct((b, s, MIX), F32)
    hb = tm // 8
    return pl.pallas_call(
        functools.partial(_rwkv_pre_kernel, tm=tm, n_tiles=n_tiles),
        grid=(b, n_tiles),
        in_specs=[pl.BlockSpec((None, tm, IN_B_PAD), lambda bi, i: (bi, i, 0)),
                  pl.BlockSpec((None, 8, IN_B_PAD), lambda bi, i: (bi, jnp.maximum(i * hb - 1, 0), 0)),
                  pl.BlockSpec((None, 8, IN_B_PAD), lambda bi, i: (bi, jnp.minimum((i + 1) * hb, s // 8 - 1), 0)),
                  vec(2, IN_B_PAD), vec(2, MIX), vec(2, MIX),
                  vec(2 * B_LORA, 2 * MIX), vec(2 * B_LORA, 2 * MIX), vec(B_GATE_LORA, MIX),
                  vec(1, MIX), vec(1, MIX), vec(1, MIX), vec(MIX, MIX)],
        out_specs=[row_spec] * 11,
        out_shape=[rows] * 11,
        compiler_params=_cparams(("parallel", "parallel")),
        name="rwkv_pre",
    )(z, z, z, mu_p, w0.astype(F32), a0.astype(F32),
      _block_diag2(w2).astype(BF16), _block_diag2(a2).astype(BF16), g2.astype(BF16),
      k_k.reshape(1, MIX), k_a.reshape(1, MIX), r_k.reshape(1, MIX), _head_ones(MIX))


def _rwkv_scan_kernel(af_ref, rf_ref, vf_ref, lwf_ref, kf_ref, bf_ref,
                      ab_ref, rb_ref, vb_ref, lwb_ref, kb_ref, bb_ref,
                      trif_ref, trib_ref, yf_ref, yb_ref, h_ref, *, tc, nb):
    @pl.when(pl.program_id(0) == 0)
    def _():
        h_ref[...] = jnp.zeros_like(h_ref)

    row = lax.broadcasted_iota(jnp.int32, (tc, tc), 0)
    col = lax.broadcasted_iota(jnp.int32, (tc, tc), 1)
    eye = (row == col).astype(F32)
    row2 = lax.broadcasted_iota(jnp.int32, (tc, 2 * tc), 0)
    col2 = lax.broadcasted_iota(jnp.int32, (tc, 2 * tc), 1) & (tc - 1)
    zeros_tv = jnp.zeros((tc, B_HDIM), BF16)
    n_sq = tc.bit_length() - 1
    tdot = lambda x, y: lax.dot_general(x, y, (((0,), (0,)), ((), ())), preferred_element_type=F32)

    seqs = []
    for reverse, (a_ref, r_ref, v_ref, lw_ref, k_ref, b_ref, tri_ref, y_ref) in (
            (False, (af_ref, rf_ref, vf_ref, lwf_ref, kf_ref, bf_ref, trif_ref, yf_ref)),
            (True, (ab_ref, rb_ref, vb_ref, lwb_ref, kb_ref, bb_ref, trib_ref, yb_ref))):
        tri = tri_ref[...]
        before = (col2 > row2) if reverse else (col2 < row2)
        upto = (col2 >= row2) if reverse else (col2 <= row2)
        last = 0 if reverse else tc - 1
        for bi in range(nb):
            lw = lw_ref[bi]
            hi, mid, lo = _split3(lw)
            cl = _dot(tri, hi) + _dot(tri, mid) + _dot(tri, lo)
            tot = cl[last:last + 1, :]
            einv = jnp.exp(-cl)
            etot = jnp.exp(tot - cl)
            b_all = b_ref[bi]
            k_all = k_ref[bi]
            seqs.append(dict(
                a_t=a_ref[bi] * jnp.exp(cl - lw), r_t=r_ref[bi] * jnp.exp(cl),
                b_t=b_all * einv, k_t=k_all * einv, b_h=b_all * etot, k_h=k_all * etot,
                g_tot=jnp.exp(tot), v=v_ref[bi], before=before, upto=upto, y_ref=y_ref, bi=bi))

    inst = [(sq, slice(hd * B_HDIM, (hd + 1) * B_HDIM)) for sq in seqs for hd in range(B_HEADS)]
    ids = range(len(inst))
    at = [sq['a_t'][:, s].astype(BF16) for sq, s in inst]
    rt = [sq['r_t'][:, s].astype(BF16) for sq, s in inst]
    bk = [jnp.concatenate([sq['b_t'][:, s], sq['k_t'][:, s]], axis=0).astype(BF16) for sq, s in inst]
    bkh = [jnp.concatenate([sq['b_h'][:, s], sq['k_h'][:, s]], axis=0).astype(BF16) for sq, s in inst]
    vv = [sq['v'][:, s].astype(BF16) for sq, s in inst]
    g = [_dot_t(jnp.concatenate([at[n], rt[n]], axis=0), bk[n]) for n in ids]
    ga = [jnp.where(inst[n][0]['before'], g[n][:tc], 0.0) for n in ids]
    gr = [jnp.where(inst[n][0]['upto'], g[n][tc:], 0.0).astype(BF16) for n in ids]
    lkv = [_dot(ga[n].astype(BF16), jnp.concatenate([zeros_tv, vv[n]], axis=0)) for n in ids]
    lb = [ga[n][:, :tc].astype(BF16) for n in ids]
    pinv = [eye + ga[n][:, :tc] for n in ids]
    lb = [_dot(x, x).astype(BF16) for x in lb]
    for _ in range(1, n_sq - 1):
        prod = [_dot(jnp.concatenate([pinv[n].astype(BF16), lb[n]], axis=0), lb[n]) for n in ids]
        pinv = [pinv[n] + prod[n][:tc] for n in ids]
        lb = [prod[n][tc:].astype(BF16) for n in ids]
    pinv = [(pinv[n] + _dot(pinv[n].astype(BF16), lb[n])).astype(BF16) for n in ids]
    ah = [_dot(pinv[n], at[n]).astype(BF16) for n in ids]
    u0 = [_dot(pinv[n], lkv[n].astype(BF16)).astype(BF16) for n in ids]
    rhs = [jnp.concatenate([jnp.concatenate([ah[n], u0[n]], axis=1),
                            jnp.concatenate([zeros_tv, vv[n]], axis=1)], axis=0) for n in ids]
    ry = [_dot(gr[n], rhs[n]) for n in ids]
    ph = [tdot(bkh[n], rhs[n]) for n in ids]
    rh = [inst[n][0]['r_t'][:, inst[n][1]] + ry[n][:, :B_HDIM] for n in ids]
    phi = [eye * inst[n][0]['g_tot'][:, inst[n][1]] + ph[n][:, :B_HDIM] for n in ids]
    h0 = [h_ref[n].astype(BF16) for n in ids]
    fin = [_dot(jnp.concatenate([rh[n], phi[n]], axis=0).astype(BF16), h0[n]) for n in ids]
    yo = [fin[n][:tc] + ry[n][:, B_HDIM:] for n in ids]
    for si, sq in enumerate(seqs):
        sq['y_ref'][sq['bi']] = jnp.concatenate(yo[si * B_HEADS:(si + 1) * B_HEADS], axis=1)
    for n in ids:
        h_ref[n] = fin[n][tc:] + ph[n][:, B_HDIM:]


def rwkv_scan(a, r, v, lwf, kf, bvf, lwb, kb, bvb, *, tc=64):
    nb, s, _ = a.shape
    nch = s // tc
    fwd = pl.BlockSpec((nb, tc, MIX), lambda c: (0, c, 0))
    bwd = pl.BlockSpec((nb, tc, MIX), lambda c: (0, nch - 1 - c, 0))
    tspec = pl.BlockSpec((tc, tc), lambda c: (0, 0))
    t_idx = np.arange(tc)
    tri_f = jnp.asarray(t_idx[None, :] <= t_idx[:, None], BF16)
    tri_b = jnp.asarray(t_idx[None, :] >= t_idx[:, None], BF16)
    out = jax.ShapeDtypeStruct((nb, s, MIX), F32)
    return pl.pallas_call(
        functools.partial(_rwkv_scan_kernel, tc=tc, nb=nb),
        grid=(nch,),
        in_specs=[fwd] * 6 + [bwd] * 6 + [tspec, tspec],
        out_specs=[fwd, bwd],
        out_shape=[out, out],
        scratch_shapes=[pltpu.VMEM((2 * nb * B_HEADS, B_HDIM, B_HDIM), F32)],
        compiler_params=_cparams(("arbitrary",)),
        name="rwkv_scan",
    )(a, r, v, lwf, kf, bvf, a, r, v, lwb, kb, bvb, tri_f, tri_b)


def _rwkv_post_kernel(yf_ref, yb_ref, bonus_ref, g_ref, lw_ref, lb_ref, ones_ref, o_ref):
    y = yf_ref[...] + yb_ref[...]
    ones = ones_ref[...]
    mean = _segsum(y, ones) * (1.0 / B_HDIM)
    yc = y - mean
    var = _segsum(yc * yc, ones) * (1.0 / B_HDIM)
    yn = yc * lax.rsqrt(var + LN_X_EPS) * lw_ref[...] + lb_ref[...]
    o_ref[...] = (yn + bonus_ref[...]) * g_ref[...]


def rwkv_post(yf, yb, bonus, g, lnx_w, lnx_b, *, tm):
    b, s, _ = yf.shape
    row_spec = pl.BlockSpec((None, tm, MIX), lambda bi, i: (bi, i, 0))
    vec = pl.BlockSpec((1, MIX), lambda bi, i: (0, 0))
    return pl.pallas_call(
        _rwkv_post_kernel,
        grid=(b, s // tm),
        in_specs=[row_spec, row_spec, row_spec, row_spec, vec, vec,
                  pl.BlockSpec((MIX, MIX), lambda bi, i: (0, 0))],
        out_specs=row_spec,
        out_shape=jax.ShapeDtypeStruct((b, s, MIX), F32),
        compiler_params=_cparams(("parallel", "parallel")),
        name="rwkv_post",
    )(yf, yb, bonus, g, lnx_w.reshape(1, MIX), lnx_b.reshape(1, MIX), _head_ones(MIX))


def rwkv7_bidir(z, mu, w0, w2, a0, a2, g2, k_k, k_a, r_k, lnx_w, lnx_b, *, tm=256, tc=64):
    r, a, wf, wb, kf, kb, bf, bb, v, g, bonus = rwkv_pre(z, mu, w0, w2, a0, a2, g2, k_k, k_a, r_k, tm=tm)
    yf, yb = rwkv_scan(a, r, v, wf, kf, bf, wb, kb, bb, tc=tc)
    return rwkv_post(yf, yb, bonus, g, lnx_w, lnx_b, tm=tm)


def _mix_out_kernel(h_ref, oa_ref, ob_ref, oc_ref, od_ref, gc_ref, gd_ref, w_ref, o_ref, mix_ref):
    @pl.when(pl.program_id(1) == 0)
    def _():
        mix_ref[...] = jnp.concatenate(
            [oa_ref[...], ob_ref[...], _rms(oc_ref[...], gc_ref[...]), _rms(od_ref[...], gd_ref[...])],
            axis=1).astype(BF16)

    o_ref[...] = h_ref[...] + _dot(mix_ref[...], w_ref[...])


def mix_out(h, oa, ob, oc, od, gc, gd, w_out, *, tm, tn):
    m, d = h.shape
    mspec = pl.BlockSpec((tm, MIX), lambda i, j: (i, 0))
    vspec = pl.BlockSpec((1, MIX), lambda i, j: (0, 0))
    hspec = pl.BlockSpec((tm, tn), lambda i, j: (i, j))
    return pl.pallas_call(
        _mix_out_kernel,
        grid=(m // tm, d // tn),
        in_specs=[hspec, mspec, mspec, mspec, mspec, vspec, vspec,
                  pl.BlockSpec((4 * MIX, tn), lambda i, j: (0, j))],
        out_specs=hspec,
        out_shape=jax.ShapeDtypeStruct((m, d), F32),
        scratch_shapes=[pltpu.VMEM((tm, 4 * MIX), BF16)],
        compiler_params=_cparams(("parallel", "arbitrary")),
        name="mix_out",
    )(h, oa, ob, oc, od, gc.reshape(1, MIX), gd.reshape(1, MIX), w_out)


def _cross_attn_kernel(h_ref, q_ref, kv_ref, wo_ref, o_ref):
    q = q_ref[...]
    outs = []
    for hd in range(XA_HEADS):
        cols = slice(hd * XA_HDIM, (hd + 1) * XA_HDIM)
        kh = kv_ref[:, cols]
        vh = kv_ref[:, D_MODEL + hd * XA_HDIM:D_MODEL + (hd + 1) * XA_HDIM]
        s = _dot_t(q[:, cols], kh) * (XA_HDIM ** -0.5)
        e = jnp.exp(s - jnp.max(s, axis=-1, keepdims=True))
        p = e / jnp.sum(e, axis=-1, keepdims=True)
        outs.append(_dot(p.astype(BF16), vh).astype(BF16))
    o = jnp.concatenate(outs, axis=1)
    o_ref[...] = h_ref[...] + _dot(o, wo_ref[...])


def cross_attention(h, q, kv, wo, *, tq):
    b, s, d = h.shape
    n_mem = kv.shape[1]
    hspec = pl.BlockSpec((None, tq, d), lambda bi, i: (bi, i, 0))
    return pl.pallas_call(
        _cross_attn_kernel,
        grid=(b, s // tq),
        in_specs=[hspec, hspec,
                  pl.BlockSpec((None, n_mem, 2 * d), lambda bi, i: (bi, 0, 0)),
                  pl.BlockSpec((d, d), lambda bi, i: (0, 0))],
        out_specs=hspec,
        out_shape=jax.ShapeDtypeStruct((b, s, d), F32),
        compiler_params=_cparams(("parallel", "parallel")),
        name="cross_attention",
    )(h, q, kv, wo)


ROUTER_W = LANES
MOE_TB = 256


def _router_kernel(h_ref, g_ref, wr_ref, lt_ref, xn_ref, eid_ref, gate_ref, rank_ref, cnt_ref, run_ref):
    @pl.when(pl.program_id(0) == 0)
    def _():
        run_ref[...] = jnp.zeros_like(run_ref)

    xn = _rms(h_ref[...], g_ref[...])
    xn_ref[...] = xn
    logits = jnp.dot(xn, wr_ref[...], preferred_element_type=F32, precision=lax.Precision.HIGHEST)
    lane = lax.broadcasted_iota(jnp.int32, logits.shape, 1)
    big = jnp.int32(ROUTER_W)

    def first_max(mask):
        m = jnp.max(jnp.where(mask, logits, NEG_INF), axis=-1, keepdims=True)
        idx = jnp.min(jnp.where(mask & (logits == m), lane, big), axis=-1, keepdims=True)
        return m, idx

    gmask = lane < N_GROUPS
    gmax, gidx = first_max(gmask)
    g_w = 1.0 / jnp.sum(jnp.where(gmask, jnp.exp(logits - gmax), 0.0), axis=-1, keepdims=True)
    e0 = N_GROUPS + gidx * EXPERTS_PER_GROUP
    emask = (lane >= e0) & (lane < e0 + EXPERTS_PER_GROUP)
    m1, i1 = first_max(emask)
    m2, i2 = first_max(emask & (lane != i1))
    e2 = jnp.exp(m2 - m1)
    w1 = 1.0 / (1.0 + e2)
    w2 = e2 / (1.0 + e2)
    e1 = i1 - N_GROUPS
    e2i = i2 - N_GROUPS
    eid_ref[...] = jnp.where(lane == 0, e1, jnp.where(lane == 1, e2i, 0))
    gate_ref[...] = jnp.where(lane == 0, g_w * w1, jnp.where(lane == 1, g_w * w2, 0.0))
    hit1 = lane == e1
    hit2 = lane == e2i
    onehot = jnp.where(hit1 | hit2, 1.0, 0.0)
    earlier = run_ref[...] + _dot(lt_ref[...], onehot.astype(BF16))
    r1 = jnp.sum(jnp.where(hit1, earlier, 0.0), axis=-1, keepdims=True)
    r2 = jnp.sum(jnp.where(hit2, earlier, 0.0), axis=-1, keepdims=True)
    rank_ref[...] = jnp.where(lane == 0, r1, jnp.where(lane == 1, r2, 0.0)).astype(jnp.int32)
    run_ref[...] = run_ref[...] + jnp.sum(onehot, axis=0, keepdims=True)
    cnt_ref[...] = run_ref[...].astype(jnp.int32)


def moe_router(h, g, wr_group, wr_expert, *, tm):
    m, d = h.shape
    wr = jnp.concatenate([wr_group, wr_expert], axis=1).astype(F32)
    wr = jnp.pad(wr, ((0, 0), (0, ROUTER_W - wr.shape[1])))
    t_idx = np.arange(tm)
    lower = jnp.asarray(t_idx[None, :] < t_idx[:, None], BF16)
    hspec = pl.BlockSpec((tm, d), lambda i: (i, 0))
    lspec = pl.BlockSpec((tm, ROUTER_W), lambda i: (i, 0))
    cspec = pl.BlockSpec((1, ROUTER_W), lambda i: (0, 0))
    lane_i = jax.ShapeDtypeStruct((m, ROUTER_W), jnp.int32)
    return pl.pallas_call(
        _router_kernel,
        grid=(m // tm,),
        in_specs=[hspec, pl.BlockSpec((1, d), lambda i: (0, 0)),
                  pl.BlockSpec((d, ROUTER_W), lambda i: (0, 0)),
                  pl.BlockSpec((tm, tm), lambda i: (0, 0))],
        out_specs=[hspec, lspec, lspec, lspec, cspec],
        out_shape=[jax.ShapeDtypeStruct((m, d), F32), lane_i,
                   jax.ShapeDtypeStruct((m, ROUTER_W), F32), lane_i,
                   jax.ShapeDtypeStruct((1, ROUTER_W), jnp.int32)],
        scratch_shapes=[pltpu.VMEM((1, ROUTER_W), F32)],
        compiler_params=_cparams(("arbitrary",)),
        name="moe_router",
    )(h, g.reshape(1, d), wr, lower)


GATHER_UNROLL = 8


def _start_row_gather(row_of, src_hbm, dst, sem, n_rows, dst_row0=0):
    def issue(r, c):
        pltpu.make_async_copy(src_hbm.at[pl.ds(row_of(r), 1)], dst.at[pl.ds(dst_row0 + r, 1)], sem).start()
        return c

    lax.fori_loop(0, n_rows, issue, 0, unroll=GATHER_UNROLL)


def _wait_row_gather(src_hbm, dst, sem, n_rows):
    pltpu.make_async_copy(src_hbm.at[pl.ds(0, n_rows)], dst, sem).wait()


def _moe_expert_kernel(pos_ref, exp_ref, nvb_ref, x_hbm, w1_ref, w3_ref, w2_ref, o_ref,
                       slot_ref, xbuf, sem, *, tb, n_assign):
    i = pl.program_id(0)
    nvb = nvb_ref[0]

    @pl.when(i == 0)
    def _():
        def clear(s, c):
            slot_ref[s] = 0
            return c

        lax.fori_loop(0, slot_ref.shape[0], clear, 0, unroll=GATHER_UNROLL)

        def place(a, c):
            slot_ref[pos_ref[a]] = lax.shift_right_logical(a, TOP_K.bit_length() - 1)
            return c

        lax.fori_loop(0, n_assign, place, 0, unroll=GATHER_UNROLL)
        _start_row_gather(lambda r: slot_ref[r], x_hbm, xbuf.at[0], sem.at[0], tb)

    @pl.when(i + 1 < nvb)
    def _():
        nxt = (i + 1) & 1
        base = (i + 1) * tb
        _start_row_gather(lambda r: slot_ref[base + r], x_hbm, xbuf.at[nxt], sem.at[nxt], tb)

    @pl.when(i < nvb)
    def _():
        cur = i & 1
        _wait_row_gather(x_hbm, xbuf.at[cur], sem.at[cur], tb)
        x = xbuf[cur].astype(BF16)
        h1 = _dot(x, w1_ref[...].astype(BF16))
        h3 = _dot(x, w3_ref[...].astype(BF16))
        act = (jax.nn.silu(h1) * h3).astype(BF16)
        o_ref[...] = _dot(act, w2_ref[...].astype(BF16))

    @pl.when(i >= nvb)
    def _():
        o_ref[...] = jnp.zeros_like(o_ref)


def moe_experts(xn, pos, blk_exp, n_valid, w1, w3, w2, layer, *, tb):
    n_assign = pos.shape[0]
    cap = n_assign + N_EXPERTS * tb
    d = xn.shape[1]
    nblk = cap // tb
    wspec = lambda shape: pl.BlockSpec((None, None) + shape,
                                       lambda i, p, ex, nv: (layer, ex[i], 0, 0))
    grid_spec = pltpu.PrefetchScalarGridSpec(
        num_scalar_prefetch=3,
        grid=(nblk,),
        in_specs=[pl.BlockSpec(memory_space=pl.ANY),
                  wspec((d, D_EXPERT)), wspec((d, D_EXPERT)), wspec((D_EXPERT, d))],
        out_specs=pl.BlockSpec((tb, d), lambda i, p, ex, nv: (i, 0)),
        scratch_shapes=[pltpu.SMEM((cap,), jnp.int32), pltpu.VMEM((2, tb, d), F32),
                        pltpu.SemaphoreType.DMA((2,))],
    )
    return pl.pallas_call(
        functools.partial(_moe_expert_kernel, tb=tb, n_assign=n_assign),
        grid_spec=grid_spec,
        out_shape=jax.ShapeDtypeStruct((cap, d), F32),
        compiler_params=_cparams(("arbitrary",)),
        name="moe_experts",
    )(pos, blk_exp, n_valid, xn, w1, w3, w2)


def _moe_combine_kernel(pos_ref, ys_hbm, h_ref, gate_ref, o_ref, buf, sem, *, tm, n_tiles):
    i = pl.program_id(0)

    def start(tile, slot):
        for choice in range(TOP_K):
            def row_of(r, choice=choice):
                return pos_ref[(tile * tm + r) * TOP_K + choice]

            _start_row_gather(row_of, ys_hbm, buf.at[slot], sem.at[slot], tm, dst_row0=choice * tm)

    @pl.when(i == 0)
    def _():
        start(0, 0)

    @pl.when(i + 1 < n_tiles)
    def _():
        start(i + 1, (i + 1) & 1)

    cur = i & 1
    _wait_row_gather(ys_hbm, buf.at[cur], sem.at[cur], TOP_K * tm)
    gate = gate_ref[...]
    o_ref[...] = h_ref[...] + gate[:, 0:1] * buf[cur, 0:tm, :] + gate[:, 1:2] * buf[cur, tm:2 * tm, :]


def moe_combine(h, ys, gate, pos, *, tm):
    m, d = h.shape
    hspec = pl.BlockSpec((tm, d), lambda i, p: (i, 0))
    grid_spec = pltpu.PrefetchScalarGridSpec(
        num_scalar_prefetch=1,
        grid=(m // tm,),
        in_specs=[pl.BlockSpec(memory_space=pl.ANY), hspec,
                  pl.BlockSpec((tm, ROUTER_W), lambda i, p: (i, 0))],
        out_specs=hspec,
        scratch_shapes=[pltpu.VMEM((2, TOP_K * tm, d), F32), pltpu.SemaphoreType.DMA((2,))],
    )
    return pl.pallas_call(
        functools.partial(_moe_combine_kernel, tm=tm, n_tiles=m // tm),
        grid_spec=grid_spec,
        out_shape=jax.ShapeDtypeStruct((m, d), F32),
        compiler_params=_cparams(("arbitrary",)),
        name="moe_combine",
    )(pos, ys, h, gate)


def _moe_blocks(counts, n_assign, tb):
    padded = ((counts + tb - 1) // tb) * tb
    pad_end = jnp.cumsum(padded)
    pad_start = (pad_end - padded).astype(jnp.int32)
    nblk = (n_assign + N_EXPERTS * tb) // tb
    n_valid = (pad_end[-1] // tb).astype(jnp.int32)
    blk = jnp.minimum(jnp.arange(nblk, dtype=jnp.int32), n_valid - 1) * tb
    blk_exp = jnp.sum(blk[:, None] >= pad_end[None, :], axis=1).astype(jnp.int32)
    return pad_start, jnp.minimum(blk_exp, N_EXPERTS - 1), n_valid.reshape(1)


def _moe_positions_kernel(eid_ref, rank_ref, pstart_ref, pos_ref):
    eid = eid_ref[...]
    lane = lax.broadcasted_iota(jnp.int32, eid.shape, 1)
    pstart = pstart_ref[...]
    first = [jnp.sum(jnp.where(lane == eid[:, c:c + 1], pstart, 0), axis=-1, keepdims=True)
             for c in range(TOP_K)]
    pos_ref[...] = rank_ref[...] + jnp.where(lane == 0, first[0], jnp.where(lane == 1, first[1], 0))


def moe_positions(eid, rank, pad_start, *, tm):
    m = eid.shape[0]
    spec = pl.BlockSpec((tm, ROUTER_W), lambda i: (i, 0))
    pstart = jnp.pad(pad_start, (0, ROUTER_W - pad_start.shape[0])).reshape(1, ROUTER_W)
    return pl.pallas_call(
        _moe_positions_kernel,
        grid=(m // tm,),
        in_specs=[spec, spec, pl.BlockSpec((1, ROUTER_W), lambda i: (0, 0))],
        out_specs=spec,
        out_shape=jax.ShapeDtypeStruct((m, ROUTER_W), jnp.int32),
        compiler_params=_cparams(("parallel",)),
        name="moe_positions",
    )(eid, rank, pstart)


def hier_moe(h, g, wr_group, wr_expert, w1, w3, w2, layer, *, tm_router=512, tm_combine=128, tb=MOE_TB):
    m, d = h.shape
    xn, eid, gate, rank, counts = moe_router(h, g, wr_group, wr_expert, tm=tm_router)
    pad_start, blk_exp, n_valid = _moe_blocks(counts[0, :N_EXPERTS], m * TOP_K, tb)
    pos = moe_positions(eid, rank, pad_start, tm=1024)[:, :TOP_K].reshape(-1)
    ys = moe_experts(xn, pos, blk_exp, n_valid, w1, w3, w2, layer, tb=tb)
    return moe_combine(h, ys, gate, pos, tm=tm_combine)


def _final_norm_kernel(h_ref, g_ref, o_ref):
    o_ref[...] = _rms(h_ref[...], g_ref[...])


def final_norm(h, g, *, tm):
    m, d = h.shape
    hspec = pl.BlockSpec((tm, d), lambda i: (i, 0))
    return pl.pallas_call(
        _final_norm_kernel,
        grid=(m // tm,),
        in_specs=[hspec, pl.BlockSpec((1, d), lambda i: (0, 0))],
        out_specs=hspec,
        out_shape=jax.ShapeDtypeStruct((m, d), F32),
        compiler_params=_cparams(("parallel",)),
        name="final_norm",
    )(h, g.reshape(1, d))


def _w_in_layout(w_in):
    wa = w_in[:, :IN_A]
    wb = w_in[:, IN_A:IN_A + IN_B]
    wc = w_in[:, IN_A + IN_B:IN_A + IN_B + IN_C]
    wd = w_in[:, IN_A + IN_B + IN_C:]
    wb = jnp.pad(wb, ((0, 0), (0, IN_B_PAD - IN_B)))
    return jnp.concatenate([wb, wa, wc, wd], axis=1).astype(BF16)


def kernel(x, mem, norm_mix, w_in, w_out, diff_lambda, diff_subln, rwkv_mu, rwkv_w0, rwkv_w2,
           rwkv_a0, rwkv_a2, rwkv_g2, rwkv_kk, rwkv_ka, rwkv_rk, rwkv_lnx_w, rwkv_lnx_b,
           s5_a_re, s5_a_im, s5_log_dt, s5_b_re, s5_b_im, s5_c_re, s5_c_im, s5_d, s5_glu_w,
           s5_glu_b, mix_out_norm, norm_cross, norm_mem, xa_wq, xa_wkv, xa_wo, norm_moe,
           router_group, router_expert, moe_w1, moe_w3, moe_w2, norm_final):
    b, s, d = x.shape
    m = b * s
    n_mem = mem.shape[1]
    depth = w_in.shape[0]
    h = x.reshape(m, d)
    mem2 = mem.reshape(b * n_mem, d)
    rope_a = _rope_tables(s, A_QKDIM, MIX)
    rope_c = _rope_tables(s, C_HDIM, MIX)
    for l in range(depth):
        z = norm_matmul(h, norm_mix[l], _w_in_layout(w_in[l]), tm=1024, tn=Z_W // 4, out_dtype=F32)
        z = z.reshape(b, s, Z_W)
        qa, ka, v1a = rope_qkv(z, Z_A, A_QKDIM, tm=512, ones_cols=A_VDIM, tables=rope_a)
        oa = diff_attention(qa, ka, v1a, diff_lambda[l], diff_subln[l], l, tq=512)
        ob = rwkv7_bidir(z, rwkv_mu[l], rwkv_w0[l], rwkv_w2[l], rwkv_a0[l], rwkv_a2[l], rwkv_g2[l],
                         rwkv_kk[l], rwkv_ka[l], rwkv_rk[l], rwkv_lnx_w[l], rwkv_lnx_b[l])
        qc, kc, vc = rope_qkv(z, Z_C, C_HDIM, tm=512, tables=rope_c)
        oc = dilated_attention(qc, kc, vc, tq=256)
        od = s5_bidir(z, s5_a_re[l], s5_a_im[l], s5_log_dt[l], s5_b_re[l], s5_b_im[l],
                      s5_c_re[l], s5_c_im[l], s5_d[l], s5_glu_w[l], s5_glu_b[l])
        h = mix_out(h, oa.reshape(m, MIX), ob.reshape(m, MIX), oc.reshape(m, MIX), od.reshape(m, MIX),
                    mix_out_norm[l, 0], mix_out_norm[l, 1], w_out[l].astype(BF16), tm=512, tn=512)
        q = norm_matmul(h, norm_cross[l], xa_wq[l].astype(BF16), tm=1024, tn=512, out_dtype=BF16)
        kv = norm_matmul(mem2, norm_mem[l], xa_wkv[l].astype(BF16), tm=b * n_mem, tn=512, out_dtype=BF16)
        h = cross_attention(h.reshape(b, s, d), q.reshape(b, s, d), kv.reshape(b, n_mem, 2 * d),
                            xa_wo[l].astype(BF16), tq=256).reshape(m, d)
        h = hier_moe(h, norm_moe[l], router_group[l], router_expert[l], moe_w1, moe_w3, moe_w2, l)
    return final_norm(h, norm_final, tm=512).reshape(b, s, d)
```

```python
import functools
import math

import numpy as np
import jax
import jax.numpy as jnp
from jax import lax
from jax.experimental import pallas as pl
from jax.experimental.pallas import tpu as pltpu

F32 = jnp.float32
BF16 = jnp.bfloat16

D_MODEL = 2048
MIX = D_MODEL // 4
A_HEADS = 4
A_VDIM = MIX // A_HEADS
A_QKDIM = A_VDIM // 2
B_HDIM = 64
B_HEADS = MIX // B_HDIM
B_LORA = 64
B_GATE_LORA = 128
LN_X_EPS = 64e-5
C_HEADS = 4
C_HDIM = MIX // C_HEADS
C_PATTERNS = ((128, 1), (512, 4), (2048, 16))
D_GSIZE = 16
D_GROUPS = MIX // D_GSIZE
D_STATE = 64
IN_A = 3 * MIX
IN_B = 3 * MIX + 4 * B_LORA + B_GATE_LORA
IN_B_PAD = 2048
IN_C = 3 * MIX
IN_D = MIX
XA_HEADS = 4
XA_HDIM = D_MODEL // XA_HEADS
N_GROUPS = 4
EXPERTS_PER_GROUP = 8
N_EXPERTS = N_GROUPS * EXPERTS_PER_GROUP
TOP_K = 2
D_EXPERT = D_MODEL // 4
ROPE_THETA = 10000.0
RMS_EPS = 1e-6
NEG_INF = -1e30

Z_B = 0
Z_A = IN_B_PAD
Z_C = Z_A + IN_A
Z_D = Z_C + IN_C
Z_W = Z_D + IN_D

LANES = 128
VMEM_LIMIT = 56 * 1024 * 1024


def _cparams(sem, vmem=VMEM_LIMIT):
    return pltpu.CompilerParams(dimension_semantics=sem, vmem_limit_bytes=vmem)


def _rms(x, g):
    return x * lax.rsqrt(jnp.mean(x * x, axis=-1, keepdims=True) + RMS_EPS) * g


def _dot(a, b):
    return jnp.dot(a, b, preferred_element_type=F32)


def _dot_t(a, b):
    return lax.dot_general(a, b, (((1,), (1,)), ((), ())), preferred_element_type=F32)


def _split3(x):
    hi = x.astype(BF16)
    r1 = x - hi.astype(F32)
    mid = r1.astype(BF16)
    lo = (r1 - mid.astype(F32)).astype(BF16)
    return hi, mid, lo


def _segsum(x, ones):
    hi, mid, lo = _split3(x)
    return _dot(hi, ones) + _dot(mid, ones) + _dot(lo, ones)


def _norm_matmul_kernel(x_ref, g_ref, w_ref, o_ref, xn_ref):
    @pl.when(pl.program_id(1) == 0)
    def _():
        xn_ref[...] = _rms(x_ref[...], g_ref[...]).astype(BF16)

    o_ref[...] = _dot(xn_ref[...], w_ref[...]).astype(o_ref.dtype)


def norm_matmul(x, g, w, *, tm, tn, out_dtype):
    m, k = x.shape
    n = w.shape[1]
    return pl.pallas_call(
        _norm_matmul_kernel,
        grid=(m // tm, n // tn),
        in_specs=[pl.BlockSpec((tm, k), lambda i, j: (i, 0)),
                  pl.BlockSpec((1, k), lambda i, j: (0, 0)),
                  pl.BlockSpec((k, tn), lambda i, j: (0, j))],
        out_specs=pl.BlockSpec((tm, tn), lambda i, j: (i, j)),
        out_shape=jax.ShapeDtypeStruct((m, n), out_dtype),
        scratch_shapes=[pltpu.VMEM((tm, k), BF16)],
        compiler_params=_cparams(("parallel", "arbitrary")),
        name="norm_matmul",
    )(x, g.reshape(1, k), w)


def _rope_tables(seq, dim, width):
    inv = 1.0 / (ROPE_THETA ** (jnp.arange(0, dim, 2, dtype=F32) / dim))
    ang = jnp.arange(seq, dtype=F32)[:, None] * inv[None, :]
    cos, sin = jnp.cos(ang), jnp.sin(ang)
    cos = jnp.concatenate([cos, cos], axis=-1)
    sin = jnp.concatenate([-sin, sin], axis=-1)
    reps = width // dim
    return jnp.tile(cos, (1, reps)), jnp.tile(sin, (1, reps))


def _rope_kernel(q_ref, k_ref, v_ref, cos_ref, sin_ref, qo_ref, ko_ref, vo_ref, *, half, scale, ones_cols):
    cos = cos_ref[...]
    sin = sin_ref[...]
    width = cos.shape[1]
    lane = lax.broadcasted_iota(jnp.int32, cos.shape, 1)
    first = (lane % (2 * half)) < half

    def rot(x):
        ahead = pltpu.roll(x, width - half, axis=1)
        behind = pltpu.roll(x, half, axis=1)
        return x * cos + jnp.where(first, ahead, behind) * sin

    qo_ref[...] = (rot(q_ref[...]) * scale).astype(BF16)
    ko_ref[...] = rot(k_ref[...]).astype(BF16)
    v = v_ref[...].astype(BF16)
    if ones_cols is None:
        vo_ref[...] = v
    else:
        ones = jnp.ones((v.shape[0], ones_cols), BF16)
        parts = []
        for hd in range(v.shape[1] // ones_cols):
            parts += [v[:, hd * ones_cols:(hd + 1) * ones_cols], ones]
        vo_ref[...] = jnp.concatenate(parts, axis=1)


def rope_qkv(z, col0, head_dim, *, tm, ones_cols=None, tables=None):
    b, s, _ = z.shape
    cos, sin = tables if tables is not None else _rope_tables(s, head_dim, MIX)
    cb = col0 // MIX
    zspec = lambda off: pl.BlockSpec((None, tm, MIX), lambda bi, i, off=off: (bi, i, cb + off))
    tspec = pl.BlockSpec((tm, MIX), lambda bi, i: (i, 0))
    ospec = pl.BlockSpec((None, tm, MIX), lambda bi, i: (bi, i, 0))
    oshape = jax.ShapeDtypeStruct((b, s, MIX), BF16)
    if ones_cols is None:
        vspec, vshape = ospec, oshape
    else:
        vspec = pl.BlockSpec((None, tm, 2 * MIX), lambda bi, i: (bi, i, 0))
        vshape = jax.ShapeDtypeStruct((b, s, 2 * MIX), BF16)
    return pl.pallas_call(
        functools.partial(_rope_kernel, half=head_dim // 2, scale=head_dim ** -0.5, ones_cols=ones_cols),
        grid=(b, s // tm),
        in_specs=[zspec(0), zspec(1), zspec(2), tspec, tspec],
        out_specs=[ospec, ospec, vspec],
        out_shape=[oshape, oshape, vshape],
        compiler_params=_cparams(("parallel", "parallel")),
        name="rope_qkv",
    )(z, z, z, cos, sin)


def _diff_attn_kernel(lam_ref, g_ref, q_ref, k_ref, v1_ref, o_ref, *, lam_init):
    lv = lam_ref[...]
    lam = (jnp.exp(jnp.sum(lv[0:1] * lv[1:2], axis=-1, keepdims=True))
           - jnp.exp(jnp.sum(lv[2:3] * lv[3:4], axis=-1, keepdims=True)) + lam_init)
    q = q_ref[...]
    k = k_ref[...]
    v1 = v1_ref[...]
    lane = lax.broadcasted_iota(jnp.int32, q.shape, 1)
    zero = jnp.zeros_like(q)

    def branch(qm):
        s = _dot_t(qm, k)
        p = jnp.exp((s - jnp.max(s, axis=-1, keepdims=True)).astype(BF16))
        acc = _dot(p, v1)
        return acc[:, :A_VDIM] / acc[:, A_VDIM:]

    o = branch(jnp.where(lane < A_QKDIM, q, zero)) - lam * branch(jnp.where(lane >= A_QKDIM, q, zero))
    o_ref[...] = _rms(o, g_ref[...]) * (1.0 - lam_init)


def diff_attention(q, k, v1, lam_vecs, subln_g, layer_idx, *, tq):
    b, s, _ = q.shape
    lam_init = 0.8 - 0.6 * math.exp(-0.3 * layer_idx)
    qspec = pl.BlockSpec((None, tq, A_VDIM), lambda bi, h, i: (bi, i, h))
    return pl.pallas_call(
        functools.partial(_diff_attn_kernel, lam_init=lam_init),
        grid=(b, A_HEADS, s // tq),
        in_specs=[pl.BlockSpec((4, A_QKDIM), lambda bi, h, i: (0, 0)),
                  pl.BlockSpec((1, A_VDIM), lambda bi, h, i: (0, 0)),
                  qspec,
                  pl.BlockSpec((None, s, A_VDIM), lambda bi, h, i: (bi, 0, h)),
                  pl.BlockSpec((None, s, 2 * A_VDIM), lambda bi, h, i: (bi, 0, h))],
        out_specs=qspec,
        out_shape=jax.ShapeDtypeStruct((b, s, MIX), F32),
        compiler_params=_cparams(("parallel", "parallel", "arbitrary")),
        name="diff_attention",
    )(lam_vecs, subln_g.reshape(1, A_VDIM), q, k, v1)


C_REACH = max(w // 2 for w, _ in C_PATTERNS)


def _dilated_bias_table(tq, window):
    n_delta = (window - tq) // tq + 1
    i = np.arange(tq)[None, :, None]
    j = np.arange(window)[None, None, :]
    n = np.arange(n_delta)[:, None, None]
    d = j - i - n * tq
    count = np.zeros(d.shape, np.int32)
    for w, dil in C_PATTERNS:
        count += ((np.abs(d) <= w // 2) & (d % dil == 0)).astype(np.int32)
    bias = np.where(count > 0, np.log(np.maximum(count, 1)), NEG_INF)
    return jnp.asarray(bias, F32)


def _dilated_attn_kernel(bias_ref, q_ref, k_ref, v_ref, o_ref, *, tq, window, seq):
    start = pl.program_id(2) * tq
    ws = pl.multiple_of(jnp.clip(start - C_REACH, 0, seq - window), tq)
    kw = k_ref[pl.ds(ws, window), :]
    vw = v_ref[pl.ds(ws, window), :]
    s = _dot_t(q_ref[...], kw) + bias_ref[...]
    e = jnp.exp(s - jnp.max(s, axis=-1, keepdims=True))
    den = jnp.sum(e, axis=-1, keepdims=True)
    o_ref[...] = _dot(e.astype(BF16), vw) / den


def dilated_attention(q, k, v, *, tq=128):
    b, s, _ = q.shape
    window = tq + 2 * C_REACH
    assert s >= window and s % tq == 0 and C_REACH % tq == 0
    bias = _dilated_bias_table(tq, window)

    def bias_map(bi, h, i):
        start = i * tq
        ws = jnp.clip(start - C_REACH, 0, s - window)
        return ((start - ws) // tq, 0, 0)

    qspec = pl.BlockSpec((None, tq, C_HDIM), lambda bi, h, i: (bi, i, h))
    kspec = pl.BlockSpec((None, s, C_HDIM), lambda bi, h, i: (bi, 0, h))
    return pl.pallas_call(
        functools.partial(_dilated_attn_kernel, tq=tq, window=window, seq=s),
        grid=(b, C_HEADS, s // tq),
        in_specs=[pl.BlockSpec((None, tq, window), bias_map), qspec, kspec, kspec],
        out_specs=qspec,
        out_shape=jax.ShapeDtypeStruct((b, s, MIX), F32),
        compiler_params=_cparams(("parallel", "parallel", "arbitrary")),
        name="dilated_attention",
    )(bias, q, k, v)


S5_BLOCKS = 4
S5_GPB = D_GROUPS // S5_BLOCKS
S5_CH = S5_GPB * D_GSIZE
S5_ST = S5_GPB * D_STATE
S5_ROWS = 8


def _s5_scan_kernel(u_ref, bm_ref, cm_ref, pr_ref, pi_ref, y_ref, cr_ref, ci_ref, *, tc, reverse):
    @pl.when(pl.program_id(2) == 0)
    def _():
        cr_ref[...] = jnp.zeros_like(cr_ref)
        ci_ref[...] = jnp.zeros_like(ci_ref)

    bu = _dot(u_ref[...].astype(BF16), bm_ref[...])
    xr = bu[:, :S5_ST]
    xi = bu[:, S5_ST:]
    row = lax.broadcasted_iota(jnp.int32, xr.shape, 0) & (S5_ROWS - 1)
    for lvl in range(S5_ROWS.bit_length() - 1):
        sh = 1 << lvl
        prow = (S5_ROWS - sh) if reverse else (sh - 1)
        ar = pr_ref[prow:prow + 1, :]
        ai = pi_ref[prow:prow + 1, :]
        if reverse:
            keep = row < (S5_ROWS - sh)
            sr = jnp.where(keep, pltpu.roll(xr, tc - sh, axis=0), 0.0)
            si = jnp.where(keep, pltpu.roll(xi, tc - sh, axis=0), 0.0)
        else:
            keep = row >= sh
            sr = jnp.where(keep, pltpu.roll(xr, sh, axis=0), 0.0)
            si = jnp.where(keep, pltpu.roll(xi, sh, axis=0), 0.0)
        xr, xi = xr + ar * sr - ai * si, xi + ar * si + ai * sr
    pr = pr_ref[...]
    pi = pi_ref[...]
    cr = cr_ref[...]
    ci = ci_ref[...]
    n_groups = tc // S5_ROWS
    last = 0 if reverse else S5_ROWS - 1
    out_r = [None] * n_groups
    out_i = [None] * n_groups
    for gi in (range(n_groups - 1, -1, -1) if reverse else range(n_groups)):
        rows = slice(gi * S5_ROWS, (gi + 1) * S5_ROWS)
        gr = xr[rows] + pr * cr - pi * ci
        gim = xi[rows] + pr * ci + pi * cr
        cr = gr[last:last + 1, :]
        ci = gim[last:last + 1, :]
        out_r[gi] = gr
        out_i[gi] = gim
    cr_ref[...] = cr
    ci_ref[...] = ci
    x = jnp.concatenate([jnp.concatenate(out_r, axis=0), jnp.concatenate(out_i, axis=0)], axis=1)
    y_ref[...] = _dot(x.astype(BF16), cm_ref[...])


def _s5_prepare(direction, a_re, a_im, log_dt, b_re, b_im, c_re, c_im):
    lr = jnp.minimum(a_re[direction].astype(F32), -1e-4)
    li = a_im[direction].astype(F32)
    dt = jnp.exp(log_dt[direction].astype(F32))[:, None]
    mag = jnp.exp(dt * lr)
    abr, abi = mag * jnp.cos(dt * li), mag * jnp.sin(dt * li)
    den = lr * lr + li * li
    qr, qi = lr / den, -li / den
    fr = (abr - 1.0) * qr - abi * qi
    fi = (abr - 1.0) * qi + abi * qr
    b_re, b_im = b_re.astype(F32), b_im.astype(F32)
    bbr = fr[..., None] * b_re - fi[..., None] * b_im
    bbi = fr[..., None] * b_im + fi[..., None] * b_re
    eye = jnp.eye(S5_GPB, dtype=F32)

    def in_block(m):
        m = m.reshape(S5_BLOCKS, S5_GPB, D_STATE, D_GSIZE)
        return jnp.einsum('kgnc,gh->kgchn', m, eye).reshape(S5_BLOCKS, S5_CH, S5_ST)

    def out_block(m):
        m = m.reshape(S5_BLOCKS, S5_GPB, D_GSIZE, D_STATE)
        return jnp.einsum('kgcn,gh->kgnhc', m, eye).reshape(S5_BLOCKS, S5_ST, S5_CH)

    bmat = jnp.concatenate([in_block(bbr), in_block(bbi)], axis=2).astype(BF16)
    cmat = jnp.concatenate([out_block(c_re.astype(F32)), -out_block(c_im.astype(F32))],
                           axis=1).astype(BF16)
    steps = jnp.arange(1, S5_ROWS + 1, dtype=F32)[:, None, None]
    pmag = jnp.exp(steps * (dt * lr)[None])
    ang = steps * (dt * li)[None]
    p_re = (pmag * jnp.cos(ang)).reshape(S5_ROWS, D_GROUPS * D_STATE)
    p_im = (pmag * jnp.sin(ang)).reshape(S5_ROWS, D_GROUPS * D_STATE)
    if direction == 1:
        p_re, p_im = p_re[::-1], p_im[::-1]
    return bmat, cmat, p_re, p_im


def s5_scan(z, direction, params, *, tc):
    b, s, _ = z.shape
    nch = s // tc
    bmat, cmat, p_re, p_im = _s5_prepare(direction, *params)
    reverse = direction == 1
    tmap = (lambda c: nch - 1 - c) if reverse else (lambda c: c)
    ub = Z_D // S5_CH
    return pl.pallas_call(
        functools.partial(_s5_scan_kernel, tc=tc, reverse=reverse),
        grid=(b, S5_BLOCKS, nch),
        in_specs=[pl.BlockSpec((None, tc, S5_CH), lambda bi, k, c: (bi, tmap(c), ub + k)),
                  pl.BlockSpec((None, S5_CH, 2 * S5_ST), lambda bi, k, c: (k, 0, 0)),
                  pl.BlockSpec((None, 2 * S5_ST, S5_CH), lambda bi, k, c: (k, 0, 0)),
                  pl.BlockSpec((S5_ROWS, S5_ST), lambda bi, k, c: (0, k)),
                  pl.BlockSpec((S5_ROWS, S5_ST), lambda bi, k, c: (0, k))],
        out_specs=pl.BlockSpec((None, tc, S5_CH), lambda bi, k, c: (bi, tmap(c), k)),
        out_shape=jax.ShapeDtypeStruct((b, s, MIX), F32),
        scratch_shapes=[pltpu.VMEM((1, S5_ST), F32), pltpu.VMEM((1, S5_ST), F32)],
        compiler_params=_cparams(("parallel", "parallel", "arbitrary")),
        name="s5_scan",
    )(z, bmat, cmat, p_re, p_im)


def _s5_out_kernel(yf_ref, yb_ref, u_ref, d_ref, w_ref, b_ref, o_ref):
    y = yf_ref[...] + yb_ref[...] + d_ref[...] * u_ref[...]
    gl = jax.nn.gelu(y)
    gate = jax.nn.sigmoid(_dot(gl.astype(BF16), w_ref[...]) + b_ref[...])
    o_ref[...] = gl * gate


def s5_output(yf, yb, z, d_skip, glu_w, glu_b, *, tm):
    b, s, _ = z.shape
    yspec = pl.BlockSpec((None, tm, MIX), lambda bi, i: (bi, i, 0))
    vspec = pl.BlockSpec((1, MIX), lambda bi, i: (0, 0))
    return pl.pallas_call(
        _s5_out_kernel,
        grid=(b, s // tm),
        in_specs=[yspec, yspec,
                  pl.BlockSpec((None, tm, MIX), lambda bi, i: (bi, i, Z_D // MIX)),
                  vspec, pl.BlockSpec((MIX, MIX), lambda bi, i: (0, 0)), vspec],
        out_specs=yspec,
        out_shape=jax.ShapeDtypeStruct((b, s, MIX), F32),
        compiler_params=_cparams(("parallel", "parallel")),
        name="s5_output",
    )(yf, yb, z, d_skip.reshape(1, MIX), glu_w.astype(BF16), glu_b.reshape(1, MIX))


def s5_bidir(z, a_re, a_im, log_dt, b_re, b_im, c_re, c_im, d_skip, glu_w, glu_b, *, tc=1024, tm=512):
    params = (a_re, a_im, log_dt, b_re, b_im, c_re, c_im)
    yf = s5_scan(z, 0, params, tc=tc)
    yb = s5_scan(z, 1, params, tc=tc)
    return s5_output(yf, yb, z, d_skip, glu_w, glu_b, tm=tm)


def _softplus(y):
    return jnp.maximum(y, 0.0) + jnp.log(1.0 + jnp.exp(-jnp.abs(y)))


def _head_ones(width):
    seg = np.arange(width) // B_HDIM
    return jnp.asarray(seg[:, None] == seg[None, :], BF16)


def _rwkv_pre_kernel(z_ref, zp_ref, zn_ref, mu_ref, w0_ref, a0_ref, w2_ref, a2_ref, g2_ref,
                     kk_ref, ka_ref, rk_ref, ones_ref,
                     r_o, a_o, w0_o, w1_o, k0_o, k1_o, b0_o, b1_o, v_o, g_o, bonus_o,
                     *, tm, n_tiles):
    i = pl.program_id(1)
    z = z_ref[...]
    row = lax.broadcasted_iota(jnp.int32, z.shape, 0)
    prev_row = jnp.where(i > 0, zp_ref[7:8, :], 0.0)
    next_row = jnp.where(i < n_tiles - 1, zn_ref[0:1, :], 0.0)
    zp = jnp.where(row == 0, prev_row, pltpu.roll(z, 1, axis=0))
    zn = jnp.where(row == tm - 1, next_row, pltpu.roll(z, tm - 1, axis=0))
    xs = z + mu_ref[0:1, :] * (zp - z) + mu_ref[1:2, :] * (zn - z)
    r = xs[:, 0:MIX]
    k = xs[:, MIX:2 * MIX]
    v = xs[:, 2 * MIX:3 * MIX]
    c0 = 3 * MIX
    wd = xs[:, c0:c0 + 2 * B_LORA]
    ad = xs[:, c0 + 2 * B_LORA:c0 + 4 * B_LORA]
    gd = xs[:, c0 + 4 * B_LORA:c0 + 4 * B_LORA + B_GATE_LORA]
    lw = _dot(jnp.tanh(wd).astype(BF16), w2_ref[...])
    la = _dot(ad.astype(BF16), a2_ref[...])
    g_o[...] = _dot(jax.nn.sigmoid(gd).astype(BF16), g2_ref[...])
    ones = ones_ref[...]
    kk = k * kk_ref[...]
    kk = kk * lax.rsqrt(_segsum(kk * kk, ones) + 1e-12)
    ka = ka_ref[...]
    ksum = jnp.zeros_like(k)
    for d, (w_o, k_o, b_o) in enumerate(((w0_o, k0_o, b0_o), (w1_o, k1_o, b1_o))):
        cols = slice(d * MIX, (d + 1) * MIX)
        logw = -_softplus(-(w0_ref[d:d + 1, :] + lw[:, cols])) - 0.5
        w_o[...] = -jnp.exp(logw)
        a = jax.nn.sigmoid(a0_ref[d:d + 1, :] + la[:, cols])
        kmod = k * (1.0 + (a - 1.0) * ka)
        k_o[...] = kmod
        b_o[...] = kk * a
        ksum = ksum + kmod
    r_o[...] = r
    a_o[...] = -kk
    bonus_o[...] = _segsum(r * (0.5 * ksum) * rk_ref[...], ones) * v
    v_o[...] = v


def _block_diag2(m):
    z = jnp.zeros_like(m[0])
    return jnp.concatenate([jnp.concatenate([m[0], z], axis=1),
                            jnp.concatenate([z, m[1]], axis=1)], axis=0)


def rwkv_pre(z, mu, w0, w2, a0, a2, g2, k_k, k_a, r_k, *, tm):
    b, s, _ = z.shape
    n_tiles = s // tm
    mu_p = jnp.pad(mu.astype(F32), ((0, 0), (0, IN_B_PAD - IN_B)))
    row_spec = pl.BlockSpec((None, tm, MIX), lambda bi, i: (bi, i, 0))
    vec = lambda n, w: pl.BlockSpec((n, w), lambda bi, i: (0, 0))
    rows = jax.ShapeDtypeStruct((b, s, MIX), F32)
    hb = tm // 8
    return pl.pallas_call(
        functools.partial(_rwkv_pre_kernel, tm=tm, n_tiles=n_tiles),
        grid=(b, n_tiles),
        in_specs=[pl.BlockSpec((None, tm, IN_B_PAD), lambda bi, i: (bi, i, 0)),
                  pl.BlockSpec((None, 8, IN_B_PAD), lambda bi, i: (bi, jnp.maximum(i * hb - 1, 0), 0)),
                  pl.BlockSpec((None, 8, IN_B_PAD), lambda bi, i: (bi, jnp.minimum((i + 1) * hb, s // 8 - 1), 0)),
                  vec(2, IN_B_PAD), vec(2, MIX), vec(2, MIX),
                  vec(2 * B_LORA, 2 * MIX), vec(2 * B_LORA, 2 * MIX), vec(B_GATE_LORA, MIX),
                  vec(1, MIX), vec(1, MIX), vec(1, MIX), vec(MIX, MIX)],
        out_specs=[row_spec] * 11,
        out_shape=[rows] * 11,
        compiler_params=_cparams(("parallel", "parallel")),
        name="rwkv_pre",
    )(z, z, z, mu_p, w0.astype(F32), a0.astype(F32),
      _block_diag2(w2).astype(BF16), _block_diag2(a2).astype(BF16), g2.astype(BF16),
      k_k.reshape(1, MIX), k_a.reshape(1, MIX), r_k.reshape(1, MIX), _head_ones(MIX))


def _rwkv_scan_kernel(af_ref, rf_ref, vf_ref, lwf_ref, kf_ref, bf_ref,
                      ab_ref, rb_ref, vb_ref, lwb_ref, kb_ref, bb_ref,
                      trif_ref, trib_ref, yf_ref, yb_ref, h_ref, *, tc, nb):
    @pl.when(pl.program_id(0) == 0)
    def _():
        h_ref[...] = jnp.zeros_like(h_ref)

    row = lax.broadcasted_iota(jnp.int32, (tc, tc), 0)
    col = lax.broadcasted_iota(jnp.int32, (tc, tc), 1)
    eye = (row == col).astype(F32)
    row2 = lax.broadcasted_iota(jnp.int32, (tc, 2 * tc), 0)
    col2 = lax.broadcasted_iota(jnp.int32, (tc, 2 * tc), 1) & (tc - 1)
    zeros_tv = jnp.zeros((tc, B_HDIM), BF16)
    n_sq = tc.bit_length() - 1
    tdot = lambda x, y: lax.dot_general(x, y, (((0,), (0,)), ((), ())), preferred_element_type=F32)

    seqs = []
    for reverse, (a_ref, r_ref, v_ref, lw_ref, k_ref, b_ref, tri_ref, y_ref) in (
            (False, (af_ref, rf_ref, vf_ref, lwf_ref, kf_ref, bf_ref, trif_ref, yf_ref)),
            (True, (ab_ref, rb_ref, vb_ref, lwb_ref, kb_ref, bb_ref, trib_ref, yb_ref))):
        tri = tri_ref[...]
        before = (col2 > row2) if reverse else (col2 < row2)
        upto = (col2 >= row2) if reverse else (col2 <= row2)
        last = 0 if reverse else tc - 1
        for bi in range(nb):
            lw = lw_ref[bi]
            hi, mid, lo = _split3(lw)
            cl = _dot(tri, hi) + _dot(tri, mid) + _dot(tri, lo)
            tot = cl[last:last + 1, :]
            einv = jnp.exp(-cl)
            etot = jnp.exp(tot - cl)
            b_all = b_ref[bi]
            k_all = k_ref[bi]
            seqs.append(dict(
                a_t=a_ref[bi] * jnp.exp(cl - lw), r_t=r_ref[bi] * jnp.exp(cl),
                b_t=b_all * einv, k_t=k_all * einv, b_h=b_all * etot, k_h=k_all * etot,
                g_tot=jnp.exp(tot), v=v_ref[bi], before=before, upto=upto, y_ref=y_ref, bi=bi))

    inst = [(sq, slice(hd * B_HDIM, (hd + 1) * B_HDIM)) for sq in seqs for hd in range(B_HEADS)]
    ids = range(len(inst))
    at = [sq['a_t'][:, s].astype(BF16) for sq, s in inst]
    rt = [sq['r_t'][:, s].astype(BF16) for sq, s in inst]
    bk = [jnp.concatenate([sq['b_t'][:, s], sq['k_t'][:, s]], axis=0).astype(BF16) for sq, s in inst]
    bkh = [jnp.concatenate([sq['b_h'][:, s], sq['k_h'][:, s]], axis=0).astype(BF16) for sq, s in inst]
    vv = [sq['v'][:, s].astype(BF16) for sq, s in inst]
    g = [_dot_t(jnp.concatenate([at[n], rt[n]], axis=0), bk[n]) for n in ids]
    ga = [jnp.where(inst[n][0]['before'], g[n][:tc], 0.0) for n in ids]
    gr = [jnp.where(inst[n][0]['upto'], g[n][tc:], 0.0).astype(BF16) for n in ids]
    lkv = [_dot(ga[n].astype(BF16), jnp.concatenate([zeros_tv, vv[n]], axis=0)) for n in ids]
    lb = [ga[n][:, :tc].astype(BF16) for n in ids]
    pinv = [eye + ga[n][:, :tc] for n in ids]
    lb = [_dot(x, x).astype(BF16) for x in lb]
    for _ in range(1, n_sq - 1):
        prod = [_dot(jnp.concatenate([pinv[n].astype(BF16), lb[n]], axis=0), lb[n]) for n in ids]
        pinv = [pinv[n] + prod[n][:tc] for n in ids]
        lb = [prod[n][tc:].astype(BF16) for n in ids]
    pinv = [(pinv[n] + _dot(pinv[n].astype(BF16), lb[n])).astype(BF16) for n in ids]
    ah = [_dot(pinv[n], at[n]).astype(BF16) for n in ids]
    u0 = [_dot(pinv[n], lkv[n].astype(BF16)).astype(BF16) for n in ids]
    rhs = [jnp.concatenate([jnp.concatenate([ah[n], u0[n]], axis=1),
                            jnp.concatenate([zeros_tv, vv[n]], axis=1)], axis=0) for n in ids]
    ry = [_dot(gr[n], rhs[n]) for n in ids]
    ph = [tdot(bkh[n], rhs[n]) for n in ids]
    rh = [inst[n][0]['r_t'][:, inst[n][1]] + ry[n][:, :B_HDIM] for n in ids]
    phi = [eye * inst[n][0]['g_tot'][:, inst[n][1]] + ph[n][:, :B_HDIM] for n in ids]
    h0 = [h_ref[n].astype(BF16) for n in ids]
    fin = [_dot(jnp.concatenate([rh[n], phi[n]], axis=0).astype(BF16), h0[n]) for n in ids]
    yo = [fin[n][:tc] + ry[n][:, B_HDIM:] for n in ids]
    for si, sq in enumerate(seqs):
        sq['y_ref'][sq['bi']] = jnp.concatenate(yo[si * B_HEADS:(si + 1) * B_HEADS], axis=1)
    for n in ids:
        h_ref[n] = fin[n][tc:] + ph[n][:, B_HDIM:]


def rwkv_scan(a, r, v, lwf, kf, bvf, lwb, kb, bvb, *, tc=64):
    nb, s, _ = a.shape
    nch = s // tc
    fwd = pl.BlockSpec((nb, tc, MIX), lambda c: (0, c, 0))
    bwd = pl.BlockSpec((nb, tc, MIX), lambda c: (0, nch - 1 - c, 0))
    tspec = pl.BlockSpec((tc, tc), lambda c: (0, 0))
    t_idx = np.arange(tc)
    tri_f = jnp.asarray(t_idx[None, :] <= t_idx[:, None], BF16)
    tri_b = jnp.asarray(t_idx[None, :] >= t_idx[:, None], BF16)
    out = jax.ShapeDtypeStruct((nb, s, MIX), F32)
    return pl.pallas_call(
        functools.partial(_rwkv_scan_kernel, tc=tc, nb=nb),
        grid=(nch,),
        in_specs=[fwd] * 6 + [bwd] * 6 + [tspec, tspec],
        out_specs=[fwd, bwd],
        out_shape=[out, out],
        scratch_shapes=[pltpu.VMEM((2 * nb * B_HEADS, B_HDIM, B_HDIM), F32)],
        compiler_params=_cparams(("arbitrary",)),
        name="rwkv_scan",
    )(a, r, v, lwf, kf, bvf, a, r, v, lwb, kb, bvb, tri_f, tri_b)


def _rwkv_post_kernel(yf_ref, yb_ref, bonus_ref, g_ref, lw_ref, lb_ref, ones_ref, o_ref):
    y = yf_ref[...] + yb_ref[...]
    ones = ones_ref[...]
    mean = _segsum(y, ones) * (1.0 / B_HDIM)
    yc = y - mean
    var = _segsum(yc * yc, ones) * (1.0 / B_HDIM)
    yn = yc * lax.rsqrt(var + LN_X_EPS) * lw_ref[...] + lb_ref[...]
    o_ref[...] = (yn + bonus_ref[...]) * g_ref[...]


def rwkv_post(yf, yb, bonus, g, lnx_w, lnx_b, *, tm):
    b, s, _ = yf.shape
    row_spec = pl.BlockSpec((None, tm, MIX), lambda bi, i: (bi, i, 0))
    vec = pl.BlockSpec((1, MIX), lambda bi, i: (0, 0))
    return pl.pallas_call(
        _rwkv_post_kernel,
        grid=(b, s // tm),
        in_specs=[row_spec, row_spec, row_spec, row_spec, vec, vec,
                  pl.BlockSpec((MIX, MIX), lambda bi, i: (0, 0))],
        out_specs=row_spec,
        out_shape=jax.ShapeDtypeStruct((b, s, MIX), F32),
        compiler_params=_cparams(("parallel", "parallel")),
        name="rwkv_post",
    )(yf, yb, bonus, g, lnx_w.reshape(1, MIX), lnx_b.reshape(1, MIX), _head_ones(MIX))


def rwkv7_bidir(z, mu, w0, w2, a0, a2, g2, k_k, k_a, r_k, lnx_w, lnx_b, *, tm=256, tc=64):
    r, a, wf, wb, kf, kb, bf, bb, v, g, bonus = rwkv_pre(z, mu, w0, w2, a0, a2, g2, k_k, k_a, r_k, tm=tm)
    yf, yb = rwkv_scan(a, r, v, wf, kf, bf, wb, kb, bb, tc=tc)
    return rwkv_post(yf, yb, bonus, g, lnx_w, lnx_b, tm=tm)


def _mix_out_kernel(h_ref, oa_ref, ob_ref, oc_ref, od_ref, gc_ref, gd_ref, w_ref, o_ref, mix_ref):
    @pl.when(pl.program_id(1) == 0)
    def _():
        mix_ref[...] = jnp.concatenate(
            [oa_ref[...], ob_ref[...], _rms(oc_ref[...], gc_ref[...]), _rms(od_ref[...], gd_ref[...])],
            axis=1).astype(BF16)

    o_ref[...] = h_ref[...] + _dot(mix_ref[...], w_ref[...])


def mix_out(h, oa, ob, oc, od, gc, gd, w_out, *, tm, tn):
    m, d = h.shape
    mspec = pl.BlockSpec((tm, MIX), lambda i, j: (i, 0))
    vspec = pl.BlockSpec((1, MIX), lambda i, j: (0, 0))
    hspec = pl.BlockSpec((tm, tn), lambda i, j: (i, j))
    return pl.pallas_call(
        _mix_out_kernel,
        grid=(m // tm, d // tn),
        in_specs=[hspec, mspec, mspec, mspec, mspec, vspec, vspec,
                  pl.BlockSpec((4 * MIX, tn), lambda i, j: (0, j))],
        out_specs=hspec,
        out_shape=jax.ShapeDtypeStruct((m, d), F32),
        scratch_shapes=[pltpu.VMEM((tm, 4 * MIX), BF16)],
        compiler_params=_cparams(("parallel", "arbitrary")),
        name="mix_out",
    )(h, oa, ob, oc, od, gc.reshape(1, MIX), gd.reshape(1, MIX), w_out)


def _cross_attn_kernel(h_ref, q_ref, kv_ref, wo_ref, o_ref):
    q = q_ref[...]
    outs = []
    for hd in range(XA_HEADS):
        cols = slice(hd * XA_HDIM, (hd + 1) * XA_HDIM)
        kh = kv_ref[:, cols]
        vh = kv_ref[:, D_MODEL + hd * XA_HDIM:D_MODEL + (hd + 1) * XA_HDIM]
        s = _dot_t(q[:, cols], kh) * (XA_HDIM ** -0.5)
        e = jnp.exp(s - jnp.max(s, axis=-1, keepdims=True))
        p = e / jnp.sum(e, axis=-1, keepdims=True)
        outs.append(_dot(p.astype(BF16), vh).astype(BF16))
    o = jnp.concatenate(outs, axis=1)
    o_ref[...] = h_ref[...] + _dot(o, wo_ref[...])


def cross_attention(h, q, kv, wo, *, tq):
    b, s, d = h.shape
    n_mem = kv.shape[1]
    hspec = pl.BlockSpec((None, tq, d), lambda bi, i: (bi, i, 0))
    return pl.pallas_call(
        _cross_attn_kernel,
        grid=(b, s // tq),
        in_specs=[hspec, hspec,
                  pl.BlockSpec((None, n_mem, 2 * d), lambda bi, i: (bi, 0, 0)),
                  pl.BlockSpec((d, d), lambda bi, i: (0, 0))],
        out_specs=hspec,
        out_shape=jax.ShapeDtypeStruct((b, s, d), F32),
        compiler_params=_cparams(("parallel", "parallel")),
        name="cross_attention",
    )(h, q, kv, wo)


ROUTER_W = LANES
MOE_TB = 256


def _router_kernel(h_ref, g_ref, wr_ref, lt_ref, xn_ref, eid_ref, gate_ref, rank_ref, cnt_ref, run_ref):
    @pl.when(pl.program_id(0) == 0)
    def _():
        run_ref[...] = jnp.zeros_like(run_ref)

    xn = _rms(h_ref[...], g_ref[...])
    xn_ref[...] = xn
    logits = jnp.dot(xn, wr_ref[...], preferred_element_type=F32, precision=lax.Precision.HIGHEST)
    lane = lax.broadcasted_iota(jnp.int32, logits.shape, 1)
    big = jnp.int32(ROUTER_W)

    def first_max(mask):
        m = jnp.max(jnp.where(mask, logits, NEG_INF), axis=-1, keepdims=True)
        idx = jnp.min(jnp.where(mask & (logits == m), lane, big), axis=-1, keepdims=True)
        return m, idx

    gmask = lane < N_GROUPS
    gmax, gidx = first_max(gmask)
    g_w = 1.0 / jnp.sum(jnp.where(gmask, jnp.exp(logits - gmax), 0.0), axis=-1, keepdims=True)
    e0 = N_GROUPS + gidx * EXPERTS_PER_GROUP
    emask = (lane >= e0) & (lane < e0 + EXPERTS_PER_GROUP)
    m1, i1 = first_max(emask)
    m2, i2 = first_max(emask & (lane != i1))
    e2 = jnp.exp(m2 - m1)
    w1 = 1.0 / (1.0 + e2)
    w2 = e2 / (1.0 + e2)
    e1 = i1 - N_GROUPS
    e2i = i2 - N_GROUPS
    eid_ref[...] = jnp.where(lane == 0, e1, jnp.where(lane == 1, e2i, 0))
    gate_ref[...] = jnp.where(lane == 0, g_w * w1, jnp.where(lane == 1, g_w * w2, 0.0))
    hit1 = lane == e1
    hit2 = lane == e2i
    onehot = jnp.where(hit1 | hit2, 1.0, 0.0)
    earlier = run_ref[...] + _dot(lt_ref[...], onehot.astype(BF16))
    r1 = jnp.sum(jnp.where(hit1, earlier, 0.0), axis=-1, keepdims=True)
    r2 = jnp.sum(jnp.where(hit2, earlier, 0.0), axis=-1, keepdims=True)
    rank_ref[...] = jnp.where(lane == 0, r1, jnp.where(lane == 1, r2, 0.0)).astype(jnp.int32)
    run_ref[...] = run_ref[...] + jnp.sum(onehot, axis=0, keepdims=True)
    cnt_ref[...] = run_ref[...].astype(jnp.int32)


def moe_router(h, g, wr_group, wr_expert, *, tm):
    m, d = h.shape
    wr = jnp.concatenate([wr_group, wr_expert], axis=1).astype(F32)
    wr = jnp.pad(wr, ((0, 0), (0, ROUTER_W - wr.shape[1])))
    t_idx = np.arange(tm)
    lower = jnp.asarray(t_idx[None, :] < t_idx[:, None], BF16)
    hspec = pl.BlockSpec((tm, d), lambda i: (i, 0))
    lspec = pl.BlockSpec((tm, ROUTER_W), lambda i: (i, 0))
    cspec = pl.BlockSpec((1, ROUTER_W), lambda i: (0, 0))
    lane_i = jax.ShapeDtypeStruct((m, ROUTER_W), jnp.int32)
    return pl.pallas_call(
        _router_kernel,
        grid=(m // tm,),
        in_specs=[hspec, pl.BlockSpec((1, d), lambda i: (0, 0)),
                  pl.BlockSpec((d, ROUTER_W), lambda i: (0, 0)),
                  pl.BlockSpec((tm, tm), lambda i: (0, 0))],
        out_specs=[hspec, lspec, lspec, lspec, cspec],
        out_shape=[jax.ShapeDtypeStruct((m, d), F32), lane_i,
                   jax.ShapeDtypeStruct((m, ROUTER_W), F32), lane_i,
                   jax.ShapeDtypeStruct((1, ROUTER_W), jnp.int32)],
        scratch_shapes=[pltpu.VMEM((1, ROUTER_W), F32)],
        compiler_params=_cparams(("arbitrary",)),
        name="moe_router",
    )(h, g.reshape(1, d), wr, lower)


GATHER_UNROLL = 8


def _start_row_gather(row_of, src_hbm, dst, sem, n_rows, dst_row0=0):
    def issue(r, c):
        pltpu.make_async_copy(src_hbm.at[pl.ds(row_of(r), 1)], dst.at[pl.ds(dst_row0 + r, 1)], sem).start()
        return c

    lax.fori_loop(0, n_rows, issue, 0, unroll=GATHER_UNROLL)


def _wait_row_gather(src_hbm, dst, sem, n_rows):
    pltpu.make_async_copy(src_hbm.at[pl.ds(0, n_rows)], dst, sem).wait()


def _moe_expert_kernel(pos_ref, exp_ref, nvb_ref, nxt_ref, run_ref, x_hbm, w1_hbm, w3_hbm, w2_hbm, o_ref,
                       slot_ref, xbuf, sem, w1buf, w3buf, w2buf, wsem, *, tb, n_assign, layer):
    i = pl.program_id(0)
    nvb = nvb_ref[0]

    def weight_copies(e, s):
        return (pltpu.make_async_copy(w1_hbm.at[layer, e], w1buf.at[s], wsem.at[s, 0]),
                pltpu.make_async_copy(w3_hbm.at[layer, e], w3buf.at[s], wsem.at[s, 1]),
                pltpu.make_async_copy(w2_hbm.at[layer, e], w2buf.at[s], wsem.at[s, 2]))

    e_cur = exp_ref[i]
    first = (i == 0) | (exp_ref[jnp.maximum(i - 1, 0)] != e_cur)
    wslot = run_ref[e_cur] & 1

    @pl.when(i == 0)
    def _():
        for cp in weight_copies(e_cur, wslot):
            cp.start()

    @pl.when((i < nvb) & first & (nxt_ref[e_cur] >= 0))
    def _():
        for cp in weight_copies(nxt_ref[e_cur], 1 - wslot):
            cp.start()

    @pl.when(i == 0)
    def _():
        def clear(s, c):
            slot_ref[s] = 0
            return c

        lax.fori_loop(0, slot_ref.shape[0], clear, 0, unroll=GATHER_UNROLL)

        def place(a, c):
            slot_ref[pos_ref[a]] = lax.shift_right_logical(a, TOP_K.bit_length() - 1)
            return c

        lax.fori_loop(0, n_assign, place, 0, unroll=GATHER_UNROLL)
        _start_row_gather(lambda r: slot_ref[r], x_hbm, xbuf.at[0], sem.at[0], tb)

    @pl.when(i + 1 < nvb)
    def _():
        nxt = (i + 1) & 1
        base = (i + 1) * tb
        _start_row_gather(lambda r: slot_ref[base + r], x_hbm, xbuf.at[nxt], sem.at[nxt], tb)

    @pl.when(i < nvb)
    def _():
        @pl.when(first)
        def _():
            for cp in weight_copies(e_cur, wslot):
                cp.wait()

        cur = i & 1
        _wait_row_gather(x_hbm, xbuf.at[cur], sem.at[cur], tb)
        x = xbuf[cur].astype(BF16)
        h1 = _dot(x, w1buf[wslot].astype(BF16))
        h3 = _dot(x, w3buf[wslot].astype(BF16))
        act = (jax.nn.silu(h1) * h3).astype(BF16)
        o_ref[...] = _dot(act, w2buf[wslot].astype(BF16))

    @pl.when(i >= nvb)
    def _():
        o_ref[...] = jnp.zeros_like(o_ref)


def moe_experts(xn, pos, blk_exp, n_valid, nxt_exp, run_idx, w1, w3, w2, layer, *, tb):
    n_assign = pos.shape[0]
    cap = n_assign + N_EXPERTS * tb
    d = xn.shape[1]
    nblk = cap // tb
    hbm = pl.BlockSpec(memory_space=pl.ANY)
    grid_spec = pltpu.PrefetchScalarGridSpec(
        num_scalar_prefetch=5,
        grid=(nblk,),
        in_specs=[hbm, hbm, hbm, hbm],
        out_specs=pl.BlockSpec((tb, d), lambda i, *_: (i, 0)),
        scratch_shapes=[pltpu.SMEM((cap,), jnp.int32), pltpu.VMEM((2, tb, d), F32),
                        pltpu.SemaphoreType.DMA((2,)),
                        pltpu.VMEM((2, d, D_EXPERT), F32), pltpu.VMEM((2, d, D_EXPERT), F32),
                        pltpu.VMEM((2, D_EXPERT, d), F32), pltpu.SemaphoreType.DMA((2, 3))],
    )
    return pl.pallas_call(
        functools.partial(_moe_expert_kernel, tb=tb, n_assign=n_assign, layer=layer),
        grid_spec=grid_spec,
        out_shape=jax.ShapeDtypeStruct((cap, d), F32),
        compiler_params=_cparams(("arbitrary",)),
        name="moe_experts",
    )(pos, blk_exp, n_valid, nxt_exp, run_idx, xn, w1, w3, w2)


def _moe_combine_kernel(pos_ref, ys_hbm, h_ref, gate_ref, g_ref, o_ref, buf, sem, *, tm, n_tiles, normalize):
    i = pl.program_id(0)

    def start(tile, slot):
        for choice in range(TOP_K):
            def row_of(r, choice=choice):
                return pos_ref[(tile * tm + r) * TOP_K + choice]

            _start_row_gather(row_of, ys_hbm, buf.at[slot], sem.at[slot], tm, dst_row0=choice * tm)

    @pl.when(i == 0)
    def _():
        start(0, 0)

    @pl.when(i + 1 < n_tiles)
    def _():
        start(i + 1, (i + 1) & 1)

    cur = i & 1
    _wait_row_gather(ys_hbm, buf.at[cur], sem.at[cur], TOP_K * tm)
    gate = gate_ref[...]
    out = h_ref[...] + gate[:, 0:1] * buf[cur, 0:tm, :] + gate[:, 1:2] * buf[cur, tm:2 * tm, :]
    o_ref[...] = _rms(out, g_ref[...]) if normalize else out


def moe_combine(h, ys, gate, pos, out_gain, *, tm, normalize):
    m, d = h.shape
    hspec = pl.BlockSpec((tm, d), lambda i, p: (i, 0))
    grid_spec = pltpu.PrefetchScalarGridSpec(
        num_scalar_prefetch=1,
        grid=(m // tm,),
        in_specs=[pl.BlockSpec(memory_space=pl.ANY), hspec,
                  pl.BlockSpec((tm, ROUTER_W), lambda i, p: (i, 0)),
                  pl.BlockSpec((1, d), lambda i, p: (0, 0))],
        out_specs=hspec,
        scratch_shapes=[pltpu.VMEM((2, TOP_K * tm, d), F32), pltpu.SemaphoreType.DMA((2,))],
    )
    return pl.pallas_call(
        functools.partial(_moe_combine_kernel, tm=tm, n_tiles=m // tm, normalize=normalize),
        grid_spec=grid_spec,
        out_shape=jax.ShapeDtypeStruct((m, d), F32),
        compiler_params=_cparams(("arbitrary",)),
        name="moe_combine",
    )(pos, ys, h, gate, out_gain.reshape(1, d))


def _moe_blocks(counts, n_assign, tb):
    padded = ((counts + tb - 1) // tb) * tb
    pad_end = jnp.cumsum(padded)
    pad_start = (pad_end - padded).astype(jnp.int32)
    nblk = (n_assign + N_EXPERTS * tb) // tb
    n_valid = (pad_end[-1] // tb).astype(jnp.int32)
    blk = jnp.minimum(jnp.arange(nblk, dtype=jnp.int32), n_valid - 1) * tb
    blk_exp = jnp.sum(blk[:, None] >= pad_end[None, :], axis=1).astype(jnp.int32)
    used = counts > 0
    run_idx = (jnp.cumsum(used) - 1).astype(jnp.int32)
    ids = jnp.where(used, jnp.arange(N_EXPERTS, dtype=jnp.int32), N_EXPERTS)
    later = lax.cummin(ids[::-1])[::-1]
    nxt = jnp.concatenate([later[1:], jnp.full((1,), N_EXPERTS, jnp.int32)])
    nxt_exp = jnp.where(nxt < N_EXPERTS, nxt, -1).astype(jnp.int32)
    return pad_start, jnp.minimum(blk_exp, N_EXPERTS - 1), n_valid.reshape(1), nxt_exp, run_idx


def _moe_positions_kernel(eid_ref, rank_ref, pstart_ref, pos_ref):
    eid = eid_ref[...]
    lane = lax.broadcasted_iota(jnp.int32, eid.shape, 1)
    pstart = pstart_ref[...]
    first = [jnp.sum(jnp.where(lane == eid[:, c:c + 1], pstart, 0), axis=-1, keepdims=True)
             for c in range(TOP_K)]
    pos_ref[...] = rank_ref[...] + jnp.where(lane == 0, first[0], jnp.where(lane == 1, first[1], 0))


def moe_positions(eid, rank, pad_start, *, tm):
    m = eid.shape[0]
    spec = pl.BlockSpec((tm, ROUTER_W), lambda i: (i, 0))
    pstart = jnp.pad(pad_start, (0, ROUTER_W - pad_start.shape[0])).reshape(1, ROUTER_W)
    return pl.pallas_call(
        _moe_positions_kernel,
        grid=(m // tm,),
        in_specs=[spec, spec, pl.BlockSpec((1, ROUTER_W), lambda i: (0, 0))],
        out_specs=spec,
        out_shape=jax.ShapeDtypeStruct((m, ROUTER_W), jnp.int32),
        compiler_params=_cparams(("parallel",)),
        name="moe_positions",
    )(eid, rank, pstart)


def hier_moe(h, g, wr_group, wr_expert, w1, w3, w2, layer, out_gain, normalize_out,
             *, tm_router=512, tm_combine=128, tb=MOE_TB):
    m, d = h.shape
    xn, eid, gate, rank, counts = moe_router(h, g, wr_group, wr_expert, tm=tm_router)
    pad_start, blk_exp, n_valid, nxt_exp, run_idx = _moe_blocks(counts[0, :N_EXPERTS], m * TOP_K, tb)
    pos = moe_positions(eid, rank, pad_start, tm=1024)[:, :TOP_K].reshape(-1)
    ys = moe_experts(xn, pos, blk_exp, n_valid, nxt_exp, run_idx, w1, w3, w2, layer, tb=tb)
    return moe_combine(h, ys, gate, pos, out_gain, tm=tm_combine, normalize=normalize_out)


def _w_in_layout(w_in):
    wa = w_in[:, :IN_A]
    wb = w_in[:, IN_A:IN_A + IN_B]
    wc = w_in[:, IN_A + IN_B:IN_A + IN_B + IN_C]
    wd = w_in[:, IN_A + IN_B + IN_C:]
    wb = jnp.pad(wb, ((0, 0), (0, IN_B_PAD - IN_B)))
    return jnp.concatenate([wb, wa, wc, wd], axis=1).astype(BF16)


def kernel(x, mem, norm_mix, w_in, w_out, diff_lambda, diff_subln, rwkv_mu, rwkv_w0, rwkv_w2,
           rwkv_a0, rwkv_a2, rwkv_g2, rwkv_kk, rwkv_ka, rwkv_rk, rwkv_lnx_w, rwkv_lnx_b,
           s5_a_re, s5_a_im, s5_log_dt, s5_b_re, s5_b_im, s5_c_re, s5_c_im, s5_d, s5_glu_w,
           s5_glu_b, mix_out_norm, norm_cross, norm_mem, xa_wq, xa_wkv, xa_wo, norm_moe,
           router_group, router_expert, moe_w1, moe_w3, moe_w2, norm_final):
    b, s, d = x.shape
    m = b * s
    n_mem = mem.shape[1]
    depth = w_in.shape[0]
    h = x.reshape(m, d)
    mem2 = mem.reshape(b * n_mem, d)
    rope_a = _rope_tables(s, A_QKDIM, MIX)
    rope_c = _rope_tables(s, C_HDIM, MIX)
    for l in range(depth):
        z = norm_matmul(h, norm_mix[l], _w_in_layout(w_in[l]), tm=1024, tn=Z_W // 4, out_dtype=F32)
        z = z.reshape(b, s, Z_W)
        qa, ka, v1a = rope_qkv(z, Z_A, A_QKDIM, tm=512, ones_cols=A_VDIM, tables=rope_a)
        oa = diff_attention(qa, ka, v1a, diff_lambda[l], diff_subln[l], l, tq=512)
        ob = rwkv7_bidir(z, rwkv_mu[l], rwkv_w0[l], rwkv_w2[l], rwkv_a0[l], rwkv_a2[l], rwkv_g2[l],
                         rwkv_kk[l], rwkv_ka[l], rwkv_rk[l], rwkv_lnx_w[l], rwkv_lnx_b[l])
        qc, kc, vc = rope_qkv(z, Z_C, C_HDIM, tm=512, tables=rope_c)
        oc = dilated_attention(qc, kc, vc, tq=256)
        od = s5_bidir(z, s5_a_re[l], s5_a_im[l], s5_log_dt[l], s5_b_re[l], s5_b_im[l],
                      s5_c_re[l], s5_c_im[l], s5_d[l], s5_glu_w[l], s5_glu_b[l])
        h = mix_out(h, oa.reshape(m, MIX), ob.reshape(m, MIX), oc.reshape(m, MIX), od.reshape(m, MIX),
                    mix_out_norm[l, 0], mix_out_norm[l, 1], w_out[l].astype(BF16), tm=1024, tn=1024)
        q = norm_matmul(h, norm_cross[l], xa_wq[l].astype(BF16), tm=1024, tn=512, out_dtype=BF16)
        kv = norm_matmul(mem2, norm_mem[l], xa_wkv[l].astype(BF16), tm=b * n_mem, tn=512, out_dtype=BF16)
        h = cross_attention(h.reshape(b, s, d), q.reshape(b, s, d), kv.reshape(b, n_mem, 2 * d),
                            xa_wo[l].astype(BF16), tq=256).reshape(m, d)
        h = hier_moe(h, norm_moe[l], router_group[l], router_expert[l], moe_w1, moe_w3, moe_w2, l,
                     norm_final, l == depth - 1)
    return h.reshape(b, s, d)
```

```python
import functools
import math

import numpy as np
import jax
import jax.numpy as jnp
from jax import lax
from jax.experimental import pallas as pl
from jax.experimental.pallas import tpu as pltpu

F32 = jnp.float32
BF16 = jnp.bfloat16

D_MODEL = 2048
MIX = D_MODEL // 4
A_HEADS = 4
A_VDIM = MIX // A_HEADS
A_QKDIM = A_VDIM // 2
B_HDIM = 64
B_HEADS = MIX // B_HDIM
B_LORA = 64
B_GATE_LORA = 128
LN_X_EPS = 64e-5
C_HEADS = 4
C_HDIM = MIX // C_HEADS
C_PATTERNS = ((128, 1), (512, 4), (2048, 16))
D_GSIZE = 16
D_GROUPS = MIX // D_GSIZE
D_STATE = 64
IN_A = 3 * MIX
IN_B = 3 * MIX + 4 * B_LORA + B_GATE_LORA
IN_B_PAD = 2048
IN_C = 3 * MIX
IN_D = MIX
XA_HEADS = 4
XA_HDIM = D_MODEL // XA_HEADS
N_GROUPS = 4
EXPERTS_PER_GROUP = 8
N_EXPERTS = N_GROUPS * EXPERTS_PER_GROUP
TOP_K = 2
D_EXPERT = D_MODEL // 4
ROPE_THETA = 10000.0
RMS_EPS = 1e-6
NEG_INF = -1e30

Z_B = 0
Z_A = IN_B_PAD
Z_C = Z_A + IN_A
Z_D = Z_C + IN_C
Z_W = Z_D + IN_D

LANES = 128
VMEM_LIMIT = 56 * 1024 * 1024


def _cparams(sem, vmem=VMEM_LIMIT):
    return pltpu.CompilerParams(dimension_semantics=sem, vmem_limit_bytes=vmem)


def _rms(x, g):
    return x * lax.rsqrt(jnp.mean(x * x, axis=-1, keepdims=True) + RMS_EPS) * g


def _dot(a, b):
    return jnp.dot(a, b, preferred_element_type=F32)


def _dot_t(a, b):
    return lax.dot_general(a, b, (((1,), (1,)), ((), ())), preferred_element_type=F32)


def _split3(x):
    hi = x.astype(BF16)
    r1 = x - hi.astype(F32)
    mid = r1.astype(BF16)
    lo = (r1 - mid.astype(F32)).astype(BF16)
    return hi, mid, lo


def _segsum(x, ones):
    hi, mid, lo = _split3(x)
    return _dot(hi, ones) + _dot(mid, ones) + _dot(lo, ones)


def _norm_matmul_kernel(x_ref, g_ref, w_ref, o_ref, xn_ref):
    @pl.when(pl.program_id(1) == 0)
    def _():
        xn_ref[...] = _rms(x_ref[...], g_ref[...]).astype(BF16)

    o_ref[...] = _dot(xn_ref[...], w_ref[...]).astype(o_ref.dtype)


def norm_matmul(x, g, w, *, tm, tn, out_dtype):
    m, k = x.shape
    n = w.shape[1]
    return pl.pallas_call(
        _norm_matmul_kernel,
        grid=(m // tm, n // tn),
        in_specs=[pl.BlockSpec((tm, k), lambda i, j: (i, 0)),
                  pl.BlockSpec((1, k), lambda i, j: (0, 0)),
                  pl.BlockSpec((k, tn), lambda i, j: (0, j))],
        out_specs=pl.BlockSpec((tm, tn), lambda i, j: (i, j)),
        out_shape=jax.ShapeDtypeStruct((m, n), out_dtype),
        scratch_shapes=[pltpu.VMEM((tm, k), BF16)],
        compiler_params=_cparams(("parallel", "arbitrary")),
        name="norm_matmul",
    )(x, g.reshape(1, k), w)


def _rope_tables(seq, dim, width):
    inv = 1.0 / (ROPE_THETA ** (jnp.arange(0, dim, 2, dtype=F32) / dim))
    ang = jnp.arange(seq, dtype=F32)[:, None] * inv[None, :]
    cos, sin = jnp.cos(ang), jnp.sin(ang)
    cos = jnp.concatenate([cos, cos], axis=-1)
    sin = jnp.concatenate([-sin, sin], axis=-1)
    reps = width // dim
    return jnp.tile(cos, (1, reps)), jnp.tile(sin, (1, reps))


def _rope_kernel(q_ref, k_ref, v_ref, cos_ref, sin_ref, qo_ref, ko_ref, vo_ref, *, half, scale, ones_cols):
    cos = cos_ref[...]
    sin = sin_ref[...]
    width = cos.shape[1]
    lane = lax.broadcasted_iota(jnp.int32, cos.shape, 1)
    first = (lane % (2 * half)) < half

    def rot(x):
        ahead = pltpu.roll(x, width - half, axis=1)
        behind = pltpu.roll(x, half, axis=1)
        return x * cos + jnp.where(first, ahead, behind) * sin

    qo_ref[...] = (rot(q_ref[...]) * scale).astype(BF16)
    ko_ref[...] = rot(k_ref[...]).astype(BF16)
    v = v_ref[...].astype(BF16)
    if ones_cols is None:
        vo_ref[...] = v
    else:
        ones = jnp.ones((v.shape[0], ones_cols), BF16)
        parts = []
        for hd in range(v.shape[1] // ones_cols):
            parts += [v[:, hd * ones_cols:(hd + 1) * ones_cols], ones]
        vo_ref[...] = jnp.concatenate(parts, axis=1)


def rope_qkv(z, col0, head_dim, *, tm, ones_cols=None, tables=None):
    b, s, _ = z.shape
    cos, sin = tables if tables is not None else _rope_tables(s, head_dim, MIX)
    cb = col0 // MIX
    zspec = lambda off: pl.BlockSpec((None, tm, MIX), lambda bi, i, off=off: (bi, i, cb + off))
    tspec = pl.BlockSpec((tm, MIX), lambda bi, i: (i, 0))
    ospec = pl.BlockSpec((None, tm, MIX), lambda bi, i: (bi, i, 0))
    oshape = jax.ShapeDtypeStruct((b, s, MIX), BF16)
    if ones_cols is None:
        vspec, vshape = ospec, oshape
    else:
        vspec = pl.BlockSpec((None, tm, 2 * MIX), lambda bi, i: (bi, i, 0))
        vshape = jax.ShapeDtypeStruct((b, s, 2 * MIX), BF16)
    return pl.pallas_call(
        functools.partial(_rope_kernel, half=head_dim // 2, scale=head_dim ** -0.5, ones_cols=ones_cols),
        grid=(b, s // tm),
        in_specs=[zspec(0), zspec(1), zspec(2), tspec, tspec],
        out_specs=[ospec, ospec, vspec],
        out_shape=[oshape, oshape, vshape],
        compiler_params=_cparams(("parallel", "parallel")),
        name="rope_qkv",
    )(z, z, z, cos, sin)


def _diff_attn_kernel(lam_ref, g_ref, q_ref, k_ref, v1_ref, o_ref, *, lam_init):
    lv = lam_ref[...]
    lam = (jnp.exp(jnp.sum(lv[0:1] * lv[1:2], axis=-1, keepdims=True))
           - jnp.exp(jnp.sum(lv[2:3] * lv[3:4], axis=-1, keepdims=True)) + lam_init)
    q = q_ref[...]
    k = k_ref[...]
    v1 = v1_ref[...]
    lane = lax.broadcasted_iota(jnp.int32, q.shape, 1)
    zero = jnp.zeros_like(q)

    def branch(qm):
        s = _dot_t(qm, k)
        p = jnp.exp((s - jnp.max(s, axis=-1, keepdims=True)).astype(BF16))
        acc = _dot(p, v1)
        return acc[:, :A_VDIM] / acc[:, A_VDIM:]

    o = branch(jnp.where(lane < A_QKDIM, q, zero)) - lam * branch(jnp.where(lane >= A_QKDIM, q, zero))
    o_ref[...] = _rms(o, g_ref[...]) * (1.0 - lam_init)


def diff_attention(q, k, v1, lam_vecs, subln_g, layer_idx, *, tq):
    b, s, _ = q.shape
    lam_init = 0.8 - 0.6 * math.exp(-0.3 * layer_idx)
    qspec = pl.BlockSpec((None, tq, A_VDIM), lambda bi, h, i: (bi, i, h))
    return pl.pallas_call(
        functools.partial(_diff_attn_kernel, lam_init=lam_init),
        grid=(b, A_HEADS, s // tq),
        in_specs=[pl.BlockSpec((4, A_QKDIM), lambda bi, h, i: (0, 0)),
                  pl.BlockSpec((1, A_VDIM), lambda bi, h, i: (0, 0)),
                  qspec,
                  pl.BlockSpec((None, s, A_VDIM), lambda bi, h, i: (bi, 0, h)),
                  pl.BlockSpec((None, s, 2 * A_VDIM), lambda bi, h, i: (bi, 0, h))],
        out_specs=qspec,
        out_shape=jax.ShapeDtypeStruct((b, s, MIX), F32),
        compiler_params=_cparams(("parallel", "parallel", "arbitrary")),
        name="diff_attention",
    )(lam_vecs, subln_g.reshape(1, A_VDIM), q, k, v1)


C_REACH = max(w // 2 for w, _ in C_PATTERNS)


def _dilated_bias_table(tq, window):
    n_delta = (window - tq) // tq + 1
    i = np.arange(tq)[None, :, None]
    j = np.arange(window)[None, None, :]
    n = np.arange(n_delta)[:, None, None]
    d = j - i - n * tq
    count = np.zeros(d.shape, np.int32)
    for w, dil in C_PATTERNS:
        count += ((np.abs(d) <= w // 2) & (d % dil == 0)).astype(np.int32)
    bias = np.where(count > 0, np.log(np.maximum(count, 1)), NEG_INF)
    return jnp.asarray(bias, F32)


def _dilated_attn_kernel(bias_ref, q_ref, k_ref, v_ref, o_ref, *, tq, window, seq):
    start = pl.program_id(2) * tq
    ws = pl.multiple_of(jnp.clip(start - C_REACH, 0, seq - window), tq)
    kw = k_ref[pl.ds(ws, window), :]
    vw = v_ref[pl.ds(ws, window), :]
    s = _dot_t(q_ref[...], kw) + bias_ref[...]
    e = jnp.exp(s - jnp.max(s, axis=-1, keepdims=True))
    den = jnp.sum(e, axis=-1, keepdims=True)
    o_ref[...] = _dot(e.astype(BF16), vw) / den


def dilated_attention(q, k, v, *, tq=128):
    b, s, _ = q.shape
    window = tq + 2 * C_REACH
    assert s >= window and s % tq == 0 and C_REACH % tq == 0
    bias = _dilated_bias_table(tq, window)

    def bias_map(bi, h, i):
        start = i * tq
        ws = jnp.clip(start - C_REACH, 0, s - window)
        return ((start - ws) // tq, 0, 0)

    qspec = pl.BlockSpec((None, tq, C_HDIM), lambda bi, h, i: (bi, i, h))
    kspec = pl.BlockSpec((None, s, C_HDIM), lambda bi, h, i: (bi, 0, h))
    return pl.pallas_call(
        functools.partial(_dilated_attn_kernel, tq=tq, window=window, seq=s),
        grid=(b, C_HEADS, s // tq),
        in_specs=[pl.BlockSpec((None, tq, window), bias_map), qspec, kspec, kspec],
        out_specs=qspec,
        out_shape=jax.ShapeDtypeStruct((b, s, MIX), F32),
        compiler_params=_cparams(("parallel", "parallel", "arbitrary")),
        name="dilated_attention",
    )(bias, q, k, v)


def _s5_out_kernel(y_ref, u_ref, d_ref, w_ref, b_ref, o_ref):
    y = y_ref[...] + d_ref[...] * u_ref[...]
    gl = jax.nn.gelu(y)
    gate = jax.nn.sigmoid(_dot(gl.astype(BF16), w_ref[...]) + b_ref[...])
    o_ref[...] = gl * gate


def s5_output(y, z, d_skip, glu_w, glu_b, *, tm):
    b, s, _ = z.shape
    yspec = pl.BlockSpec((None, tm, MIX), lambda bi, i: (bi, i, 0))
    vspec = pl.BlockSpec((1, MIX), lambda bi, i: (0, 0))
    return pl.pallas_call(
        _s5_out_kernel,
        grid=(b, s // tm),
        in_specs=[yspec,
                  pl.BlockSpec((None, tm, MIX), lambda bi, i: (bi, i, Z_D // MIX)),
                  vspec, pl.BlockSpec((MIX, MIX), lambda bi, i: (0, 0)), vspec],
        out_specs=yspec,
        out_shape=jax.ShapeDtypeStruct((b, s, MIX), F32),
        compiler_params=_cparams(("parallel", "parallel")),
        name="s5_output",
    )(y, z, d_skip.reshape(1, MIX), glu_w.astype(BF16), glu_b.reshape(1, MIX))


S5_T = 16
S5_KW = S5_T * D_GSIZE
S5_SW = 2 * D_STATE


def _s5_chunk_kernel(u_ref, wm_ref, we_ref, wf_ref, ar_ref, ai_ref, y_ref, *, n_chunks):
    u = u_ref[...]
    y = _dot(u, wm_ref[...])
    xinc = _dot(u, we_ref[...])
    rows = u.shape[0]
    row = lax.broadcasted_iota(jnp.int32, (rows, S5_SW), 0) & (n_chunks - 1)
    lane = lax.broadcasted_iota(jnp.int32, (rows, S5_SW), 1)

    def cmul(t, lvl, d):
        ar = ar_ref[d, lvl:lvl + 1, :]
        ai = ai_ref[d, lvl:lvl + 1, :]
        return t * ar + pltpu.roll(t, D_STATE, axis=1) * ai

    carries = []
    for d in range(2):
        x = xinc[:, d * S5_SW:(d + 1) * S5_SW]
        for lvl in range(n_chunks.bit_length() - 1):
            sh = 1 << lvl
            if d == 0:
                prev = jnp.where(row >= sh, pltpu.roll(x, sh, axis=0), 0.0)
            else:
                prev = jnp.where(row < n_chunks - sh, pltpu.roll(x, rows - sh, axis=0), 0.0)
            x = x + cmul(prev, lvl, d)
        if d == 0:
            carries.append(jnp.where(row >= 1, pltpu.roll(x, 1, axis=0), 0.0))
        else:
            carries.append(jnp.where(row < n_chunks - 1, pltpu.roll(x, rows - 1, axis=0), 0.0))
    cin = jnp.concatenate(carries, axis=1).astype(BF16)
    y_ref[...] = y + _dot(cin, wf_ref[...])


def _cpow_table(ar, ai, count):
    res_r, res_i = [ar], [ai]
    for _ in range(count - 1):
        ar, ai = ar * ar - ai * ai, 2.0 * ar * ai
        res_r.append(ar)
        res_i.append(ai)
    return jnp.stack(res_r), jnp.stack(res_i)


def _s5_chunk_weights(a_re, a_im, log_dt, b_re, b_im, c_re, c_im, n_levels):
    steps = jnp.arange(S5_T + 1, dtype=F32)[:, None, None]
    cr, ci = c_re.astype(F32), c_im.astype(F32)
    kk, ee, ff, lvl_r, lvl_i = [], [], [], [], []
    for direction in range(2):
        lr = jnp.minimum(a_re[direction].astype(F32), -1e-4)
        li = a_im[direction].astype(F32)
        dt = jnp.exp(log_dt[direction].astype(F32))[:, None]
        mag = jnp.exp(dt * lr)
        abr, abi = mag * jnp.cos(dt * li), mag * jnp.sin(dt * li)
        den = lr * lr + li * li
        qr, qi = lr / den, -li / den
        fr = (abr - 1.0) * qr - abi * qi
        fi = (abr - 1.0) * qi + abi * qr
        br, bi = b_re.astype(F32), b_im.astype(F32)
        bbr = fr[..., None] * br - fi[..., None] * bi
        bbi = fr[..., None] * bi + fi[..., None] * br
        pmag = jnp.exp(steps * (dt * lr)[None])
        ang = steps * (dt * li)[None]
        pr, pi = pmag * jnp.cos(ang), pmag * jnp.sin(ang)
        cpr = cr[None] * pr[:S5_T, :, None, :] - ci[None] * pi[:S5_T, :, None, :]
        cpi = cr[None] * pi[:S5_T, :, None, :] + ci[None] * pr[:S5_T, :, None, :]
        kk.append(jnp.einsum('tgon,gni->tgoi', cpr, bbr) - jnp.einsum('tgon,gni->tgoi', cpi, bbi))
        order = jnp.arange(S5_T - 1, -1, -1) if direction == 0 else jnp.arange(S5_T)
        er = pr[order][..., None] * bbr[None] - pi[order][..., None] * bbi[None]
        ei = pr[order][..., None] * bbi[None] + pi[order][..., None] * bbr[None]
        e = jnp.concatenate([er, ei], axis=2)
        ee.append(e.transpose(1, 0, 3, 2).reshape(D_GROUPS, S5_KW, S5_SW))
        order = jnp.arange(1, S5_T + 1) if direction == 0 else jnp.arange(S5_T, 0, -1)
        gr = cr[None] * pr[order][:, :, None, :] - ci[None] * pi[order][:, :, None, :]
        gi = cr[None] * pi[order][:, :, None, :] + ci[None] * pr[order][:, :, None, :]
        f = jnp.concatenate([gr, -gi], axis=3)
        ff.append(f.transpose(1, 3, 0, 2).reshape(D_GROUPS, S5_SW, S5_KW))
        tr, ti = _cpow_table(pr[S5_T], pi[S5_T], n_levels)
        lvl_r.append(jnp.concatenate([tr, tr], axis=2))
        lvl_i.append(jnp.concatenate([-ti, ti], axis=2))
    s_idx = jnp.arange(S5_T)[:, None]
    j_idx = jnp.arange(S5_T)[None, :]
    kf = kk[0][jnp.clip(j_idx - s_idx, 0, S5_T - 1)]
    kb = kk[1][jnp.clip(s_idx - j_idx, 0, S5_T - 1)]
    sel = (j_idx - s_idx)[:, :, None, None, None]
    m = jnp.where(sel > 0, kf, 0.0) + jnp.where(sel < 0, kb, 0.0) + jnp.where(sel == 0, kf + kb, 0.0)
    wm = m.transpose(2, 0, 4, 1, 3).reshape(D_GROUPS, S5_KW, S5_KW)
    we = jnp.concatenate(ee, axis=2)
    wf = jnp.concatenate(ff, axis=1)
    ar = jnp.stack(lvl_r).transpose(2, 0, 1, 3)
    ai = jnp.stack(lvl_i).transpose(2, 0, 1, 3)
    return wm.astype(BF16), we.astype(BF16), wf.astype(BF16), ar, ai


def s5_chunked(u, a_re, a_im, log_dt, b_re, b_im, c_re, c_im):
    b, s, _ = u.shape
    n_chunks = s // S5_T
    n_levels = n_chunks.bit_length() - 1
    assert n_chunks == 1 << n_levels
    wm, we, wf, ar, ai = _s5_chunk_weights(a_re, a_im, log_dt, b_re, b_im, c_re, c_im, max(n_levels, 1))
    rows = b * n_chunks
    ug = u.astype(BF16).reshape(b, n_chunks, S5_T, D_GROUPS, D_GSIZE).transpose(3, 0, 1, 2, 4)
    ug = ug.reshape(D_GROUPS, rows, S5_KW)
    per_group = lambda shape: pl.BlockSpec((None,) + shape, lambda g: (g,) + (0,) * len(shape))
    yg = pl.pallas_call(
        functools.partial(_s5_chunk_kernel, n_chunks=n_chunks),
        grid=(D_GROUPS,),
        in_specs=[per_group((rows, S5_KW)), per_group((S5_KW, S5_KW)), per_group((S5_KW, 2 * S5_SW)),
                  per_group((2 * S5_SW, S5_KW)), per_group((2, ar.shape[2], S5_SW)),
                  per_group((2, ar.shape[2], S5_SW))],
        out_specs=per_group((rows, S5_KW)),
        out_shape=jax.ShapeDtypeStruct((D_GROUPS, rows, S5_KW), F32),
        compiler_params=_cparams(("parallel",)),
        name="s5_chunked",
    )(ug, wm, we, wf, ar, ai)
    yg = yg.reshape(D_GROUPS, b, n_chunks, S5_T, D_GSIZE).transpose(1, 2, 3, 0, 4)
    return yg.reshape(b, s, MIX)


def s5_bidir(z, a_re, a_im, log_dt, b_re, b_im, c_re, c_im, d_skip, glu_w, glu_b, *, tm=512):
    u = z[:, :, Z_D:Z_D + IN_D]
    y = s5_chunked(u, a_re, a_im, log_dt, b_re, b_im, c_re, c_im)
    return s5_output(y, z, d_skip, glu_w, glu_b, tm=tm)


def _softplus(y):
    return jnp.maximum(y, 0.0) + jnp.log(1.0 + jnp.exp(-jnp.abs(y)))


def _head_ones(width):
    seg = np.arange(width) // B_HDIM
    return jnp.asarray(seg[:, None] == seg[None, :], BF16)


def _rwkv_pre_kernel(z_ref, zp_ref, zn_ref, mu_ref, w0_ref, a0_ref, w2_ref, a2_ref, g2_ref,
                     kk_ref, ka_ref, rk_ref, ones_ref,
                     r_o, a_o, w0_o, w1_o, k0_o, k1_o, b0_o, b1_o, v_o, g_o, bonus_o,
                     *, tm, n_tiles):
    i = pl.program_id(1)
    z = z_ref[...]
    row = lax.broadcasted_iota(jnp.int32, z.shape, 0)
    prev_row = jnp.where(i > 0, zp_ref[7:8, :], 0.0)
    next_row = jnp.where(i < n_tiles - 1, zn_ref[0:1, :], 0.0)
    zp = jnp.where(row == 0, prev_row, pltpu.roll(z, 1, axis=0))
    zn = jnp.where(row == tm - 1, next_row, pltpu.roll(z, tm - 1, axis=0))
    xs = z + mu_ref[0:1, :] * (zp - z) + mu_ref[1:2, :] * (zn - z)
    r = xs[:, 0:MIX]
    k = xs[:, MIX:2 * MIX]
    v = xs[:, 2 * MIX:3 * MIX]
    c0 = 3 * MIX
    wd = xs[:, c0:c0 + 2 * B_LORA]
    ad = xs[:, c0 + 2 * B_LORA:c0 + 4 * B_LORA]
    gd = xs[:, c0 + 4 * B_LORA:c0 + 4 * B_LORA + B_GATE_LORA]
    lw = _dot(jnp.tanh(wd).astype(BF16), w2_ref[...])
    la = _dot(ad.astype(BF16), a2_ref[...])
    g_o[...] = _dot(jax.nn.sigmoid(gd).astype(BF16), g2_ref[...])
    ones = ones_ref[...]
    kk = k * kk_ref[...]
    kk = kk * lax.rsqrt(_segsum(kk * kk, ones) + 1e-12)
    ka = ka_ref[...]
    ksum = jnp.zeros_like(k)
    for d, (w_o, k_o, b_o) in enumerate(((w0_o, k0_o, b0_o), (w1_o, k1_o, b1_o))):
        cols = slice(d * MIX, (d + 1) * MIX)
        logw = -_softplus(-(w0_ref[d:d + 1, :] + lw[:, cols])) - 0.5
        w_o[...] = -jnp.exp(logw)
        a = jax.nn.sigmoid(a0_ref[d:d + 1, :] + la[:, cols])
        kmod = k * (1.0 + (a - 1.0) * ka)
        k_o[...] = kmod
        b_o[...] = kk * a
        ksum = ksum + kmod
    r_o[...] = r
    a_o[...] = -kk
    bonus_o[...] = _segsum(r * (0.5 * ksum) * rk_ref[...], ones) * v
    v_o[...] = v


def _block_diag2(m):
    z = jnp.zeros_like(m[0])
    return jnp.concatenate([jnp.concatenate([m[0], z], axis=1),
                            jnp.concatenate([z, m[1]], axis=1)], axis=0)


def rwkv_pre(z, mu, w0, w2, a0, a2, g2, k_k, k_a, r_k, *, tm):
    b, s, _ = z.shape
    n_tiles = s // tm
    mu_p = jnp.pad(mu.astype(F32), ((0, 0), (0, IN_B_PAD - IN_B)))
    row_spec = pl.BlockSpec((None, tm, MIX), lambda bi, i: (bi, i, 0))
    vec = lambda n, w: pl.BlockSpec((n, w), lambda bi, i: (0, 0))
    rows = jax.ShapeDtypeStruct((b, s, MIX), F32)
    hb = tm // 8
    return pl.pallas_call(
        functools.partial(_rwkv_pre_kernel, tm=tm, n_tiles=n_tiles),
        grid=(b, n_tiles),
        in_specs=[pl.BlockSpec((None, tm, IN_B_PAD), lambda bi, i: (bi, i, 0)),
                  pl.BlockSpec((None, 8, IN_B_PAD), lambda bi, i: (bi, jnp.maximum(i * hb - 1, 0), 0)),
                  pl.BlockSpec((None, 8, IN_B_PAD), lambda bi, i: (bi, jnp.minimum((i + 1) * hb, s // 8 - 1), 0)),
                  vec(2, IN_B_PAD), vec(2, MIX), vec(2, MIX),
                  vec(2 * B_LORA, 2 * MIX), vec(2 * B_LORA, 2 * MIX), vec(B_GATE_LORA, MIX),
                  vec(1, MIX), vec(1, MIX), vec(1, MIX), vec(MIX, MIX)],
        out_specs=[row_spec] * 11,
        out_shape=[rows] * 11,
        compiler_params=_cparams(("parallel", "parallel")),
        name="rwkv_pre",
    )(z, z, z, mu_p, w0.astype(F32), a0.astype(F32),
      _block_diag2(w2).astype(BF16), _block_diag2(a2).astype(BF16), g2.astype(BF16),
      k_k.reshape(1, MIX), k_a.reshape(1, MIX), r_k.reshape(1, MIX), _head_ones(MIX))


def _rwkv_scan_kernel(af_ref, rf_ref, vf_ref, lwf_ref, kf_ref, bf_ref,
                      ab_ref, rb_ref, vb_ref, lwb_ref, kb_ref, bb_ref,
                      trif_ref, trib_ref, yf_ref, yb_ref, h_ref, *, tc, nb):
    @pl.when(pl.program_id(0) == 0)
    def _():
        h_ref[...] = jnp.zeros_like(h_ref)

    row = lax.broadcasted_iota(jnp.int32, (tc, tc), 0)
    col = lax.broadcasted_iota(jnp.int32, (tc, tc), 1)
    eye = (row == col).astype(F32)
    row2 = lax.broadcasted_iota(jnp.int32, (tc, 2 * tc), 0)
    col2 = lax.broadcasted_iota(jnp.int32, (tc, 2 * tc), 1) & (tc - 1)
    zeros_tv = jnp.zeros((tc, B_HDIM), BF16)
    n_sq = tc.bit_length() - 1
    tdot = lambda x, y: lax.dot_general(x, y, (((0,), (0,)), ((), ())), preferred_element_type=F32)

    seqs = []
    for reverse, (a_ref, r_ref, v_ref, lw_ref, k_ref, b_ref, tri_ref, y_ref) in (
            (False, (af_ref, rf_ref, vf_ref, lwf_ref, kf_ref, bf_ref, trif_ref, yf_ref)),
            (True, (ab_ref, rb_ref, vb_ref, lwb_ref, kb_ref, bb_ref, trib_ref, yb_ref))):
        tri = tri_ref[...]
        before = (col2 > row2) if reverse else (col2 < row2)
        upto = (col2 >= row2) if reverse else (col2 <= row2)
        last = 0 if reverse else tc - 1
        for bi in range(nb):
            lw = lw_ref[bi]
            hi, mid, lo = _split3(lw)
            cl = _dot(tri, hi) + _dot(tri, mid) + _dot(tri, lo)
            tot = cl[last:last + 1, :]
            einv = jnp.exp(-cl)
            etot = jnp.exp(tot - cl)
            b_all = b_ref[bi]
            k_all = k_ref[bi]
            seqs.append(dict(
                a_t=a_ref[bi] * jnp.exp(cl - lw), r_t=r_ref[bi] * jnp.exp(cl),
                b_t=b_all * einv, k_t=k_all * einv, b_h=b_all * etot, k_h=k_all * etot,
                g_tot=jnp.exp(tot), v=v_ref[bi], before=before, upto=upto, y_ref=y_ref, bi=bi))

    inst = [(sq, slice(hd * B_HDIM, (hd + 1) * B_HDIM)) for sq in seqs for hd in range(B_HEADS)]
    ids = range(len(inst))
    at = [sq['a_t'][:, s].astype(BF16) for sq, s in inst]
    rt = [sq['r_t'][:, s].astype(BF16) for sq, s in inst]
    bk = [jnp.concatenate([sq['b_t'][:, s], sq['k_t'][:, s]], axis=0).astype(BF16) for sq, s in inst]
    bkh = [jnp.concatenate([sq['b_h'][:, s], sq['k_h'][:, s]], axis=0).astype(BF16) for sq, s in inst]
    vv = [sq['v'][:, s].astype(BF16) for sq, s in inst]
    g = [_dot_t(jnp.concatenate([at[n], rt[n]], axis=0), bk[n]) for n in ids]
    ga = [jnp.where(inst[n][0]['before'], g[n][:tc], 0.0) for n in ids]
    gr = [jnp.where(inst[n][0]['upto'], g[n][tc:], 0.0).astype(BF16) for n in ids]
    lkv = [_dot(ga[n].astype(BF16), jnp.concatenate([zeros_tv, vv[n]], axis=0)) for n in ids]
    lb = [ga[n][:, :tc].astype(BF16) for n in ids]
    pinv = [eye + ga[n][:, :tc] for n in ids]
    lb = [_dot(x, x).astype(BF16) for x in lb]
    for _ in range(1, n_sq - 1):
        prod = [_dot(jnp.concatenate([pinv[n].astype(BF16), lb[n]], axis=0), lb[n]) for n in ids]
        pinv = [pinv[n] + prod[n][:tc] for n in ids]
        lb = [prod[n][tc:].astype(BF16) for n in ids]
    pinv = [(pinv[n] + _dot(pinv[n].astype(BF16), lb[n])).astype(BF16) for n in ids]
    ah = [_dot(pinv[n], at[n]).astype(BF16) for n in ids]
    u0 = [_dot(pinv[n], lkv[n].astype(BF16)).astype(BF16) for n in ids]
    rhs = [jnp.concatenate([jnp.concatenate([ah[n], u0[n]], axis=1),
                            jnp.concatenate([zeros_tv, vv[n]], axis=1)], axis=0) for n in ids]
    ry = [_dot(gr[n], rhs[n]) for n in ids]
    ph = [tdot(bkh[n], rhs[n]) for n in ids]
    rh = [inst[n][0]['r_t'][:, inst[n][1]] + ry[n][:, :B_HDIM] for n in ids]
    phi = [eye * inst[n][0]['g_tot'][:, inst[n][1]] + ph[n][:, :B_HDIM] for n in ids]
    h0 = [h_ref[n].astype(BF16) for n in ids]
    fin = [_dot(jnp.concatenate([rh[n], phi[n]], axis=0).astype(BF16), h0[n]) for n in ids]
    yo = [fin[n][:tc] + ry[n][:, B_HDIM:] for n in ids]
    for si, sq in enumerate(seqs):
        sq['y_ref'][sq['bi']] = jnp.concatenate(yo[si * B_HEADS:(si + 1) * B_HEADS], axis=1)
    for n in ids:
        h_ref[n] = fin[n][tc:] + ph[n][:, B_HDIM:]


def rwkv_scan(a, r, v, lwf, kf, bvf, lwb, kb, bvb, *, tc=64):
    nb, s, _ = a.shape
    nch = s // tc
    fwd = pl.BlockSpec((nb, tc, MIX), lambda c: (0, c, 0))
    bwd = pl.BlockSpec((nb, tc, MIX), lambda c: (0, nch - 1 - c, 0))
    tspec = pl.BlockSpec((tc, tc), lambda c: (0, 0))
    t_idx = np.arange(tc)
    tri_f = jnp.asarray(t_idx[None, :] <= t_idx[:, None], BF16)
    tri_b = jnp.asarray(t_idx[None, :] >= t_idx[:, None], BF16)
    out = jax.ShapeDtypeStruct((nb, s, MIX), F32)
    return pl.pallas_call(
        functools.partial(_rwkv_scan_kernel, tc=tc, nb=nb),
        grid=(nch,),
        in_specs=[fwd] * 6 + [bwd] * 6 + [tspec, tspec],
        out_specs=[fwd, bwd],
        out_shape=[out, out],
        scratch_shapes=[pltpu.VMEM((2 * nb * B_HEADS, B_HDIM, B_HDIM), F32)],
        compiler_params=_cparams(("arbitrary",)),
        name="rwkv_scan",
    )(a, r, v, lwf, kf, bvf, a, r, v, lwb, kb, bvb, tri_f, tri_b)


def _rwkv_post_kernel(yf_ref, yb_ref, bonus_ref, g_ref, lw_ref, lb_ref, ones_ref, o_ref):
    y = yf_ref[...] + yb_ref[...]
    ones = ones_ref[...]
    mean = _segsum(y, ones) * (1.0 / B_HDIM)
    yc = y - mean
    var = _segsum(yc * yc, ones) * (1.0 / B_HDIM)
    yn = yc * lax.rsqrt(var + LN_X_EPS) * lw_ref[...] + lb_ref[...]
    o_ref[...] = (yn + bonus_ref[...]) * g_ref[...]


def rwkv_post(yf, yb, bonus, g, lnx_w, lnx_b, *, tm):
    b, s, _ = yf.shape
    row_spec = pl.BlockSpec((None, tm, MIX), lambda bi, i: (bi, i, 0))
    vec = pl.BlockSpec((1, MIX), lambda bi, i: (0, 0))
    return pl.pallas_call(
        _rwkv_post_kernel,
        grid=(b, s // tm),
        in_specs=[row_spec, row_spec, row_spec, row_spec, vec, vec,
                  pl.BlockSpec((MIX, MIX), lambda bi, i: (0, 0))],
        out_specs=row_spec,
        out_shape=jax.ShapeDtypeStruct((b, s, MIX), F32),
        compiler_params=_cparams(("parallel", "parallel")),
        name="rwkv_post",
    )(yf, yb, bonus, g, lnx_w.reshape(1, MIX), lnx_b.reshape(1, MIX), _head_ones(MIX))


def rwkv7_bidir(z, mu, w0, w2, a0, a2, g2, k_k, k_a, r_k, lnx_w, lnx_b, *, tm=256, tc=64):
    r, a, wf, wb, kf, kb, bf, bb, v, g, bonus = rwkv_pre(z, mu, w0, w2, a0, a2, g2, k_k, k_a, r_k, tm=tm)
    yf, yb = rwkv_scan(a, r, v, wf, kf, bf, wb, kb, bb, tc=tc)
    return rwkv_post(yf, yb, bonus, g, lnx_w, lnx_b, tm=tm)


def _mix_out_kernel(h_ref, oa_ref, ob_ref, oc_ref, od_ref, gc_ref, gd_ref, w_ref, o_ref, mix_ref):
    @pl.when(pl.program_id(1) == 0)
    def _():
        mix_ref[...] = jnp.concatenate(
            [oa_ref[...], ob_ref[...], _rms(oc_ref[...], gc_ref[...]), _rms(od_ref[...], gd_ref[...])],
            axis=1).astype(BF16)

    o_ref[...] = h_ref[...] + _dot(mix_ref[...], w_ref[...])


def mix_out(h, oa, ob, oc, od, gc, gd, w_out, *, tm, tn):
    m, d = h.shape
    mspec = pl.BlockSpec((tm, MIX), lambda i, j: (i, 0))
    vspec = pl.BlockSpec((1, MIX), lambda i, j: (0, 0))
    hspec = pl.BlockSpec((tm, tn), lambda i, j: (i, j))
    return pl.pallas_call(
        _mix_out_kernel,
        grid=(m // tm, d // tn),
        in_specs=[hspec, mspec, mspec, mspec, mspec, vspec, vspec,
                  pl.BlockSpec((4 * MIX, tn), lambda i, j: (0, j))],
        out_specs=hspec,
        out_shape=jax.ShapeDtypeStruct((m, d), F32),
        scratch_shapes=[pltpu.VMEM((tm, 4 * MIX), BF16)],
        compiler_params=_cparams(("parallel", "arbitrary")),
        name="mix_out",
    )(h, oa, ob, oc, od, gc.reshape(1, MIX), gd.reshape(1, MIX), w_out)


def _cross_attn_kernel(h_ref, q_ref, kv_ref, wo_ref, o_ref):
    q = q_ref[...]
    outs = []
    for hd in range(XA_HEADS):
        cols = slice(hd * XA_HDIM, (hd + 1) * XA_HDIM)
        kh = kv_ref[:, cols]
        vh = kv_ref[:, D_MODEL + hd * XA_HDIM:D_MODEL + (hd + 1) * XA_HDIM]
        s = _dot_t(q[:, cols], kh) * (XA_HDIM ** -0.5)
        e = jnp.exp(s - jnp.max(s, axis=-1, keepdims=True))
        p = e / jnp.sum(e, axis=-1, keepdims=True)
        outs.append(_dot(p.astype(BF16), vh).astype(BF16))
    o = jnp.concatenate(outs, axis=1)
    o_ref[...] = h_ref[...] + _dot(o, wo_ref[...])


def cross_attention(h, q, kv, wo, *, tq):
    b, s, d = h.shape
    n_mem = kv.shape[1]
    hspec = pl.BlockSpec((None, tq, d), lambda bi, i: (bi, i, 0))
    return pl.pallas_call(
        _cross_attn_kernel,
        grid=(b, s // tq),
        in_specs=[hspec, hspec,
                  pl.BlockSpec((None, n_mem, 2 * d), lambda bi, i: (bi, 0, 0)),
                  pl.BlockSpec((d, d), lambda bi, i: (0, 0))],
        out_specs=hspec,
        out_shape=jax.ShapeDtypeStruct((b, s, d), F32),
        compiler_params=_cparams(("parallel", "parallel")),
        name="cross_attention",
    )(h, q, kv, wo)


ROUTER_W = LANES
MOE_TB = 256


def _router_kernel(h_ref, g_ref, wr_ref, lt_ref, xn_ref, eid_ref, gate_ref, rank_ref, cnt_ref, run_ref):
    @pl.when(pl.program_id(0) == 0)
    def _():
        run_ref[...] = jnp.zeros_like(run_ref)

    xn = _rms(h_ref[...], g_ref[...])
    xn_ref[...] = xn
    logits = jnp.dot(xn, wr_ref[...], preferred_element_type=F32, precision=lax.Precision.HIGHEST)
    lane = lax.broadcasted_iota(jnp.int32, logits.shape, 1)
    big = jnp.int32(ROUTER_W)

    def first_max(mask):
        m = jnp.max(jnp.where(mask, logits, NEG_INF), axis=-1, keepdims=True)
        idx = jnp.min(jnp.where(mask & (logits == m), lane, big), axis=-1, keepdims=True)
        return m, idx

    gmask = lane < N_GROUPS
    gmax, gidx = first_max(gmask)
    g_w = 1.0 / jnp.sum(jnp.where(gmask, jnp.exp(logits - gmax), 0.0), axis=-1, keepdims=True)
    e0 = N_GROUPS + gidx * EXPERTS_PER_GROUP
    emask = (lane >= e0) & (lane < e0 + EXPERTS_PER_GROUP)
    m1, i1 = first_max(emask)
    m2, i2 = first_max(emask & (lane != i1))
    e2 = jnp.exp(m2 - m1)
    w1 = 1.0 / (1.0 + e2)
    w2 = e2 / (1.0 + e2)
    e1 = i1 - N_GROUPS
    e2i = i2 - N_GROUPS
    eid_ref[...] = jnp.where(lane == 0, e1, jnp.where(lane == 1, e2i, 0))
    gate_ref[...] = jnp.where(lane == 0, g_w * w1, jnp.where(lane == 1, g_w * w2, 0.0))
    hit1 = lane == e1
    hit2 = lane == e2i
    onehot = jnp.where(hit1 | hit2, 1.0, 0.0)
    earlier = run_ref[...] + _dot(lt_ref[...], onehot.astype(BF16))
    r1 = jnp.sum(jnp.where(hit1, earlier, 0.0), axis=-1, keepdims=True)
    r2 = jnp.sum(jnp.where(hit2, earlier, 0.0), axis=-1, keepdims=True)
    rank_ref[...] = jnp.where(lane == 0, r1, jnp.where(lane == 1, r2, 0.0)).astype(jnp.int32)
    run_ref[...] = run_ref[...] + jnp.sum(onehot, axis=0, keepdims=True)
    cnt_ref[...] = run_ref[...].astype(jnp.int32)


def moe_router(h, g, wr_group, wr_expert, *, tm):
    m, d = h.shape
    wr = jnp.concatenate([wr_group, wr_expert], axis=1).astype(F32)
    wr = jnp.pad(wr, ((0, 0), (0, ROUTER_W - wr.shape[1])))
    t_idx = np.arange(tm)
    lower = jnp.asarray(t_idx[None, :] < t_idx[:, None], BF16)
    hspec = pl.BlockSpec((tm, d), lambda i: (i, 0))
    lspec = pl.BlockSpec((tm, ROUTER_W), lambda i: (i, 0))
    cspec = pl.BlockSpec((1, ROUTER_W), lambda i: (0, 0))
    lane_i = jax.ShapeDtypeStruct((m, ROUTER_W), jnp.int32)
    return pl.pallas_call(
        _router_kernel,
        grid=(m // tm,),
        in_specs=[hspec, pl.BlockSpec((1, d), lambda i: (0, 0)),
                  pl.BlockSpec((d, ROUTER_W), lambda i: (0, 0)),
                  pl.BlockSpec((tm, tm), lambda i: (0, 0))],
        out_specs=[hspec, lspec, lspec, lspec, cspec],
        out_shape=[jax.ShapeDtypeStruct((m, d), F32), lane_i,
                   jax.ShapeDtypeStruct((m, ROUTER_W), F32), lane_i,
                   jax.ShapeDtypeStruct((1, ROUTER_W), jnp.int32)],
        scratch_shapes=[pltpu.VMEM((1, ROUTER_W), F32)],
        compiler_params=_cparams(("arbitrary",)),
        name="moe_router",
    )(h, g.reshape(1, d), wr, lower)


GATHER_UNROLL = 8


def _start_row_gather(row_of, src_hbm, dst, sem, n_rows, dst_row0=0):
    def issue(r, c):
        pltpu.make_async_copy(src_hbm.at[pl.ds(row_of(r), 1)], dst.at[pl.ds(dst_row0 + r, 1)], sem).start()
        return c

    lax.fori_loop(0, n_rows, issue, 0, unroll=GATHER_UNROLL)


def _wait_row_gather(src_hbm, dst, sem, n_rows):
    pltpu.make_async_copy(src_hbm.at[pl.ds(0, n_rows)], dst, sem).wait()


def _moe_expert_kernel(pos_ref, exp_ref, nvb_ref, nxt_ref, run_ref, x_hbm, w1_hbm, w3_hbm, w2_hbm, o_ref,
                       slot_ref, xbuf, sem, w1buf, w3buf, w2buf, wsem, *, tb, n_assign, layer):
    i = pl.program_id(0)
    nvb = nvb_ref[0]

    def weight_copies(e, s):
        return (pltpu.make_async_copy(w1_hbm.at[layer, e], w1buf.at[s], wsem.at[s, 0]),
                pltpu.make_async_copy(w3_hbm.at[layer, e], w3buf.at[s], wsem.at[s, 1]),
                pltpu.make_async_copy(w2_hbm.at[layer, e], w2buf.at[s], wsem.at[s, 2]))

    e_cur = exp_ref[i]
    first = (i == 0) | (exp_ref[jnp.maximum(i - 1, 0)] != e_cur)
    wslot = run_ref[e_cur] & 1

    @pl.when(i == 0)
    def _():
        for cp in weight_copies(e_cur, wslot):
            cp.start(priority=1)

    @pl.when((i < nvb) & first & (nxt_ref[e_cur] >= 0))
    def _():
        for cp in weight_copies(nxt_ref[e_cur], 1 - wslot):
            cp.start(priority=1)

    @pl.when(i == 0)
    def _():
        def clear(s, c):
            slot_ref[s] = 0
            return c

        lax.fori_loop(0, slot_ref.shape[0], clear, 0, unroll=GATHER_UNROLL)

        def place(a, c):
            slot_ref[pos_ref[a]] = lax.shift_right_logical(a, TOP_K.bit_length() - 1)
            return c

        lax.fori_loop(0, n_assign, place, 0, unroll=GATHER_UNROLL)
        _start_row_gather(lambda r: slot_ref[r], x_hbm, xbuf.at[0], sem.at[0], tb)

    @pl.when(i + 1 < nvb)
    def _():
        nxt = (i + 1) & 1
        base = (i + 1) * tb
        _start_row_gather(lambda r: slot_ref[base + r], x_hbm, xbuf.at[nxt], sem.at[nxt], tb)

    @pl.when(i < nvb)
    def _():
        @pl.when(first)
        def _():
            for cp in weight_copies(e_cur, wslot):
                cp.wait()

        cur = i & 1
        _wait_row_gather(x_hbm, xbuf.at[cur], sem.at[cur], tb)
        x = xbuf[cur].astype(BF16)
        h1 = _dot(x, w1buf[wslot].astype(BF16))
        h3 = _dot(x, w3buf[wslot].astype(BF16))
        act = (jax.nn.silu(h1) * h3).astype(BF16)
        o_ref[...] = _dot(act, w2buf[wslot].astype(BF16))

    @pl.when(i >= nvb)
    def _():
        o_ref[...] = jnp.zeros_like(o_ref)


def moe_experts(xn, pos, blk_exp, n_valid, nxt_exp, run_idx, w1, w3, w2, layer, *, tb):
    n_assign = pos.shape[0]
    cap = n_assign + N_EXPERTS * tb
    d = xn.shape[1]
    nblk = cap // tb
    hbm = pl.BlockSpec(memory_space=pl.ANY)
    grid_spec = pltpu.PrefetchScalarGridSpec(
        num_scalar_prefetch=5,
        grid=(nblk,),
        in_specs=[hbm, hbm, hbm, hbm],
        out_specs=pl.BlockSpec((tb, d), lambda i, *_: (i, 0)),
        scratch_shapes=[pltpu.SMEM((cap,), jnp.int32), pltpu.VMEM((2, tb, d), F32),
                        pltpu.SemaphoreType.DMA((2,)),
                        pltpu.VMEM((2, d, D_EXPERT), F32), pltpu.VMEM((2, d, D_EXPERT), F32),
                        pltpu.VMEM((2, D_EXPERT, d), F32), pltpu.SemaphoreType.DMA((2, 3))],
    )
    return pl.pallas_call(
        functools.partial(_moe_expert_kernel, tb=tb, n_assign=n_assign, layer=layer),
        grid_spec=grid_spec,
        out_shape=jax.ShapeDtypeStruct((cap, d), F32),
        compiler_params=_cparams(("arbitrary",)),
        name="moe_experts",
    )(pos, blk_exp, n_valid, nxt_exp, run_idx, xn, w1, w3, w2)


def _moe_combine_kernel(pos_ref, ys_hbm, h_ref, gate_ref, g_ref, o_ref, buf, sem, *, tm, n_tiles, normalize):
    i = pl.program_id(0)

    def start(tile, slot):
        for choice in range(TOP_K):
            def row_of(r, choice=choice):
                return pos_ref[(tile * tm + r) * TOP_K + choice]

            _start_row_gather(row_of, ys_hbm, buf.at[slot], sem.at[slot], tm, dst_row0=choice * tm)

    @pl.when(i == 0)
    def _():
        start(0, 0)

    @pl.when(i + 1 < n_tiles)
    def _():
        start(i + 1, (i + 1) & 1)

    cur = i & 1
    _wait_row_gather(ys_hbm, buf.at[cur], sem.at[cur], TOP_K * tm)
    gate = gate_ref[...]
    out = h_ref[...] + gate[:, 0:1] * buf[cur, 0:tm, :] + gate[:, 1:2] * buf[cur, tm:2 * tm, :]
    o_ref[...] = _rms(out, g_ref[...]) if normalize else out


def moe_combine(h, ys, gate, pos, out_gain, *, tm, normalize):
    m, d = h.shape
    hspec = pl.BlockSpec((tm, d), lambda i, p: (i, 0))
    grid_spec = pltpu.PrefetchScalarGridSpec(
        num_scalar_prefetch=1,
        grid=(m // tm,),
        in_specs=[pl.BlockSpec(memory_space=pl.ANY), hspec,
                  pl.BlockSpec((tm, ROUTER_W), lambda i, p: (i, 0)),
                  pl.BlockSpec((1, d), lambda i, p: (0, 0))],
        out_specs=hspec,
        scratch_shapes=[pltpu.VMEM((2, TOP_K * tm, d), F32), pltpu.SemaphoreType.DMA((2,))],
    )
    return pl.pallas_call(
        functools.partial(_moe_combine_kernel, tm=tm, n_tiles=m // tm, normalize=normalize),
        grid_spec=grid_spec,
        out_shape=jax.ShapeDtypeStruct((m, d), F32),
        compiler_params=_cparams(("arbitrary",)),
        name="moe_combine",
    )(pos, ys, h, gate, out_gain.reshape(1, d))


def _moe_blocks(counts, n_assign, tb):
    padded = ((counts + tb - 1) // tb) * tb
    pad_end = jnp.cumsum(padded)
    pad_start = (pad_end - padded).astype(jnp.int32)
    nblk = (n_assign + N_EXPERTS * tb) // tb
    n_valid = (pad_end[-1] // tb).astype(jnp.int32)
    blk = jnp.minimum(jnp.arange(nblk, dtype=jnp.int32), n_valid - 1) * tb
    blk_exp = jnp.sum(blk[:, None] >= pad_end[None, :], axis=1).astype(jnp.int32)
    used = counts > 0
    run_idx = (jnp.cumsum(used) - 1).astype(jnp.int32)
    ids = jnp.where(used, jnp.arange(N_EXPERTS, dtype=jnp.int32), N_EXPERTS)
    later = lax.cummin(ids[::-1])[::-1]
    nxt = jnp.concatenate([later[1:], jnp.full((1,), N_EXPERTS, jnp.int32)])
    nxt_exp = jnp.where(nxt < N_EXPERTS, nxt, -1).astype(jnp.int32)
    return pad_start, jnp.minimum(blk_exp, N_EXPERTS - 1), n_valid.reshape(1), nxt_exp, run_idx


def _moe_positions_kernel(eid_ref, rank_ref, pstart_ref, pos_ref):
    eid = eid_ref[...]
    lane = lax.broadcasted_iota(jnp.int32, eid.shape, 1)
    pstart = pstart_ref[...]
    first = [jnp.sum(jnp.where(lane == eid[:, c:c + 1], pstart, 0), axis=-1, keepdims=True)
             for c in range(TOP_K)]
    pos_ref[...] = rank_ref[...] + jnp.where(lane == 0, first[0], jnp.where(lane == 1, first[1], 0))


def moe_positions(eid, rank, pad_start, *, tm):
    m = eid.shape[0]
    spec = pl.BlockSpec((tm, ROUTER_W), lambda i: (i, 0))
    pstart = jnp.pad(pad_start, (0, ROUTER_W - pad_start.shape[0])).reshape(1, ROUTER_W)
    return pl.pallas_call(
        _moe_positions_kernel,
        grid=(m // tm,),
        in_specs=[spec, spec, pl.BlockSpec((1, ROUTER_W), lambda i: (0, 0))],
        out_specs=spec,
        out_shape=jax.ShapeDtypeStruct((m, ROUTER_W), jnp.int32),
        compiler_params=_cparams(("parallel",)),
        name="moe_positions",
    )(eid, rank, pstart)


def hier_moe(h, g, wr_group, wr_expert, w1, w3, w2, layer, out_gain, normalize_out,
             *, tm_router=512, tm_combine=128, tb=MOE_TB):
    m, d = h.shape
    xn, eid, gate, rank, counts = moe_router(h, g, wr_group, wr_expert, tm=tm_router)
    pad_start, blk_exp, n_valid, nxt_exp, run_idx = _moe_blocks(counts[0, :N_EXPERTS], m * TOP_K, tb)
    pos = moe_positions(eid, rank, pad_start, tm=1024)[:, :TOP_K].reshape(-1)
    ys = moe_experts(xn, pos, blk_exp, n_valid, nxt_exp, run_idx, w1, w3, w2, layer, tb=tb)
    return moe_combine(h, ys, gate, pos, out_gain, tm=tm_combine, normalize=normalize_out)


def _w_in_layout(w_in):
    wa = w_in[:, :IN_A]
    wb = w_in[:, IN_A:IN_A + IN_B]
    wc = w_in[:, IN_A + IN_B:IN_A + IN_B + IN_C]
    wd = w_in[:, IN_A + IN_B + IN_C:]
    wb = jnp.pad(wb, ((0, 0), (0, IN_B_PAD - IN_B)))
    return jnp.concatenate([wb, wa, wc, wd], axis=1).astype(BF16)


def kernel(x, mem, norm_mix, w_in, w_out, diff_lambda, diff_subln, rwkv_mu, rwkv_w0, rwkv_w2,
           rwkv_a0, rwkv_a2, rwkv_g2, rwkv_kk, rwkv_ka, rwkv_rk, rwkv_lnx_w, rwkv_lnx_b,
           s5_a_re, s5_a_im, s5_log_dt, s5_b_re, s5_b_im, s5_c_re, s5_c_im, s5_d, s5_glu_w,
           s5_glu_b, mix_out_norm, norm_cross, norm_mem, xa_wq, xa_wkv, xa_wo, norm_moe,
           router_group, router_expert, moe_w1, moe_w3, moe_w2, norm_final):
    b, s, d = x.shape
    m = b * s
    n_mem = mem.shape[1]
    depth = w_in.shape[0]
    h = x.reshape(m, d)
    mem2 = mem.reshape(b * n_mem, d)
    rope_a = _rope_tables(s, A_QKDIM, MIX)
    rope_c = _rope_tables(s, C_HDIM, MIX)
    for l in range(depth):
        z = norm_matmul(h, norm_mix[l], _w_in_layout(w_in[l]), tm=1024, tn=Z_W // 4, out_dtype=F32)
        z = z.reshape(b, s, Z_W)
        qa, ka, v1a = rope_qkv(z, Z_A, A_QKDIM, tm=512, ones_cols=A_VDIM, tables=rope_a)
        oa = diff_attention(qa, ka, v1a, diff_lambda[l], diff_subln[l], l, tq=512)
        ob = rwkv7_bidir(z, rwkv_mu[l], rwkv_w0[l], rwkv_w2[l], rwkv_a0[l], rwkv_a2[l], rwkv_g2[l],
                         rwkv_kk[l], rwkv_ka[l], rwkv_rk[l], rwkv_lnx_w[l], rwkv_lnx_b[l])
        qc, kc, vc = rope_qkv(z, Z_C, C_HDIM, tm=512, tables=rope_c)
        oc = dilated_attention(qc, kc, vc, tq=256)
        od = s5_bidir(z, s5_a_re[l], s5_a_im[l], s5_log_dt[l], s5_b_re[l], s5_b_im[l],
                      s5_c_re[l], s5_c_im[l], s5_d[l], s5_glu_w[l], s5_glu_b[l])
        h = mix_out(h, oa.reshape(m, MIX), ob.reshape(m, MIX), oc.reshape(m, MIX), od.reshape(m, MIX),
                    mix_out_norm[l, 0], mix_out_norm[l, 1], w_out[l].astype(BF16), tm=1024, tn=1024)
        q = norm_matmul(h, norm_cross[l], xa_wq[l].astype(BF16), tm=1024, tn=512, out_dtype=BF16)
        kv = norm_matmul(mem2, norm_mem[l], xa_wkv[l].astype(BF16), tm=b * n_mem, tn=512, out_dtype=BF16)
        h = cross_attention(h.reshape(b, s, d), q.reshape(b, s, d), kv.reshape(b, n_mem, 2 * d),
                            xa_wo[l].astype(BF16), tq=256).reshape(m, d)
        h = hier_moe(h, norm_moe[l], router_group[l], router_expert[l], moe_w1, moe_w3, moe_w2, l,
                     norm_final, l == depth - 1)
    return h.reshape(b, s, d)
```

```python
import functools
import math

import numpy as np
import jax
import jax.numpy as jnp
from jax import lax
from jax.experimental import pallas as pl
from jax.experimental.pallas import tpu as pltpu

F32 = jnp.float32
BF16 = jnp.bfloat16

D_MODEL = 2048
MIX = D_MODEL // 4
A_HEADS = 4
A_VDIM = MIX // A_HEADS
A_QKDIM = A_VDIM // 2
B_HDIM = 64
B_HEADS = MIX // B_HDIM
B_LORA = 64
B_GATE_LORA = 128
LN_X_EPS = 64e-5
C_HEADS = 4
C_HDIM = MIX // C_HEADS
C_PATTERNS = ((128, 1), (512, 4), (2048, 16))
D_GSIZE = 16
D_GROUPS = MIX // D_GSIZE
D_STATE = 64
IN_A = 3 * MIX
IN_B = 3 * MIX + 4 * B_LORA + B_GATE_LORA
IN_B_PAD = 2048
IN_C = 3 * MIX
IN_D = MIX
XA_HEADS = 4
XA_HDIM = D_MODEL // XA_HEADS
N_GROUPS = 4
EXPERTS_PER_GROUP = 8
N_EXPERTS = N_GROUPS * EXPERTS_PER_GROUP
TOP_K = 2
D_EXPERT = D_MODEL // 4
ROPE_THETA = 10000.0
RMS_EPS = 1e-6
NEG_INF = -1e30

Z_B = 0
Z_A = IN_B_PAD
Z_C = Z_A + IN_A
Z_D = Z_C + IN_C
Z_W = Z_D + IN_D

LANES = 128
VMEM_LIMIT = 56 * 1024 * 1024


def _cparams(sem, vmem=VMEM_LIMIT):
    return pltpu.CompilerParams(dimension_semantics=sem, vmem_limit_bytes=vmem)


def _rms(x, g):
    return x * lax.rsqrt(jnp.mean(x * x, axis=-1, keepdims=True) + RMS_EPS) * g


def _dot(a, b):
    return jnp.dot(a, b, preferred_element_type=F32)


def _dot_t(a, b):
    return lax.dot_general(a, b, (((1,), (1,)), ((), ())), preferred_element_type=F32)


def _split3(x):
    hi = x.astype(BF16)
    r1 = x - hi.astype(F32)
    mid = r1.astype(BF16)
    lo = (r1 - mid.astype(F32)).astype(BF16)
    return hi, mid, lo


def _segsum(x, ones):
    hi, mid, lo = _split3(x)
    return _dot(hi, ones) + _dot(mid, ones) + _dot(lo, ones)


def _norm_matmul_kernel(x_ref, g_ref, w_ref, o_ref, xn_ref):
    @pl.when(pl.program_id(1) == 0)
    def _():
        xn_ref[...] = _rms(x_ref[...], g_ref[...]).astype(BF16)

    o_ref[...] = _dot(xn_ref[...], w_ref[...]).astype(o_ref.dtype)


def norm_matmul(x, g, w, *, tm, tn, out_dtype):
    m, k = x.shape
    n = w.shape[1]
    return pl.pallas_call(
        _norm_matmul_kernel,
        grid=(m // tm, n // tn),
        in_specs=[pl.BlockSpec((tm, k), lambda i, j: (i, 0)),
                  pl.BlockSpec((1, k), lambda i, j: (0, 0)),
                  pl.BlockSpec((k, tn), lambda i, j: (0, j))],
        out_specs=pl.BlockSpec((tm, tn), lambda i, j: (i, j)),
        out_shape=jax.ShapeDtypeStruct((m, n), out_dtype),
        scratch_shapes=[pltpu.VMEM((tm, k), BF16)],
        compiler_params=_cparams(("parallel", "arbitrary")),
        name="norm_matmul",
    )(x, g.reshape(1, k), w)


def _rope_tables(seq, dim, width):
    inv = 1.0 / (ROPE_THETA ** (jnp.arange(0, dim, 2, dtype=F32) / dim))
    ang = jnp.arange(seq, dtype=F32)[:, None] * inv[None, :]
    cos, sin = jnp.cos(ang), jnp.sin(ang)
    cos = jnp.concatenate([cos, cos], axis=-1)
    sin = jnp.concatenate([-sin, sin], axis=-1)
    reps = width // dim
    return jnp.tile(cos, (1, reps)), jnp.tile(sin, (1, reps))


def _rope_kernel(q_ref, k_ref, v_ref, cos_ref, sin_ref, qo_ref, ko_ref, vo_ref, *, half, scale, ones_cols):
    cos = cos_ref[...]
    sin = sin_ref[...]
    width = cos.shape[1]
    lane = lax.broadcasted_iota(jnp.int32, cos.shape, 1)
    first = (lane % (2 * half)) < half

    def rot(x):
        ahead = pltpu.roll(x, width - half, axis=1)
        behind = pltpu.roll(x, half, axis=1)
        return x * cos + jnp.where(first, ahead, behind) * sin

    qo_ref[...] = (rot(q_ref[...]) * scale).astype(BF16)
    ko_ref[...] = rot(k_ref[...]).astype(BF16)
    v = v_ref[...].astype(BF16)
    if ones_cols is None:
        vo_ref[...] = v
    else:
        ones = jnp.ones((v.shape[0], ones_cols), BF16)
        parts = []
        for hd in range(v.shape[1] // ones_cols):
            parts += [v[:, hd * ones_cols:(hd + 1) * ones_cols], ones]
        vo_ref[...] = jnp.concatenate(parts, axis=1)


def rope_qkv(z, col0, head_dim, *, tm, ones_cols=None, tables=None):
    b, s, _ = z.shape
    cos, sin = tables if tables is not None else _rope_tables(s, head_dim, MIX)
    cb = col0 // MIX
    zspec = lambda off: pl.BlockSpec((None, tm, MIX), lambda bi, i, off=off: (bi, i, cb + off))
    tspec = pl.BlockSpec((tm, MIX), lambda bi, i: (i, 0))
    ospec = pl.BlockSpec((None, tm, MIX), lambda bi, i: (bi, i, 0))
    oshape = jax.ShapeDtypeStruct((b, s, MIX), BF16)
    if ones_cols is None:
        vspec, vshape = ospec, oshape
    else:
        vspec = pl.BlockSpec((None, tm, 2 * MIX), lambda bi, i: (bi, i, 0))
        vshape = jax.ShapeDtypeStruct((b, s, 2 * MIX), BF16)
    return pl.pallas_call(
        functools.partial(_rope_kernel, half=head_dim // 2, scale=head_dim ** -0.5, ones_cols=ones_cols),
        grid=(b, s // tm),
        in_specs=[zspec(0), zspec(1), zspec(2), tspec, tspec],
        out_specs=[ospec, ospec, vspec],
        out_shape=[oshape, oshape, vshape],
        compiler_params=_cparams(("parallel", "parallel")),
        name="rope_qkv",
    )(z, z, z, cos, sin)


def _diff_attn_kernel(lam_ref, g_ref, q_ref, k_ref, v1_ref, o_ref, *, lam_init):
    lv = lam_ref[...]
    lam = (jnp.exp(jnp.sum(lv[0:1] * lv[1:2], axis=-1, keepdims=True))
           - jnp.exp(jnp.sum(lv[2:3] * lv[3:4], axis=-1, keepdims=True)) + lam_init)
    q = q_ref[...]
    k = k_ref[...]
    v1 = v1_ref[...]
    lane = lax.broadcasted_iota(jnp.int32, q.shape, 1)
    zero = jnp.zeros_like(q)

    def branch(qm):
        s = _dot_t(qm, k)
        p = jnp.exp((s - jnp.max(s, axis=-1, keepdims=True)).astype(BF16))
        acc = _dot(p, v1)
        return acc[:, :A_VDIM] / acc[:, A_VDIM:]

    o = branch(jnp.where(lane < A_QKDIM, q, zero)) - lam * branch(jnp.where(lane >= A_QKDIM, q, zero))
    o_ref[...] = _rms(o, g_ref[...]) * (1.0 - lam_init)


def diff_attention(q, k, v1, lam_vecs, subln_g, layer_idx, *, tq):
    b, s, _ = q.shape
    lam_init = 0.8 - 0.6 * math.exp(-0.3 * layer_idx)
    qspec = pl.BlockSpec((None, tq, A_VDIM), lambda bi, h, i: (bi, i, h))
    return pl.pallas_call(
        functools.partial(_diff_attn_kernel, lam_init=lam_init),
        grid=(b, A_HEADS, s // tq),
        in_specs=[pl.BlockSpec((4, A_QKDIM), lambda bi, h, i: (0, 0)),
                  pl.BlockSpec((1, A_VDIM), lambda bi, h, i: (0, 0)),
                  qspec,
                  pl.BlockSpec((None, s, A_VDIM), lambda bi, h, i: (bi, 0, h)),
                  pl.BlockSpec((None, s, 2 * A_VDIM), lambda bi, h, i: (bi, 0, h))],
        out_specs=qspec,
        out_shape=jax.ShapeDtypeStruct((b, s, MIX), F32),
        compiler_params=_cparams(("parallel", "parallel", "arbitrary")),
        name="diff_attention",
    )(lam_vecs, subln_g.reshape(1, A_VDIM), q, k, v1)


C_REACH = max(w // 2 for w, _ in C_PATTERNS)


def _dilated_bias_table(tq, window):
    n_delta = (window - tq) // tq + 1
    i = np.arange(tq)[None, :, None]
    j = np.arange(window)[None, None, :]
    n = np.arange(n_delta)[:, None, None]
    d = j - i - n * tq
    count = np.zeros(d.shape, np.int32)
    for w, dil in C_PATTERNS:
        count += ((np.abs(d) <= w // 2) & (d % dil == 0)).astype(np.int32)
    bias = np.where(count > 0, np.log(np.maximum(count, 1)), NEG_INF)
    return jnp.asarray(bias, F32)


def _dilated_attn_kernel(bias_ref, q_ref, k_ref, v_ref, o_ref, *, tq, window, seq):
    start = pl.program_id(2) * tq
    ws = pl.multiple_of(jnp.clip(start - C_REACH, 0, seq - window), tq)
    kw = k_ref[pl.ds(ws, window), :]
    vw = v_ref[pl.ds(ws, window), :]
    s = _dot_t(q_ref[...], kw) + bias_ref[...]
    e = jnp.exp(s - jnp.max(s, axis=-1, keepdims=True))
    den = jnp.sum(e, axis=-1, keepdims=True)
    o_ref[...] = _dot(e.astype(BF16), vw) / den


def dilated_attention(q, k, v, *, tq=128):
    b, s, _ = q.shape
    window = tq + 2 * C_REACH
    assert s >= window and s % tq == 0 and C_REACH % tq == 0
    bias = _dilated_bias_table(tq, window)

    def bias_map(bi, h, i):
        start = i * tq
        ws = jnp.clip(start - C_REACH, 0, s - window)
        return ((start - ws) // tq, 0, 0)

    qspec = pl.BlockSpec((None, tq, C_HDIM), lambda bi, h, i: (bi, i, h))
    kspec = pl.BlockSpec((None, s, C_HDIM), lambda bi, h, i: (bi, 0, h))
    return pl.pallas_call(
        functools.partial(_dilated_attn_kernel, tq=tq, window=window, seq=s),
        grid=(b, C_HEADS, s // tq),
        in_specs=[pl.BlockSpec((None, tq, window), bias_map), qspec, kspec, kspec],
        out_specs=qspec,
        out_shape=jax.ShapeDtypeStruct((b, s, MIX), F32),
        compiler_params=_cparams(("parallel", "parallel", "arbitrary")),
        name="dilated_attention",
    )(bias, q, k, v)


def _s5_out_kernel(y_ref, u_ref, d_ref, w_ref, b_ref, o_ref):
    y = y_ref[...] + d_ref[...] * u_ref[...]
    gl = jax.nn.gelu(y)
    gate = jax.nn.sigmoid(_dot(gl.astype(BF16), w_ref[...]) + b_ref[...])
    o_ref[...] = gl * gate


def s5_output(y, z, d_skip, glu_w, glu_b, *, tm):
    b, s, _ = z.shape
    yspec = pl.BlockSpec((None, tm, MIX), lambda bi, i: (bi, i, 0))
    vspec = pl.BlockSpec((1, MIX), lambda bi, i: (0, 0))
    return pl.pallas_call(
        _s5_out_kernel,
        grid=(b, s // tm),
        in_specs=[yspec,
                  pl.BlockSpec((None, tm, MIX), lambda bi, i: (bi, i, Z_D // MIX)),
                  vspec, pl.BlockSpec((MIX, MIX), lambda bi, i: (0, 0)), vspec],
        out_specs=yspec,
        out_shape=jax.ShapeDtypeStruct((b, s, MIX), F32),
        compiler_params=_cparams(("parallel", "parallel")),
        name="s5_output",
    )(y, z, d_skip.reshape(1, MIX), glu_w.astype(BF16), glu_b.reshape(1, MIX))


S5_T = 16
S5_KW = S5_T * D_GSIZE
S5_SW = 2 * D_STATE


S5_GPS = LANES // D_GSIZE


def _s5_chunk_kernel(u_ref, wm_ref, we_ref, wft_ref, ar_ref, ai_ref, y_ref, *, n_chunks, nb):
    rows = nb * n_chunks
    lane = lax.broadcasted_iota(jnp.int32, (n_chunks, LANES), 1)
    seg = [(lane >= D_GSIZE * j) & (lane < D_GSIZE * (j + 1)) for j in range(S5_GPS)]
    per_tile = LANES // D_GSIZE
    u_t = [[u_ref[b, pl.ds(t, n_chunks, stride=S5_T), :] for t in range(S5_T)] for b in range(nb)]
    row = lax.broadcasted_iota(jnp.int32, (rows, S5_SW), 0) & (n_chunks - 1)

    def lane_move(x, src, dst):
        shift = (D_GSIZE * (dst - src)) % LANES
        return pltpu.roll(x, shift, axis=1) if shift else x

    y_groups = []
    for gl in range(S5_GPS):
        packed = []
        for b in range(nb):
            tiles = []
            for half in range(S5_T // per_tile):
                acc = jnp.zeros((n_chunks, LANES), F32)
                for j in range(per_tile):
                    acc = jnp.where(seg[j], lane_move(u_t[b][half * per_tile + j], gl, j), acc)
                tiles.append(acc)
            packed.append(jnp.concatenate(tiles, axis=1))
        u = jnp.concatenate(packed, axis=0).astype(BF16)
        y = _dot(u, wm_ref[gl])
        xinc = _dot(u, we_ref[gl])

        def cmul(t, lvl, d, gl=gl):
            ar = ar_ref[gl, d, lvl:lvl + 1, :]
            ai = ai_ref[gl, d, lvl:lvl + 1, :]
            return t * ar + pltpu.roll(t, D_STATE, axis=1) * ai

        carries = []
        for d in range(2):
            x = xinc[:, d * S5_SW:(d + 1) * S5_SW]
            for lvl in range(n_chunks.bit_length() - 1):
                sh = 1 << lvl
                if d == 0:
                    prev = jnp.where(row >= sh, pltpu.roll(x, sh, axis=0), 0.0)
                else:
                    prev = jnp.where(row < n_chunks - sh, pltpu.roll(x, rows - sh, axis=0), 0.0)
                x = x + cmul(prev, lvl, d)
            if d == 0:
                carries.append(jnp.where(row >= 1, pltpu.roll(x, 1, axis=0), 0.0))
            else:
                carries.append(jnp.where(row < n_chunks - 1, pltpu.roll(x, rows - 1, axis=0), 0.0))
        cin = jnp.concatenate(carries, axis=1).astype(BF16)
        y_groups.append(y + _dot_t(cin, wft_ref[gl]))

    for b in range(nb):
        for t in range(S5_T):
            half, j = divmod(t, per_tile)
            out = jnp.zeros((n_chunks, LANES), F32)
            for gl in range(S5_GPS):
                tile = y_groups[gl][b * n_chunks:(b + 1) * n_chunks, half * LANES:(half + 1) * LANES]
                out = jnp.where(seg[gl], lane_move(tile, j, gl), out)
            y_ref[b, pl.ds(t, n_chunks, stride=S5_T), :] = out


def _cpow_table(ar, ai, count):
    res_r, res_i = [ar], [ai]
    for _ in range(count - 1):
        ar, ai = ar * ar - ai * ai, 2.0 * ar * ai
        res_r.append(ar)
        res_i.append(ai)
    return jnp.stack(res_r), jnp.stack(res_i)


def _s5_chunk_weights(a_re, a_im, log_dt, b_re, b_im, c_re, c_im, n_levels):
    g_n, t_n, c_n = D_GROUPS, S5_T, D_GSIZE
    taus = jnp.arange(t_n + 1, dtype=F32)[None, :, None]
    cr, ci = c_re.astype(F32)[:, None], c_im.astype(F32)[:, None]
    toe, ee, ff, lvl_r, lvl_i = [], [], [], [], []
    for direction in range(2):
        lr = jnp.minimum(a_re[direction].astype(F32), -1e-4)
        li = a_im[direction].astype(F32)
        dt = jnp.exp(log_dt[direction].astype(F32))[:, None]
        mag = jnp.exp(dt * lr)
        abr, abi = mag * jnp.cos(dt * li), mag * jnp.sin(dt * li)
        den = lr * lr + li * li
        qr, qi = lr / den, -li / den
        fr = (abr - 1.0) * qr - abi * qi
        fi = (abr - 1.0) * qi + abi * qr
        br, bi = b_re.astype(F32), b_im.astype(F32)
        btr = (fr[..., None] * br - fi[..., None] * bi).transpose(0, 2, 1)
        bti = (fr[..., None] * bi + fi[..., None] * br).transpose(0, 2, 1)
        pmag = jnp.exp(taus * (dt * lr)[:, None])
        ang = taus * (dt * li)[:, None]
        pr, pi = pmag * jnp.cos(ang), pmag * jnp.sin(ang)

        def c_times(order):
            por, poi = pr[:, order][:, :, None, :], pi[:, order][:, :, None, :]
            return cr * por - ci * poi, cr * poi + ci * por

        cpr, cpi = c_times(jnp.arange(t_n))
        k = (jnp.einsum('gin,gkn->gik', btr, cpr.reshape(g_n, t_n * c_n, D_STATE))
             - jnp.einsum('gin,gkn->gik', bti, cpi.reshape(g_n, t_n * c_n, D_STATE)))
        pad = (t_n - 1) * c_n
        if direction == 0:
            kp = jnp.pad(k, ((0, 0), (0, 0), (pad, 0)))
        else:
            krev = k.reshape(g_n, c_n, t_n, c_n)[:, :, ::-1, :].reshape(g_n, c_n, t_n * c_n)
            kp = jnp.pad(krev, ((0, 0), (0, 0), (0, pad)))
        toe.append(jnp.stack([kp[:, :, (t_n - 1 - s) * c_n:(t_n - 1 - s) * c_n + t_n * c_n]
                              for s in range(t_n)], axis=1))
        order = jnp.arange(t_n - 1, -1, -1) if direction == 0 else jnp.arange(t_n)
        por, poi = pr[:, order][:, :, None, :], pi[:, order][:, :, None, :]
        er = por * btr[:, None] - poi * bti[:, None]
        ei = por * bti[:, None] + poi * btr[:, None]
        ee.append(jnp.concatenate([er, ei], axis=3).reshape(g_n, S5_KW, S5_SW))
        order = jnp.arange(1, t_n + 1) if direction == 0 else jnp.arange(t_n, 0, -1)
        gr, gi = c_times(order)
        ff.append(jnp.concatenate([gr, -gi], axis=3).reshape(g_n, S5_KW, S5_SW))
        tr, ti = _cpow_table(pr[:, t_n], pi[:, t_n], n_levels)
        lvl_r.append(jnp.concatenate([tr, tr], axis=2))
        lvl_i.append(jnp.concatenate([-ti, ti], axis=2))
    wm = (toe[0] + toe[1]).reshape(g_n, S5_KW, S5_KW)
    we = jnp.concatenate(ee, axis=2)
    wft = jnp.concatenate(ff, axis=2)
    ar = jnp.stack(lvl_r).transpose(2, 0, 1, 3)
    ai = jnp.stack(lvl_i).transpose(2, 0, 1, 3)
    return wm.astype(BF16), we.astype(BF16), wft.astype(BF16), ar, ai


def s5_chunked(z, a_re, a_im, log_dt, b_re, b_im, c_re, c_im):
    b, s, _ = z.shape
    n_chunks = s // S5_T
    n_levels = n_chunks.bit_length() - 1
    assert n_chunks == 1 << n_levels
    wm, we, wft, ar, ai = _s5_chunk_weights(a_re, a_im, log_dt, b_re, b_im, c_re, c_im, max(n_levels, 1))
    groups = lambda shape: pl.BlockSpec((S5_GPS,) + shape, lambda k: (k,) + (0,) * len(shape))
    return pl.pallas_call(
        functools.partial(_s5_chunk_kernel, n_chunks=n_chunks, nb=b),
        grid=(D_GROUPS // S5_GPS,),
        in_specs=[pl.BlockSpec((b, s, LANES), lambda k: (0, 0, Z_D // LANES + k)),
                  groups((S5_KW, S5_KW)), groups((S5_KW, 2 * S5_SW)), groups((S5_KW, 2 * S5_SW)),
                  groups((2, ar.shape[2], S5_SW)), groups((2, ar.shape[2], S5_SW))],
        out_specs=pl.BlockSpec((b, s, LANES), lambda k: (0, 0, k)),
        out_shape=jax.ShapeDtypeStruct((b, s, MIX), F32),
        compiler_params=_cparams(("parallel",)),
        name="s5_chunked",
    )(z, wm, we, wft, ar, ai)


def s5_bidir(z, a_re, a_im, log_dt, b_re, b_im, c_re, c_im, d_skip, glu_w, glu_b, *, tm=512):
    y = s5_chunked(z, a_re, a_im, log_dt, b_re, b_im, c_re, c_im)
    return s5_output(y, z, d_skip, glu_w, glu_b, tm=tm)


def _softplus(y):
    return jnp.maximum(y, 0.0) + jnp.log(1.0 + jnp.exp(-jnp.abs(y)))


def _head_ones(width):
    seg = np.arange(width) // B_HDIM
    return jnp.asarray(seg[:, None] == seg[None, :], BF16)


def _rwkv_pre_kernel(z_ref, zp_ref, zn_ref, mu_ref, w0_ref, a0_ref, w2_ref, a2_ref, g2_ref,
                     kk_ref, ka_ref, rk_ref, ones_ref,
                     r_o, a_o, w0_o, w1_o, k0_o, k1_o, b0_o, b1_o, v_o, g_o, bonus_o,
                     *, tm, n_tiles):
    i = pl.program_id(1)
    z = z_ref[...]
    row = lax.broadcasted_iota(jnp.int32, z.shape, 0)
    prev_row = jnp.where(i > 0, zp_ref[7:8, :], 0.0)
    next_row = jnp.where(i < n_tiles - 1, zn_ref[0:1, :], 0.0)
    zp = jnp.where(row == 0, prev_row, pltpu.roll(z, 1, axis=0))
    zn = jnp.where(row == tm - 1, next_row, pltpu.roll(z, tm - 1, axis=0))
    xs = z + mu_ref[0:1, :] * (zp - z) + mu_ref[1:2, :] * (zn - z)
    r = xs[:, 0:MIX]
    k = xs[:, MIX:2 * MIX]
    v = xs[:, 2 * MIX:3 * MIX]
    c0 = 3 * MIX
    wd = xs[:, c0:c0 + 2 * B_LORA]
    ad = xs[:, c0 + 2 * B_LORA:c0 + 4 * B_LORA]
    gd = xs[:, c0 + 4 * B_LORA:c0 + 4 * B_LORA + B_GATE_LORA]
    lw = _dot(jnp.tanh(wd).astype(BF16), w2_ref[...])
    la = _dot(ad.astype(BF16), a2_ref[...])
    g_o[...] = _dot(jax.nn.sigmoid(gd).astype(BF16), g2_ref[...])
    ones = ones_ref[...]
    kk = k * kk_ref[...]
    kk = kk * lax.rsqrt(_segsum(kk * kk, ones) + 1e-12)
    ka = ka_ref[...]
    ksum = jnp.zeros_like(k)
    for d, (w_o, k_o, b_o) in enumerate(((w0_o, k0_o, b0_o), (w1_o, k1_o, b1_o))):
        cols = slice(d * MIX, (d + 1) * MIX)
        logw = -_softplus(-(w0_ref[d:d + 1, :] + lw[:, cols])) - 0.5
        w_o[...] = -jnp.exp(logw)
        a = jax.nn.sigmoid(a0_ref[d:d + 1, :] + la[:, cols])
        kmod = k * (1.0 + (a - 1.0) * ka)
        k_o[...] = kmod
        b_o[...] = kk * a
        ksum = ksum + kmod
    r_o[...] = r
    a_o[...] = -kk
    bonus_o[...] = _segsum(r * (0.5 * ksum) * rk_ref[...], ones) * v
    v_o[...] = v


def _block_diag2(m):
    z = jnp.zeros_like(m[0])
    return jnp.concatenate([jnp.concatenate([m[0], z], axis=1),
                            jnp.concatenate([z, m[1]], axis=1)], axis=0)


def rwkv_pre(z, mu, w0, w2, a0, a2, g2, k_k, k_a, r_k, *, tm):
    b, s, _ = z.shape
    n_tiles = s // tm
    mu_p = jnp.pad(mu.astype(F32), ((0, 0), (0, IN_B_PAD - IN_B)))
    row_spec = pl.BlockSpec((None, tm, MIX), lambda bi, i: (bi, i, 0))
    vec = lambda n, w: pl.BlockSpec((n, w), lambda bi, i: (0, 0))
    rows = jax.ShapeDtypeStruct((b, s, MIX), F32)
    hb = tm // 8
    return pl.pallas_call(
        functools.partial(_rwkv_pre_kernel, tm=tm, n_tiles=n_tiles),
        grid=(b, n_tiles),
        in_specs=[pl.BlockSpec((None, tm, IN_B_PAD), lambda bi, i: (bi, i, 0)),
                  pl.BlockSpec((None, 8, IN_B_PAD), lambda bi, i: (bi, jnp.maximum(i * hb - 1, 0), 0)),
                  pl.BlockSpec((None, 8, IN_B_PAD), lambda bi, i: (bi, jnp.minimum((i + 1) * hb, s // 8 - 1), 0)),
                  vec(2, IN_B_PAD), vec(2, MIX), vec(2, MIX),
                  vec(2 * B_LORA, 2 * MIX), vec(2 * B_LORA, 2 * MIX), vec(B_GATE_LORA, MIX),
                  vec(1, MIX), vec(1, MIX), vec(1, MIX), vec(MIX, MIX)],
        out_specs=[row_spec] * 11,
        out_shape=[rows] * 11,
        compiler_params=_cparams(("parallel", "parallel")),
        name="rwkv_pre",
    )(z, z, z, mu_p, w0.astype(F32), a0.astype(F32),
      _block_diag2(w2).astype(BF16), _block_diag2(a2).astype(BF16), g2.astype(BF16),
      k_k.reshape(1, MIX), k_a.reshape(1, MIX), r_k.reshape(1, MIX), _head_ones(MIX))


def _rwkv_scan_kernel(af_ref, rf_ref, vf_ref, lwf_ref, kf_ref, bf_ref,
                      ab_ref, rb_ref, vb_ref, lwb_ref, kb_ref, bb_ref,
                      trif_ref, trib_ref, yf_ref, yb_ref, h_ref, *, tc, nb):
    @pl.when(pl.program_id(0) == 0)
    def _():
        h_ref[...] = jnp.zeros_like(h_ref)

    row = lax.broadcasted_iota(jnp.int32, (tc, tc), 0)
    col = lax.broadcasted_iota(jnp.int32, (tc, tc), 1)
    eye = (row == col).astype(F32)
    row2 = lax.broadcasted_iota(jnp.int32, (tc, 2 * tc), 0)
    col2 = lax.broadcasted_iota(jnp.int32, (tc, 2 * tc), 1) & (tc - 1)
    zeros_tv = jnp.zeros((tc, B_HDIM), BF16)
    n_sq = tc.bit_length() - 1
    tdot = lambda x, y: lax.dot_general(x, y, (((0,), (0,)), ((), ())), preferred_element_type=F32)

    seqs = []
    for reverse, (a_ref, r_ref, v_ref, lw_ref, k_ref, b_ref, tri_ref, y_ref) in (
            (False, (af_ref, rf_ref, vf_ref, lwf_ref, kf_ref, bf_ref, trif_ref, yf_ref)),
            (True, (ab_ref, rb_ref, vb_ref, lwb_ref, kb_ref, bb_ref, trib_ref, yb_ref))):
        tri = tri_ref[...]
        before = (col2 > row2) if reverse else (col2 < row2)
        upto = (col2 >= row2) if reverse else (col2 <= row2)
        last = 0 if reverse else tc - 1
        for bi in range(nb):
            lw = lw_ref[bi]
            hi, mid, lo = _split3(lw)
            cl = _dot(tri, hi) + _dot(tri, mid) + _dot(tri, lo)
            tot = cl[last:last + 1, :]
            einv = jnp.exp(-cl)
            etot = jnp.exp(tot - cl)
            b_all = b_ref[bi]
            k_all = k_ref[bi]
            seqs.append(dict(
                a_t=a_ref[bi] * jnp.exp(cl - lw), r_t=r_ref[bi] * jnp.exp(cl),
                b_t=b_all * einv, k_t=k_all * einv, b_h=b_all * etot, k_h=k_all * etot,
                g_tot=jnp.exp(tot), v=v_ref[bi], before=before, upto=upto, y_ref=y_ref, bi=bi))

    inst = [(sq, slice(hd * B_HDIM, (hd + 1) * B_HDIM)) for sq in seqs for hd in range(B_HEADS)]
    ids = range(len(inst))
    at = [sq['a_t'][:, s].astype(BF16) for sq, s in inst]
    rt = [sq['r_t'][:, s].astype(BF16) for sq, s in inst]
    bk = [jnp.concatenate([sq['b_t'][:, s], sq['k_t'][:, s]], axis=0).astype(BF16) for sq, s in inst]
    bkh = [jnp.concatenate([sq['b_h'][:, s], sq['k_h'][:, s]], axis=0).astype(BF16) for sq, s in inst]
    vv = [sq['v'][:, s].astype(BF16) for sq, s in inst]
    g = [_dot_t(jnp.concatenate([at[n], rt[n]], axis=0), bk[n]) for n in ids]
    ga = [jnp.where(inst[n][0]['before'], g[n][:tc], 0.0) for n in ids]
    gr = [jnp.where(inst[n][0]['upto'], g[n][tc:], 0.0).astype(BF16) for n in ids]
    lkv = [_dot(ga[n].astype(BF16), jnp.concatenate([zeros_tv, vv[n]], axis=0)) for n in ids]
    lb = [ga[n][:, :tc].astype(BF16) for n in ids]
    pinv = [eye + ga[n][:, :tc] for n in ids]
    lb = [_dot(x, x).astype(BF16) for x in lb]
    for _ in range(1, n_sq - 1):
        prod = [_dot(jnp.concatenate([pinv[n].astype(BF16), lb[n]], axis=0), lb[n]) for n in ids]
        pinv = [pinv[n] + prod[n][:tc] for n in ids]
        lb = [prod[n][tc:].astype(BF16) for n in ids]
    pinv = [(pinv[n] + _dot(pinv[n].astype(BF16), lb[n])).astype(BF16) for n in ids]
    ah = [_dot(pinv[n], at[n]).astype(BF16) for n in ids]
    u0 = [_dot(pinv[n], lkv[n].astype(BF16)).astype(BF16) for n in ids]
    rhs = [jnp.concatenate([jnp.concatenate([ah[n], u0[n]], axis=1),
                            jnp.concatenate([zeros_tv, vv[n]], axis=1)], axis=0) for n in ids]
    ry = [_dot(gr[n], rhs[n]) for n in ids]
    ph = [tdot(bkh[n], rhs[n]) for n in ids]
    rh = [inst[n][0]['r_t'][:, inst[n][1]] + ry[n][:, :B_HDIM] for n in ids]
    phi = [eye * inst[n][0]['g_tot'][:, inst[n][1]] + ph[n][:, :B_HDIM] for n in ids]
    h0 = [h_ref[n].astype(BF16) for n in ids]
    fin = [_dot(jnp.concatenate([rh[n], phi[n]], axis=0).astype(BF16), h0[n]) for n in ids]
    yo = [fin[n][:tc] + ry[n][:, B_HDIM:] for n in ids]
    for si, sq in enumerate(seqs):
        sq['y_ref'][sq['bi']] = jnp.concatenate(yo[si * B_HEADS:(si + 1) * B_HEADS], axis=1)
    for n in ids:
        h_ref[n] = fin[n][tc:] + ph[n][:, B_HDIM:]


def rwkv_scan(a, r, v, lwf, kf, bvf, lwb, kb, bvb, *, tc=64):
    nb, s, _ = a.shape
    nch = s // tc
    fwd = pl.BlockSpec((nb, tc, MIX), lambda c: (0, c, 0))
    bwd = pl.BlockSpec((nb, tc, MIX), lambda c: (0, nch - 1 - c, 0))
    tspec = pl.BlockSpec((tc, tc), lambda c: (0, 0))
    t_idx = np.arange(tc)
    tri_f = jnp.asarray(t_idx[None, :] <= t_idx[:, None], BF16)
    tri_b = jnp.asarray(t_idx[None, :] >= t_idx[:, None], BF16)
    out = jax.ShapeDtypeStruct((nb, s, MIX), F32)
    return pl.pallas_call(
        functools.partial(_rwkv_scan_kernel, tc=tc, nb=nb),
        grid=(nch,),
        in_specs=[fwd] * 6 + [bwd] * 6 + [tspec, tspec],
        out_specs=[fwd, bwd],
        out_shape=[out, out],
        scratch_shapes=[pltpu.VMEM((2 * nb * B_HEADS, B_HDIM, B_HDIM), F32)],
        compiler_params=_cparams(("arbitrary",)),
        name="rwkv_scan",
    )(a, r, v, lwf, kf, bvf, a, r, v, lwb, kb, bvb, tri_f, tri_b)


def _rwkv_post_kernel(yf_ref, yb_ref, bonus_ref, g_ref, lw_ref, lb_ref, ones_ref, o_ref):
    y = yf_ref[...] + yb_ref[...]
    ones = ones_ref[...]
    mean = _segsum(y, ones) * (1.0 / B_HDIM)
    yc = y - mean
    var = _segsum(yc * yc, ones) * (1.0 / B_HDIM)
    yn = yc * lax.rsqrt(var + LN_X_EPS) * lw_ref[...] + lb_ref[...]
    o_ref[...] = (yn + bonus_ref[...]) * g_ref[...]


def rwkv_post(yf, yb, bonus, g, lnx_w, lnx_b, *, tm):
    b, s, _ = yf.shape
    row_spec = pl.BlockSpec((None, tm, MIX), lambda bi, i: (bi, i, 0))
    vec = pl.BlockSpec((1, MIX), lambda bi, i: (0, 0))
    return pl.pallas_call(
        _rwkv_post_kernel,
        grid=(b, s // tm),
        in_specs=[row_spec, row_spec, row_spec, row_spec, vec, vec,
                  pl.BlockSpec((MIX, MIX), lambda bi, i: (0, 0))],
        out_specs=row_spec,
        out_shape=jax.ShapeDtypeStruct((b, s, MIX), F32),
        compiler_params=_cparams(("parallel", "parallel")),
        name="rwkv_post",
    )(yf, yb, bonus, g, lnx_w.reshape(1, MIX), lnx_b.reshape(1, MIX), _head_ones(MIX))


def rwkv7_bidir(z, mu, w0, w2, a0, a2, g2, k_k, k_a, r_k, lnx_w, lnx_b, *, tm=256, tc=64):
    r, a, wf, wb, kf, kb, bf, bb, v, g, bonus = rwkv_pre(z, mu, w0, w2, a0, a2, g2, k_k, k_a, r_k, tm=tm)
    yf, yb = rwkv_scan(a, r, v, wf, kf, bf, wb, kb, bb, tc=tc)
    return rwkv_post(yf, yb, bonus, g, lnx_w, lnx_b, tm=tm)


def _mix_out_kernel(h_ref, oa_ref, ob_ref, oc_ref, od_ref, gc_ref, gd_ref, w_ref, o_ref, mix_ref):
    @pl.when(pl.program_id(1) == 0)
    def _():
        mix_ref[...] = jnp.concatenate(
            [oa_ref[...], ob_ref[...], _rms(oc_ref[...], gc_ref[...]), _rms(od_ref[...], gd_ref[...])],
            axis=1).astype(BF16)

    o_ref[...] = h_ref[...] + _dot(mix_ref[...], w_ref[...])


def mix_out(h, oa, ob, oc, od, gc, gd, w_out, *, tm, tn):
    m, d = h.shape
    mspec = pl.BlockSpec((tm, MIX), lambda i, j: (i, 0))
    vspec = pl.BlockSpec((1, MIX), lambda i, j: (0, 0))
    hspec = pl.BlockSpec((tm, tn), lambda i, j: (i, j))
    return pl.pallas_call(
        _mix_out_kernel,
        grid=(m // tm, d // tn),
        in_specs=[hspec, mspec, mspec, mspec, mspec, vspec, vspec,
                  pl.BlockSpec((4 * MIX, tn), lambda i, j: (0, j))],
        out_specs=hspec,
        out_shape=jax.ShapeDtypeStruct((m, d), F32),
        scratch_shapes=[pltpu.VMEM((tm, 4 * MIX), BF16)],
        compiler_params=_cparams(("parallel", "arbitrary")),
        name="mix_out",
    )(h, oa, ob, oc, od, gc.reshape(1, MIX), gd.reshape(1, MIX), w_out)


def _cross_attn_kernel(h_ref, q_ref, kv_ref, wo_ref, o_ref):
    q = q_ref[...]
    outs = []
    for hd in range(XA_HEADS):
        cols = slice(hd * XA_HDIM, (hd + 1) * XA_HDIM)
        kh = kv_ref[:, cols]
        vh = kv_ref[:, D_MODEL + hd * XA_HDIM:D_MODEL + (hd + 1) * XA_HDIM]
        s = _dot_t(q[:, cols], kh) * (XA_HDIM ** -0.5)
        e = jnp.exp(s - jnp.max(s, axis=-1, keepdims=True))
        p = e / jnp.sum(e, axis=-1, keepdims=True)
        outs.append(_dot(p.astype(BF16), vh).astype(BF16))
    o = jnp.concatenate(outs, axis=1)
    o_ref[...] = h_ref[...] + _dot(o, wo_ref[...])


def cross_attention(h, q, kv, wo, *, tq):
    b, s, d = h.shape
    n_mem = kv.shape[1]
    hspec = pl.BlockSpec((None, tq, d), lambda bi, i: (bi, i, 0))
    return pl.pallas_call(
        _cross_attn_kernel,
        grid=(b, s // tq),
        in_specs=[hspec, hspec,
                  pl.BlockSpec((None, n_mem, 2 * d), lambda bi, i: (bi, 0, 0)),
                  pl.BlockSpec((d, d), lambda bi, i: (0, 0))],
        out_specs=hspec,
        out_shape=jax.ShapeDtypeStruct((b, s, d), F32),
        compiler_params=_cparams(("parallel", "parallel")),
        name="cross_attention",
    )(h, q, kv, wo)


ROUTER_W = LANES
MOE_TB = 256


def _router_kernel(h_ref, g_ref, wr_ref, lt_ref, xn_ref, eid_ref, gate_ref, rank_ref, cnt_ref, run_ref):
    @pl.when(pl.program_id(0) == 0)
    def _():
        run_ref[...] = jnp.zeros_like(run_ref)

    xn = _rms(h_ref[...], g_ref[...])
    xn_ref[...] = xn
    logits = jnp.dot(xn, wr_ref[...], preferred_element_type=F32, precision=lax.Precision.HIGHEST)
    lane = lax.broadcasted_iota(jnp.int32, logits.shape, 1)
    big = jnp.int32(ROUTER_W)

    def first_max(mask):
        m = jnp.max(jnp.where(mask, logits, NEG_INF), axis=-1, keepdims=True)
        idx = jnp.min(jnp.where(mask & (logits == m), lane, big), axis=-1, keepdims=True)
        return m, idx

    gmask = lane < N_GROUPS
    gmax, gidx = first_max(gmask)
    g_w = 1.0 / jnp.sum(jnp.where(gmask, jnp.exp(logits - gmax), 0.0), axis=-1, keepdims=True)
    e0 = N_GROUPS + gidx * EXPERTS_PER_GROUP
    emask = (lane >= e0) & (lane < e0 + EXPERTS_PER_GROUP)
    m1, i1 = first_max(emask)
    m2, i2 = first_max(emask & (lane != i1))
    e2 = jnp.exp(m2 - m1)
    w1 = 1.0 / (1.0 + e2)
    w2 = e2 / (1.0 + e2)
    e1 = i1 - N_GROUPS
    e2i = i2 - N_GROUPS
    eid_ref[...] = jnp.where(lane == 0, e1, jnp.where(lane == 1, e2i, 0))
    gate_ref[...] = jnp.where(lane == 0, g_w * w1, jnp.where(lane == 1, g_w * w2, 0.0))
    hit1 = lane == e1
    hit2 = lane == e2i
    onehot = jnp.where(hit1 | hit2, 1.0, 0.0)
    earlier = run_ref[...] + _dot(lt_ref[...], onehot.astype(BF16))
    r1 = jnp.sum(jnp.where(hit1, earlier, 0.0), axis=-1, keepdims=True)
    r2 = jnp.sum(jnp.where(hit2, earlier, 0.0), axis=-1, keepdims=True)
    rank_ref[...] = jnp.where(lane == 0, r1, jnp.where(lane == 1, r2, 0.0)).astype(jnp.int32)
    run_ref[...] = run_ref[...] + jnp.sum(onehot, axis=0, keepdims=True)
    cnt_ref[...] = run_ref[...].astype(jnp.int32)


def moe_router(h, g, wr_group, wr_expert, *, tm):
    m, d = h.shape
    wr = jnp.concatenate([wr_group, wr_expert], axis=1).astype(F32)
    wr = jnp.pad(wr, ((0, 0), (0, ROUTER_W - wr.shape[1])))
    t_idx = np.arange(tm)
    lower = jnp.asarray(t_idx[None, :] < t_idx[:, None], BF16)
    hspec = pl.BlockSpec((tm, d), lambda i: (i, 0))
    lspec = pl.BlockSpec((tm, ROUTER_W), lambda i: (i, 0))
    cspec = pl.BlockSpec((1, ROUTER_W), lambda i: (0, 0))
    lane_i = jax.ShapeDtypeStruct((m, ROUTER_W), jnp.int32)
    return pl.pallas_call(
        _router_kernel,
        grid=(m // tm,),
        in_specs=[hspec, pl.BlockSpec((1, d), lambda i: (0, 0)),
                  pl.BlockSpec((d, ROUTER_W), lambda i: (0, 0)),
                  pl.BlockSpec((tm, tm), lambda i: (0, 0))],
        out_specs=[hspec, lspec, lspec, lspec, cspec],
        out_shape=[jax.ShapeDtypeStruct((m, d), F32), lane_i,
                   jax.ShapeDtypeStruct((m, ROUTER_W), F32), lane_i,
                   jax.ShapeDtypeStruct((1, ROUTER_W), jnp.int32)],
        scratch_shapes=[pltpu.VMEM((1, ROUTER_W), F32)],
        compiler_params=_cparams(("arbitrary",)),
        name="moe_router",
    )(h, g.reshape(1, d), wr, lower)


GATHER_UNROLL = 8


def _start_row_gather(row_of, src_hbm, dst, sem, n_rows, dst_row0=0):
    def issue(r, c):
        pltpu.make_async_copy(src_hbm.at[pl.ds(row_of(r), 1)], dst.at[pl.ds(dst_row0 + r, 1)], sem).start()
        return c

    lax.fori_loop(0, n_rows, issue, 0, unroll=GATHER_UNROLL)


def _wait_row_gather(src_hbm, dst, sem, n_rows):
    pltpu.make_async_copy(src_hbm.at[pl.ds(0, n_rows)], dst, sem).wait()


def _moe_expert_kernel(pos_ref, exp_ref, nvb_ref, nxt_ref, run_ref, x_hbm, w1_hbm, w3_hbm, w2_hbm, o_ref,
                       slot_ref, xbuf, sem, w1buf, w3buf, w2buf, wsem, *, tb, n_assign, layer):
    i = pl.program_id(0)
    nvb = nvb_ref[0]

    def weight_copies(e, s):
        return (pltpu.make_async_copy(w1_hbm.at[layer, e], w1buf.at[s], wsem.at[s, 0]),
                pltpu.make_async_copy(w3_hbm.at[layer, e], w3buf.at[s], wsem.at[s, 1]),
                pltpu.make_async_copy(w2_hbm.at[layer, e], w2buf.at[s], wsem.at[s, 2]))

    e_cur = exp_ref[i]
    first = (i == 0) | (exp_ref[jnp.maximum(i - 1, 0)] != e_cur)
    wslot = run_ref[e_cur] & 1

    @pl.when(i == 0)
    def _():
        for cp in weight_copies(e_cur, wslot):
            cp.start()

    @pl.when((i < nvb) & first & (nxt_ref[e_cur] >= 0))
    def _():
        for cp in weight_copies(nxt_ref[e_cur], 1 - wslot):
            cp.start()

    @pl.when(i == 0)
    def _():
        def clear(s, c):
            slot_ref[s] = 0
            return c

        lax.fori_loop(0, slot_ref.shape[0], clear, 0, unroll=GATHER_UNROLL)

        def place(a, c):
            slot_ref[pos_ref[a]] = lax.shift_right_logical(a, TOP_K.bit_length() - 1)
            return c

        lax.fori_loop(0, n_assign, place, 0, unroll=GATHER_UNROLL)
        _start_row_gather(lambda r: slot_ref[r], x_hbm, xbuf.at[0], sem.at[0], tb)

    @pl.when(i + 1 < nvb)
    def _():
        nxt = (i + 1) & 1
        base = (i + 1) * tb
        _start_row_gather(lambda r: slot_ref[base + r], x_hbm, xbuf.at[nxt], sem.at[nxt], tb)

    @pl.when(i < nvb)
    def _():
        @pl.when(first)
        def _():
            for cp in weight_copies(e_cur, wslot):
                cp.wait()

        cur = i & 1
        _wait_row_gather(x_hbm, xbuf.at[cur], sem.at[cur], tb)
        x = xbuf[cur].astype(BF16)
        h1 = _dot(x, w1buf[wslot].astype(BF16))
        h3 = _dot(x, w3buf[wslot].astype(BF16))
        act = (jax.nn.silu(h1) * h3).astype(BF16)
        o_ref[...] = _dot(act, w2buf[wslot].astype(BF16))

    @pl.when(i >= nvb)
    def _():
        o_ref[...] = jnp.zeros_like(o_ref)


def moe_experts(xn, pos, blk_exp, n_valid, nxt_exp, run_idx, w1, w3, w2, layer, *, tb):
    n_assign = pos.shape[0]
    cap = n_assign + N_EXPERTS * tb
    d = xn.shape[1]
    nblk = cap // tb
    hbm = pl.BlockSpec(memory_space=pl.ANY)
    grid_spec = pltpu.PrefetchScalarGridSpec(
        num_scalar_prefetch=5,
        grid=(nblk,),
        in_specs=[hbm, hbm, hbm, hbm],
        out_specs=pl.BlockSpec((tb, d), lambda i, *_: (i, 0)),
        scratch_shapes=[pltpu.SMEM((cap,), jnp.int32), pltpu.VMEM((2, tb, d), F32),
                        pltpu.SemaphoreType.DMA((2,)),
                        pltpu.VMEM((2, d, D_EXPERT), F32), pltpu.VMEM((2, d, D_EXPERT), F32),
                        pltpu.VMEM((2, D_EXPERT, d), F32), pltpu.SemaphoreType.DMA((2, 3))],
    )
    return pl.pallas_call(
        functools.partial(_moe_expert_kernel, tb=tb, n_assign=n_assign, layer=layer),
        grid_spec=grid_spec,
        out_shape=jax.ShapeDtypeStruct((cap, d), F32),
        compiler_params=_cparams(("arbitrary",)),
        name="moe_experts",
    )(pos, blk_exp, n_valid, nxt_exp, run_idx, xn, w1, w3, w2)


def _moe_combine_kernel(pos_ref, ys_hbm, h_ref, gate_ref, g_ref, o_ref, buf, sem, *, tm, n_tiles, normalize):
    i = pl.program_id(0)

    def start(tile, slot):
        for choice in range(TOP_K):
            def row_of(r, choice=choice):
                return pos_ref[(tile * tm + r) * TOP_K + choice]

            _start_row_gather(row_of, ys_hbm, buf.at[slot], sem.at[slot], tm, dst_row0=choice * tm)

    @pl.when(i == 0)
    def _():
        start(0, 0)

    @pl.when(i + 1 < n_tiles)
    def _():
        start(i + 1, (i + 1) & 1)

    cur = i & 1
    _wait_row_gather(ys_hbm, buf.at[cur], sem.at[cur], TOP_K * tm)
    gate = gate_ref[...]
    out = h_ref[...] + gate[:, 0:1] * buf[cur, 0:tm, :] + gate[:, 1:2] * buf[cur, tm:2 * tm, :]
    o_ref[...] = _rms(out, g_ref[...]) if normalize else out


def moe_combine(h, ys, gate, pos, out_gain, *, tm, normalize):
    m, d = h.shape
    hspec = pl.BlockSpec((tm, d), lambda i, p: (i, 0))
    grid_spec = pltpu.PrefetchScalarGridSpec(
        num_scalar_prefetch=1,
        grid=(m // tm,),
        in_specs=[pl.BlockSpec(memory_space=pl.ANY), hspec,
                  pl.BlockSpec((tm, ROUTER_W), lambda i, p: (i, 0)),
                  pl.BlockSpec((1, d), lambda i, p: (0, 0))],
        out_specs=hspec,
        scratch_shapes=[pltpu.VMEM((2, TOP_K * tm, d), F32), pltpu.SemaphoreType.DMA((2,))],
    )
    return pl.pallas_call(
        functools.partial(_moe_combine_kernel, tm=tm, n_tiles=m // tm, normalize=normalize),
        grid_spec=grid_spec,
        out_shape=jax.ShapeDtypeStruct((m, d), F32),
        compiler_params=_cparams(("arbitrary",)),
        name="moe_combine",
    )(pos, ys, h, gate, out_gain.reshape(1, d))


def _moe_blocks(counts, n_assign, tb):
    padded = ((counts + tb - 1) // tb) * tb
    pad_end = jnp.cumsum(padded)
    pad_start = (pad_end - padded).astype(jnp.int32)
    nblk = (n_assign + N_EXPERTS * tb) // tb
    n_valid = (pad_end[-1] // tb).astype(jnp.int32)
    blk = jnp.minimum(jnp.arange(nblk, dtype=jnp.int32), n_valid - 1) * tb
    blk_exp = jnp.sum(blk[:, None] >= pad_end[None, :], axis=1).astype(jnp.int32)
    used = counts > 0
    run_idx = (jnp.cumsum(used) - 1).astype(jnp.int32)
    ids = jnp.where(used, jnp.arange(N_EXPERTS, dtype=jnp.int32), N_EXPERTS)
    later = lax.cummin(ids[::-1])[::-1]
    nxt = jnp.concatenate([later[1:], jnp.full((1,), N_EXPERTS, jnp.int32)])
    nxt_exp = jnp.where(nxt < N_EXPERTS, nxt, -1).astype(jnp.int32)
    return pad_start, jnp.minimum(blk_exp, N_EXPERTS - 1), n_valid.reshape(1), nxt_exp, run_idx


def _moe_positions_kernel(eid_ref, rank_ref, pstart_ref, pos_ref):
    eid = eid_ref[...]
    lane = lax.broadcasted_iota(jnp.int32, eid.shape, 1)
    pstart = pstart_ref[...]
    first = [jnp.sum(jnp.where(lane == eid[:, c:c + 1], pstart, 0), axis=-1, keepdims=True)
             for c in range(TOP_K)]
    pos_ref[...] = rank_ref[...] + jnp.where(lane == 0, first[0], jnp.where(lane == 1, first[1], 0))


def moe_positions(eid, rank, pad_start, *, tm):
    m = eid.shape[0]
    spec = pl.BlockSpec((tm, ROUTER_W), lambda i: (i, 0))
    pstart = jnp.pad(pad_start, (0, ROUTER_W - pad_start.shape[0])).reshape(1, ROUTER_W)
    return pl.pallas_call(
        _moe_positions_kernel,
        grid=(m // tm,),
        in_specs=[spec, spec, pl.BlockSpec((1, ROUTER_W), lambda i: (0, 0))],
        out_specs=spec,
        out_shape=jax.ShapeDtypeStruct((m, ROUTER_W), jnp.int32),
        compiler_params=_cparams(("parallel",)),
        name="moe_positions",
    )(eid, rank, pstart)


def hier_moe(h, g, wr_group, wr_expert, w1, w3, w2, layer, out_gain, normalize_out,
             *, tm_router=512, tm_combine=128, tb=MOE_TB):
    m, d = h.shape
    xn, eid, gate, rank, counts = moe_router(h, g, wr_group, wr_expert, tm=tm_router)
    pad_start, blk_exp, n_valid, nxt_exp, run_idx = _moe_blocks(counts[0, :N_EXPERTS], m * TOP_K, tb)
    pos = moe_positions(eid, rank, pad_start, tm=1024)[:, :TOP_K].reshape(-1)
    ys = moe_experts(xn, pos, blk_exp, n_valid, nxt_exp, run_idx, w1, w3, w2, layer, tb=tb)
    return moe_combine(h, ys, gate, pos, out_gain, tm=tm_combine, normalize=normalize_out)


def _w_in_layout(w_in):
    wa = w_in[:, :IN_A]
    wb = w_in[:, IN_A:IN_A + IN_B]
    wc = w_in[:, IN_A + IN_B:IN_A + IN_B + IN_C]
    wd = w_in[:, IN_A + IN_B + IN_C:]
    wb = jnp.pad(wb, ((0, 0), (0, IN_B_PAD - IN_B)))
    return jnp.concatenate([wb, wa, wc, wd], axis=1).astype(BF16)


def kernel(x, mem, norm_mix, w_in, w_out, diff_lambda, diff_subln, rwkv_mu, rwkv_w0, rwkv_w2,
           rwkv_a0, rwkv_a2, rwkv_g2, rwkv_kk, rwkv_ka, rwkv_rk, rwkv_lnx_w, rwkv_lnx_b,
           s5_a_re, s5_a_im, s5_log_dt, s5_b_re, s5_b_im, s5_c_re, s5_c_im, s5_d, s5_glu_w,
           s5_glu_b, mix_out_norm, norm_cross, norm_mem, xa_wq, xa_wkv, xa_wo, norm_moe,
           router_group, router_expert, moe_w1, moe_w3, moe_w2, norm_final):
    b, s, d = x.shape
    m = b * s
    n_mem = mem.shape[1]
    depth = w_in.shape[0]
    h = x.reshape(m, d)
    mem2 = mem.reshape(b * n_mem, d)
    rope_a = _rope_tables(s, A_QKDIM, MIX)
    rope_c = _rope_tables(s, C_HDIM, MIX)
    for l in range(depth):
        z = norm_matmul(h, norm_mix[l], _w_in_layout(w_in[l]), tm=1024, tn=Z_W // 4, out_dtype=F32)
        z = z.reshape(b, s, Z_W)
        qa, ka, v1a = rope_qkv(z, Z_A, A_QKDIM, tm=512, ones_cols=A_VDIM, tables=rope_a)
        oa = diff_attention(qa, ka, v1a, diff_lambda[l], diff_subln[l], l, tq=512)
        ob = rwkv7_bidir(z, rwkv_mu[l], rwkv_w0[l], rwkv_w2[l], rwkv_a0[l], rwkv_a2[l], rwkv_g2[l],
                         rwkv_kk[l], rwkv_ka[l], rwkv_rk[l], rwkv_lnx_w[l], rwkv_lnx_b[l])
        qc, kc, vc = rope_qkv(z, Z_C, C_HDIM, tm=512, tables=rope_c)
        oc = dilated_attention(qc, kc, vc, tq=256)
        od = s5_bidir(z, s5_a_re[l], s5_a_im[l], s5_log_dt[l], s5_b_re[l], s5_b_im[l],
                      s5_c_re[l], s5_c_im[l], s5_d[l], s5_glu_w[l], s5_glu_b[l])
        h = mix_out(h, oa.reshape(m, MIX), ob.reshape(m, MIX), oc.reshape(m, MIX), od.reshape(m, MIX),
                    mix_out_norm[l, 0], mix_out_norm[l, 1], w_out[l].astype(BF16), tm=1024, tn=1024)
        q = norm_matmul(h, norm_cross[l], xa_wq[l].astype(BF16), tm=1024, tn=512, out_dtype=BF16)
        kv = norm_matmul(mem2, norm_mem[l], xa_wkv[l].astype(BF16), tm=b * n_mem, tn=512, out_dtype=BF16)
        h = cross_attention(h.reshape(b, s, d), q.reshape(b, s, d), kv.reshape(b, n_mem, 2 * d),
                            xa_wo[l].astype(BF16), tq=256).reshape(m, d)
        h = hier_moe(h, norm_moe[l], router_group[l], router_expert[l], moe_w1, moe_w3, moe_w2, l,
                     norm_final, l == depth - 1)
    return h.reshape(b, s, d)
```

```python
import functools
import math

import numpy as np
import jax
import jax.numpy as jnp
from jax import lax
from jax.experimental import pallas as pl
from jax.experimental.pallas import tpu as pltpu

F32 = jnp.float32
BF16 = jnp.bfloat16

D_MODEL = 2048
MIX = D_MODEL // 4
A_HEADS = 4
A_VDIM = MIX // A_HEADS
A_QKDIM = A_VDIM // 2
B_HDIM = 64
B_HEADS = MIX // B_HDIM
B_LORA = 64
B_GATE_LORA = 128
LN_X_EPS = 64e-5
C_HEADS = 4
C_HDIM = MIX // C_HEADS
C_PATTERNS = ((128, 1), (512, 4), (2048, 16))
D_GSIZE = 16
D_GROUPS = MIX // D_GSIZE
D_STATE = 64
IN_A = 3 * MIX
IN_B = 3 * MIX + 4 * B_LORA + B_GATE_LORA
IN_B_PAD = 2048
IN_C = 3 * MIX
IN_D = MIX
XA_HEADS = 4
XA_HDIM = D_MODEL // XA_HEADS
N_GROUPS = 4
EXPERTS_PER_GROUP = 8
N_EXPERTS = N_GROUPS * EXPERTS_PER_GROUP
TOP_K = 2
D_EXPERT = D_MODEL // 4
ROPE_THETA = 10000.0
RMS_EPS = 1e-6
NEG_INF = -1e30

Z_B = 0
Z_A = IN_B_PAD
Z_C = Z_A + IN_A
Z_D = Z_C + IN_C
Z_W = Z_D + IN_D

LANES = 128
VMEM_LIMIT = 56 * 1024 * 1024


def _cparams(sem, vmem=VMEM_LIMIT):
    return pltpu.CompilerParams(dimension_semantics=sem, vmem_limit_bytes=vmem)


def _rms(x, g):
    return x * lax.rsqrt(jnp.mean(x * x, axis=-1, keepdims=True) + RMS_EPS) * g


def _dot(a, b):
    return jnp.dot(a, b, preferred_element_type=F32)


def _dot_t(a, b):
    return lax.dot_general(a, b, (((1,), (1,)), ((), ())), preferred_element_type=F32)


def _split3(x):
    hi = x.astype(BF16)
    r1 = x - hi.astype(F32)
    mid = r1.astype(BF16)
    lo = (r1 - mid.astype(F32)).astype(BF16)
    return hi, mid, lo


def _segsum(x, ones):
    hi, mid, lo = _split3(x)
    return _dot(hi, ones) + _dot(mid, ones) + _dot(lo, ones)


def _norm_matmul_kernel(x_ref, g_ref, w_ref, o_ref, xn_ref):
    @pl.when(pl.program_id(1) == 0)
    def _():
        xn_ref[...] = _rms(x_ref[...], g_ref[...]).astype(BF16)

    o_ref[...] = _dot(xn_ref[...], w_ref[...]).astype(o_ref.dtype)


def norm_matmul(x, g, w, *, tm, tn, out_dtype):
    m, k = x.shape
    n = w.shape[1]
    return pl.pallas_call(
        _norm_matmul_kernel,
        grid=(m // tm, n // tn),
        in_specs=[pl.BlockSpec((tm, k), lambda i, j: (i, 0)),
                  pl.BlockSpec((1, k), lambda i, j: (0, 0)),
                  pl.BlockSpec((k, tn), lambda i, j: (0, j))],
        out_specs=pl.BlockSpec((tm, tn), lambda i, j: (i, j)),
        out_shape=jax.ShapeDtypeStruct((m, n), out_dtype),
        scratch_shapes=[pltpu.VMEM((tm, k), BF16)],
        compiler_params=_cparams(("parallel", "arbitrary")),
        name="norm_matmul",
    )(x, g.reshape(1, k), w)


def _rope_tables(seq, dim, width):
    inv = 1.0 / (ROPE_THETA ** (jnp.arange(0, dim, 2, dtype=F32) / dim))
    ang = jnp.arange(seq, dtype=F32)[:, None] * inv[None, :]
    cos, sin = jnp.cos(ang), jnp.sin(ang)
    cos = jnp.concatenate([cos, cos], axis=-1)
    sin = jnp.concatenate([-sin, sin], axis=-1)
    reps = width // dim
    return jnp.tile(cos, (1, reps)), jnp.tile(sin, (1, reps))


def _rope_kernel(q_ref, k_ref, v_ref, cos_ref, sin_ref, qo_ref, ko_ref, vo_ref, *, half, scale, ones_cols):
    cos = cos_ref[...]
    sin = sin_ref[...]
    width = cos.shape[1]
    lane = lax.broadcasted_iota(jnp.int32, cos.shape, 1)
    first = (lane % (2 * half)) < half

    def rot(x):
        ahead = pltpu.roll(x, width - half, axis=1)
        behind = pltpu.roll(x, half, axis=1)
        return x * cos + jnp.where(first, ahead, behind) * sin

    qo_ref[...] = (rot(q_ref[...]) * scale).astype(BF16)
    ko_ref[...] = rot(k_ref[...]).astype(BF16)
    v = v_ref[...].astype(BF16)
    if ones_cols is None:
        vo_ref[...] = v
    else:
        ones = jnp.ones((v.shape[0], ones_cols), BF16)
        parts = []
        for hd in range(v.shape[1] // ones_cols):
            parts += [v[:, hd * ones_cols:(hd + 1) * ones_cols], ones]
        vo_ref[...] = jnp.concatenate(parts, axis=1)


def rope_qkv(z, col0, head_dim, *, tm, ones_cols=None, tables=None):
    b, s, _ = z.shape
    cos, sin = tables if tables is not None else _rope_tables(s, head_dim, MIX)
    cb = col0 // MIX
    zspec = lambda off: pl.BlockSpec((None, tm, MIX), lambda bi, i, off=off: (bi, i, cb + off))
    tspec = pl.BlockSpec((tm, MIX), lambda bi, i: (i, 0))
    ospec = pl.BlockSpec((None, tm, MIX), lambda bi, i: (bi, i, 0))
    oshape = jax.ShapeDtypeStruct((b, s, MIX), BF16)
    if ones_cols is None:
        vspec, vshape = ospec, oshape
    else:
        vspec = pl.BlockSpec((None, tm, 2 * MIX), lambda bi, i: (bi, i, 0))
        vshape = jax.ShapeDtypeStruct((b, s, 2 * MIX), BF16)
    return pl.pallas_call(
        functools.partial(_rope_kernel, half=head_dim // 2, scale=head_dim ** -0.5, ones_cols=ones_cols),
        grid=(b, s // tm),
        in_specs=[zspec(0), zspec(1), zspec(2), tspec, tspec],
        out_specs=[ospec, ospec, vspec],
        out_shape=[oshape, oshape, vshape],
        compiler_params=_cparams(("parallel", "parallel")),
        name="rope_qkv",
    )(z, z, z, cos, sin)


def _diff_attn_kernel(lam_ref, g_ref, q_ref, k_ref, v1_ref, o_ref, *, lam_init):
    lv = lam_ref[...]
    lam = (jnp.exp(jnp.sum(lv[0:1] * lv[1:2], axis=-1, keepdims=True))
           - jnp.exp(jnp.sum(lv[2:3] * lv[3:4], axis=-1, keepdims=True)) + lam_init)
    q = q_ref[...]
    k = k_ref[...]
    v1 = v1_ref[...]
    lane = lax.broadcasted_iota(jnp.int32, q.shape, 1)
    zero = jnp.zeros_like(q)

    def branch(qm):
        s = _dot_t(qm, k)
        p = jnp.exp((s - jnp.max(s, axis=-1, keepdims=True)).astype(BF16))
        acc = _dot(p, v1)
        return acc[:, :A_VDIM] / acc[:, A_VDIM:]

    o = branch(jnp.where(lane < A_QKDIM, q, zero)) - lam * branch(jnp.where(lane >= A_QKDIM, q, zero))
    o_ref[...] = _rms(o, g_ref[...]) * (1.0 - lam_init)


def diff_attention(q, k, v1, lam_vecs, subln_g, layer_idx, *, tq):
    b, s, _ = q.shape
    lam_init = 0.8 - 0.6 * math.exp(-0.3 * layer_idx)
    qspec = pl.BlockSpec((None, tq, A_VDIM), lambda bi, h, i: (bi, i, h))
    return pl.pallas_call(
        functools.partial(_diff_attn_kernel, lam_init=lam_init),
        grid=(b, A_HEADS, s // tq),
        in_specs=[pl.BlockSpec((4, A_QKDIM), lambda bi, h, i: (0, 0)),
                  pl.BlockSpec((1, A_VDIM), lambda bi, h, i: (0, 0)),
                  qspec,
                  pl.BlockSpec((None, s, A_VDIM), lambda bi, h, i: (bi, 0, h)),
                  pl.BlockSpec((None, s, 2 * A_VDIM), lambda bi, h, i: (bi, 0, h))],
        out_specs=qspec,
        out_shape=jax.ShapeDtypeStruct((b, s, MIX), F32),
        compiler_params=_cparams(("parallel", "parallel", "arbitrary")),
        name="diff_attention",
    )(lam_vecs, subln_g.reshape(1, A_VDIM), q, k, v1)


C_REACH = max(w // 2 for w, _ in C_PATTERNS)


def _dilated_bias_table(tq, window):
    n_delta = (window - tq) // tq + 1
    i = np.arange(tq)[None, :, None]
    j = np.arange(window)[None, None, :]
    n = np.arange(n_delta)[:, None, None]
    d = j - i - n * tq
    count = np.zeros(d.shape, np.int32)
    for w, dil in C_PATTERNS:
        count += ((np.abs(d) <= w // 2) & (d % dil == 0)).astype(np.int32)
    bias = np.where(count > 0, np.log(np.maximum(count, 1)), NEG_INF)
    return jnp.asarray(bias, F32)


def _dilated_attn_kernel(bias_ref, q_ref, k_ref, v_ref, o_ref, *, tq, window, seq):
    start = pl.program_id(2) * tq
    ws = pl.multiple_of(jnp.clip(start - C_REACH, 0, seq - window), tq)
    kw = k_ref[pl.ds(ws, window), :]
    vw = v_ref[pl.ds(ws, window), :]
    s = _dot_t(q_ref[...], kw) + bias_ref[...]
    e = jnp.exp(s - jnp.max(s, axis=-1, keepdims=True))
    den = jnp.sum(e, axis=-1, keepdims=True)
    o_ref[...] = _dot(e.astype(BF16), vw) / den


def dilated_attention(q, k, v, *, tq=128):
    b, s, _ = q.shape
    window = tq + 2 * C_REACH
    assert s >= window and s % tq == 0 and C_REACH % tq == 0
    bias = _dilated_bias_table(tq, window)

    def bias_map(bi, h, i):
        start = i * tq
        ws = jnp.clip(start - C_REACH, 0, s - window)
        return ((start - ws) // tq, 0, 0)

    qspec = pl.BlockSpec((None, tq, C_HDIM), lambda bi, h, i: (bi, i, h))
    kspec = pl.BlockSpec((None, s, C_HDIM), lambda bi, h, i: (bi, 0, h))
    return pl.pallas_call(
        functools.partial(_dilated_attn_kernel, tq=tq, window=window, seq=s),
        grid=(b, C_HEADS, s // tq),
        in_specs=[pl.BlockSpec((None, tq, window), bias_map), qspec, kspec, kspec],
        out_specs=qspec,
        out_shape=jax.ShapeDtypeStruct((b, s, MIX), F32),
        compiler_params=_cparams(("parallel", "parallel", "arbitrary")),
        name="dilated_attention",
    )(bias, q, k, v)


def _s5_out_kernel(y_ref, u_ref, d_ref, w_ref, b_ref, o_ref):
    y = y_ref[...] + d_ref[...] * u_ref[...]
    gl = jax.nn.gelu(y)
    gate = jax.nn.sigmoid(_dot(gl.astype(BF16), w_ref[...]) + b_ref[...])
    o_ref[...] = gl * gate


def s5_output(y, z, d_skip, glu_w, glu_b, *, tm):
    b, s, _ = z.shape
    yspec = pl.BlockSpec((None, tm, MIX), lambda bi, i: (bi, i, 0))
    vspec = pl.BlockSpec((1, MIX), lambda bi, i: (0, 0))
    return pl.pallas_call(
        _s5_out_kernel,
        grid=(b, s // tm),
        in_specs=[yspec,
                  pl.BlockSpec((None, tm, MIX), lambda bi, i: (bi, i, Z_D // MIX)),
                  vspec, pl.BlockSpec((MIX, MIX), lambda bi, i: (0, 0)), vspec],
        out_specs=yspec,
        out_shape=jax.ShapeDtypeStruct((b, s, MIX), F32),
        compiler_params=_cparams(("parallel", "parallel")),
        name="s5_output",
    )(y, z, d_skip.reshape(1, MIX), glu_w.astype(BF16), glu_b.reshape(1, MIX))


S5_T = 16
S5_KW = S5_T * D_GSIZE
S5_SW = 2 * D_STATE


S5_GPS = LANES // D_GSIZE


def _s5_chunk_kernel(u_ref, wm_ref, we_ref, wft_ref, ar_ref, ai_ref, y_ref, *, n_chunks, nb):
    rows = nb * n_chunks
    lane = lax.broadcasted_iota(jnp.int32, (n_chunks, LANES), 1)
    seg = [(lane >= D_GSIZE * j) & (lane < D_GSIZE * (j + 1)) for j in range(S5_GPS)]
    per_tile = LANES // D_GSIZE
    u_t = [[u_ref[b, pl.ds(t, n_chunks, stride=S5_T), :] for t in range(S5_T)] for b in range(nb)]
    row = lax.broadcasted_iota(jnp.int32, (rows, S5_SW), 0) & (n_chunks - 1)

    def lane_move(x, src, dst):
        shift = (D_GSIZE * (dst - src)) % LANES
        return pltpu.roll(x, shift, axis=1) if shift else x

    y_groups = []
    for gl in range(S5_GPS):
        packed = []
        for b in range(nb):
            tiles = []
            for half in range(S5_T // per_tile):
                acc = jnp.zeros((n_chunks, LANES), F32)
                for j in range(per_tile):
                    acc = jnp.where(seg[j], lane_move(u_t[b][half * per_tile + j], gl, j), acc)
                tiles.append(acc)
            packed.append(jnp.concatenate(tiles, axis=1))
        u = jnp.concatenate(packed, axis=0).astype(BF16)
        y = _dot(u, wm_ref[gl])
        xinc = _dot(u, we_ref[gl])

        def cmul(t, lvl, d, gl=gl):
            ar = ar_ref[gl, d, lvl:lvl + 1, :]
            ai = ai_ref[gl, d, lvl:lvl + 1, :]
            return t * ar + pltpu.roll(t, D_STATE, axis=1) * ai

        carries = []
        for d in range(2):
            x = xinc[:, d * S5_SW:(d + 1) * S5_SW]
            for lvl in range(n_chunks.bit_length() - 1):
                sh = 1 << lvl
                if d == 0:
                    prev = jnp.where(row >= sh, pltpu.roll(x, sh, axis=0), 0.0)
                else:
                    prev = jnp.where(row < n_chunks - sh, pltpu.roll(x, rows - sh, axis=0), 0.0)
                x = x + cmul(prev, lvl, d)
            if d == 0:
                carries.append(jnp.where(row >= 1, pltpu.roll(x, 1, axis=0), 0.0))
            else:
                carries.append(jnp.where(row < n_chunks - 1, pltpu.roll(x, rows - 1, axis=0), 0.0))
        cin = jnp.concatenate(carries, axis=1).astype(BF16)
        y_groups.append(y + _dot_t(cin, wft_ref[gl]))

    for b in range(nb):
        for t in range(S5_T):
            half, j = divmod(t, per_tile)
            out = jnp.zeros((n_chunks, LANES), F32)
            for gl in range(S5_GPS):
                tile = y_groups[gl][b * n_chunks:(b + 1) * n_chunks, half * LANES:(half + 1) * LANES]
                out = jnp.where(seg[gl], lane_move(tile, j, gl), out)
            y_ref[b, pl.ds(t, n_chunks, stride=S5_T), :] = out


def _cpow_table(ar, ai, count):
    res_r, res_i = [ar], [ai]
    for _ in range(count - 1):
        ar, ai = ar * ar - ai * ai, 2.0 * ar * ai
        res_r.append(ar)
        res_i.append(ai)
    return jnp.stack(res_r), jnp.stack(res_i)


def _s5_chunk_weights(a_re, a_im, log_dt, b_re, b_im, c_re, c_im, n_levels):
    g_n, t_n, c_n = D_GROUPS, S5_T, D_GSIZE
    taus = jnp.arange(t_n + 1, dtype=F32)[None, :, None]
    cr, ci = c_re.astype(F32)[:, None], c_im.astype(F32)[:, None]
    toe, ee, ff, lvl_r, lvl_i = [], [], [], [], []
    for direction in range(2):
        lr = jnp.minimum(a_re[direction].astype(F32), -1e-4)
        li = a_im[direction].astype(F32)
        dt = jnp.exp(log_dt[direction].astype(F32))[:, None]
        mag = jnp.exp(dt * lr)
        abr, abi = mag * jnp.cos(dt * li), mag * jnp.sin(dt * li)
        den = lr * lr + li * li
        qr, qi = lr / den, -li / den
        fr = (abr - 1.0) * qr - abi * qi
        fi = (abr - 1.0) * qi + abi * qr
        br, bi = b_re.astype(F32), b_im.astype(F32)
        btr = (fr[..., None] * br - fi[..., None] * bi).transpose(0, 2, 1)
        bti = (fr[..., None] * bi + fi[..., None] * br).transpose(0, 2, 1)
        pmag = jnp.exp(taus * (dt * lr)[:, None])
        ang = taus * (dt * li)[:, None]
        pr, pi = pmag * jnp.cos(ang), pmag * jnp.sin(ang)

        def c_times(order):
            por, poi = pr[:, order][:, :, None, :], pi[:, order][:, :, None, :]
            return cr * por - ci * poi, cr * poi + ci * por

        cpr, cpi = c_times(jnp.arange(t_n))
        k = (jnp.einsum('gin,gkn->gik', btr, cpr.reshape(g_n, t_n * c_n, D_STATE))
             - jnp.einsum('gin,gkn->gik', bti, cpi.reshape(g_n, t_n * c_n, D_STATE)))
        pad = (t_n - 1) * c_n
        if direction == 0:
            kp = jnp.pad(k, ((0, 0), (0, 0), (pad, 0)))
        else:
            krev = k.reshape(g_n, c_n, t_n, c_n)[:, :, ::-1, :].reshape(g_n, c_n, t_n * c_n)
            kp = jnp.pad(krev, ((0, 0), (0, 0), (0, pad)))
        toe.append(jnp.stack([kp[:, :, (t_n - 1 - s) * c_n:(t_n - 1 - s) * c_n + t_n * c_n]
                              for s in range(t_n)], axis=1))
        order = jnp.arange(t_n - 1, -1, -1) if direction == 0 else jnp.arange(t_n)
        por, poi = pr[:, order][:, :, None, :], pi[:, order][:, :, None, :]
        er = por * btr[:, None] - poi * bti[:, None]
        ei = por * bti[:, None] + poi * btr[:, None]
        ee.append(jnp.concatenate([er, ei], axis=3).reshape(g_n, S5_KW, S5_SW))
        order = jnp.arange(1, t_n + 1) if direction == 0 else jnp.arange(t_n, 0, -1)
        gr, gi = c_times(order)
        ff.append(jnp.concatenate([gr, -gi], axis=3).reshape(g_n, S5_KW, S5_SW))
        tr, ti = _cpow_table(pr[:, t_n], pi[:, t_n], n_levels)
        lvl_r.append(jnp.concatenate([tr, tr], axis=2))
        lvl_i.append(jnp.concatenate([-ti, ti], axis=2))
    wm = (toe[0] + toe[1]).reshape(g_n, S5_KW, S5_KW)
    we = jnp.concatenate(ee, axis=2)
    wft = jnp.concatenate(ff, axis=2)
    ar = jnp.stack(lvl_r).transpose(2, 0, 1, 3)
    ai = jnp.stack(lvl_i).transpose(2, 0, 1, 3)
    return wm.astype(BF16), we.astype(BF16), wft.astype(BF16), ar, ai


def s5_chunked(z, a_re, a_im, log_dt, b_re, b_im, c_re, c_im):
    b, s, _ = z.shape
    n_chunks = s // S5_T
    n_levels = n_chunks.bit_length() - 1
    assert n_chunks == 1 << n_levels
    wm, we, wft, ar, ai = _s5_chunk_weights(a_re, a_im, log_dt, b_re, b_im, c_re, c_im, max(n_levels, 1))
    groups = lambda shape: pl.BlockSpec((S5_GPS,) + shape, lambda k: (k,) + (0,) * len(shape))
    return pl.pallas_call(
        functools.partial(_s5_chunk_kernel, n_chunks=n_chunks, nb=b),
        grid=(D_GROUPS // S5_GPS,),
        in_specs=[pl.BlockSpec((b, s, LANES), lambda k: (0, 0, Z_D // LANES + k)),
                  groups((S5_KW, S5_KW)), groups((S5_KW, 2 * S5_SW)), groups((S5_KW, 2 * S5_SW)),
                  groups((2, ar.shape[2], S5_SW)), groups((2, ar.shape[2], S5_SW))],
        out_specs=pl.BlockSpec((b, s, LANES), lambda k: (0, 0, k)),
        out_shape=jax.ShapeDtypeStruct((b, s, MIX), F32),
        compiler_params=_cparams(("parallel",)),
        name="s5_chunked",
    )(z, wm, we, wft, ar, ai)


def s5_bidir(z, a_re, a_im, log_dt, b_re, b_im, c_re, c_im, d_skip, glu_w, glu_b, *, tm=512):
    y = s5_chunked(z, a_re, a_im, log_dt, b_re, b_im, c_re, c_im)
    return s5_output(y, z, d_skip, glu_w, glu_b, tm=tm)


def _softplus(y):
    return jnp.maximum(y, 0.0) + jnp.log(1.0 + jnp.exp(-jnp.abs(y)))


def _head_ones(width):
    seg = np.arange(width) // B_HDIM
    return jnp.asarray(seg[:, None] == seg[None, :], BF16)


def _rwkv_pre_kernel(z_ref, zp_ref, zn_ref, mu_ref, w0_ref, a0_ref, w2_ref, a2_ref, g2_ref,
                     kk_ref, ka_ref, rk_ref, ones_ref,
                     r_o, a_o, w0_o, w1_o, k0_o, k1_o, b0_o, b1_o, v_o, g_o, bonus_o,
                     *, tm, n_tiles):
    i = pl.program_id(1)
    z = z_ref[...]
    row = lax.broadcasted_iota(jnp.int32, z.shape, 0)
    prev_row = jnp.where(i > 0, zp_ref[7:8, :], 0.0)
    next_row = jnp.where(i < n_tiles - 1, zn_ref[0:1, :], 0.0)
    zp = jnp.where(row == 0, prev_row, pltpu.roll(z, 1, axis=0))
    zn = jnp.where(row == tm - 1, next_row, pltpu.roll(z, tm - 1, axis=0))
    xs = z + mu_ref[0:1, :] * (zp - z) + mu_ref[1:2, :] * (zn - z)
    r = xs[:, 0:MIX]
    k = xs[:, MIX:2 * MIX]
    v = xs[:, 2 * MIX:3 * MIX]
    c0 = 3 * MIX
    wd = xs[:, c0:c0 + 2 * B_LORA]
    ad = xs[:, c0 + 2 * B_LORA:c0 + 4 * B_LORA]
    gd = xs[:, c0 + 4 * B_LORA:c0 + 4 * B_LORA + B_GATE_LORA]
    lw = _dot(jnp.tanh(wd).astype(BF16), w2_ref[...])
    la = _dot(ad.astype(BF16), a2_ref[...])
    g_o[...] = _dot(jax.nn.sigmoid(gd).astype(BF16), g2_ref[...])
    ones = ones_ref[...]
    kk = k * kk_ref[...]
    kk = kk * lax.rsqrt(_segsum(kk * kk, ones) + 1e-12)
    ka = ka_ref[...]
    ksum = jnp.zeros_like(k)
    for d, (w_o, k_o, b_o) in enumerate(((w0_o, k0_o, b0_o), (w1_o, k1_o, b1_o))):
        cols = slice(d * MIX, (d + 1) * MIX)
        logw = -_softplus(-(w0_ref[d:d + 1, :] + lw[:, cols])) - 0.5
        w_o[...] = -jnp.exp(logw)
        a = jax.nn.sigmoid(a0_ref[d:d + 1, :] + la[:, cols])
        kmod = k * (1.0 + (a - 1.0) * ka)
        k_o[...] = kmod
        b_o[...] = kk * a
        ksum = ksum + kmod
    r_o[...] = r
    a_o[...] = -kk
    bonus_o[...] = _segsum(r * (0.5 * ksum) * rk_ref[...], ones) * v
    v_o[...] = v


def _block_diag2(m):
    z = jnp.zeros_like(m[0])
    return jnp.concatenate([jnp.concatenate([m[0], z], axis=1),
                            jnp.concatenate([z, m[1]], axis=1)], axis=0)


def rwkv_pre(z, mu, w0, w2, a0, a2, g2, k_k, k_a, r_k, *, tm):
    b, s, _ = z.shape
    n_tiles = s // tm
    mu_p = jnp.pad(mu.astype(F32), ((0, 0), (0, IN_B_PAD - IN_B)))
    row_spec = pl.BlockSpec((None, tm, MIX), lambda bi, i: (bi, i, 0))
    vec = lambda n, w: pl.BlockSpec((n, w), lambda bi, i: (0, 0))
    rows = jax.ShapeDtypeStruct((b, s, MIX), F32)
    hb = tm // 8
    return pl.pallas_call(
        functools.partial(_rwkv_pre_kernel, tm=tm, n_tiles=n_tiles),
        grid=(b, n_tiles),
        in_specs=[pl.BlockSpec((None, tm, IN_B_PAD), lambda bi, i: (bi, i, 0)),
                  pl.BlockSpec((None, 8, IN_B_PAD), lambda bi, i: (bi, jnp.maximum(i * hb - 1, 0), 0)),
                  pl.BlockSpec((None, 8, IN_B_PAD), lambda bi, i: (bi, jnp.minimum((i + 1) * hb, s // 8 - 1), 0)),
                  vec(2, IN_B_PAD), vec(2, MIX), vec(2, MIX),
                  vec(2 * B_LORA, 2 * MIX), vec(2 * B_LORA, 2 * MIX), vec(B_GATE_LORA, MIX),
                  vec(1, MIX), vec(1, MIX), vec(1, MIX), vec(MIX, MIX)],
        out_specs=[row_spec] * 11,
        out_shape=[rows] * 11,
        compiler_params=_cparams(("parallel", "parallel")),
        name="rwkv_pre",
    )(z, z, z, mu_p, w0.astype(F32), a0.astype(F32),
      _block_diag2(w2).astype(BF16), _block_diag2(a2).astype(BF16), g2.astype(BF16),
      k_k.reshape(1, MIX), k_a.reshape(1, MIX), r_k.reshape(1, MIX), _head_ones(MIX))


def _rwkv_scan_kernel(af_ref, rf_ref, vf_ref, lwf_ref, kf_ref, bf_ref,
                      ab_ref, rb_ref, vb_ref, lwb_ref, kb_ref, bb_ref,
                      trif_ref, trib_ref, yf_ref, yb_ref, h_ref, *, tc, nb):
    @pl.when(pl.program_id(0) == 0)
    def _():
        h_ref[...] = jnp.zeros_like(h_ref)

    row = lax.broadcasted_iota(jnp.int32, (tc, tc), 0)
    col = lax.broadcasted_iota(jnp.int32, (tc, tc), 1)
    eye = (row == col).astype(F32)
    row2 = lax.broadcasted_iota(jnp.int32, (tc, 2 * tc), 0)
    col2 = lax.broadcasted_iota(jnp.int32, (tc, 2 * tc), 1) & (tc - 1)
    zeros_tv = jnp.zeros((tc, B_HDIM), BF16)
    n_sq = tc.bit_length() - 1
    tdot = lambda x, y: lax.dot_general(x, y, (((0,), (0,)), ((), ())), preferred_element_type=F32)

    seqs = []
    for reverse, (a_ref, r_ref, v_ref, lw_ref, k_ref, b_ref, tri_ref, y_ref) in (
            (False, (af_ref, rf_ref, vf_ref, lwf_ref, kf_ref, bf_ref, trif_ref, yf_ref)),
            (True, (ab_ref, rb_ref, vb_ref, lwb_ref, kb_ref, bb_ref, trib_ref, yb_ref))):
        tri = tri_ref[...]
        before = (col2 > row2) if reverse else (col2 < row2)
        upto = (col2 >= row2) if reverse else (col2 <= row2)
        last = 0 if reverse else tc - 1
        for bi in range(nb):
            lw = lw_ref[bi]
            hi, mid, lo = _split3(lw)
            cl = _dot(tri, hi) + _dot(tri, mid) + _dot(tri, lo)
            tot = cl[last:last + 1, :]
            einv = jnp.exp(-cl)
            etot = jnp.exp(tot - cl)
            b_all = b_ref[bi]
            k_all = k_ref[bi]
            seqs.append(dict(
                a_t=a_ref[bi] * jnp.exp(cl - lw), r_t=r_ref[bi] * jnp.exp(cl),
                b_t=b_all * einv, k_t=k_all * einv, b_h=b_all * etot, k_h=k_all * etot,
                g_tot=jnp.exp(tot), v=v_ref[bi], before=before, upto=upto, y_ref=y_ref, bi=bi))

    inst = [(sq, slice(hd * B_HDIM, (hd + 1) * B_HDIM)) for sq in seqs for hd in range(B_HEADS)]
    ids = range(len(inst))
    at = [sq['a_t'][:, s].astype(BF16) for sq, s in inst]
    rt = [sq['r_t'][:, s].astype(BF16) for sq, s in inst]
    bk = [jnp.concatenate([sq['b_t'][:, s], sq['k_t'][:, s]], axis=0).astype(BF16) for sq, s in inst]
    bkh = [jnp.concatenate([sq['b_h'][:, s], sq['k_h'][:, s]], axis=0).astype(BF16) for sq, s in inst]
    vv = [sq['v'][:, s].astype(BF16) for sq, s in inst]
    g = [_dot_t(jnp.concatenate([at[n], rt[n]], axis=0), bk[n]) for n in ids]
    ga = [jnp.where(inst[n][0]['before'], g[n][:tc], 0.0) for n in ids]
    gr = [jnp.where(inst[n][0]['upto'], g[n][tc:], 0.0).astype(BF16) for n in ids]
    lkv = [_dot(ga[n].astype(BF16), jnp.concatenate([zeros_tv, vv[n]], axis=0)) for n in ids]
    lb = [ga[n][:, :tc].astype(BF16) for n in ids]
    pinv = [eye + ga[n][:, :tc] for n in ids]
    lb = [_dot(x, x).astype(BF16) for x in lb]
    for _ in range(1, n_sq - 1):
        prod = [_dot(jnp.concatenate([pinv[n].astype(BF16), lb[n]], axis=0), lb[n]) for n in ids]
        pinv = [pinv[n] + prod[n][:tc] for n in ids]
        lb = [prod[n][tc:].astype(BF16) for n in ids]
    pinv = [(pinv[n] + _dot(pinv[n].astype(BF16), lb[n])).astype(BF16) for n in ids]
    ah = [_dot(pinv[n], at[n]).astype(BF16) for n in ids]
    u0 = [_dot(pinv[n], lkv[n].astype(BF16)).astype(BF16) for n in ids]
    rhs = [jnp.concatenate([jnp.concatenate([ah[n], u0[n]], axis=1),
                            jnp.concatenate([zeros_tv, vv[n]], axis=1)], axis=0) for n in ids]
    ry = [_dot(gr[n], rhs[n]) for n in ids]
    ph = [tdot(bkh[n], rhs[n]) for n in ids]
    rh = [inst[n][0]['r_t'][:, inst[n][1]] + ry[n][:, :B_HDIM] for n in ids]
    phi = [eye * inst[n][0]['g_tot'][:, inst[n][1]] + ph[n][:, :B_HDIM] for n in ids]
    h0 = [h_ref[n].astype(BF16) for n in ids]
    fin = [_dot(jnp.concatenate([rh[n], phi[n]], axis=0).astype(BF16), h0[n]) for n in ids]
    yo = [fin[n][:tc] + ry[n][:, B_HDIM:] for n in ids]
    for si, sq in enumerate(seqs):
        sq['y_ref'][sq['bi']] = jnp.concatenate(yo[si * B_HEADS:(si + 1) * B_HEADS], axis=1)
    for n in ids:
        h_ref[n] = fin[n][tc:] + ph[n][:, B_HDIM:]


def rwkv_scan(a, r, v, lwf, kf, bvf, lwb, kb, bvb, *, tc=64):
    nb, s, _ = a.shape
    nch = s // tc
    fwd = pl.BlockSpec((nb, tc, MIX), lambda c: (0, c, 0))
    bwd = pl.BlockSpec((nb, tc, MIX), lambda c: (0, nch - 1 - c, 0))
    tspec = pl.BlockSpec((tc, tc), lambda c: (0, 0))
    t_idx = np.arange(tc)
    tri_f = jnp.asarray(t_idx[None, :] <= t_idx[:, None], BF16)
    tri_b = jnp.asarray(t_idx[None, :] >= t_idx[:, None], BF16)
    out = jax.ShapeDtypeStruct((nb, s, MIX), F32)
    return pl.pallas_call(
        functools.partial(_rwkv_scan_kernel, tc=tc, nb=nb),
        grid=(nch,),
        in_specs=[fwd] * 6 + [bwd] * 6 + [tspec, tspec],
        out_specs=[fwd, bwd],
        out_shape=[out, out],
        scratch_shapes=[pltpu.VMEM((2 * nb * B_HEADS, B_HDIM, B_HDIM), F32)],
        compiler_params=_cparams(("arbitrary",)),
        name="rwkv_scan",
    )(a, r, v, lwf, kf, bvf, a, r, v, lwb, kb, bvb, tri_f, tri_b)


def _rwkv_post_kernel(yf_ref, yb_ref, bonus_ref, g_ref, lw_ref, lb_ref, ones_ref, o_ref):
    y = yf_ref[...] + yb_ref[...]
    ones = ones_ref[...]
    mean = _segsum(y, ones) * (1.0 / B_HDIM)
    yc = y - mean
    var = _segsum(yc * yc, ones) * (1.0 / B_HDIM)
    yn = yc * lax.rsqrt(var + LN_X_EPS) * lw_ref[...] + lb_ref[...]
    o_ref[...] = (yn + bonus_ref[...]) * g_ref[...]


def rwkv_post(yf, yb, bonus, g, lnx_w, lnx_b, *, tm):
    b, s, _ = yf.shape
    row_spec = pl.BlockSpec((None, tm, MIX), lambda bi, i: (bi, i, 0))
    vec = pl.BlockSpec((1, MIX), lambda bi, i: (0, 0))
    return pl.pallas_call(
        _rwkv_post_kernel,
        grid=(b, s // tm),
        in_specs=[row_spec, row_spec, row_spec, row_spec, vec, vec,
                  pl.BlockSpec((MIX, MIX), lambda bi, i: (0, 0))],
        out_specs=row_spec,
        out_shape=jax.ShapeDtypeStruct((b, s, MIX), F32),
        compiler_params=_cparams(("parallel", "parallel")),
        name="rwkv_post",
    )(yf, yb, bonus, g, lnx_w.reshape(1, MIX), lnx_b.reshape(1, MIX), _head_ones(MIX))


def rwkv7_bidir(z, mu, w0, w2, a0, a2, g2, k_k, k_a, r_k, lnx_w, lnx_b, *, tm=256, tc=64):
    r, a, wf, wb, kf, kb, bf, bb, v, g, bonus = rwkv_pre(z, mu, w0, w2, a0, a2, g2, k_k, k_a, r_k, tm=tm)
    yf, yb = rwkv_scan(a, r, v, wf, kf, bf, wb, kb, bb, tc=tc)
    return rwkv_post(yf, yb, bonus, g, lnx_w, lnx_b, tm=tm)


def _mix_out_kernel(h_ref, oa_ref, ob_ref, oc_ref, od_ref, gc_ref, gd_ref, w_ref, o_ref, mix_ref):
    @pl.when(pl.program_id(1) == 0)
    def _():
        mix_ref[...] = jnp.concatenate(
            [oa_ref[...], ob_ref[...], _rms(oc_ref[...], gc_ref[...]), _rms(od_ref[...], gd_ref[...])],
            axis=1).astype(BF16)

    o_ref[...] = h_ref[...] + _dot(mix_ref[...], w_ref[...])


def mix_out(h, oa, ob, oc, od, gc, gd, w_out, *, tm, tn):
    m, d = h.shape
    mspec = pl.BlockSpec((tm, MIX), lambda i, j: (i, 0))
    vspec = pl.BlockSpec((1, MIX), lambda i, j: (0, 0))
    hspec = pl.BlockSpec((tm, tn), lambda i, j: (i, j))
    return pl.pallas_call(
        _mix_out_kernel,
        grid=(m // tm, d // tn),
        in_specs=[hspec, mspec, mspec, mspec, mspec, vspec, vspec,
                  pl.BlockSpec((4 * MIX, tn), lambda i, j: (0, j))],
        out_specs=hspec,
        out_shape=jax.ShapeDtypeStruct((m, d), F32),
        scratch_shapes=[pltpu.VMEM((tm, 4 * MIX), BF16)],
        compiler_params=_cparams(("parallel", "arbitrary")),
        name="mix_out",
    )(h, oa, ob, oc, od, gc.reshape(1, MIX), gd.reshape(1, MIX), w_out)


def _cross_attn_kernel(h_ref, g_ref, wq_ref, kv_ref, wo_ref, o_ref):
    h = h_ref[...]
    q = _dot(_rms(h, g_ref[...]).astype(BF16), wq_ref[...]).astype(BF16)
    outs = []
    for hd in range(XA_HEADS):
        cols = slice(hd * XA_HDIM, (hd + 1) * XA_HDIM)
        kh = kv_ref[:, cols]
        vh = kv_ref[:, D_MODEL + hd * XA_HDIM:D_MODEL + (hd + 1) * XA_HDIM]
        s = _dot_t(q[:, cols], kh) * (XA_HDIM ** -0.5)
        e = jnp.exp(s - jnp.max(s, axis=-1, keepdims=True))
        p = e / jnp.sum(e, axis=-1, keepdims=True)
        outs.append(_dot(p.astype(BF16), vh).astype(BF16))
    o = jnp.concatenate(outs, axis=1)
    o_ref[...] = h + _dot(o, wo_ref[...])


def cross_attention(h, g, wq, kv, wo, *, tq):
    b, s, d = h.shape
    n_mem = kv.shape[1]
    hspec = pl.BlockSpec((None, tq, d), lambda bi, i: (bi, i, 0))
    wspec = pl.BlockSpec((d, d), lambda bi, i: (0, 0), pipeline_mode=pl.Buffered(1))
    return pl.pallas_call(
        _cross_attn_kernel,
        grid=(b, s // tq),
        in_specs=[hspec, pl.BlockSpec((1, d), lambda bi, i: (0, 0)), wspec,
                  pl.BlockSpec((None, n_mem, 2 * d), lambda bi, i: (bi, 0, 0)), wspec],
        out_specs=hspec,
        out_shape=jax.ShapeDtypeStruct((b, s, d), F32),
        compiler_params=_cparams(("parallel", "parallel")),
        name="cross_attention",
    )(h, g.reshape(1, d), wq, kv, wo)


ROUTER_W = LANES
MOE_TB = 256


def _router_kernel(h_ref, g_ref, wr_ref, lt_ref, xn_ref, eid_ref, gate_ref, rank_ref, cnt_ref, run_ref):
    @pl.when(pl.program_id(0) == 0)
    def _():
        run_ref[...] = jnp.zeros_like(run_ref)

    xn = _rms(h_ref[...], g_ref[...])
    xn_ref[...] = xn
    logits = jnp.dot(xn, wr_ref[...], preferred_element_type=F32, precision=lax.Precision.HIGHEST)
    lane = lax.broadcasted_iota(jnp.int32, logits.shape, 1)
    big = jnp.int32(ROUTER_W)

    def first_max(mask):
        m = jnp.max(jnp.where(mask, logits, NEG_INF), axis=-1, keepdims=True)
        idx = jnp.min(jnp.where(mask & (logits == m), lane, big), axis=-1, keepdims=True)
        return m, idx

    gmask = lane < N_GROUPS
    gmax, gidx = first_max(gmask)
    g_w = 1.0 / jnp.sum(jnp.where(gmask, jnp.exp(logits - gmax), 0.0), axis=-1, keepdims=True)
    e0 = N_GROUPS + gidx * EXPERTS_PER_GROUP
    emask = (lane >= e0) & (lane < e0 + EXPERTS_PER_GROUP)
    m1, i1 = first_max(emask)
    m2, i2 = first_max(emask & (lane != i1))
    e2 = jnp.exp(m2 - m1)
    w1 = 1.0 / (1.0 + e2)
    w2 = e2 / (1.0 + e2)
    e1 = i1 - N_GROUPS
    e2i = i2 - N_GROUPS
    eid_ref[...] = jnp.where(lane == 0, e1, jnp.where(lane == 1, e2i, 0))
    gate_ref[...] = jnp.where(lane == 0, g_w * w1, jnp.where(lane == 1, g_w * w2, 0.0))
    hit1 = lane == e1
    hit2 = lane == e2i
    onehot = jnp.where(hit1 | hit2, 1.0, 0.0)
    earlier = run_ref[...] + _dot(lt_ref[...], onehot.astype(BF16))
    r1 = jnp.sum(jnp.where(hit1, earlier, 0.0), axis=-1, keepdims=True)
    r2 = jnp.sum(jnp.where(hit2, earlier, 0.0), axis=-1, keepdims=True)
    rank_ref[...] = jnp.where(lane == 0, r1, jnp.where(lane == 1, r2, 0.0)).astype(jnp.int32)
    run_ref[...] = run_ref[...] + jnp.sum(onehot, axis=0, keepdims=True)
    cnt_ref[...] = run_ref[...].astype(jnp.int32)


def moe_router(h, g, wr_group, wr_expert, *, tm):
    m, d = h.shape
    wr = jnp.concatenate([wr_group, wr_expert], axis=1).astype(F32)
    wr = jnp.pad(wr, ((0, 0), (0, ROUTER_W - wr.shape[1])))
    t_idx = np.arange(tm)
    lower = jnp.asarray(t_idx[None, :] < t_idx[:, None], BF16)
    hspec = pl.BlockSpec((tm, d), lambda i: (i, 0))
    lspec = pl.BlockSpec((tm, ROUTER_W), lambda i: (i, 0))
    cspec = pl.BlockSpec((1, ROUTER_W), lambda i: (0, 0))
    lane_i = jax.ShapeDtypeStruct((m, ROUTER_W), jnp.int32)
    return pl.pallas_call(
        _router_kernel,
        grid=(m // tm,),
        in_specs=[hspec, pl.BlockSpec((1, d), lambda i: (0, 0)),
                  pl.BlockSpec((d, ROUTER_W), lambda i: (0, 0)),
                  pl.BlockSpec((tm, tm), lambda i: (0, 0))],
        out_specs=[hspec, lspec, lspec, lspec, cspec],
        out_shape=[jax.ShapeDtypeStruct((m, d), F32), lane_i,
                   jax.ShapeDtypeStruct((m, ROUTER_W), F32), lane_i,
                   jax.ShapeDtypeStruct((1, ROUTER_W), jnp.int32)],
        scratch_shapes=[pltpu.VMEM((1, ROUTER_W), F32)],
        compiler_params=_cparams(("arbitrary",)),
        name="moe_router",
    )(h, g.reshape(1, d), wr, lower)


GATHER_UNROLL = 8


def _start_row_gather(row_of, src_hbm, dst, sem, n_rows, dst_row0=0):
    def issue(r, c):
        pltpu.make_async_copy(src_hbm.at[pl.ds(row_of(r), 1)], dst.at[pl.ds(dst_row0 + r, 1)], sem).start()
        return c

    lax.fori_loop(0, n_rows, issue, 0, unroll=GATHER_UNROLL)


def _wait_row_gather(src_hbm, dst, sem, n_rows):
    pltpu.make_async_copy(src_hbm.at[pl.ds(0, n_rows)], dst, sem).wait()


def _moe_expert_kernel(pos_ref, exp_ref, nvb_ref, nxt_ref, run_ref, x_hbm, w1_hbm, w3_hbm, w2_hbm, o_ref,
                       slot_ref, xbuf, sem, w1buf, w3buf, w2buf, wsem, *, tb, n_assign, layer):
    i = pl.program_id(0)
    nvb = nvb_ref[0]

    def weight_copies(e, s):
        return (pltpu.make_async_copy(w1_hbm.at[layer, e], w1buf.at[s], wsem.at[s, 0]),
                pltpu.make_async_copy(w3_hbm.at[layer, e], w3buf.at[s], wsem.at[s, 1]),
                pltpu.make_async_copy(w2_hbm.at[layer, e], w2buf.at[s], wsem.at[s, 2]))

    e_cur = exp_ref[i]
    first = (i == 0) | (exp_ref[jnp.maximum(i - 1, 0)] != e_cur)
    wslot = run_ref[e_cur] & 1

    @pl.when(i == 0)
    def _():
        for cp in weight_copies(e_cur, wslot):
            cp.start()

    @pl.when(i == 0)
    def _():
        def clear(s, c):
            slot_ref[s] = 0
            return c

        lax.fori_loop(0, slot_ref.shape[0], clear, 0, unroll=GATHER_UNROLL)

        def place(a, c):
            slot_ref[pos_ref[a]] = lax.shift_right_logical(a, TOP_K.bit_length() - 1)
            return c

        lax.fori_loop(0, n_assign, place, 0, unroll=GATHER_UNROLL)
        _start_row_gather(lambda r: slot_ref[r], x_hbm, xbuf.at[0], sem.at[0], tb)

    @pl.when(i + 1 < nvb)
    def _():
        nxt = (i + 1) & 1
        base = (i + 1) * tb
        _start_row_gather(lambda r: slot_ref[base + r], x_hbm, xbuf.at[nxt], sem.at[nxt], tb)

    @pl.when((i < nvb) & first & (nxt_ref[e_cur] >= 0))
    def _():
        for cp in weight_copies(nxt_ref[e_cur], 1 - wslot):
            cp.start()

    @pl.when(i < nvb)
    def _():
        @pl.when(first)
        def _():
            for cp in weight_copies(e_cur, wslot):
                cp.wait()

        cur = i & 1
        _wait_row_gather(x_hbm, xbuf.at[cur], sem.at[cur], tb)
        x = xbuf[cur].astype(BF16)
        h1 = _dot(x, w1buf[wslot].astype(BF16))
        h3 = _dot(x, w3buf[wslot].astype(BF16))
        act = (jax.nn.silu(h1) * h3).astype(BF16)
        o_ref[...] = _dot(act, w2buf[wslot].astype(BF16))

    @pl.when(i >= nvb)
    def _():
        o_ref[...] = jnp.zeros_like(o_ref)


def moe_experts(xn, pos, blk_exp, n_valid, nxt_exp, run_idx, w1, w3, w2, layer, *, tb):
    n_assign = pos.shape[0]
    cap = n_assign + N_EXPERTS * tb
    d = xn.shape[1]
    nblk = cap // tb
    hbm = pl.BlockSpec(memory_space=pl.ANY)
    grid_spec = pltpu.PrefetchScalarGridSpec(
        num_scalar_prefetch=5,
        grid=(nblk,),
        in_specs=[hbm, hbm, hbm, hbm],
        out_specs=pl.BlockSpec((tb, d), lambda i, *_: (i, 0)),
        scratch_shapes=[pltpu.SMEM((cap,), jnp.int32), pltpu.VMEM((2, tb, d), F32),
                        pltpu.SemaphoreType.DMA((2,)),
                        pltpu.VMEM((2, d, D_EXPERT), F32), pltpu.VMEM((2, d, D_EXPERT), F32),
                        pltpu.VMEM((2, D_EXPERT, d), F32), pltpu.SemaphoreType.DMA((2, 3))],
    )
    return pl.pallas_call(
        functools.partial(_moe_expert_kernel, tb=tb, n_assign=n_assign, layer=layer),
        grid_spec=grid_spec,
        out_shape=jax.ShapeDtypeStruct((cap, d), F32),
        compiler_params=_cparams(("arbitrary",)),
        name="moe_experts",
    )(pos, blk_exp, n_valid, nxt_exp, run_idx, xn, w1, w3, w2)


def _moe_combine_kernel(pos_ref, ys_hbm, h_ref, gate_ref, g_ref, o_ref, buf, sem, *, tm, n_tiles, normalize):
    i = pl.program_id(0)

    def start(tile, slot):
        for choice in range(TOP_K):
            def row_of(r, choice=choice):
                return pos_ref[(tile * tm + r) * TOP_K + choice]

            _start_row_gather(row_of, ys_hbm, buf.at[slot], sem.at[slot], tm, dst_row0=choice * tm)

    @pl.when(i == 0)
    def _():
        start(0, 0)

    @pl.when(i + 1 < n_tiles)
    def _():
        start(i + 1, (i + 1) & 1)

    cur = i & 1
    _wait_row_gather(ys_hbm, buf.at[cur], sem.at[cur], TOP_K * tm)
    gate = gate_ref[...]
    out = h_ref[...] + gate[:, 0:1] * buf[cur, 0:tm, :] + gate[:, 1:2] * buf[cur, tm:2 * tm, :]
    o_ref[...] = _rms(out, g_ref[...]) if normalize else out


def moe_combine(h, ys, gate, pos, out_gain, *, tm, normalize):
    m, d = h.shape
    hspec = pl.BlockSpec((tm, d), lambda i, p: (i, 0))
    grid_spec = pltpu.PrefetchScalarGridSpec(
        num_scalar_prefetch=1,
        grid=(m // tm,),
        in_specs=[pl.BlockSpec(memory_space=pl.ANY), hspec,
                  pl.BlockSpec((tm, ROUTER_W), lambda i, p: (i, 0)),
                  pl.BlockSpec((1, d), lambda i, p: (0, 0))],
        out_specs=hspec,
        scratch_shapes=[pltpu.VMEM((2, TOP_K * tm, d), F32), pltpu.SemaphoreType.DMA((2,))],
    )
    return pl.pallas_call(
        functools.partial(_moe_combine_kernel, tm=tm, n_tiles=m // tm, normalize=normalize),
        grid_spec=grid_spec,
        out_shape=jax.ShapeDtypeStruct((m, d), F32),
        compiler_params=_cparams(("arbitrary",)),
        name="moe_combine",
    )(pos, ys, h, gate, out_gain.reshape(1, d))


def _moe_blocks(counts, n_assign, tb):
    padded = ((counts + tb - 1) // tb) * tb
    pad_end = jnp.cumsum(padded)
    pad_start = (pad_end - padded).astype(jnp.int32)
    nblk = (n_assign + N_EXPERTS * tb) // tb
    n_valid = (pad_end[-1] // tb).astype(jnp.int32)
    blk = jnp.minimum(jnp.arange(nblk, dtype=jnp.int32), n_valid - 1) * tb
    blk_exp = jnp.sum(blk[:, None] >= pad_end[None, :], axis=1).astype(jnp.int32)
    used = counts > 0
    run_idx = (jnp.cumsum(used) - 1).astype(jnp.int32)
    ids = jnp.where(used, jnp.arange(N_EXPERTS, dtype=jnp.int32), N_EXPERTS)
    later = lax.cummin(ids[::-1])[::-1]
    nxt = jnp.concatenate([later[1:], jnp.full((1,), N_EXPERTS, jnp.int32)])
    nxt_exp = jnp.where(nxt < N_EXPERTS, nxt, -1).astype(jnp.int32)
    return pad_start, jnp.minimum(blk_exp, N_EXPERTS - 1), n_valid.reshape(1), nxt_exp, run_idx


def _moe_positions_kernel(eid_ref, rank_ref, pstart_ref, pos_ref):
    eid = eid_ref[...]
    lane = lax.broadcasted_iota(jnp.int32, eid.shape, 1)
    pstart = pstart_ref[...]
    first = [jnp.sum(jnp.where(lane == eid[:, c:c + 1], pstart, 0), axis=-1, keepdims=True)
             for c in range(TOP_K)]
    pos_ref[...] = rank_ref[...] + jnp.where(lane == 0, first[0], jnp.where(lane == 1, first[1], 0))


def moe_positions(eid, rank, pad_start, *, tm):
    m = eid.shape[0]
    spec = pl.BlockSpec((tm, ROUTER_W), lambda i: (i, 0))
    pstart = jnp.pad(pad_start, (0, ROUTER_W - pad_start.shape[0])).reshape(1, ROUTER_W)
    return pl.pallas_call(
        _moe_positions_kernel,
        grid=(m // tm,),
        in_specs=[spec, spec, pl.BlockSpec((1, ROUTER_W), lambda i: (0, 0))],
        out_specs=spec,
        out_shape=jax.ShapeDtypeStruct((m, ROUTER_W), jnp.int32),
        compiler_params=_cparams(("parallel",)),
        name="moe_positions",
    )(eid, rank, pstart)


def hier_moe(h, g, wr_group, wr_expert, w1, w3, w2, layer, out_gain, normalize_out,
             *, tm_router=512, tm_combine=128, tb=MOE_TB):
    m, d = h.shape
    xn, eid, gate, rank, counts = moe_router(h, g, wr_group, wr_expert, tm=tm_router)
    pad_start, blk_exp, n_valid, nxt_exp, run_idx = _moe_blocks(counts[0, :N_EXPERTS], m * TOP_K, tb)
    pos = moe_positions(eid, rank, pad_start, tm=1024)[:, :TOP_K].reshape(-1)
    ys = moe_experts(xn, pos, blk_exp, n_valid, nxt_exp, run_idx, w1, w3, w2, layer, tb=tb)
    return moe_combine(h, ys, gate, pos, out_gain, tm=tm_combine, normalize=normalize_out)


def _w_in_layout(w_in):
    wa = w_in[:, :IN_A]
    wb = w_in[:, IN_A:IN_A + IN_B]
    wc = w_in[:, IN_A + IN_B:IN_A + IN_B + IN_C]
    wd = w_in[:, IN_A + IN_B + IN_C:]
    wb = jnp.pad(wb, ((0, 0), (0, IN_B_PAD - IN_B)))
    return jnp.concatenate([wb, wa, wc, wd], axis=1).astype(BF16)


def kernel(x, mem, norm_mix, w_in, w_out, diff_lambda, diff_subln, rwkv_mu, rwkv_w0, rwkv_w2,
           rwkv_a0, rwkv_a2, rwkv_g2, rwkv_kk, rwkv_ka, rwkv_rk, rwkv_lnx_w, rwkv_lnx_b,
           s5_a_re, s5_a_im, s5_log_dt, s5_b_re, s5_b_im, s5_c_re, s5_c_im, s5_d, s5_glu_w,
           s5_glu_b, mix_out_norm, norm_cross, norm_mem, xa_wq, xa_wkv, xa_wo, norm_moe,
           router_group, router_expert, moe_w1, moe_w3, moe_w2, norm_final):
    b, s, d = x.shape
    m = b * s
    n_mem = mem.shape[1]
    depth = w_in.shape[0]
    h = x.reshape(m, d)
    mem2 = mem.reshape(b * n_mem, d)
    rope_a = _rope_tables(s, A_QKDIM, MIX)
    rope_c = _rope_tables(s, C_HDIM, MIX)
    for l in range(depth):
        z = norm_matmul(h, norm_mix[l], _w_in_layout(w_in[l]), tm=1024, tn=Z_W // 4, out_dtype=F32)
        z = z.reshape(b, s, Z_W)
        qa, ka, v1a = rope_qkv(z, Z_A, A_QKDIM, tm=512, ones_cols=A_VDIM, tables=rope_a)
        oa = diff_attention(qa, ka, v1a, diff_lambda[l], diff_subln[l], l, tq=512)
        ob = rwkv7_bidir(z, rwkv_mu[l], rwkv_w0[l], rwkv_w2[l], rwkv_a0[l], rwkv_a2[l], rwkv_g2[l],
                         rwkv_kk[l], rwkv_ka[l], rwkv_rk[l], rwkv_lnx_w[l], rwkv_lnx_b[l])
        qc, kc, vc = rope_qkv(z, Z_C, C_HDIM, tm=512, tables=rope_c)
        oc = dilated_attention(qc, kc, vc, tq=256)
        od = s5_bidir(z, s5_a_re[l], s5_a_im[l], s5_log_dt[l], s5_b_re[l], s5_b_im[l],
                      s5_c_re[l], s5_c_im[l], s5_d[l], s5_glu_w[l], s5_glu_b[l])
        h = mix_out(h, oa.reshape(m, MIX), ob.reshape(m, MIX), oc.reshape(m, MIX), od.reshape(m, MIX),
                    mix_out_norm[l, 0], mix_out_norm[l, 1], w_out[l].astype(BF16), tm=1024, tn=1024)
        kv = norm_matmul(mem2, norm_mem[l], xa_wkv[l].astype(BF16), tm=b * n_mem, tn=512, out_dtype=BF16)
        h = cross_attention(h.reshape(b, s, d), norm_cross[l], xa_wq[l].astype(BF16),
                            kv.reshape(b, n_mem, 2 * d), xa_wo[l].astype(BF16), tq=256).reshape(m, d)
        h = hier_moe(h, norm_moe[l], router_group[l], router_expert[l], moe_w1, moe_w3, moe_w2, l,
                     norm_final, l == depth - 1)
    return h.reshape(b, s, d)
```

```python
import functools
import math

import numpy as np
import jax
import jax.numpy as jnp
from jax import lax
from jax.experimental import pallas as pl
from jax.experimental.pallas import tpu as pltpu

F32 = jnp.float32
BF16 = jnp.bfloat16

D_MODEL = 2048
MIX = D_MODEL // 4
A_HEADS = 4
A_VDIM = MIX // A_HEADS
A_QKDIM = A_VDIM // 2
B_HDIM = 64
B_HEADS = MIX // B_HDIM
B_LORA = 64
B_GATE_LORA = 128
LN_X_EPS = 64e-5
C_HEADS = 4
C_HDIM = MIX // C_HEADS
C_PATTERNS = ((128, 1), (512, 4), (2048, 16))
D_GSIZE = 16
D_GROUPS = MIX // D_GSIZE
D_STATE = 64
IN_A = 3 * MIX
IN_B = 3 * MIX + 4 * B_LORA + B_GATE_LORA
IN_B_PAD = 2048
IN_C = 3 * MIX
IN_D = MIX
XA_HEADS = 4
XA_HDIM = D_MODEL // XA_HEADS
N_GROUPS = 4
EXPERTS_PER_GROUP = 8
N_EXPERTS = N_GROUPS * EXPERTS_PER_GROUP
TOP_K = 2
D_EXPERT = D_MODEL // 4
ROPE_THETA = 10000.0
RMS_EPS = 1e-6
NEG_INF = -1e30

Z_B = 0
Z_A = IN_B_PAD
Z_C = Z_A + IN_A
Z_D = Z_C + IN_C
Z_W = Z_D + IN_D

LANES = 128
VMEM_LIMIT = 56 * 1024 * 1024


def _cparams(sem, vmem=VMEM_LIMIT):
    return pltpu.CompilerParams(dimension_semantics=sem, vmem_limit_bytes=vmem)


def _rms(x, g):
    return x * lax.rsqrt(jnp.mean(x * x, axis=-1, keepdims=True) + RMS_EPS) * g


def _dot(a, b):
    return jnp.dot(a, b, preferred_element_type=F32)


def _dot_t(a, b):
    return lax.dot_general(a, b, (((1,), (1,)), ((), ())), preferred_element_type=F32)


def _split3(x):
    hi = x.astype(BF16)
    r1 = x - hi.astype(F32)
    mid = r1.astype(BF16)
    lo = (r1 - mid.astype(F32)).astype(BF16)
    return hi, mid, lo


def _segsum(x, ones):
    hi, mid, lo = _split3(x)
    return _dot(hi, ones) + _dot(mid, ones) + _dot(lo, ones)


def _norm_matmul_kernel(x_ref, g_ref, w_ref, o_ref, xn_ref):
    @pl.when(pl.program_id(1) == 0)
    def _():
        xn_ref[...] = _rms(x_ref[...], g_ref[...]).astype(BF16)

    o_ref[...] = _dot(xn_ref[...], w_ref[...].astype(BF16)).astype(o_ref.dtype)


def norm_matmul(x, g, w, *, tm, tn, out_dtype):
    m, k = x.shape
    n = w.shape[1]
    return pl.pallas_call(
        _norm_matmul_kernel,
        grid=(m // tm, n // tn),
        in_specs=[pl.BlockSpec((tm, k), lambda i, j: (i, 0)),
                  pl.BlockSpec((1, k), lambda i, j: (0, 0)),
                  pl.BlockSpec((k, tn), lambda i, j: (0, j))],
        out_specs=pl.BlockSpec((tm, tn), lambda i, j: (i, j)),
        out_shape=jax.ShapeDtypeStruct((m, n), out_dtype),
        scratch_shapes=[pltpu.VMEM((tm, k), BF16)],
        compiler_params=_cparams(("parallel", "arbitrary")),
        name="norm_matmul",
    )(x, g.reshape(1, k), w)


def _rope_tables(seq, dim, width):
    inv = 1.0 / (ROPE_THETA ** (jnp.arange(0, dim, 2, dtype=F32) / dim))
    ang = jnp.arange(seq, dtype=F32)[:, None] * inv[None, :]
    cos, sin = jnp.cos(ang), jnp.sin(ang)
    cos = jnp.concatenate([cos, cos], axis=-1)
    sin = jnp.concatenate([-sin, sin], axis=-1)
    reps = width // dim
    return jnp.tile(cos, (1, reps)), jnp.tile(sin, (1, reps))


def _rope_kernel(q_ref, k_ref, v_ref, cos_ref, sin_ref, qo_ref, ko_ref, vo_ref, *, half, scale, ones_cols):
    cos = cos_ref[...]
    sin = sin_ref[...]
    width = cos.shape[1]
    lane = lax.broadcasted_iota(jnp.int32, cos.shape, 1)
    first = (lane % (2 * half)) < half

    def rot(x):
        ahead = pltpu.roll(x, width - half, axis=1)
        behind = pltpu.roll(x, half, axis=1)
        return x * cos + jnp.where(first, ahead, behind) * sin

    qo_ref[...] = (rot(q_ref[...]) * scale).astype(BF16)
    ko_ref[...] = rot(k_ref[...]).astype(BF16)
    v = v_ref[...].astype(BF16)
    if ones_cols is None:
        vo_ref[...] = v
    else:
        ones = jnp.ones((v.shape[0], ones_cols), BF16)
        parts = []
        for hd in range(v.shape[1] // ones_cols):
            parts += [v[:, hd * ones_cols:(hd + 1) * ones_cols], ones]
        vo_ref[...] = jnp.concatenate(parts, axis=1)


def rope_qkv(z, col0, head_dim, *, tm, ones_cols=None, tables=None):
    b, s, _ = z.shape
    cos, sin = tables if tables is not None else _rope_tables(s, head_dim, MIX)
    cb = col0 // MIX
    zspec = lambda off: pl.BlockSpec((None, tm, MIX), lambda bi, i, off=off: (bi, i, cb + off))
    tspec = pl.BlockSpec((tm, MIX), lambda bi, i: (i, 0))
    ospec = pl.BlockSpec((None, tm, MIX), lambda bi, i: (bi, i, 0))
    oshape = jax.ShapeDtypeStruct((b, s, MIX), BF16)
    if ones_cols is None:
        vspec, vshape = ospec, oshape
    else:
        vspec = pl.BlockSpec((None, tm, 2 * MIX), lambda bi, i: (bi, i, 0))
        vshape = jax.ShapeDtypeStruct((b, s, 2 * MIX), BF16)
    return pl.pallas_call(
        functools.partial(_rope_kernel, half=head_dim // 2, scale=head_dim ** -0.5, ones_cols=ones_cols),
        grid=(b, s // tm),
        in_specs=[zspec(0), zspec(1), zspec(2), tspec, tspec],
        out_specs=[ospec, ospec, vspec],
        out_shape=[oshape, oshape, vshape],
        compiler_params=_cparams(("parallel", "parallel")),
        name="rope_qkv",
    )(z, z, z, cos, sin)


def _diff_attn_kernel(lam_ref, g_ref, q_ref, k_ref, v1_ref, o_ref, *, lam_init):
    lv = lam_ref[...]
    lam = (jnp.exp(jnp.sum(lv[0:1] * lv[1:2], axis=-1, keepdims=True))
           - jnp.exp(jnp.sum(lv[2:3] * lv[3:4], axis=-1, keepdims=True)) + lam_init)
    q = q_ref[...]
    k = k_ref[...]
    v1 = v1_ref[...]
    lane = lax.broadcasted_iota(jnp.int32, q.shape, 1)
    zero = jnp.zeros_like(q)

    def branch(qm):
        s = _dot_t(qm, k)
        p = jnp.exp((s - jnp.max(s, axis=-1, keepdims=True)).astype(BF16))
        acc = _dot(p, v1)
        return acc[:, :A_VDIM] / acc[:, A_VDIM:]

    o = branch(jnp.where(lane < A_QKDIM, q, zero)) - lam * branch(jnp.where(lane >= A_QKDIM, q, zero))
    o_ref[...] = _rms(o, g_ref[...]) * (1.0 - lam_init)


def diff_attention(q, k, v1, lam_vecs, subln_g, layer_idx, *, tq):
    b, s, _ = q.shape
    lam_init = 0.8 - 0.6 * math.exp(-0.3 * layer_idx)
    qspec = pl.BlockSpec((None, tq, A_VDIM), lambda bi, h, i: (bi, i, h))
    return pl.pallas_call(
        functools.partial(_diff_attn_kernel, lam_init=lam_init),
        grid=(b, A_HEADS, s // tq),
        in_specs=[pl.BlockSpec((4, A_QKDIM), lambda bi, h, i: (0, 0)),
                  pl.BlockSpec((1, A_VDIM), lambda bi, h, i: (0, 0)),
                  qspec,
                  pl.BlockSpec((None, s, A_VDIM), lambda bi, h, i: (bi, 0, h)),
                  pl.BlockSpec((None, s, 2 * A_VDIM), lambda bi, h, i: (bi, 0, h))],
        out_specs=qspec,
        out_shape=jax.ShapeDtypeStruct((b, s, MIX), F32),
        compiler_params=_cparams(("parallel", "parallel", "arbitrary")),
        name="diff_attention",
    )(lam_vecs, subln_g.reshape(1, A_VDIM), q, k, v1)


C_REACH = max(w // 2 for w, _ in C_PATTERNS)


def _dilated_bias_table(tq, window):
    n_delta = (window - tq) // tq + 1
    i = np.arange(tq)[None, :, None]
    j = np.arange(window)[None, None, :]
    n = np.arange(n_delta)[:, None, None]
    d = j - i - n * tq
    count = np.zeros(d.shape, np.int32)
    for w, dil in C_PATTERNS:
        count += ((np.abs(d) <= w // 2) & (d % dil == 0)).astype(np.int32)
    bias = np.where(count > 0, np.log(np.maximum(count, 1)), NEG_INF)
    return jnp.asarray(bias, F32)


def _dilated_attn_kernel(bias_ref, q_ref, k_ref, v_ref, o_ref, *, tq, window, seq):
    start = pl.program_id(2) * tq
    ws = pl.multiple_of(jnp.clip(start - C_REACH, 0, seq - window), tq)
    kw = k_ref[pl.ds(ws, window), :]
    vw = v_ref[pl.ds(ws, window), :]
    s = _dot_t(q_ref[...], kw) + bias_ref[...]
    e = jnp.exp(s - jnp.max(s, axis=-1, keepdims=True))
    den = jnp.sum(e, axis=-1, keepdims=True)
    o_ref[...] = _dot(e.astype(BF16), vw) / den


def dilated_attention(q, k, v, *, tq=128):
    b, s, _ = q.shape
    window = tq + 2 * C_REACH
    assert s >= window and s % tq == 0 and C_REACH % tq == 0
    bias = _dilated_bias_table(tq, window)

    def bias_map(bi, h, i):
        start = i * tq
        ws = jnp.clip(start - C_REACH, 0, s - window)
        return ((start - ws) // tq, 0, 0)

    qspec = pl.BlockSpec((None, tq, C_HDIM), lambda bi, h, i: (bi, i, h))
    kspec = pl.BlockSpec((None, s, C_HDIM), lambda bi, h, i: (bi, 0, h))
    return pl.pallas_call(
        functools.partial(_dilated_attn_kernel, tq=tq, window=window, seq=s),
        grid=(b, C_HEADS, s // tq),
        in_specs=[pl.BlockSpec((None, tq, window), bias_map), qspec, kspec, kspec],
        out_specs=qspec,
        out_shape=jax.ShapeDtypeStruct((b, s, MIX), F32),
        compiler_params=_cparams(("parallel", "parallel", "arbitrary")),
        name="dilated_attention",
    )(bias, q, k, v)


def _s5_out_kernel(y_ref, u_ref, d_ref, w_ref, b_ref, o_ref):
    y = y_ref[...] + d_ref[...] * u_ref[...]
    gl = jax.nn.gelu(y)
    gate = jax.nn.sigmoid(_dot(gl.astype(BF16), w_ref[...]) + b_ref[...])
    o_ref[...] = gl * gate


def s5_output(y, z, d_skip, glu_w, glu_b, *, tm):
    b, s, _ = z.shape
    yspec = pl.BlockSpec((None, tm, MIX), lambda bi, i: (bi, i, 0))
    vspec = pl.BlockSpec((1, MIX), lambda bi, i: (0, 0))
    return pl.pallas_call(
        _s5_out_kernel,
        grid=(b, s // tm),
        in_specs=[yspec,
                  pl.BlockSpec((None, tm, MIX), lambda bi, i: (bi, i, Z_D // MIX)),
                  vspec, pl.BlockSpec((MIX, MIX), lambda bi, i: (0, 0)), vspec],
        out_specs=yspec,
        out_shape=jax.ShapeDtypeStruct((b, s, MIX), F32),
        compiler_params=_cparams(("parallel", "parallel")),
        name="s5_output",
    )(y, z, d_skip.reshape(1, MIX), glu_w.astype(BF16), glu_b.reshape(1, MIX))


S5_T = 16
S5_KW = S5_T * D_GSIZE
S5_SW = 2 * D_STATE


S5_GPS = LANES // D_GSIZE


def _s5_chunk_kernel(u_ref, wm_ref, we_ref, wft_ref, ar_ref, ai_ref, y_ref, *, n_chunks, nb):
    rows = nb * n_chunks
    lane = lax.broadcasted_iota(jnp.int32, (n_chunks, LANES), 1)
    seg = [(lane >= D_GSIZE * j) & (lane < D_GSIZE * (j + 1)) for j in range(S5_GPS)]
    per_tile = LANES // D_GSIZE
    u_t = [[u_ref[b, pl.ds(t, n_chunks, stride=S5_T), :] for t in range(S5_T)] for b in range(nb)]
    row = lax.broadcasted_iota(jnp.int32, (rows, S5_SW), 0) & (n_chunks - 1)

    def lane_move(x, src, dst):
        shift = (D_GSIZE * (dst - src)) % LANES
        return pltpu.roll(x, shift, axis=1) if shift else x

    y_groups = []
    for gl in range(S5_GPS):
        packed = []
        for b in range(nb):
            tiles = []
            for half in range(S5_T // per_tile):
                acc = jnp.zeros((n_chunks, LANES), F32)
                for j in range(per_tile):
                    acc = jnp.where(seg[j], lane_move(u_t[b][half * per_tile + j], gl, j), acc)
                tiles.append(acc)
            packed.append(jnp.concatenate(tiles, axis=1))
        u = jnp.concatenate(packed, axis=0).astype(BF16)
        y = _dot(u, wm_ref[gl])
        xinc = _dot(u, we_ref[gl])

        def cmul(t, lvl, d, gl=gl):
            ar = ar_ref[gl, d, lvl:lvl + 1, :]
            ai = ai_ref[gl, d, lvl:lvl + 1, :]
            return t * ar + pltpu.roll(t, D_STATE, axis=1) * ai

        carries = []
        for d in range(2):
            x = xinc[:, d * S5_SW:(d + 1) * S5_SW]
            for lvl in range(n_chunks.bit_length() - 1):
                sh = 1 << lvl
                if d == 0:
                    prev = jnp.where(row >= sh, pltpu.roll(x, sh, axis=0), 0.0)
                else:
                    prev = jnp.where(row < n_chunks - sh, pltpu.roll(x, rows - sh, axis=0), 0.0)
                x = x + cmul(prev, lvl, d)
            if d == 0:
                carries.append(jnp.where(row >= 1, pltpu.roll(x, 1, axis=0), 0.0))
            else:
                carries.append(jnp.where(row < n_chunks - 1, pltpu.roll(x, rows - 1, axis=0), 0.0))
        cin = jnp.concatenate(carries, axis=1).astype(BF16)
        y_groups.append(y + _dot_t(cin, wft_ref[gl]))

    for b in range(nb):
        for t in range(S5_T):
            half, j = divmod(t, per_tile)
            out = jnp.zeros((n_chunks, LANES), F32)
            for gl in range(S5_GPS):
                tile = y_groups[gl][b * n_chunks:(b + 1) * n_chunks, half * LANES:(half + 1) * LANES]
                out = jnp.where(seg[gl], lane_move(tile, j, gl), out)
            y_ref[b, pl.ds(t, n_chunks, stride=S5_T), :] = out


def _cpow_table(ar, ai, count):
    res_r, res_i = [ar], [ai]
    for _ in range(count - 1):
        ar, ai = ar * ar - ai * ai, 2.0 * ar * ai
        res_r.append(ar)
        res_i.append(ai)
    return jnp.stack(res_r), jnp.stack(res_i)


def _s5_chunk_weights(a_re, a_im, log_dt, b_re, b_im, c_re, c_im, n_levels):
    g_n, t_n, c_n = D_GROUPS, S5_T, D_GSIZE
    taus = jnp.arange(t_n + 1, dtype=F32)[None, :, None]
    cr, ci = c_re.astype(F32)[:, None], c_im.astype(F32)[:, None]
    toe, ee, ff, lvl_r, lvl_i = [], [], [], [], []
    for direction in range(2):
        lr = jnp.minimum(a_re[direction].astype(F32), -1e-4)
        li = a_im[direction].astype(F32)
        dt = jnp.exp(log_dt[direction].astype(F32))[:, None]
        mag = jnp.exp(dt * lr)
        abr, abi = mag * jnp.cos(dt * li), mag * jnp.sin(dt * li)
        den = lr * lr + li * li
        qr, qi = lr / den, -li / den
        fr = (abr - 1.0) * qr - abi * qi
        fi = (abr - 1.0) * qi + abi * qr
        br, bi = b_re.astype(F32), b_im.astype(F32)
        btr = (fr[..., None] * br - fi[..., None] * bi).transpose(0, 2, 1)
        bti = (fr[..., None] * bi + fi[..., None] * br).transpose(0, 2, 1)
        pmag = jnp.exp(taus * (dt * lr)[:, None])
        ang = taus * (dt * li)[:, None]
        pr, pi = pmag * jnp.cos(ang), pmag * jnp.sin(ang)

        def c_times(order):
            por, poi = pr[:, order][:, :, None, :], pi[:, order][:, :, None, :]
            return cr * por - ci * poi, cr * poi + ci * por

        cpr, cpi = c_times(jnp.arange(t_n))
        k = (jnp.einsum('gin,gkn->gik', btr, cpr.reshape(g_n, t_n * c_n, D_STATE))
             - jnp.einsum('gin,gkn->gik', bti, cpi.reshape(g_n, t_n * c_n, D_STATE)))
        pad = (t_n - 1) * c_n
        if direction == 0:
            kp = jnp.pad(k, ((0, 0), (0, 0), (pad, 0)))
        else:
            krev = k.reshape(g_n, c_n, t_n, c_n)[:, :, ::-1, :].reshape(g_n, c_n, t_n * c_n)
            kp = jnp.pad(krev, ((0, 0), (0, 0), (0, pad)))
        toe.append(jnp.stack([kp[:, :, (t_n - 1 - s) * c_n:(t_n - 1 - s) * c_n + t_n * c_n]
                              for s in range(t_n)], axis=1))
        order = jnp.arange(t_n - 1, -1, -1) if direction == 0 else jnp.arange(t_n)
        por, poi = pr[:, order][:, :, None, :], pi[:, order][:, :, None, :]
        er = por * btr[:, None] - poi * bti[:, None]
        ei = por * bti[:, None] + poi * btr[:, None]
        ee.append(jnp.concatenate([er, ei], axis=3).reshape(g_n, S5_KW, S5_SW))
        order = jnp.arange(1, t_n + 1) if direction == 0 else jnp.arange(t_n, 0, -1)
        gr, gi = c_times(order)
        ff.append(jnp.concatenate([gr, -gi], axis=3).reshape(g_n, S5_KW, S5_SW))
        tr, ti = _cpow_table(pr[:, t_n], pi[:, t_n], n_levels)
        lvl_r.append(jnp.concatenate([tr, tr], axis=2))
        lvl_i.append(jnp.concatenate([-ti, ti], axis=2))
    wm = (toe[0] + toe[1]).reshape(g_n, S5_KW, S5_KW)
    we = jnp.concatenate(ee, axis=2)
    wft = jnp.concatenate(ff, axis=2)
    ar = jnp.stack(lvl_r).transpose(2, 0, 1, 3)
    ai = jnp.stack(lvl_i).transpose(2, 0, 1, 3)
    return wm.astype(BF16), we.astype(BF16), wft.astype(BF16), ar, ai


def s5_chunked(z, a_re, a_im, log_dt, b_re, b_im, c_re, c_im):
    b, s, _ = z.shape
    n_chunks = s // S5_T
    n_levels = n_chunks.bit_length() - 1
    assert n_chunks == 1 << n_levels
    wm, we, wft, ar, ai = _s5_chunk_weights(a_re, a_im, log_dt, b_re, b_im, c_re, c_im, max(n_levels, 1))
    groups = lambda shape: pl.BlockSpec((S5_GPS,) + shape, lambda k: (k,) + (0,) * len(shape))
    return pl.pallas_call(
        functools.partial(_s5_chunk_kernel, n_chunks=n_chunks, nb=b),
        grid=(D_GROUPS // S5_GPS,),
        in_specs=[pl.BlockSpec((b, s, LANES), lambda k: (0, 0, Z_D // LANES + k)),
                  groups((S5_KW, S5_KW)), groups((S5_KW, 2 * S5_SW)), groups((S5_KW, 2 * S5_SW)),
                  groups((2, ar.shape[2], S5_SW)), groups((2, ar.shape[2], S5_SW))],
        out_specs=pl.BlockSpec((b, s, LANES), lambda k: (0, 0, k)),
        out_shape=jax.ShapeDtypeStruct((b, s, MIX), F32),
        compiler_params=_cparams(("parallel",)),
        name="s5_chunked",
    )(z, wm, we, wft, ar, ai)


def s5_bidir(z, a_re, a_im, log_dt, b_re, b_im, c_re, c_im, d_skip, glu_w, glu_b, *, tm=1024):
    y = s5_chunked(z, a_re, a_im, log_dt, b_re, b_im, c_re, c_im)
    return s5_output(y, z, d_skip, glu_w, glu_b, tm=tm)


def _softplus(y):
    return jnp.maximum(y, 0.0) + jnp.log(1.0 + jnp.exp(-jnp.abs(y)))


def _head_ones(width):
    seg = np.arange(width) // B_HDIM
    return jnp.asarray(seg[:, None] == seg[None, :], BF16)


def _rwkv_pre_kernel(z_ref, zp_ref, zn_ref, mu_ref, w0_ref, a0_ref, w2_ref, a2_ref, g2_ref,
                     kk_ref, ka_ref, rk_ref, ones_ref,
                     r_o, a_o, w0_o, w1_o, k0_o, k1_o, b0_o, b1_o, v_o, g_o, bonus_o,
                     *, tm, n_tiles):
    i = pl.program_id(1)
    z = z_ref[...]
    row = lax.broadcasted_iota(jnp.int32, z.shape, 0)
    prev_row = jnp.where(i > 0, zp_ref[7:8, :], 0.0)
    next_row = jnp.where(i < n_tiles - 1, zn_ref[0:1, :], 0.0)
    zp = jnp.where(row == 0, prev_row, pltpu.roll(z, 1, axis=0))
    zn = jnp.where(row == tm - 1, next_row, pltpu.roll(z, tm - 1, axis=0))
    xs = z + mu_ref[0:1, :] * (zp - z) + mu_ref[1:2, :] * (zn - z)
    r = xs[:, 0:MIX]
    k = xs[:, MIX:2 * MIX]
    v = xs[:, 2 * MIX:3 * MIX]
    c0 = 3 * MIX
    wd = xs[:, c0:c0 + 2 * B_LORA]
    ad = xs[:, c0 + 2 * B_LORA:c0 + 4 * B_LORA]
    gd = xs[:, c0 + 4 * B_LORA:c0 + 4 * B_LORA + B_GATE_LORA]
    lw = _dot(jnp.tanh(wd).astype(BF16), w2_ref[...])
    la = _dot(ad.astype(BF16), a2_ref[...])
    g_o[...] = _dot(jax.nn.sigmoid(gd).astype(BF16), g2_ref[...])
    ones = ones_ref[...]
    kk = k * kk_ref[...]
    kk = kk * lax.rsqrt(_segsum(kk * kk, ones) + 1e-12)
    ka = ka_ref[...]
    ksum = jnp.zeros_like(k)
    for d, (w_o, k_o, b_o) in enumerate(((w0_o, k0_o, b0_o), (w1_o, k1_o, b1_o))):
        cols = slice(d * MIX, (d + 1) * MIX)
        logw = -_softplus(-(w0_ref[d:d + 1, :] + lw[:, cols])) - 0.5
        w_o[...] = -jnp.exp(logw)
        a = jax.nn.sigmoid(a0_ref[d:d + 1, :] + la[:, cols])
        kmod = k * (1.0 + (a - 1.0) * ka)
        k_o[...] = kmod
        b_o[...] = kk * a
        ksum = ksum + kmod
    r_o[...] = r
    a_o[...] = -kk
    bonus_o[...] = _segsum(r * (0.5 * ksum) * rk_ref[...], ones) * v
    v_o[...] = v


def _block_diag2(m):
    z = jnp.zeros_like(m[0])
    return jnp.concatenate([jnp.concatenate([m[0], z], axis=1),
                            jnp.concatenate([z, m[1]], axis=1)], axis=0)


def rwkv_pre(z, mu, w0, w2, a0, a2, g2, k_k, k_a, r_k, *, tm):
    b, s, _ = z.shape
    n_tiles = s // tm
    mu_p = jnp.pad(mu.astype(F32), ((0, 0), (0, IN_B_PAD - IN_B)))
    row_spec = pl.BlockSpec((None, tm, MIX), lambda bi, i: (bi, i, 0))
    vec = lambda n, w: pl.BlockSpec((n, w), lambda bi, i: (0, 0))
    rows = jax.ShapeDtypeStruct((b, s, MIX), F32)
    hb = tm // 8
    return pl.pallas_call(
        functools.partial(_rwkv_pre_kernel, tm=tm, n_tiles=n_tiles),
        grid=(b, n_tiles),
        in_specs=[pl.BlockSpec((None, tm, IN_B_PAD), lambda bi, i: (bi, i, 0)),
                  pl.BlockSpec((None, 8, IN_B_PAD), lambda bi, i: (bi, jnp.maximum(i * hb - 1, 0), 0)),
                  pl.BlockSpec((None, 8, IN_B_PAD), lambda bi, i: (bi, jnp.minimum((i + 1) * hb, s // 8 - 1), 0)),
                  vec(2, IN_B_PAD), vec(2, MIX), vec(2, MIX),
                  vec(2 * B_LORA, 2 * MIX), vec(2 * B_LORA, 2 * MIX), vec(B_GATE_LORA, MIX),
                  vec(1, MIX), vec(1, MIX), vec(1, MIX), vec(MIX, MIX)],
        out_specs=[row_spec] * 11,
        out_shape=[rows] * 11,
        compiler_params=_cparams(("parallel", "parallel")),
        name="rwkv_pre",
    )(z, z, z, mu_p, w0.astype(F32), a0.astype(F32),
      _block_diag2(w2).astype(BF16), _block_diag2(a2).astype(BF16), g2.astype(BF16),
      k_k.reshape(1, MIX), k_a.reshape(1, MIX), r_k.reshape(1, MIX), _head_ones(MIX))


def _rwkv_scan_kernel(af_ref, rf_ref, vf_ref, lwf_ref, kf_ref, bf_ref,
                      ab_ref, rb_ref, vb_ref, lwb_ref, kb_ref, bb_ref,
                      trif_ref, trib_ref, yf_ref, yb_ref, h_ref, *, tc, nb):
    @pl.when(pl.program_id(0) == 0)
    def _():
        h_ref[...] = jnp.zeros_like(h_ref)

    row = lax.broadcasted_iota(jnp.int32, (tc, tc), 0)
    col = lax.broadcasted_iota(jnp.int32, (tc, tc), 1)
    eye = (row == col).astype(F32)
    row2 = lax.broadcasted_iota(jnp.int32, (tc, 2 * tc), 0)
    col2 = lax.broadcasted_iota(jnp.int32, (tc, 2 * tc), 1) & (tc - 1)
    zeros_tv = jnp.zeros((tc, B_HDIM), BF16)
    n_sq = tc.bit_length() - 1
    tdot = lambda x, y: lax.dot_general(x, y, (((0,), (0,)), ((), ())), preferred_element_type=F32)

    seqs = []
    for reverse, (a_ref, r_ref, v_ref, lw_ref, k_ref, b_ref, tri_ref, y_ref) in (
            (False, (af_ref, rf_ref, vf_ref, lwf_ref, kf_ref, bf_ref, trif_ref, yf_ref)),
            (True, (ab_ref, rb_ref, vb_ref, lwb_ref, kb_ref, bb_ref, trib_ref, yb_ref))):
        tri = tri_ref[...]
        before = (col2 > row2) if reverse else (col2 < row2)
        upto = (col2 >= row2) if reverse else (col2 <= row2)
        last = 0 if reverse else tc - 1
        for bi in range(nb):
            lw = lw_ref[bi]
            hi, mid, lo = _split3(lw)
            cl = _dot(tri, hi) + _dot(tri, mid) + _dot(tri, lo)
            tot = cl[last:last + 1, :]
            einv = jnp.exp(-cl)
            etot = jnp.exp(tot - cl)
            b_all = b_ref[bi]
            k_all = k_ref[bi]
            seqs.append(dict(
                a_t=a_ref[bi] * jnp.exp(cl - lw), r_t=r_ref[bi] * jnp.exp(cl),
                b_t=b_all * einv, k_t=k_all * einv, b_h=b_all * etot, k_h=k_all * etot,
                g_tot=jnp.exp(tot), v=v_ref[bi], before=before, upto=upto, y_ref=y_ref, bi=bi))

    inst = [(sq, slice(hd * B_HDIM, (hd + 1) * B_HDIM)) for sq in seqs for hd in range(B_HEADS)]
    ids = range(len(inst))
    at = [sq['a_t'][:, s].astype(BF16) for sq, s in inst]
    rt = [sq['r_t'][:, s].astype(BF16) for sq, s in inst]
    bk = [jnp.concatenate([sq['b_t'][:, s], sq['k_t'][:, s]], axis=0).astype(BF16) for sq, s in inst]
    bkh = [jnp.concatenate([sq['b_h'][:, s], sq['k_h'][:, s]], axis=0).astype(BF16) for sq, s in inst]
    vv = [sq['v'][:, s].astype(BF16) for sq, s in inst]
    g = [_dot_t(jnp.concatenate([at[n], rt[n]], axis=0), bk[n]) for n in ids]
    ga = [jnp.where(inst[n][0]['before'], g[n][:tc], 0.0) for n in ids]
    gr = [jnp.where(inst[n][0]['upto'], g[n][tc:], 0.0).astype(BF16) for n in ids]
    lkv = [_dot(ga[n].astype(BF16), jnp.concatenate([zeros_tv, vv[n]], axis=0)) for n in ids]
    lb = [ga[n][:, :tc].astype(BF16) for n in ids]
    pinv = [eye + ga[n][:, :tc] for n in ids]
    lb = [_dot(x, x).astype(BF16) for x in lb]
    for _ in range(1, n_sq - 1):
        prod = [_dot(jnp.concatenate([pinv[n].astype(BF16), lb[n]], axis=0), lb[n]) for n in ids]
        pinv = [pinv[n] + prod[n][:tc] for n in ids]
        lb = [prod[n][tc:].astype(BF16) for n in ids]
    pinv = [(pinv[n] + _dot(pinv[n].astype(BF16), lb[n])).astype(BF16) for n in ids]
    ah = [_dot(pinv[n], at[n]).astype(BF16) for n in ids]
    u0 = [_dot(pinv[n], lkv[n].astype(BF16)).astype(BF16) for n in ids]
    rhs = [jnp.concatenate([jnp.concatenate([ah[n], u0[n]], axis=1),
                            jnp.concatenate([zeros_tv, vv[n]], axis=1)], axis=0) for n in ids]
    ry = [_dot(gr[n], rhs[n]) for n in ids]
    ph = [tdot(bkh[n], rhs[n]) for n in ids]
    rh = [inst[n][0]['r_t'][:, inst[n][1]] + ry[n][:, :B_HDIM] for n in ids]
    phi = [eye * inst[n][0]['g_tot'][:, inst[n][1]] + ph[n][:, :B_HDIM] for n in ids]
    h0 = [h_ref[n].astype(BF16) for n in ids]
    fin = [_dot(jnp.concatenate([rh[n], phi[n]], axis=0).astype(BF16), h0[n]) for n in ids]
    yo = [fin[n][:tc] + ry[n][:, B_HDIM:] for n in ids]
    for si, sq in enumerate(seqs):
        sq['y_ref'][sq['bi']] = jnp.concatenate(yo[si * B_HEADS:(si + 1) * B_HEADS], axis=1)
    for n in ids:
        h_ref[n] = fin[n][tc:] + ph[n][:, B_HDIM:]


def rwkv_scan(a, r, v, lwf, kf, bvf, lwb, kb, bvb, *, tc=64):
    nb, s, _ = a.shape
    nch = s // tc
    fwd = pl.BlockSpec((nb, tc, MIX), lambda c: (0, c, 0))
    bwd = pl.BlockSpec((nb, tc, MIX), lambda c: (0, nch - 1 - c, 0))
    tspec = pl.BlockSpec((tc, tc), lambda c: (0, 0))
    t_idx = np.arange(tc)
    tri_f = jnp.asarray(t_idx[None, :] <= t_idx[:, None], BF16)
    tri_b = jnp.asarray(t_idx[None, :] >= t_idx[:, None], BF16)
    out = jax.ShapeDtypeStruct((nb, s, MIX), F32)
    return pl.pallas_call(
        functools.partial(_rwkv_scan_kernel, tc=tc, nb=nb),
        grid=(nch,),
        in_specs=[fwd] * 6 + [bwd] * 6 + [tspec, tspec],
        out_specs=[fwd, bwd],
        out_shape=[out, out],
        scratch_shapes=[pltpu.VMEM((2 * nb * B_HEADS, B_HDIM, B_HDIM), F32)],
        compiler_params=_cparams(("arbitrary",)),
        name="rwkv_scan",
    )(a, r, v, lwf, kf, bvf, a, r, v, lwb, kb, bvb, tri_f, tri_b)


def _rwkv_post_kernel(yf_ref, yb_ref, bonus_ref, g_ref, lw_ref, lb_ref, ones_ref, o_ref):
    y = yf_ref[...] + yb_ref[...]
    ones = ones_ref[...]
    mean = _segsum(y, ones) * (1.0 / B_HDIM)
    yc = y - mean
    var = _segsum(yc * yc, ones) * (1.0 / B_HDIM)
    yn = yc * lax.rsqrt(var + LN_X_EPS) * lw_ref[...] + lb_ref[...]
    o_ref[...] = (yn + bonus_ref[...]) * g_ref[...]


def rwkv_post(yf, yb, bonus, g, lnx_w, lnx_b, *, tm):
    b, s, _ = yf.shape
    row_spec = pl.BlockSpec((None, tm, MIX), lambda bi, i: (bi, i, 0))
    vec = pl.BlockSpec((1, MIX), lambda bi, i: (0, 0))
    return pl.pallas_call(
        _rwkv_post_kernel,
        grid=(b, s // tm),
        in_specs=[row_spec, row_spec, row_spec, row_spec, vec, vec,
                  pl.BlockSpec((MIX, MIX), lambda bi, i: (0, 0))],
        out_specs=row_spec,
        out_shape=jax.ShapeDtypeStruct((b, s, MIX), F32),
        compiler_params=_cparams(("parallel", "parallel")),
        name="rwkv_post",
    )(yf, yb, bonus, g, lnx_w.reshape(1, MIX), lnx_b.reshape(1, MIX), _head_ones(MIX))


def rwkv7_bidir(z, mu, w0, w2, a0, a2, g2, k_k, k_a, r_k, lnx_w, lnx_b, *, tm=512, tc=64):
    r, a, wf, wb, kf, kb, bf, bb, v, g, bonus = rwkv_pre(z, mu, w0, w2, a0, a2, g2, k_k, k_a, r_k, tm=tm)
    yf, yb = rwkv_scan(a, r, v, wf, kf, bf, wb, kb, bb, tc=tc)
    return rwkv_post(yf, yb, bonus, g, lnx_w, lnx_b, tm=tm)


def _mix_out_kernel(h_ref, oa_ref, ob_ref, oc_ref, od_ref, gc_ref, gd_ref, w_ref, o_ref, mix_ref):
    @pl.when(pl.program_id(1) == 0)
    def _():
        mix_ref[...] = jnp.concatenate(
            [oa_ref[...], ob_ref[...], _rms(oc_ref[...], gc_ref[...]), _rms(od_ref[...], gd_ref[...])],
            axis=1).astype(BF16)

    o_ref[...] = h_ref[...] + _dot(mix_ref[...], w_ref[...])


def mix_out(h, oa, ob, oc, od, gc, gd, w_out, *, tm, tn):
    m, d = h.shape
    mspec = pl.BlockSpec((tm, MIX), lambda i, j: (i, 0))
    vspec = pl.BlockSpec((1, MIX), lambda i, j: (0, 0))
    hspec = pl.BlockSpec((tm, tn), lambda i, j: (i, j))
    return pl.pallas_call(
        _mix_out_kernel,
        grid=(m // tm, d // tn),
        in_specs=[hspec, mspec, mspec, mspec, mspec, vspec, vspec,
                  pl.BlockSpec((4 * MIX, tn), lambda i, j: (0, j))],
        out_specs=hspec,
        out_shape=jax.ShapeDtypeStruct((m, d), F32),
        scratch_shapes=[pltpu.VMEM((tm, 4 * MIX), BF16)],
        compiler_params=_cparams(("parallel", "arbitrary")),
        name="mix_out",
    )(h, oa, ob, oc, od, gc.reshape(1, MIX), gd.reshape(1, MIX), w_out)


def _cross_attn_kernel(h_ref, g_ref, wq_ref, kv_ref, wo_ref, o_ref):
    h = h_ref[...]
    q = _dot(_rms(h, g_ref[...]).astype(BF16), wq_ref[...]).astype(BF16)
    outs = []
    for hd in range(XA_HEADS):
        cols = slice(hd * XA_HDIM, (hd + 1) * XA_HDIM)
        kh = kv_ref[:, cols]
        vh = kv_ref[:, D_MODEL + hd * XA_HDIM:D_MODEL + (hd + 1) * XA_HDIM]
        s = _dot_t(q[:, cols], kh) * (XA_HDIM ** -0.5)
        e = jnp.exp(s - jnp.max(s, axis=-1, keepdims=True))
        p = e / jnp.sum(e, axis=-1, keepdims=True)
        outs.append(_dot(p.astype(BF16), vh).astype(BF16))
    o = jnp.concatenate(outs, axis=1)
    o_ref[...] = h + _dot(o, wo_ref[...])


def cross_attention(h, g, wq, kv, wo, *, tq):
    b, s, d = h.shape
    n_mem = kv.shape[1]
    hspec = pl.BlockSpec((None, tq, d), lambda bi, i: (bi, i, 0))
    wspec = pl.BlockSpec((d, d), lambda bi, i: (0, 0), pipeline_mode=pl.Buffered(1))
    return pl.pallas_call(
        _cross_attn_kernel,
        grid=(b, s // tq),
        in_specs=[hspec, pl.BlockSpec((1, d), lambda bi, i: (0, 0)), wspec,
                  pl.BlockSpec((None, n_mem, 2 * d), lambda bi, i: (bi, 0, 0)), wspec],
        out_specs=hspec,
        out_shape=jax.ShapeDtypeStruct((b, s, d), F32),
        compiler_params=_cparams(("parallel", "parallel")),
        name="cross_attention",
    )(h, g.reshape(1, d), wq, kv, wo)


ROUTER_W = LANES
MOE_TB = 512


def _router_kernel(h_ref, g_ref, wr_ref, lt_ref, xn_ref, eid_ref, gate_ref, rank_ref, cnt_ref, run_ref):
    @pl.when(pl.program_id(0) == 0)
    def _():
        run_ref[...] = jnp.zeros_like(run_ref)

    xn = _rms(h_ref[...], g_ref[...])
    xn_ref[...] = xn
    logits = jnp.dot(xn, wr_ref[...], preferred_element_type=F32, precision=lax.Precision.HIGHEST)
    lane = lax.broadcasted_iota(jnp.int32, logits.shape, 1)
    big = jnp.int32(ROUTER_W)

    def first_max(mask):
        m = jnp.max(jnp.where(mask, logits, NEG_INF), axis=-1, keepdims=True)
        idx = jnp.min(jnp.where(mask & (logits == m), lane, big), axis=-1, keepdims=True)
        return m, idx

    gmask = lane < N_GROUPS
    gmax, gidx = first_max(gmask)
    g_w = 1.0 / jnp.sum(jnp.where(gmask, jnp.exp(logits - gmax), 0.0), axis=-1, keepdims=True)
    e0 = N_GROUPS + gidx * EXPERTS_PER_GROUP
    emask = (lane >= e0) & (lane < e0 + EXPERTS_PER_GROUP)
    m1, i1 = first_max(emask)
    m2, i2 = first_max(emask & (lane != i1))
    e2 = jnp.exp(m2 - m1)
    w1 = 1.0 / (1.0 + e2)
    w2 = e2 / (1.0 + e2)
    e1 = i1 - N_GROUPS
    e2i = i2 - N_GROUPS
    eid_ref[...] = jnp.where(lane == 0, e1, jnp.where(lane == 1, e2i, 0))
    gate_ref[...] = jnp.where(lane == 0, g_w * w1, jnp.where(lane == 1, g_w * w2, 0.0))
    hit1 = lane == e1
    hit2 = lane == e2i
    onehot = jnp.where(hit1 | hit2, 1.0, 0.0)
    earlier = run_ref[...] + _dot(lt_ref[...], onehot.astype(BF16))
    r1 = jnp.sum(jnp.where(hit1, earlier, 0.0), axis=-1, keepdims=True)
    r2 = jnp.sum(jnp.where(hit2, earlier, 0.0), axis=-1, keepdims=True)
    rank_ref[...] = jnp.where(lane == 0, r1, jnp.where(lane == 1, r2, 0.0)).astype(jnp.int32)
    run_ref[...] = run_ref[...] + jnp.sum(onehot, axis=0, keepdims=True)
    cnt_ref[...] = run_ref[...].astype(jnp.int32)


def moe_router(h, g, wr_group, wr_expert, *, tm):
    m, d = h.shape
    wr = jnp.concatenate([wr_group, wr_expert], axis=1).astype(F32)
    wr = jnp.pad(wr, ((0, 0), (0, ROUTER_W - wr.shape[1])))
    t_idx = np.arange(tm)
    lower = jnp.asarray(t_idx[None, :] < t_idx[:, None], BF16)
    hspec = pl.BlockSpec((tm, d), lambda i: (i, 0))
    lspec = pl.BlockSpec((tm, ROUTER_W), lambda i: (i, 0))
    cspec = pl.BlockSpec((1, ROUTER_W), lambda i: (0, 0))
    lane_i = jax.ShapeDtypeStruct((m, ROUTER_W), jnp.int32)
    return pl.pallas_call(
        _router_kernel,
        grid=(m // tm,),
        in_specs=[hspec, pl.BlockSpec((1, d), lambda i: (0, 0)),
                  pl.BlockSpec((d, ROUTER_W), lambda i: (0, 0)),
                  pl.BlockSpec((tm, tm), lambda i: (0, 0))],
        out_specs=[hspec, lspec, lspec, lspec, cspec],
        out_shape=[jax.ShapeDtypeStruct((m, d), F32), lane_i,
                   jax.ShapeDtypeStruct((m, ROUTER_W), F32), lane_i,
                   jax.ShapeDtypeStruct((1, ROUTER_W), jnp.int32)],
        scratch_shapes=[pltpu.VMEM((1, ROUTER_W), F32)],
        compiler_params=_cparams(("arbitrary",)),
        name="moe_router",
    )(h, g.reshape(1, d), wr, lower)


GATHER_UNROLL = 8


def _start_row_gather(row_of, src_hbm, dst, sem, n_rows, dst_row0=0):
    def issue(r, c):
        pltpu.make_async_copy(src_hbm.at[pl.ds(row_of(r), 1)], dst.at[pl.ds(dst_row0 + r, 1)], sem).start()
        return c

    lax.fori_loop(0, n_rows, issue, 0, unroll=GATHER_UNROLL)


def _wait_row_gather(src_hbm, dst, sem, n_rows):
    pltpu.make_async_copy(src_hbm.at[pl.ds(0, n_rows)], dst, sem).wait()


def _moe_expert_kernel(pos_ref, exp_ref, nvb_ref, nxt_ref, run_ref, x_hbm, w1_hbm, w3_hbm, w2_hbm, o_ref,
                       slot_ref, xbuf, sem, w1buf, w3buf, w2buf, wsem, *, tb, n_assign, layer):
    i = pl.program_id(0)
    nvb = nvb_ref[0]

    def weight_copies(e, s):
        return (pltpu.make_async_copy(w1_hbm.at[layer, e], w1buf.at[s], wsem.at[s, 0]),
                pltpu.make_async_copy(w3_hbm.at[layer, e], w3buf.at[s], wsem.at[s, 1]),
                pltpu.make_async_copy(w2_hbm.at[layer, e], w2buf.at[s], wsem.at[s, 2]))

    e_cur = exp_ref[i]
    first = (i == 0) | (exp_ref[jnp.maximum(i - 1, 0)] != e_cur)
    wslot = run_ref[e_cur] & 1

    @pl.when(i == 0)
    def _():
        for cp in weight_copies(e_cur, wslot):
            cp.start()

    @pl.when(i == 0)
    def _():
        def clear(s, c):
            slot_ref[s] = 0
            return c

        lax.fori_loop(0, slot_ref.shape[0], clear, 0, unroll=GATHER_UNROLL)

        def place(a, c):
            slot_ref[pos_ref[a]] = lax.shift_right_logical(a, TOP_K.bit_length() - 1)
            return c

        lax.fori_loop(0, n_assign, place, 0, unroll=GATHER_UNROLL)
        _start_row_gather(lambda r: slot_ref[r], x_hbm, xbuf.at[0], sem.at[0], tb)

    @pl.when(i + 1 < nvb)
    def _():
        nxt = (i + 1) & 1
        base = (i + 1) * tb
        _start_row_gather(lambda r: slot_ref[base + r], x_hbm, xbuf.at[nxt], sem.at[nxt], tb)

    @pl.when((i < nvb) & first & (nxt_ref[e_cur] >= 0))
    def _():
        for cp in weight_copies(nxt_ref[e_cur], 1 - wslot):
            cp.start()

    @pl.when(i < nvb)
    def _():
        @pl.when(first)
        def _():
            for cp in weight_copies(e_cur, wslot):
                cp.wait()

        cur = i & 1
        _wait_row_gather(x_hbm, xbuf.at[cur], sem.at[cur], tb)
        x = xbuf[cur].astype(BF16)
        h1 = _dot(x, w1buf[wslot].astype(BF16))
        h3 = _dot(x, w3buf[wslot].astype(BF16))
        act = (jax.nn.silu(h1) * h3).astype(BF16)
        o_ref[...] = _dot(act, w2buf[wslot].astype(BF16))

    @pl.when(i >= nvb)
    def _():
        o_ref[...] = jnp.zeros_like(o_ref)


def moe_experts(xn, pos, blk_exp, n_valid, nxt_exp, run_idx, w1, w3, w2, layer, *, tb):
    n_assign = pos.shape[0]
    cap = n_assign + N_EXPERTS * tb
    d = xn.shape[1]
    nblk = cap // tb
    hbm = pl.BlockSpec(memory_space=pl.ANY)
    grid_spec = pltpu.PrefetchScalarGridSpec(
        num_scalar_prefetch=5,
        grid=(nblk,),
        in_specs=[hbm, hbm, hbm, hbm],
        out_specs=pl.BlockSpec((tb, d), lambda i, *_: (i, 0)),
        scratch_shapes=[pltpu.SMEM((cap,), jnp.int32), pltpu.VMEM((2, tb, d), F32),
                        pltpu.SemaphoreType.DMA((2,)),
                        pltpu.VMEM((2, d, D_EXPERT), F32), pltpu.VMEM((2, d, D_EXPERT), F32),
                        pltpu.VMEM((2, D_EXPERT, d), F32), pltpu.SemaphoreType.DMA((2, 3))],
    )
    return pl.pallas_call(
        functools.partial(_moe_expert_kernel, tb=tb, n_assign=n_assign, layer=layer),
        grid_spec=grid_spec,
        out_shape=jax.ShapeDtypeStruct((cap, d), F32),
        compiler_params=_cparams(("arbitrary",)),
        name="moe_experts",
    )(pos, blk_exp, n_valid, nxt_exp, run_idx, xn, w1, w3, w2)


def _moe_combine_kernel(pos_ref, ys_hbm, h_ref, gate_ref, g_ref, o_ref, buf, sem, *, tm, n_tiles, normalize):
    i = pl.program_id(0)

    def start(tile, slot):
        for choice in range(TOP_K):
            def row_of(r, choice=choice):
                return pos_ref[(tile * tm + r) * TOP_K + choice]

            _start_row_gather(row_of, ys_hbm, buf.at[slot], sem.at[slot], tm, dst_row0=choice * tm)

    @pl.when(i == 0)
    def _():
        start(0, 0)

    @pl.when(i + 1 < n_tiles)
    def _():
        start(i + 1, (i + 1) & 1)

    cur = i & 1
    _wait_row_gather(ys_hbm, buf.at[cur], sem.at[cur], TOP_K * tm)
    gate = gate_ref[...]
    out = h_ref[...] + gate[:, 0:1] * buf[cur, 0:tm, :] + gate[:, 1:2] * buf[cur, tm:2 * tm, :]
    o_ref[...] = _rms(out, g_ref[...]) if normalize else out


def moe_combine(h, ys, gate, pos, out_gain, *, tm, normalize):
    m, d = h.shape
    hspec = pl.BlockSpec((tm, d), lambda i, p: (i, 0))
    grid_spec = pltpu.PrefetchScalarGridSpec(
        num_scalar_prefetch=1,
        grid=(m // tm,),
        in_specs=[pl.BlockSpec(memory_space=pl.ANY), hspec,
                  pl.BlockSpec((tm, ROUTER_W), lambda i, p: (i, 0)),
                  pl.BlockSpec((1, d), lambda i, p: (0, 0))],
        out_specs=hspec,
        scratch_shapes=[pltpu.VMEM((2, TOP_K * tm, d), F32), pltpu.SemaphoreType.DMA((2,))],
    )
    return pl.pallas_call(
        functools.partial(_moe_combine_kernel, tm=tm, n_tiles=m // tm, normalize=normalize),
        grid_spec=grid_spec,
        out_shape=jax.ShapeDtypeStruct((m, d), F32),
        compiler_params=_cparams(("arbitrary",)),
        name="moe_combine",
    )(pos, ys, h, gate, out_gain.reshape(1, d))


def _moe_blocks(counts, n_assign, tb):
    padded = ((counts + tb - 1) // tb) * tb
    pad_end = jnp.cumsum(padded)
    pad_start = (pad_end - padded).astype(jnp.int32)
    nblk = (n_assign + N_EXPERTS * tb) // tb
    n_valid = (pad_end[-1] // tb).astype(jnp.int32)
    blk = jnp.minimum(jnp.arange(nblk, dtype=jnp.int32), n_valid - 1) * tb
    blk_exp = jnp.sum(blk[:, None] >= pad_end[None, :], axis=1).astype(jnp.int32)
    used = counts > 0
    run_idx = (jnp.cumsum(used) - 1).astype(jnp.int32)
    ids = jnp.where(used, jnp.arange(N_EXPERTS, dtype=jnp.int32), N_EXPERTS)
    later = lax.cummin(ids[::-1])[::-1]
    nxt = jnp.concatenate([later[1:], jnp.full((1,), N_EXPERTS, jnp.int32)])
    nxt_exp = jnp.where(nxt < N_EXPERTS, nxt, -1).astype(jnp.int32)
    return pad_start, jnp.minimum(blk_exp, N_EXPERTS - 1), n_valid.reshape(1), nxt_exp, run_idx


def _moe_positions_kernel(eid_ref, rank_ref, pstart_ref, pos_ref):
    eid = eid_ref[...]
    lane = lax.broadcasted_iota(jnp.int32, eid.shape, 1)
    pstart = pstart_ref[...]
    first = [jnp.sum(jnp.where(lane == eid[:, c:c + 1], pstart, 0), axis=-1, keepdims=True)
             for c in range(TOP_K)]
    pos_ref[...] = rank_ref[...] + jnp.where(lane == 0, first[0], jnp.where(lane == 1, first[1], 0))


def moe_positions(eid, rank, pad_start, *, tm):
    m = eid.shape[0]
    spec = pl.BlockSpec((tm, ROUTER_W), lambda i: (i, 0))
    pstart = jnp.pad(pad_start, (0, ROUTER_W - pad_start.shape[0])).reshape(1, ROUTER_W)
    return pl.pallas_call(
        _moe_positions_kernel,
        grid=(m // tm,),
        in_specs=[spec, spec, pl.BlockSpec((1, ROUTER_W), lambda i: (0, 0))],
        out_specs=spec,
        out_shape=jax.ShapeDtypeStruct((m, ROUTER_W), jnp.int32),
        compiler_params=_cparams(("parallel",)),
        name="moe_positions",
    )(eid, rank, pstart)


def hier_moe(h, g, wr_group, wr_expert, w1, w3, w2, layer, out_gain, normalize_out,
             *, tm_router=1024, tm_combine=128, tb=MOE_TB):
    m, d = h.shape
    xn, eid, gate, rank, counts = moe_router(h, g, wr_group, wr_expert, tm=tm_router)
    pad_start, blk_exp, n_valid, nxt_exp, run_idx = _moe_blocks(counts[0, :N_EXPERTS], m * TOP_K, tb)
    pos = moe_positions(eid, rank, pad_start, tm=1024)[:, :TOP_K].reshape(-1)
    ys = moe_experts(xn, pos, blk_exp, n_valid, nxt_exp, run_idx, w1, w3, w2, layer, tb=tb)
    return moe_combine(h, ys, gate, pos, out_gain, tm=tm_combine, normalize=normalize_out)


def _w_in_layout(w_in):
    wa = w_in[:, :IN_A]
    wb = w_in[:, IN_A:IN_A + IN_B]
    wc = w_in[:, IN_A + IN_B:IN_A + IN_B + IN_C]
    wd = w_in[:, IN_A + IN_B + IN_C:]
    wb = jnp.pad(wb, ((0, 0), (0, IN_B_PAD - IN_B)))
    return jnp.concatenate([wb, wa, wc, wd], axis=1).astype(BF16)


def kernel(x, mem, norm_mix, w_in, w_out, diff_lambda, diff_subln, rwkv_mu, rwkv_w0, rwkv_w2,
           rwkv_a0, rwkv_a2, rwkv_g2, rwkv_kk, rwkv_ka, rwkv_rk, rwkv_lnx_w, rwkv_lnx_b,
           s5_a_re, s5_a_im, s5_log_dt, s5_b_re, s5_b_im, s5_c_re, s5_c_im, s5_d, s5_glu_w,
           s5_glu_b, mix_out_norm, norm_cross, norm_mem, xa_wq, xa_wkv, xa_wo, norm_moe,
           router_group, router_expert, moe_w1, moe_w3, moe_w2, norm_final):
    b, s, d = x.shape
    m = b * s
    n_mem = mem.shape[1]
    depth = w_in.shape[0]
    h = x.reshape(m, d)
    mem2 = mem.reshape(b * n_mem, d)
    rope_a = _rope_tables(s, A_QKDIM, MIX)
    rope_c = _rope_tables(s, C_HDIM, MIX)
    for l in range(depth):
        z = norm_matmul(h, norm_mix[l], _w_in_layout(w_in[l]), tm=1024, tn=Z_W // 4, out_dtype=F32)
        z = z.reshape(b, s, Z_W)
        qa, ka, v1a = rope_qkv(z, Z_A, A_QKDIM, tm=1024, ones_cols=A_VDIM, tables=rope_a)
        oa = diff_attention(qa, ka, v1a, diff_lambda[l], diff_subln[l], l, tq=512)
        ob = rwkv7_bidir(z, rwkv_mu[l], rwkv_w0[l], rwkv_w2[l], rwkv_a0[l], rwkv_a2[l], rwkv_g2[l],
                         rwkv_kk[l], rwkv_ka[l], rwkv_rk[l], rwkv_lnx_w[l], rwkv_lnx_b[l])
        qc, kc, vc = rope_qkv(z, Z_C, C_HDIM, tm=1024, tables=rope_c)
        oc = dilated_attention(qc, kc, vc, tq=256)
        od = s5_bidir(z, s5_a_re[l], s5_a_im[l], s5_log_dt[l], s5_b_re[l], s5_b_im[l],
                      s5_c_re[l], s5_c_im[l], s5_d[l], s5_glu_w[l], s5_glu_b[l])
        h = mix_out(h, oa.reshape(m, MIX), ob.reshape(m, MIX), oc.reshape(m, MIX), od.reshape(m, MIX),
                    mix_out_norm[l, 0], mix_out_norm[l, 1], w_out[l].astype(BF16), tm=1024, tn=1024)
        kv = norm_matmul(mem2, norm_mem[l], xa_wkv[l], tm=b * n_mem, tn=512, out_dtype=BF16)
        h = cross_attention(h.reshape(b, s, d), norm_cross[l], xa_wq[l].astype(BF16),
                            kv.reshape(b, n_mem, 2 * d), xa_wo[l].astype(BF16), tq=256).reshape(m, d)
        h = hier_moe(h, norm_moe[l], router_group[l], router_expert[l], moe_w1, moe_w3, moe_w2, l,
                     norm_final, l == depth - 1)
    return h.reshape(b, s, d)
```

```python
import functools
import math

import numpy as np
import jax
import jax.numpy as jnp
from jax import lax
from jax.experimental import pallas as pl
from jax.experimental.pallas import tpu as pltpu

F32 = jnp.float32
BF16 = jnp.bfloat16

D_MODEL = 2048
MIX = D_MODEL // 4
A_HEADS = 4
A_VDIM = MIX // A_HEADS
A_QKDIM = A_VDIM // 2
B_HDIM = 64
B_HEADS = MIX // B_HDIM
B_LORA = 64
B_GATE_LORA = 128
LN_X_EPS = 64e-5
C_HEADS = 4
C_HDIM = MIX // C_HEADS
C_PATTERNS = ((128, 1), (512, 4), (2048, 16))
D_GSIZE = 16
D_GROUPS = MIX // D_GSIZE
D_STATE = 64
IN_A = 3 * MIX
IN_B = 3 * MIX + 4 * B_LORA + B_GATE_LORA
IN_B_PAD = 2048
IN_C = 3 * MIX
IN_D = MIX
XA_HEADS = 4
XA_HDIM = D_MODEL // XA_HEADS
N_GROUPS = 4
EXPERTS_PER_GROUP = 8
N_EXPERTS = N_GROUPS * EXPERTS_PER_GROUP
TOP_K = 2
D_EXPERT = D_MODEL // 4
ROPE_THETA = 10000.0
RMS_EPS = 1e-6
NEG_INF = -1e30

Z_B = 0
Z_A = IN_B_PAD
Z_C = Z_A + IN_A
Z_D = Z_C + IN_C
Z_W = Z_D + IN_D

LANES = 128
VMEM_LIMIT = 56 * 1024 * 1024


def _cparams(sem, vmem=VMEM_LIMIT):
    return pltpu.CompilerParams(dimension_semantics=sem, vmem_limit_bytes=vmem)


def _rms(x, g):
    return x * lax.rsqrt(jnp.mean(x * x, axis=-1, keepdims=True) + RMS_EPS) * g


def _dot(a, b):
    return jnp.dot(a, b, preferred_element_type=F32)


def _dot_t(a, b):
    return lax.dot_general(a, b, (((1,), (1,)), ((), ())), preferred_element_type=F32)


def _split3(x):
    hi = x.astype(BF16)
    r1 = x - hi.astype(F32)
    mid = r1.astype(BF16)
    lo = (r1 - mid.astype(F32)).astype(BF16)
    return hi, mid, lo


def _segsum(x, ones):
    hi, mid, lo = _split3(x)
    return _dot(hi, ones) + _dot(mid, ones) + _dot(lo, ones)


def _norm_matmul_kernel(x_ref, g_ref, w_ref, o_ref, xn_ref):
    @pl.when(pl.program_id(1) == 0)
    def _():
        xn_ref[...] = _rms(x_ref[...], g_ref[...]).astype(BF16)

    o_ref[...] = _dot(xn_ref[...], w_ref[...].astype(BF16)).astype(o_ref.dtype)


def norm_matmul(x, g, w, *, tm, tn, out_dtype, layer=None):
    m, k = x.shape
    n = w.shape[-1]
    if layer is None:
        wspec = pl.BlockSpec((k, tn), lambda i, j: (0, j))
    else:
        wspec = pl.BlockSpec((None, k, tn), lambda i, j: (layer, 0, j))
    return pl.pallas_call(
        _norm_matmul_kernel,
        grid=(m // tm, n // tn),
        in_specs=[pl.BlockSpec((tm, k), lambda i, j: (i, 0)),
                  pl.BlockSpec((1, k), lambda i, j: (0, 0)),
                  wspec],
        out_specs=pl.BlockSpec((tm, tn), lambda i, j: (i, j)),
        out_shape=jax.ShapeDtypeStruct((m, n), out_dtype),
        scratch_shapes=[pltpu.VMEM((tm, k), BF16)],
        compiler_params=_cparams(("parallel", "arbitrary")),
        name="norm_matmul",
    )(x, g.reshape(1, k), w)


def _rope_tables(seq, dim, width):
    inv = 1.0 / (ROPE_THETA ** (jnp.arange(0, dim, 2, dtype=F32) / dim))
    ang = jnp.arange(seq, dtype=F32)[:, None] * inv[None, :]
    cos, sin = jnp.cos(ang), jnp.sin(ang)
    cos = jnp.concatenate([cos, cos], axis=-1)
    sin = jnp.concatenate([-sin, sin], axis=-1)
    reps = width // dim
    return jnp.tile(cos, (1, reps)), jnp.tile(sin, (1, reps))


def _rope_kernel(q_ref, k_ref, v_ref, cos_ref, sin_ref, qo_ref, ko_ref, vo_ref, *, half, scale, ones_cols):
    cos = cos_ref[...]
    sin = sin_ref[...]
    width = cos.shape[1]
    lane = lax.broadcasted_iota(jnp.int32, cos.shape, 1)
    first = (lane % (2 * half)) < half

    def rot(x):
        ahead = pltpu.roll(x, width - half, axis=1)
        behind = pltpu.roll(x, half, axis=1)
        return x * cos + jnp.where(first, ahead, behind) * sin

    qo_ref[...] = (rot(q_ref[...]) * scale).astype(BF16)
    ko_ref[...] = rot(k_ref[...]).astype(BF16)
    v = v_ref[...].astype(BF16)
    if ones_cols is None:
        vo_ref[...] = v
    else:
        ones = jnp.ones((v.shape[0], ones_cols), BF16)
        parts = []
        for hd in range(v.shape[1] // ones_cols):
            parts += [v[:, hd * ones_cols:(hd + 1) * ones_cols], ones]
        vo_ref[...] = jnp.concatenate(parts, axis=1)


def rope_qkv(z, col0, head_dim, *, tm, ones_cols=None, tables=None):
    b, s, _ = z.shape
    cos, sin = tables if tables is not None else _rope_tables(s, head_dim, MIX)
    cb = col0 // MIX
    zspec = lambda off: pl.BlockSpec((None, tm, MIX), lambda bi, i, off=off: (bi, i, cb + off))
    tspec = pl.BlockSpec((tm, MIX), lambda bi, i: (i, 0))
    ospec = pl.BlockSpec((None, tm, MIX), lambda bi, i: (bi, i, 0))
    oshape = jax.ShapeDtypeStruct((b, s, MIX), BF16)
    if ones_cols is None:
        vspec, vshape = ospec, oshape
    else:
        vspec = pl.BlockSpec((None, tm, 2 * MIX), lambda bi, i: (bi, i, 0))
        vshape = jax.ShapeDtypeStruct((b, s, 2 * MIX), BF16)
    return pl.pallas_call(
        functools.partial(_rope_kernel, half=head_dim // 2, scale=head_dim ** -0.5, ones_cols=ones_cols),
        grid=(b, s // tm),
        in_specs=[zspec(0), zspec(1), zspec(2), tspec, tspec],
        out_specs=[ospec, ospec, vspec],
        out_shape=[oshape, oshape, vshape],
        compiler_params=_cparams(("parallel", "parallel")),
        name="rope_qkv",
    )(z, z, z, cos, sin)


def _diff_attn_kernel(lam_ref, g_ref, q_ref, k_ref, v1_ref, o_ref, *, lam_init):
    lv = lam_ref[...]
    lam = (jnp.exp(jnp.sum(lv[0:1] * lv[1:2], axis=-1, keepdims=True))
           - jnp.exp(jnp.sum(lv[2:3] * lv[3:4], axis=-1, keepdims=True)) + lam_init)
    q = q_ref[...]
    k = k_ref[...]
    v1 = v1_ref[...]
    lane = lax.broadcasted_iota(jnp.int32, q.shape, 1)
    zero = jnp.zeros_like(q)

    def branch(qm):
        s = _dot_t(qm, k)
        p = jnp.exp((s - jnp.max(s, axis=-1, keepdims=True)).astype(BF16))
        acc = _dot(p, v1)
        return acc[:, :A_VDIM] / acc[:, A_VDIM:]

    o = branch(jnp.where(lane < A_QKDIM, q, zero)) - lam * branch(jnp.where(lane >= A_QKDIM, q, zero))
    o_ref[...] = _rms(o, g_ref[...]) * (1.0 - lam_init)


def diff_attention(q, k, v1, lam_vecs, subln_g, layer_idx, *, tq):
    b, s, _ = q.shape
    lam_init = 0.8 - 0.6 * math.exp(-0.3 * layer_idx)
    qspec = pl.BlockSpec((None, tq, A_VDIM), lambda bi, h, i: (bi, i, h))
    return pl.pallas_call(
        functools.partial(_diff_attn_kernel, lam_init=lam_init),
        grid=(b, A_HEADS, s // tq),
        in_specs=[pl.BlockSpec((4, A_QKDIM), lambda bi, h, i: (0, 0)),
                  pl.BlockSpec((1, A_VDIM), lambda bi, h, i: (0, 0)),
                  qspec,
                  pl.BlockSpec((None, s, A_VDIM), lambda bi, h, i: (bi, 0, h)),
                  pl.BlockSpec((None, s, 2 * A_VDIM), lambda bi, h, i: (bi, 0, h))],
        out_specs=qspec,
        out_shape=jax.ShapeDtypeStruct((b, s, MIX), F32),
        compiler_params=_cparams(("parallel", "parallel", "arbitrary")),
        name="diff_attention",
    )(lam_vecs, subln_g.reshape(1, A_VDIM), q, k, v1)


C_REACH = max(w // 2 for w, _ in C_PATTERNS)


def _dilated_bias_table(tq, window):
    n_delta = (window - tq) // tq + 1
    i = np.arange(tq)[None, :, None]
    j = np.arange(window)[None, None, :]
    n = np.arange(n_delta)[:, None, None]
    d = j - i - n * tq
    count = np.zeros(d.shape, np.int32)
    for w, dil in C_PATTERNS:
        count += ((np.abs(d) <= w // 2) & (d % dil == 0)).astype(np.int32)
    bias = np.where(count > 0, np.log(np.maximum(count, 1)), NEG_INF)
    return jnp.asarray(bias, F32)


def _dilated_attn_kernel(bias_ref, q_ref, k_ref, v_ref, o_ref, *, tq, window, seq):
    start = pl.program_id(2) * tq
    ws = pl.multiple_of(jnp.clip(start - C_REACH, 0, seq - window), tq)
    kw = k_ref[pl.ds(ws, window), :]
    vw = v_ref[pl.ds(ws, window), :]
    s = _dot_t(q_ref[...], kw) + bias_ref[...]
    e = jnp.exp(s - jnp.max(s, axis=-1, keepdims=True))
    den = jnp.sum(e, axis=-1, keepdims=True)
    o_ref[...] = _dot(e.astype(BF16), vw) / den


def dilated_attention(q, k, v, *, tq=128):
    b, s, _ = q.shape
    window = tq + 2 * C_REACH
    assert s >= window and s % tq == 0 and C_REACH % tq == 0
    bias = _dilated_bias_table(tq, window)

    def bias_map(bi, h, i):
        start = i * tq
        ws = jnp.clip(start - C_REACH, 0, s - window)
        return ((start - ws) // tq, 0, 0)

    qspec = pl.BlockSpec((None, tq, C_HDIM), lambda bi, h, i: (bi, i, h))
    kspec = pl.BlockSpec((None, s, C_HDIM), lambda bi, h, i: (bi, 0, h))
    return pl.pallas_call(
        functools.partial(_dilated_attn_kernel, tq=tq, window=window, seq=s),
        grid=(b, C_HEADS, s // tq),
        in_specs=[pl.BlockSpec((None, tq, window), bias_map), qspec, kspec, kspec],
        out_specs=qspec,
        out_shape=jax.ShapeDtypeStruct((b, s, MIX), F32),
        compiler_params=_cparams(("parallel", "parallel", "arbitrary")),
        name="dilated_attention",
    )(bias, q, k, v)


def _s5_out_kernel(y_ref, u_ref, d_ref, w_ref, b_ref, o_ref):
    y = y_ref[...] + d_ref[...] * u_ref[...]
    gl = jax.nn.gelu(y)
    gate = jax.nn.sigmoid(_dot(gl.astype(BF16), w_ref[...]) + b_ref[...])
    o_ref[...] = gl * gate


def s5_output(y, z, d_skip, glu_w, glu_b, *, tm):
    b, s, _ = z.shape
    yspec = pl.BlockSpec((None, tm, MIX), lambda bi, i: (bi, i, 0))
    vspec = pl.BlockSpec((1, MIX), lambda bi, i: (0, 0))
    return pl.pallas_call(
        _s5_out_kernel,
        grid=(b, s // tm),
        in_specs=[yspec,
                  pl.BlockSpec((None, tm, MIX), lambda bi, i: (bi, i, Z_D // MIX)),
                  vspec, pl.BlockSpec((MIX, MIX), lambda bi, i: (0, 0)), vspec],
        out_specs=yspec,
        out_shape=jax.ShapeDtypeStruct((b, s, MIX), F32),
        compiler_params=_cparams(("parallel", "parallel")),
        name="s5_output",
    )(y, z, d_skip.reshape(1, MIX), glu_w.astype(BF16), glu_b.reshape(1, MIX))


S5_T = 16
S5_KW = S5_T * D_GSIZE
S5_SW = 2 * D_STATE


S5_GPS = LANES // D_GSIZE


def _s5_chunk_kernel(u_ref, wm_ref, we_ref, wft_ref, ar_ref, ai_ref, y_ref, *, n_chunks, nb):
    rows = nb * n_chunks
    lane = lax.broadcasted_iota(jnp.int32, (n_chunks, LANES), 1)
    seg = [(lane >= D_GSIZE * j) & (lane < D_GSIZE * (j + 1)) for j in range(S5_GPS)]
    per_tile = LANES // D_GSIZE
    u_t = [[u_ref[b, pl.ds(t, n_chunks, stride=S5_T), :] for t in range(S5_T)] for b in range(nb)]
    row = lax.broadcasted_iota(jnp.int32, (rows, S5_SW), 0) & (n_chunks - 1)

    def lane_move(x, src, dst):
        shift = (D_GSIZE * (dst - src)) % LANES
        return pltpu.roll(x, shift, axis=1) if shift else x

    y_groups = []
    for gl in range(S5_GPS):
        packed = []
        for b in range(nb):
            tiles = []
            for half in range(S5_T // per_tile):
                acc = jnp.zeros((n_chunks, LANES), F32)
                for j in range(per_tile):
                    acc = jnp.where(seg[j], lane_move(u_t[b][half * per_tile + j], gl, j), acc)
                tiles.append(acc)
            packed.append(jnp.concatenate(tiles, axis=1))
        u = jnp.concatenate(packed, axis=0).astype(BF16)
        y = _dot(u, wm_ref[gl])
        xinc = _dot(u, we_ref[gl])

        def cmul(t, lvl, d, gl=gl):
            ar = ar_ref[gl, d, lvl:lvl + 1, :]
            ai = ai_ref[gl, d, lvl:lvl + 1, :]
            return t * ar + pltpu.roll(t, D_STATE, axis=1) * ai

        carries = []
        for d in range(2):
            x = xinc[:, d * S5_SW:(d + 1) * S5_SW]
            for lvl in range(n_chunks.bit_length() - 1):
                sh = 1 << lvl
                if d == 0:
                    prev = jnp.where(row >= sh, pltpu.roll(x, sh, axis=0), 0.0)
                else:
                    prev = jnp.where(row < n_chunks - sh, pltpu.roll(x, rows - sh, axis=0), 0.0)
                x = x + cmul(prev, lvl, d)
            if d == 0:
                carries.append(jnp.where(row >= 1, pltpu.roll(x, 1, axis=0), 0.0))
            else:
                carries.append(jnp.where(row < n_chunks - 1, pltpu.roll(x, rows - 1, axis=0), 0.0))
        cin = jnp.concatenate(carries, axis=1).astype(BF16)
        y_groups.append(y + _dot_t(cin, wft_ref[gl]))

    for b in range(nb):
        for t in range(S5_T):
            half, j = divmod(t, per_tile)
            out = jnp.zeros((n_chunks, LANES), F32)
            for gl in range(S5_GPS):
                tile = y_groups[gl][b * n_chunks:(b + 1) * n_chunks, half * LANES:(half + 1) * LANES]
                out = jnp.where(seg[gl], lane_move(tile, j, gl), out)
            y_ref[b, pl.ds(t, n_chunks, stride=S5_T), :] = out


def _cpow_table(ar, ai, count):
    res_r, res_i = [ar], [ai]
    for _ in range(count - 1):
        ar, ai = ar * ar - ai * ai, 2.0 * ar * ai
        res_r.append(ar)
        res_i.append(ai)
    return jnp.stack(res_r), jnp.stack(res_i)


def _s5_chunk_weights(a_re, a_im, log_dt, b_re, b_im, c_re, c_im, n_levels):
    g_n, t_n, c_n = D_GROUPS, S5_T, D_GSIZE
    taus = jnp.arange(t_n + 1, dtype=F32)[None, :, None]
    cr, ci = c_re.astype(F32)[:, None], c_im.astype(F32)[:, None]
    toe, ee, ff, lvl_r, lvl_i = [], [], [], [], []
    for direction in range(2):
        lr = jnp.minimum(a_re[direction].astype(F32), -1e-4)
        li = a_im[direction].astype(F32)
        dt = jnp.exp(log_dt[direction].astype(F32))[:, None]
        mag = jnp.exp(dt * lr)
        abr, abi = mag * jnp.cos(dt * li), mag * jnp.sin(dt * li)
        den = lr * lr + li * li
        qr, qi = lr / den, -li / den
        fr = (abr - 1.0) * qr - abi * qi
        fi = (abr - 1.0) * qi + abi * qr
        br, bi = b_re.astype(F32), b_im.astype(F32)
        btr = (fr[..., None] * br - fi[..., None] * bi).transpose(0, 2, 1)
        bti = (fr[..., None] * bi + fi[..., None] * br).transpose(0, 2, 1)
        pmag = jnp.exp(taus * (dt * lr)[:, None])
        ang = taus * (dt * li)[:, None]
        pr, pi = pmag * jnp.cos(ang), pmag * jnp.sin(ang)

        def c_times(order):
            por, poi = pr[:, order][:, :, None, :], pi[:, order][:, :, None, :]
            return cr * por - ci * poi, cr * poi + ci * por

        cpr, cpi = c_times(jnp.arange(t_n))
        k = (jnp.einsum('gin,gkn->gik', btr, cpr.reshape(g_n, t_n * c_n, D_STATE))
             - jnp.einsum('gin,gkn->gik', bti, cpi.reshape(g_n, t_n * c_n, D_STATE)))
        pad = (t_n - 1) * c_n
        if direction == 0:
            kp = jnp.pad(k, ((0, 0), (0, 0), (pad, 0)))
        else:
            krev = k.reshape(g_n, c_n, t_n, c_n)[:, :, ::-1, :].reshape(g_n, c_n, t_n * c_n)
            kp = jnp.pad(krev, ((0, 0), (0, 0), (0, pad)))
        toe.append(jnp.stack([kp[:, :, (t_n - 1 - s) * c_n:(t_n - 1 - s) * c_n + t_n * c_n]
                              for s in range(t_n)], axis=1))
        order = jnp.arange(t_n - 1, -1, -1) if direction == 0 else jnp.arange(t_n)
        por, poi = pr[:, order][:, :, None, :], pi[:, order][:, :, None, :]
        er = por * btr[:, None] - poi * bti[:, None]
        ei = por * bti[:, None] + poi * btr[:, None]
        ee.append(jnp.concatenate([er, ei], axis=3).reshape(g_n, S5_KW, S5_SW))
        order = jnp.arange(1, t_n + 1) if direction == 0 else jnp.arange(t_n, 0, -1)
        gr, gi = c_times(order)
        ff.append(jnp.concatenate([gr, -gi], axis=3).reshape(g_n, S5_KW, S5_SW))
        tr, ti = _cpow_table(pr[:, t_n], pi[:, t_n], n_levels)
        lvl_r.append(jnp.concatenate([tr, tr], axis=2))
        lvl_i.append(jnp.concatenate([-ti, ti], axis=2))
    wm = (toe[0] + toe[1]).reshape(g_n, S5_KW, S5_KW)
    we = jnp.concatenate(ee, axis=2)
    wft = jnp.concatenate(ff, axis=2)
    ar = jnp.stack(lvl_r).transpose(2, 0, 1, 3)
    ai = jnp.stack(lvl_i).transpose(2, 0, 1, 3)
    return wm.astype(BF16), we.astype(BF16), wft.astype(BF16), ar, ai


def s5_chunked(z, a_re, a_im, log_dt, b_re, b_im, c_re, c_im):
    b, s, _ = z.shape
    n_chunks = s // S5_T
    n_levels = n_chunks.bit_length() - 1
    assert n_chunks == 1 << n_levels
    wm, we, wft, ar, ai = _s5_chunk_weights(a_re, a_im, log_dt, b_re, b_im, c_re, c_im, max(n_levels, 1))
    groups = lambda shape: pl.BlockSpec((S5_GPS,) + shape, lambda k: (k,) + (0,) * len(shape))
    return pl.pallas_call(
        functools.partial(_s5_chunk_kernel, n_chunks=n_chunks, nb=b),
        grid=(D_GROUPS // S5_GPS,),
        in_specs=[pl.BlockSpec((b, s, LANES), lambda k: (0, 0, Z_D // LANES + k)),
                  groups((S5_KW, S5_KW)), groups((S5_KW, 2 * S5_SW)), groups((S5_KW, 2 * S5_SW)),
                  groups((2, ar.shape[2], S5_SW)), groups((2, ar.shape[2], S5_SW))],
        out_specs=pl.BlockSpec((b, s, LANES), lambda k: (0, 0, k)),
        out_shape=jax.ShapeDtypeStruct((b, s, MIX), F32),
        compiler_params=_cparams(("parallel",)),
        name="s5_chunked",
    )(z, wm, we, wft, ar, ai)


def s5_bidir(z, a_re, a_im, log_dt, b_re, b_im, c_re, c_im, d_skip, glu_w, glu_b, *, tm=1024):
    y = s5_chunked(z, a_re, a_im, log_dt, b_re, b_im, c_re, c_im)
    return s5_output(y, z, d_skip, glu_w, glu_b, tm=tm)


def _softplus(y):
    return jnp.maximum(y, 0.0) + jnp.log(1.0 + jnp.exp(-jnp.abs(y)))


def _head_ones(width):
    seg = np.arange(width) // B_HDIM
    return jnp.asarray(seg[:, None] == seg[None, :], BF16)


def _rwkv_pre_kernel(z_ref, zp_ref, zn_ref, mu_ref, w0_ref, a0_ref, w2_ref, a2_ref, g2_ref,
                     kk_ref, ka_ref, rk_ref, ones_ref,
                     r_o, a_o, w0_o, w1_o, k0_o, k1_o, b0_o, b1_o, v_o, g_o, bonus_o,
                     *, tm, n_tiles):
    i = pl.program_id(1)
    z = z_ref[...]
    row = lax.broadcasted_iota(jnp.int32, z.shape, 0)
    prev_row = jnp.where(i > 0, zp_ref[7:8, :], 0.0)
    next_row = jnp.where(i < n_tiles - 1, zn_ref[0:1, :], 0.0)
    zp = jnp.where(row == 0, prev_row, pltpu.roll(z, 1, axis=0))
    zn = jnp.where(row == tm - 1, next_row, pltpu.roll(z, tm - 1, axis=0))
    xs = z + mu_ref[0:1, :] * (zp - z) + mu_ref[1:2, :] * (zn - z)
    r = xs[:, 0:MIX]
    k = xs[:, MIX:2 * MIX]
    v = xs[:, 2 * MIX:3 * MIX]
    c0 = 3 * MIX
    wd = xs[:, c0:c0 + 2 * B_LORA]
    ad = xs[:, c0 + 2 * B_LORA:c0 + 4 * B_LORA]
    gd = xs[:, c0 + 4 * B_LORA:c0 + 4 * B_LORA + B_GATE_LORA]
    lw = _dot(jnp.tanh(wd).astype(BF16), w2_ref[...])
    la = _dot(ad.astype(BF16), a2_ref[...])
    g_o[...] = _dot(jax.nn.sigmoid(gd).astype(BF16), g2_ref[...])
    ones = ones_ref[...]
    kk = k * kk_ref[...]
    kk = kk * lax.rsqrt(_segsum(kk * kk, ones) + 1e-12)
    ka = ka_ref[...]
    ksum = jnp.zeros_like(k)
    for d, (w_o, k_o, b_o) in enumerate(((w0_o, k0_o, b0_o), (w1_o, k1_o, b1_o))):
        cols = slice(d * MIX, (d + 1) * MIX)
        logw = -_softplus(-(w0_ref[d:d + 1, :] + lw[:, cols])) - 0.5
        w_o[...] = -jnp.exp(logw)
        a = jax.nn.sigmoid(a0_ref[d:d + 1, :] + la[:, cols])
        kmod = k * (1.0 + (a - 1.0) * ka)
        k_o[...] = kmod
        b_o[...] = kk * a
        ksum = ksum + kmod
    r_o[...] = r
    a_o[...] = -kk
    bonus_o[...] = _segsum(r * (0.5 * ksum) * rk_ref[...], ones) * v
    v_o[...] = v


def _block_diag2(m):
    z = jnp.zeros_like(m[0])
    return jnp.concatenate([jnp.concatenate([m[0], z], axis=1),
                            jnp.concatenate([z, m[1]], axis=1)], axis=0)


def rwkv_pre(z, mu, w0, w2, a0, a2, g2, k_k, k_a, r_k, *, tm):
    b, s, _ = z.shape
    n_tiles = s // tm
    mu_p = jnp.pad(mu.astype(F32), ((0, 0), (0, IN_B_PAD - IN_B)))
    row_spec = pl.BlockSpec((None, tm, MIX), lambda bi, i: (bi, i, 0))
    vec = lambda n, w: pl.BlockSpec((n, w), lambda bi, i: (0, 0))
    rows = jax.ShapeDtypeStruct((b, s, MIX), F32)
    hb = tm // 8
    return pl.pallas_call(
        functools.partial(_rwkv_pre_kernel, tm=tm, n_tiles=n_tiles),
        grid=(b, n_tiles),
        in_specs=[pl.BlockSpec((None, tm, IN_B_PAD), lambda bi, i: (bi, i, 0)),
                  pl.BlockSpec((None, 8, IN_B_PAD), lambda bi, i: (bi, jnp.maximum(i * hb - 1, 0), 0)),
                  pl.BlockSpec((None, 8, IN_B_PAD), lambda bi, i: (bi, jnp.minimum((i + 1) * hb, s // 8 - 1), 0)),
                  vec(2, IN_B_PAD), vec(2, MIX), vec(2, MIX),
                  vec(2 * B_LORA, 2 * MIX), vec(2 * B_LORA, 2 * MIX), vec(B_GATE_LORA, MIX),
                  vec(1, MIX), vec(1, MIX), vec(1, MIX), vec(MIX, MIX)],
        out_specs=[row_spec] * 11,
        out_shape=[rows] * 11,
        compiler_params=_cparams(("parallel", "parallel")),
        name="rwkv_pre",
    )(z, z, z, mu_p, w0.astype(F32), a0.astype(F32),
      _block_diag2(w2).astype(BF16), _block_diag2(a2).astype(BF16), g2.astype(BF16),
      k_k.reshape(1, MIX), k_a.reshape(1, MIX), r_k.reshape(1, MIX), _head_ones(MIX))


def _rwkv_scan_kernel(af_ref, rf_ref, vf_ref, lwf_ref, kf_ref, bf_ref,
                      ab_ref, rb_ref, vb_ref, lwb_ref, kb_ref, bb_ref,
                      trif_ref, trib_ref, yf_ref, yb_ref, h_ref, *, tc, nb):
    @pl.when(pl.program_id(0) == 0)
    def _():
        h_ref[...] = jnp.zeros_like(h_ref)

    row = lax.broadcasted_iota(jnp.int32, (tc, tc), 0)
    col = lax.broadcasted_iota(jnp.int32, (tc, tc), 1)
    eye = (row == col).astype(F32)
    row2 = lax.broadcasted_iota(jnp.int32, (tc, 2 * tc), 0)
    col2 = lax.broadcasted_iota(jnp.int32, (tc, 2 * tc), 1) & (tc - 1)
    zeros_tv = jnp.zeros((tc, B_HDIM), BF16)
    n_sq = tc.bit_length() - 1
    tdot = lambda x, y: lax.dot_general(x, y, (((0,), (0,)), ((), ())), preferred_element_type=F32)

    seqs = []
    for reverse, (a_ref, r_ref, v_ref, lw_ref, k_ref, b_ref, tri_ref, y_ref) in (
            (False, (af_ref, rf_ref, vf_ref, lwf_ref, kf_ref, bf_ref, trif_ref, yf_ref)),
            (True, (ab_ref, rb_ref, vb_ref, lwb_ref, kb_ref, bb_ref, trib_ref, yb_ref))):
        tri = tri_ref[...]
        before = (col2 > row2) if reverse else (col2 < row2)
        upto = (col2 >= row2) if reverse else (col2 <= row2)
        last = 0 if reverse else tc - 1
        for bi in range(nb):
            lw = lw_ref[bi]
            hi, mid, lo = _split3(lw)
            cl = _dot(tri, hi) + _dot(tri, mid) + _dot(tri, lo)
            tot = cl[last:last + 1, :]
            einv = jnp.exp(-cl)
            etot = jnp.exp(tot - cl)
            b_all = b_ref[bi]
            k_all = k_ref[bi]
            seqs.append(dict(
                a_t=a_ref[bi] * jnp.exp(cl - lw), r_t=r_ref[bi] * jnp.exp(cl),
                b_t=b_all * einv, k_t=k_all * einv, b_h=b_all * etot, k_h=k_all * etot,
                g_tot=jnp.exp(tot), v=v_ref[bi], before=before, upto=upto, y_ref=y_ref, bi=bi))

    inst = [(sq, slice(hd * B_HDIM, (hd + 1) * B_HDIM)) for sq in seqs for hd in range(B_HEADS)]
    ids = range(len(inst))
    at = [sq['a_t'][:, s].astype(BF16) for sq, s in inst]
    rt = [sq['r_t'][:, s].astype(BF16) for sq, s in inst]
    bk = [jnp.concatenate([sq['b_t'][:, s], sq['k_t'][:, s]], axis=0).astype(BF16) for sq, s in inst]
    bkh = [jnp.concatenate([sq['b_h'][:, s], sq['k_h'][:, s]], axis=0).astype(BF16) for sq, s in inst]
    vv = [sq['v'][:, s].astype(BF16) for sq, s in inst]
    g = [_dot_t(jnp.concatenate([at[n], rt[n]], axis=0), bk[n]) for n in ids]
    ga = [jnp.where(inst[n][0]['before'], g[n][:tc], 0.0) for n in ids]
    gr = [jnp.where(inst[n][0]['upto'], g[n][tc:], 0.0).astype(BF16) for n in ids]
    lkv = [_dot(ga[n].astype(BF16), jnp.concatenate([zeros_tv, vv[n]], axis=0)) for n in ids]
    lb = [ga[n][:, :tc].astype(BF16) for n in ids]
    pinv = [eye + ga[n][:, :tc] for n in ids]
    lb = [_dot(x, x).astype(BF16) for x in lb]
    for _ in range(1, n_sq - 1):
        prod = [_dot(jnp.concatenate([pinv[n].astype(BF16), lb[n]], axis=0), lb[n]) for n in ids]
        pinv = [pinv[n] + prod[n][:tc] for n in ids]
        lb = [prod[n][tc:].astype(BF16) for n in ids]
    pinv = [(pinv[n] + _dot(pinv[n].astype(BF16), lb[n])).astype(BF16) for n in ids]
    ah = [_dot(pinv[n], at[n]).astype(BF16) for n in ids]
    u0 = [_dot(pinv[n], lkv[n].astype(BF16)).astype(BF16) for n in ids]
    rhs = [jnp.concatenate([jnp.concatenate([ah[n], u0[n]], axis=1),
                            jnp.concatenate([zeros_tv, vv[n]], axis=1)], axis=0) for n in ids]
    ry = [_dot(gr[n], rhs[n]) for n in ids]
    ph = [tdot(bkh[n], rhs[n]) for n in ids]
    rh = [inst[n][0]['r_t'][:, inst[n][1]] + ry[n][:, :B_HDIM] for n in ids]
    phi = [eye * inst[n][0]['g_tot'][:, inst[n][1]] + ph[n][:, :B_HDIM] for n in ids]
    h0 = [h_ref[n].astype(BF16) for n in ids]
    fin = [_dot(jnp.concatenate([rh[n], phi[n]], axis=0).astype(BF16), h0[n]) for n in ids]
    yo = [fin[n][:tc] + ry[n][:, B_HDIM:] for n in ids]
    for si, sq in enumerate(seqs):
        sq['y_ref'][sq['bi']] = jnp.concatenate(yo[si * B_HEADS:(si + 1) * B_HEADS], axis=1)
    for n in ids:
        h_ref[n] = fin[n][tc:] + ph[n][:, B_HDIM:]


def rwkv_scan(a, r, v, lwf, kf, bvf, lwb, kb, bvb, *, tc=64):
    nb, s, _ = a.shape
    nch = s // tc
    fwd = pl.BlockSpec((nb, tc, MIX), lambda c: (0, c, 0))
    bwd = pl.BlockSpec((nb, tc, MIX), lambda c: (0, nch - 1 - c, 0))
    tspec = pl.BlockSpec((tc, tc), lambda c: (0, 0))
    t_idx = np.arange(tc)
    tri_f = jnp.asarray(t_idx[None, :] <= t_idx[:, None], BF16)
    tri_b = jnp.asarray(t_idx[None, :] >= t_idx[:, None], BF16)
    out = jax.ShapeDtypeStruct((nb, s, MIX), F32)
    return pl.pallas_call(
        functools.partial(_rwkv_scan_kernel, tc=tc, nb=nb),
        grid=(nch,),
        in_specs=[fwd] * 6 + [bwd] * 6 + [tspec, tspec],
        out_specs=[fwd, bwd],
        out_shape=[out, out],
        scratch_shapes=[pltpu.VMEM((2 * nb * B_HEADS, B_HDIM, B_HDIM), F32)],
        compiler_params=_cparams(("arbitrary",)),
        name="rwkv_scan",
    )(a, r, v, lwf, kf, bvf, a, r, v, lwb, kb, bvb, tri_f, tri_b)


def _rwkv_post_kernel(yf_ref, yb_ref, bonus_ref, g_ref, lw_ref, lb_ref, ones_ref, o_ref):
    y = yf_ref[...] + yb_ref[...]
    ones = ones_ref[...]
    mean = _segsum(y, ones) * (1.0 / B_HDIM)
    yc = y - mean
    var = _segsum(yc * yc, ones) * (1.0 / B_HDIM)
    yn = yc * lax.rsqrt(var + LN_X_EPS) * lw_ref[...] + lb_ref[...]
    o_ref[...] = (yn + bonus_ref[...]) * g_ref[...]


def rwkv_post(yf, yb, bonus, g, lnx_w, lnx_b, *, tm):
    b, s, _ = yf.shape
    row_spec = pl.BlockSpec((None, tm, MIX), lambda bi, i: (bi, i, 0))
    vec = pl.BlockSpec((1, MIX), lambda bi, i: (0, 0))
    return pl.pallas_call(
        _rwkv_post_kernel,
        grid=(b, s // tm),
        in_specs=[row_spec, row_spec, row_spec, row_spec, vec, vec,
                  pl.BlockSpec((MIX, MIX), lambda bi, i: (0, 0))],
        out_specs=row_spec,
        out_shape=jax.ShapeDtypeStruct((b, s, MIX), F32),
        compiler_params=_cparams(("parallel", "parallel")),
        name="rwkv_post",
    )(yf, yb, bonus, g, lnx_w.reshape(1, MIX), lnx_b.reshape(1, MIX), _head_ones(MIX))


def rwkv7_bidir(z, mu, w0, w2, a0, a2, g2, k_k, k_a, r_k, lnx_w, lnx_b, *, tm=512, tc=64):
    r, a, wf, wb, kf, kb, bf, bb, v, g, bonus = rwkv_pre(z, mu, w0, w2, a0, a2, g2, k_k, k_a, r_k, tm=tm)
    yf, yb = rwkv_scan(a, r, v, wf, kf, bf, wb, kb, bb, tc=tc)
    return rwkv_post(yf, yb, bonus, g, lnx_w, lnx_b, tm=tm)


def _mix_out_kernel(h_ref, oa_ref, ob_ref, oc_ref, od_ref, gc_ref, gd_ref, w_ref, o_ref, mix_ref):
    @pl.when(pl.program_id(1) == 0)
    def _():
        mix_ref[...] = jnp.concatenate(
            [oa_ref[...], ob_ref[...], _rms(oc_ref[...], gc_ref[...]), _rms(od_ref[...], gd_ref[...])],
            axis=1).astype(BF16)

    o_ref[...] = h_ref[...] + _dot(mix_ref[...], w_ref[...])


def mix_out(h, oa, ob, oc, od, gc, gd, w_out, *, tm, tn):
    m, d = h.shape
    mspec = pl.BlockSpec((tm, MIX), lambda i, j: (i, 0))
    vspec = pl.BlockSpec((1, MIX), lambda i, j: (0, 0))
    hspec = pl.BlockSpec((tm, tn), lambda i, j: (i, j))
    return pl.pallas_call(
        _mix_out_kernel,
        grid=(m // tm, d // tn),
        in_specs=[hspec, mspec, mspec, mspec, mspec, vspec, vspec,
                  pl.BlockSpec((4 * MIX, tn), lambda i, j: (0, j))],
        out_specs=hspec,
        out_shape=jax.ShapeDtypeStruct((m, d), F32),
        scratch_shapes=[pltpu.VMEM((tm, 4 * MIX), BF16)],
        compiler_params=_cparams(("parallel", "arbitrary")),
        name="mix_out",
    )(h, oa, ob, oc, od, gc.reshape(1, MIX), gd.reshape(1, MIX), w_out)


def _cross_attn_kernel(h_ref, g_ref, wq_ref, kv_ref, wo_ref, o_ref):
    h = h_ref[...]
    q = _dot(_rms(h, g_ref[...]).astype(BF16), wq_ref[...]).astype(BF16)
    outs = []
    for hd in range(XA_HEADS):
        cols = slice(hd * XA_HDIM, (hd + 1) * XA_HDIM)
        kh = kv_ref[:, cols]
        vh = kv_ref[:, D_MODEL + hd * XA_HDIM:D_MODEL + (hd + 1) * XA_HDIM]
        s = _dot_t(q[:, cols], kh) * (XA_HDIM ** -0.5)
        e = jnp.exp(s - jnp.max(s, axis=-1, keepdims=True))
        p = e / jnp.sum(e, axis=-1, keepdims=True)
        outs.append(_dot(p.astype(BF16), vh).astype(BF16))
    o = jnp.concatenate(outs, axis=1)
    o_ref[...] = h + _dot(o, wo_ref[...])


def cross_attention(h, g, wq, kv, wo, *, tq):
    b, s, d = h.shape
    n_mem = kv.shape[1]
    hspec = pl.BlockSpec((None, tq, d), lambda bi, i: (bi, i, 0))
    wspec = pl.BlockSpec((d, d), lambda bi, i: (0, 0), pipeline_mode=pl.Buffered(1))
    return pl.pallas_call(
        _cross_attn_kernel,
        grid=(b, s // tq),
        in_specs=[hspec, pl.BlockSpec((1, d), lambda bi, i: (0, 0)), wspec,
                  pl.BlockSpec((None, n_mem, 2 * d), lambda bi, i: (bi, 0, 0)), wspec],
        out_specs=hspec,
        out_shape=jax.ShapeDtypeStruct((b, s, d), F32),
        compiler_params=_cparams(("parallel", "parallel")),
        name="cross_attention",
    )(h, g.reshape(1, d), wq, kv, wo)


ROUTER_W = LANES
MOE_TB = 256


def _router_kernel(h_ref, g_ref, wr_ref, lt_ref, xn_ref, eid_ref, gate_ref, rank_ref, cnt_ref, run_ref):
    @pl.when(pl.program_id(0) == 0)
    def _():
        run_ref[...] = jnp.zeros_like(run_ref)

    xn = _rms(h_ref[...], g_ref[...])
    xn_ref[...] = xn
    logits = jnp.dot(xn, wr_ref[...], preferred_element_type=F32, precision=lax.Precision.HIGHEST)
    lane = lax.broadcasted_iota(jnp.int32, logits.shape, 1)
    big = jnp.int32(ROUTER_W)

    def first_max(mask):
        m = jnp.max(jnp.where(mask, logits, NEG_INF), axis=-1, keepdims=True)
        idx = jnp.min(jnp.where(mask & (logits == m), lane, big), axis=-1, keepdims=True)
        return m, idx

    gmask = lane < N_GROUPS
    gmax, gidx = first_max(gmask)
    g_w = 1.0 / jnp.sum(jnp.where(gmask, jnp.exp(logits - gmax), 0.0), axis=-1, keepdims=True)
    e0 = N_GROUPS + gidx * EXPERTS_PER_GROUP
    emask = (lane >= e0) & (lane < e0 + EXPERTS_PER_GROUP)
    m1, i1 = first_max(emask)
    m2, i2 = first_max(emask & (lane != i1))
    e2 = jnp.exp(m2 - m1)
    w1 = 1.0 / (1.0 + e2)
    w2 = e2 / (1.0 + e2)
    e1 = i1 - N_GROUPS
    e2i = i2 - N_GROUPS
    eid_ref[...] = jnp.where(lane == 0, e1, jnp.where(lane == 1, e2i, 0))
    gate_ref[...] = jnp.where(lane == 0, g_w * w1, jnp.where(lane == 1, g_w * w2, 0.0))
    hit1 = lane == e1
    hit2 = lane == e2i
    onehot = jnp.where(hit1 | hit2, 1.0, 0.0)
    earlier = run_ref[...] + _dot(lt_ref[...], onehot.astype(BF16))
    r1 = jnp.sum(jnp.where(hit1, earlier, 0.0), axis=-1, keepdims=True)
    r2 = jnp.sum(jnp.where(hit2, earlier, 0.0), axis=-1, keepdims=True)
    rank_ref[...] = jnp.where(lane == 0, r1, jnp.where(lane == 1, r2, 0.0)).astype(jnp.int32)
    run_ref[...] = run_ref[...] + jnp.sum(onehot, axis=0, keepdims=True)
    cnt_ref[...] = run_ref[...].astype(jnp.int32)


def moe_router(h, g, wr_group, wr_expert, *, tm):
    m, d = h.shape
    wr = jnp.concatenate([wr_group, wr_expert], axis=1).astype(F32)
    wr = jnp.pad(wr, ((0, 0), (0, ROUTER_W - wr.shape[1])))
    t_idx = np.arange(tm)
    lower = jnp.asarray(t_idx[None, :] < t_idx[:, None], BF16)
    hspec = pl.BlockSpec((tm, d), lambda i: (i, 0))
    lspec = pl.BlockSpec((tm, ROUTER_W), lambda i: (i, 0))
    cspec = pl.BlockSpec((1, ROUTER_W), lambda i: (0, 0))
    lane_i = jax.ShapeDtypeStruct((m, ROUTER_W), jnp.int32)
    return pl.pallas_call(
        _router_kernel,
        grid=(m // tm,),
        in_specs=[hspec, pl.BlockSpec((1, d), lambda i: (0, 0)),
                  pl.BlockSpec((d, ROUTER_W), lambda i: (0, 0)),
                  pl.BlockSpec((tm, tm), lambda i: (0, 0))],
        out_specs=[hspec, lspec, lspec, lspec, cspec],
        out_shape=[jax.ShapeDtypeStruct((m, d), F32), lane_i,
                   jax.ShapeDtypeStruct((m, ROUTER_W), F32), lane_i,
                   jax.ShapeDtypeStruct((1, ROUTER_W), jnp.int32)],
        scratch_shapes=[pltpu.VMEM((1, ROUTER_W), F32)],
        compiler_params=_cparams(("arbitrary",)),
        name="moe_router",
    )(h, g.reshape(1, d), wr, lower)


GATHER_UNROLL = 8


def _start_row_gather(row_of, src_hbm, dst, sem, n_rows, dst_row0=0):
    def issue(r, c):
        pltpu.make_async_copy(src_hbm.at[pl.ds(row_of(r), 1)], dst.at[pl.ds(dst_row0 + r, 1)], sem).start()
        return c

    lax.fori_loop(0, n_rows, issue, 0, unroll=GATHER_UNROLL)


def _wait_row_gather(src_hbm, dst, sem, n_rows):
    pltpu.make_async_copy(src_hbm.at[pl.ds(0, n_rows)], dst, sem).wait()


def _moe_expert_kernel(pos_ref, exp_ref, nvb_ref, nxt_ref, run_ref, x_hbm, w1_hbm, w3_hbm, w2_hbm, o_ref,
                       slot_ref, xbuf, sem, w1buf, w3buf, w2buf, wsem, *, tb, n_assign, layer):
    i = pl.program_id(0)
    nvb = nvb_ref[0]

    def weight_copies(e, s):
        return (pltpu.make_async_copy(w1_hbm.at[layer, e], w1buf.at[s], wsem.at[s, 0]),
                pltpu.make_async_copy(w3_hbm.at[layer, e], w3buf.at[s], wsem.at[s, 1]),
                pltpu.make_async_copy(w2_hbm.at[layer, e], w2buf.at[s], wsem.at[s, 2]))

    e_cur = exp_ref[i]
    first = (i == 0) | (exp_ref[jnp.maximum(i - 1, 0)] != e_cur)
    wslot = run_ref[e_cur] & 1

    @pl.when(i == 0)
    def _():
        for cp in weight_copies(e_cur, wslot):
            cp.start()

    @pl.when(i == 0)
    def _():
        def clear(s, c):
            slot_ref[s] = 0
            return c

        lax.fori_loop(0, slot_ref.shape[0], clear, 0, unroll=GATHER_UNROLL)

        def place(a, c):
            slot_ref[pos_ref[a]] = lax.shift_right_logical(a, TOP_K.bit_length() - 1)
            return c

        lax.fori_loop(0, n_assign, place, 0, unroll=GATHER_UNROLL)
        _start_row_gather(lambda r: slot_ref[r], x_hbm, xbuf.at[0], sem.at[0], tb)

    @pl.when(i + 1 < nvb)
    def _():
        nxt = (i + 1) & 1
        base = (i + 1) * tb
        _start_row_gather(lambda r: slot_ref[base + r], x_hbm, xbuf.at[nxt], sem.at[nxt], tb)

    @pl.when((i < nvb) & first & (nxt_ref[e_cur] >= 0))
    def _():
        for cp in weight_copies(nxt_ref[e_cur], 1 - wslot):
            cp.start()

    @pl.when(i < nvb)
    def _():
        @pl.when(first)
        def _():
            for cp in weight_copies(e_cur, wslot):
                cp.wait()

        cur = i & 1
        _wait_row_gather(x_hbm, xbuf.at[cur], sem.at[cur], tb)
        x = xbuf[cur].astype(BF16)
        h1 = _dot(x, w1buf[wslot].astype(BF16))
        h3 = _dot(x, w3buf[wslot].astype(BF16))
        act = (jax.nn.silu(h1) * h3).astype(BF16)
        o_ref[...] = _dot(act, w2buf[wslot].astype(BF16))

    @pl.when(i >= nvb)
    def _():
        o_ref[...] = jnp.zeros_like(o_ref)


def moe_experts(xn, pos, blk_exp, n_valid, nxt_exp, run_idx, w1, w3, w2, layer, *, tb):
    n_assign = pos.shape[0]
    cap = n_assign + N_EXPERTS * tb
    d = xn.shape[1]
    nblk = cap // tb
    hbm = pl.BlockSpec(memory_space=pl.ANY)
    grid_spec = pltpu.PrefetchScalarGridSpec(
        num_scalar_prefetch=5,
        grid=(nblk,),
        in_specs=[hbm, hbm, hbm, hbm],
        out_specs=pl.BlockSpec((tb, d), lambda i, *_: (i, 0)),
        scratch_shapes=[pltpu.SMEM((cap,), jnp.int32), pltpu.VMEM((2, tb, d), F32),
                        pltpu.SemaphoreType.DMA((2,)),
                        pltpu.VMEM((2, d, D_EXPERT), F32), pltpu.VMEM((2, d, D_EXPERT), F32),
                        pltpu.VMEM((2, D_EXPERT, d), F32), pltpu.SemaphoreType.DMA((2, 3))],
    )
    return pl.pallas_call(
        functools.partial(_moe_expert_kernel, tb=tb, n_assign=n_assign, layer=layer),
        grid_spec=grid_spec,
        out_shape=jax.ShapeDtypeStruct((cap, d), F32),
        compiler_params=_cparams(("arbitrary",)),
        name="moe_experts",
    )(pos, blk_exp, n_valid, nxt_exp, run_idx, xn, w1, w3, w2)


def _moe_combine_kernel(pos_ref, ys_hbm, h_ref, gate_ref, g_ref, o_ref, buf, sem, *, tm, n_tiles, normalize):
    i = pl.program_id(0)

    def start(tile, slot):
        for choice in range(TOP_K):
            def row_of(r, choice=choice):
                return pos_ref[(tile * tm + r) * TOP_K + choice]

            _start_row_gather(row_of, ys_hbm, buf.at[slot], sem.at[slot], tm, dst_row0=choice * tm)

    @pl.when(i == 0)
    def _():
        start(0, 0)

    @pl.when(i + 1 < n_tiles)
    def _():
        start(i + 1, (i + 1) & 1)

    cur = i & 1
    _wait_row_gather(ys_hbm, buf.at[cur], sem.at[cur], TOP_K * tm)
    gate = gate_ref[...]
    out = h_ref[...] + gate[:, 0:1] * buf[cur, 0:tm, :] + gate[:, 1:2] * buf[cur, tm:2 * tm, :]
    o_ref[...] = _rms(out, g_ref[...]) if normalize else out


def moe_combine(h, ys, gate, pos, out_gain, *, tm, normalize):
    m, d = h.shape
    hspec = pl.BlockSpec((tm, d), lambda i, p: (i, 0))
    grid_spec = pltpu.PrefetchScalarGridSpec(
        num_scalar_prefetch=1,
        grid=(m // tm,),
        in_specs=[pl.BlockSpec(memory_space=pl.ANY), hspec,
                  pl.BlockSpec((tm, ROUTER_W), lambda i, p: (i, 0)),
                  pl.BlockSpec((1, d), lambda i, p: (0, 0))],
        out_specs=hspec,
        scratch_shapes=[pltpu.VMEM((2, TOP_K * tm, d), F32), pltpu.SemaphoreType.DMA((2,))],
    )
    return pl.pallas_call(
        functools.partial(_moe_combine_kernel, tm=tm, n_tiles=m // tm, normalize=normalize),
        grid_spec=grid_spec,
        out_shape=jax.ShapeDtypeStruct((m, d), F32),
        compiler_params=_cparams(("arbitrary",)),
        name="moe_combine",
    )(pos, ys, h, gate, out_gain.reshape(1, d))


def _moe_blocks(counts, n_assign, tb):
    padded = ((counts + tb - 1) // tb) * tb
    pad_end = jnp.cumsum(padded)
    pad_start = (pad_end - padded).astype(jnp.int32)
    nblk = (n_assign + N_EXPERTS * tb) // tb
    n_valid = (pad_end[-1] // tb).astype(jnp.int32)
    blk = jnp.minimum(jnp.arange(nblk, dtype=jnp.int32), n_valid - 1) * tb
    blk_exp = jnp.sum(blk[:, None] >= pad_end[None, :], axis=1).astype(jnp.int32)
    used = counts > 0
    run_idx = (jnp.cumsum(used) - 1).astype(jnp.int32)
    ids = jnp.where(used, jnp.arange(N_EXPERTS, dtype=jnp.int32), N_EXPERTS)
    later = lax.cummin(ids[::-1])[::-1]
    nxt = jnp.concatenate([later[1:], jnp.full((1,), N_EXPERTS, jnp.int32)])
    nxt_exp = jnp.where(nxt < N_EXPERTS, nxt, -1).astype(jnp.int32)
    return pad_start, jnp.minimum(blk_exp, N_EXPERTS - 1), n_valid.reshape(1), nxt_exp, run_idx


def _moe_positions_kernel(eid_ref, rank_ref, pstart_ref, pos_ref):
    eid = eid_ref[...]
    lane = lax.broadcasted_iota(jnp.int32, eid.shape, 1)
    pstart = pstart_ref[...]
    first = [jnp.sum(jnp.where(lane == eid[:, c:c + 1], pstart, 0), axis=-1, keepdims=True)
             for c in range(TOP_K)]
    pos_ref[...] = rank_ref[...] + jnp.where(lane == 0, first[0], jnp.where(lane == 1, first[1], 0))


def moe_positions(eid, rank, pad_start, *, tm):
    m = eid.shape[0]
    spec = pl.BlockSpec((tm, ROUTER_W), lambda i: (i, 0))
    pstart = jnp.pad(pad_start, (0, ROUTER_W - pad_start.shape[0])).reshape(1, ROUTER_W)
    return pl.pallas_call(
        _moe_positions_kernel,
        grid=(m // tm,),
        in_specs=[spec, spec, pl.BlockSpec((1, ROUTER_W), lambda i: (0, 0))],
        out_specs=spec,
        out_shape=jax.ShapeDtypeStruct((m, ROUTER_W), jnp.int32),
        compiler_params=_cparams(("parallel",)),
        name="moe_positions",
    )(eid, rank, pstart)


def hier_moe(h, g, wr_group, wr_expert, w1, w3, w2, layer, out_gain, normalize_out,
             *, tm_router=1024, tm_combine=128, tb=MOE_TB):
    m, d = h.shape
    xn, eid, gate, rank, counts = moe_router(h, g, wr_group, wr_expert, tm=tm_router)
    pad_start, blk_exp, n_valid, nxt_exp, run_idx = _moe_blocks(counts[0, :N_EXPERTS], m * TOP_K, tb)
    pos = moe_positions(eid, rank, pad_start, tm=1024)[:, :TOP_K].reshape(-1)
    ys = moe_experts(xn, pos, blk_exp, n_valid, nxt_exp, run_idx, w1, w3, w2, layer, tb=tb)
    return moe_combine(h, ys, gate, pos, out_gain, tm=tm_combine, normalize=normalize_out)


def _w_in_layout(w_in):
    wa = w_in[:, :IN_A]
    wb = w_in[:, IN_A:IN_A + IN_B]
    wc = w_in[:, IN_A + IN_B:IN_A + IN_B + IN_C]
    wd = w_in[:, IN_A + IN_B + IN_C:]
    wb = jnp.pad(wb, ((0, 0), (0, IN_B_PAD - IN_B)))
    return jnp.concatenate([wb, wa, wc, wd], axis=1).astype(BF16)


def kernel(x, mem, norm_mix, w_in, w_out, diff_lambda, diff_subln, rwkv_mu, rwkv_w0, rwkv_w2,
           rwkv_a0, rwkv_a2, rwkv_g2, rwkv_kk, rwkv_ka, rwkv_rk, rwkv_lnx_w, rwkv_lnx_b,
           s5_a_re, s5_a_im, s5_log_dt, s5_b_re, s5_b_im, s5_c_re, s5_c_im, s5_d, s5_glu_w,
           s5_glu_b, mix_out_norm, norm_cross, norm_mem, xa_wq, xa_wkv, xa_wo, norm_moe,
           router_group, router_expert, moe_w1, moe_w3, moe_w2, norm_final):
    b, s, d = x.shape
    m = b * s
    n_mem = mem.shape[1]
    depth = w_in.shape[0]
    h = x.reshape(m, d)
    mem2 = mem.reshape(b * n_mem, d)
    rope_a = _rope_tables(s, A_QKDIM, MIX)
    rope_c = _rope_tables(s, C_HDIM, MIX)
    for l in range(depth):
        z = norm_matmul(h, norm_mix[l], _w_in_layout(w_in[l]), tm=1024, tn=Z_W // 4, out_dtype=F32)
        z = z.reshape(b, s, Z_W)
        qa, ka, v1a = rope_qkv(z, Z_A, A_QKDIM, tm=1024, ones_cols=A_VDIM, tables=rope_a)
        oa = diff_attention(qa, ka, v1a, diff_lambda[l], diff_subln[l], l, tq=512)
        ob = rwkv7_bidir(z, rwkv_mu[l], rwkv_w0[l], rwkv_w2[l], rwkv_a0[l], rwkv_a2[l], rwkv_g2[l],
                         rwkv_kk[l], rwkv_ka[l], rwkv_rk[l], rwkv_lnx_w[l], rwkv_lnx_b[l])
        qc, kc, vc = rope_qkv(z, Z_C, C_HDIM, tm=1024, tables=rope_c)
        oc = dilated_attention(qc, kc, vc, tq=256)
        od = s5_bidir(z, s5_a_re[l], s5_a_im[l], s5_log_dt[l], s5_b_re[l], s5_b_im[l],
                      s5_c_re[l], s5_c_im[l], s5_d[l], s5_glu_w[l], s5_glu_b[l])
        h = mix_out(h, oa.reshape(m, MIX), ob.reshape(m, MIX), oc.reshape(m, MIX), od.reshape(m, MIX),
                    mix_out_norm[l, 0], mix_out_norm[l, 1], w_out[l].astype(BF16), tm=1024, tn=1024)
        kv = norm_matmul(mem2, norm_mem[l], xa_wkv, tm=b * n_mem, tn=512, out_dtype=BF16, layer=l)
        h = cross_attention(h.reshape(b, s, d), norm_cross[l], xa_wq[l].astype(BF16),
                            kv.reshape(b, n_mem, 2 * d), xa_wo[l].astype(BF16), tq=256).reshape(m, d)
        h = hier_moe(h, norm_moe[l], router_group[l], router_expert[l], moe_w1, moe_w3, moe_w2, l,
                     norm_final, l == depth - 1)
    return h.reshape(b, s, d)
```

```python
import functools
import math

import numpy as np
import jax
import jax.numpy as jnp
from jax import lax
from jax.experimental import pallas as pl
from jax.experimental.pallas import tpu as pltpu

F32 = jnp.float32
BF16 = jnp.bfloat16

D_MODEL = 2048
MIX = D_MODEL // 4
A_HEADS = 4
A_VDIM = MIX // A_HEADS
A_QKDIM = A_VDIM // 2
B_HDIM = 64
B_HEADS = MIX // B_HDIM
B_LORA = 64
B_GATE_LORA = 128
LN_X_EPS = 64e-5
C_HEADS = 4
C_HDIM = MIX // C_HEADS
C_PATTERNS = ((128, 1), (512, 4), (2048, 16))
D_GSIZE = 16
D_GROUPS = MIX // D_GSIZE
D_STATE = 64
IN_A = 3 * MIX
IN_B = 3 * MIX + 4 * B_LORA + B_GATE_LORA
IN_B_PAD = 2048
IN_C = 3 * MIX
IN_D = MIX
XA_HEADS = 4
XA_HDIM = D_MODEL // XA_HEADS
N_GROUPS = 4
EXPERTS_PER_GROUP = 8
N_EXPERTS = N_GROUPS * EXPERTS_PER_GROUP
TOP_K = 2
D_EXPERT = D_MODEL // 4
ROPE_THETA = 10000.0
RMS_EPS = 1e-6
NEG_INF = -1e30

Z_B = 0
Z_A = IN_B_PAD
Z_C = Z_A + IN_A
Z_D = Z_C + IN_C
Z_W = Z_D + IN_D

LANES = 128
VMEM_LIMIT = 56 * 1024 * 1024


def _cparams(sem, vmem=VMEM_LIMIT):
    return pltpu.CompilerParams(dimension_semantics=sem, vmem_limit_bytes=vmem)


def _rms(x, g):
    return x * lax.rsqrt(jnp.mean(x * x, axis=-1, keepdims=True) + RMS_EPS) * g


def _dot(a, b):
    return jnp.dot(a, b, preferred_element_type=F32)


def _dot_t(a, b):
    return lax.dot_general(a, b, (((1,), (1,)), ((), ())), preferred_element_type=F32)


def _split3(x):
    hi = x.astype(BF16)
    r1 = x - hi.astype(F32)
    mid = r1.astype(BF16)
    lo = (r1 - mid.astype(F32)).astype(BF16)
    return hi, mid, lo


def _segsum(x, ones):
    hi, mid, lo = _split3(x)
    return _dot(hi, ones) + _dot(mid, ones) + _dot(lo, ones)


def _norm_matmul_kernel(x_ref, g_ref, w_ref, o_ref, xn_ref):
    @pl.when(pl.program_id(1) == 0)
    def _():
        xn_ref[...] = _rms(x_ref[...], g_ref[...]).astype(BF16)

    o_ref[...] = _dot(xn_ref[...], w_ref[...].astype(BF16)).astype(o_ref.dtype)


def norm_matmul(x, g, w, *, tm, tn, out_dtype, layer=None):
    m, k = x.shape
    n = w.shape[-1]
    if layer is None:
        wspec = pl.BlockSpec((k, tn), lambda i, j: (0, j))
    else:
        wspec = pl.BlockSpec((None, k, tn), lambda i, j: (layer, 0, j))
    return pl.pallas_call(
        _norm_matmul_kernel,
        grid=(m // tm, n // tn),
        in_specs=[pl.BlockSpec((tm, k), lambda i, j: (i, 0)),
                  pl.BlockSpec((1, k), lambda i, j: (0, 0)),
                  wspec],
        out_specs=pl.BlockSpec((tm, tn), lambda i, j: (i, j)),
        out_shape=jax.ShapeDtypeStruct((m, n), out_dtype),
        scratch_shapes=[pltpu.VMEM((tm, k), BF16)],
        compiler_params=_cparams(("parallel", "arbitrary")),
        name="norm_matmul",
    )(x, g.reshape(1, k), w)


def _rope_tables(seq, dim, width):
    inv = (1.0 / (np.float32(ROPE_THETA) ** (np.arange(0, dim, 2, dtype=np.float32) / np.float32(dim))))
    ang = np.arange(seq, dtype=np.float32)[:, None] * inv.astype(np.float32)[None, :]
    cos, sin = np.cos(ang), np.sin(ang)
    cos = np.concatenate([cos, cos], axis=-1)
    sin = np.concatenate([-sin, sin], axis=-1)
    reps = width // dim
    return jnp.asarray(np.tile(cos, (1, reps)), F32), jnp.asarray(np.tile(sin, (1, reps)), F32)


def _rope_kernel(q_ref, k_ref, v_ref, cos_ref, sin_ref, qo_ref, ko_ref, vo_ref, *, half, scale, ones_cols):
    cos = cos_ref[...]
    sin = sin_ref[...]
    width = cos.shape[1]
    lane = lax.broadcasted_iota(jnp.int32, cos.shape, 1)
    first = (lane % (2 * half)) < half

    def rot(x):
        ahead = pltpu.roll(x, width - half, axis=1)
        behind = pltpu.roll(x, half, axis=1)
        return x * cos + jnp.where(first, ahead, behind) * sin

    qo_ref[...] = (rot(q_ref[...]) * scale).astype(BF16)
    ko_ref[...] = rot(k_ref[...]).astype(BF16)
    v = v_ref[...].astype(BF16)
    if ones_cols is None:
        vo_ref[...] = v
    else:
        ones = jnp.ones((v.shape[0], ones_cols), BF16)
        parts = []
        for hd in range(v.shape[1] // ones_cols):
            parts += [v[:, hd * ones_cols:(hd + 1) * ones_cols], ones]
        vo_ref[...] = jnp.concatenate(parts, axis=1)


def rope_qkv(z, col0, head_dim, *, tm, ones_cols=None, tables=None):
    b, s, _ = z.shape
    cos, sin = tables if tables is not None else _rope_tables(s, head_dim, MIX)
    cb = col0 // MIX
    zspec = lambda off: pl.BlockSpec((None, tm, MIX), lambda bi, i, off=off: (bi, i, cb + off))
    tspec = pl.BlockSpec((tm, MIX), lambda bi, i: (i, 0))
    ospec = pl.BlockSpec((None, tm, MIX), lambda bi, i: (bi, i, 0))
    oshape = jax.ShapeDtypeStruct((b, s, MIX), BF16)
    if ones_cols is None:
        vspec, vshape = ospec, oshape
    else:
        vspec = pl.BlockSpec((None, tm, 2 * MIX), lambda bi, i: (bi, i, 0))
        vshape = jax.ShapeDtypeStruct((b, s, 2 * MIX), BF16)
    return pl.pallas_call(
        functools.partial(_rope_kernel, half=head_dim // 2, scale=head_dim ** -0.5, ones_cols=ones_cols),
        grid=(b, s // tm),
        in_specs=[zspec(0), zspec(1), zspec(2), tspec, tspec],
        out_specs=[ospec, ospec, vspec],
        out_shape=[oshape, oshape, vshape],
        compiler_params=_cparams(("parallel", "parallel")),
        name="rope_qkv",
    )(z, z, z, cos, sin)


def _diff_attn_kernel(lam_ref, g_ref, q_ref, k_ref, v1_ref, o_ref, *, lam_init):
    lv = lam_ref[...]
    lam = (jnp.exp(jnp.sum(lv[0:1] * lv[1:2], axis=-1, keepdims=True))
           - jnp.exp(jnp.sum(lv[2:3] * lv[3:4], axis=-1, keepdims=True)) + lam_init)
    q = q_ref[...]
    k = k_ref[...]
    v1 = v1_ref[...]
    lane = lax.broadcasted_iota(jnp.int32, q.shape, 1)
    zero = jnp.zeros_like(q)

    def branch(qm):
        s = _dot_t(qm, k)
        p = jnp.exp((s - jnp.max(s, axis=-1, keepdims=True)).astype(BF16))
        acc = _dot(p, v1)
        return acc[:, :A_VDIM] / acc[:, A_VDIM:]

    o = branch(jnp.where(lane < A_QKDIM, q, zero)) - lam * branch(jnp.where(lane >= A_QKDIM, q, zero))
    o_ref[...] = _rms(o, g_ref[...]) * (1.0 - lam_init)


def diff_attention(q, k, v1, lam_vecs, subln_g, layer_idx, *, tq):
    b, s, _ = q.shape
    lam_init = 0.8 - 0.6 * math.exp(-0.3 * layer_idx)
    qspec = pl.BlockSpec((None, tq, A_VDIM), lambda bi, h, i: (bi, i, h))
    return pl.pallas_call(
        functools.partial(_diff_attn_kernel, lam_init=lam_init),
        grid=(b, A_HEADS, s // tq),
        in_specs=[pl.BlockSpec((4, A_QKDIM), lambda bi, h, i: (0, 0)),
                  pl.BlockSpec((1, A_VDIM), lambda bi, h, i: (0, 0)),
                  qspec,
                  pl.BlockSpec((None, s, A_VDIM), lambda bi, h, i: (bi, 0, h)),
                  pl.BlockSpec((None, s, 2 * A_VDIM), lambda bi, h, i: (bi, 0, h))],
        out_specs=qspec,
        out_shape=jax.ShapeDtypeStruct((b, s, MIX), F32),
        compiler_params=_cparams(("parallel", "parallel", "arbitrary")),
        name="diff_attention",
    )(lam_vecs, subln_g.reshape(1, A_VDIM), q, k, v1)


C_REACH = max(w // 2 for w, _ in C_PATTERNS)


def _dilated_bias_table(tq, window):
    n_delta = (window - tq) // tq + 1
    i = np.arange(tq)[None, :, None]
    j = np.arange(window)[None, None, :]
    n = np.arange(n_delta)[:, None, None]
    d = j - i - n * tq
    count = np.zeros(d.shape, np.int32)
    for w, dil in C_PATTERNS:
        count += ((np.abs(d) <= w // 2) & (d % dil == 0)).astype(np.int32)
    bias = np.where(count > 0, np.log(np.maximum(count, 1)), NEG_INF)
    return jnp.asarray(bias, F32)


def _dilated_attn_kernel(bias_ref, q_ref, k_ref, v_ref, o_ref, *, tq, window, seq):
    start = pl.program_id(2) * tq
    ws = pl.multiple_of(jnp.clip(start - C_REACH, 0, seq - window), tq)
    kw = k_ref[pl.ds(ws, window), :]
    vw = v_ref[pl.ds(ws, window), :]
    s = _dot_t(q_ref[...], kw) + bias_ref[...]
    e = jnp.exp(s - jnp.max(s, axis=-1, keepdims=True))
    den = jnp.sum(e, axis=-1, keepdims=True)
    o_ref[...] = _dot(e.astype(BF16), vw) / den


def dilated_attention(q, k, v, *, tq=128):
    b, s, _ = q.shape
    window = tq + 2 * C_REACH
    assert s >= window and s % tq == 0 and C_REACH % tq == 0
    bias = _dilated_bias_table(tq, window)

    def bias_map(bi, h, i):
        start = i * tq
        ws = jnp.clip(start - C_REACH, 0, s - window)
        return ((start - ws) // tq, 0, 0)

    qspec = pl.BlockSpec((None, tq, C_HDIM), lambda bi, h, i: (bi, i, h))
    kspec = pl.BlockSpec((None, s, C_HDIM), lambda bi, h, i: (bi, 0, h))
    return pl.pallas_call(
        functools.partial(_dilated_attn_kernel, tq=tq, window=window, seq=s),
        grid=(b, C_HEADS, s // tq),
        in_specs=[pl.BlockSpec((None, tq, window), bias_map), qspec, kspec, kspec],
        out_specs=qspec,
        out_shape=jax.ShapeDtypeStruct((b, s, MIX), F32),
        compiler_params=_cparams(("parallel", "parallel", "arbitrary")),
        name="dilated_attention",
    )(bias, q, k, v)


def _s5_out_kernel(y_ref, u_ref, d_ref, w_ref, b_ref, o_ref):
    y = y_ref[...] + d_ref[...] * u_ref[...]
    gl = jax.nn.gelu(y)
    gate = jax.nn.sigmoid(_dot(gl.astype(BF16), w_ref[...]) + b_ref[...])
    o_ref[...] = gl * gate


def s5_output(y, z, d_skip, glu_w, glu_b, *, tm):
    b, s, _ = z.shape
    yspec = pl.BlockSpec((None, tm, MIX), lambda bi, i: (bi, i, 0))
    vspec = pl.BlockSpec((1, MIX), lambda bi, i: (0, 0))
    return pl.pallas_call(
        _s5_out_kernel,
        grid=(b, s // tm),
        in_specs=[yspec,
                  pl.BlockSpec((None, tm, MIX), lambda bi, i: (bi, i, Z_D // MIX)),
                  vspec, pl.BlockSpec((MIX, MIX), lambda bi, i: (0, 0)), vspec],
        out_specs=yspec,
        out_shape=jax.ShapeDtypeStruct((b, s, MIX), F32),
        compiler_params=_cparams(("parallel", "parallel")),
        name="s5_output",
    )(y, z, d_skip.reshape(1, MIX), glu_w.astype(BF16), glu_b.reshape(1, MIX))


S5_T = 16
S5_KW = S5_T * D_GSIZE
S5_SW = 2 * D_STATE


S5_GPS = LANES // D_GSIZE


def _s5_chunk_kernel(u_ref, wm_ref, we_ref, wft_ref, ar_ref, ai_ref, y_ref, *, n_chunks, nb):
    rows = nb * n_chunks
    lane = lax.broadcasted_iota(jnp.int32, (n_chunks, LANES), 1)
    seg = [(lane >= D_GSIZE * j) & (lane < D_GSIZE * (j + 1)) for j in range(S5_GPS)]
    per_tile = LANES // D_GSIZE
    u_t = [[u_ref[b, pl.ds(t, n_chunks, stride=S5_T), :] for t in range(S5_T)] for b in range(nb)]
    row = lax.broadcasted_iota(jnp.int32, (rows, S5_SW), 0) & (n_chunks - 1)

    def lane_move(x, src, dst):
        shift = (D_GSIZE * (dst - src)) % LANES
        return pltpu.roll(x, shift, axis=1) if shift else x

    y_groups = []
    for gl in range(S5_GPS):
        packed = []
        for b in range(nb):
            tiles = []
            for half in range(S5_T // per_tile):
                acc = jnp.zeros((n_chunks, LANES), F32)
                for j in range(per_tile):
                    acc = jnp.where(seg[j], lane_move(u_t[b][half * per_tile + j], gl, j), acc)
                tiles.append(acc)
            packed.append(jnp.concatenate(tiles, axis=1))
        u = jnp.concatenate(packed, axis=0).astype(BF16)
        y = _dot(u, wm_ref[gl])
        xinc = _dot(u, we_ref[gl])

        def cmul(t, lvl, d, gl=gl):
            ar = ar_ref[gl, d, lvl:lvl + 1, :]
            ai = ai_ref[gl, d, lvl:lvl + 1, :]
            return t * ar + pltpu.roll(t, D_STATE, axis=1) * ai

        carries = []
        for d in range(2):
            x = xinc[:, d * S5_SW:(d + 1) * S5_SW]
            for lvl in range(n_chunks.bit_length() - 1):
                sh = 1 << lvl
                if d == 0:
                    prev = jnp.where(row >= sh, pltpu.roll(x, sh, axis=0), 0.0)
                else:
                    prev = jnp.where(row < n_chunks - sh, pltpu.roll(x, rows - sh, axis=0), 0.0)
                x = x + cmul(prev, lvl, d)
            if d == 0:
                carries.append(jnp.where(row >= 1, pltpu.roll(x, 1, axis=0), 0.0))
            else:
                carries.append(jnp.where(row < n_chunks - 1, pltpu.roll(x, rows - 1, axis=0), 0.0))
        cin = jnp.concatenate(carries, axis=1).astype(BF16)
        y_groups.append(y + _dot_t(cin, wft_ref[gl]))

    for b in range(nb):
        for t in range(S5_T):
            half, j = divmod(t, per_tile)
            out = jnp.zeros((n_chunks, LANES), F32)
            for gl in range(S5_GPS):
                tile = y_groups[gl][b * n_chunks:(b + 1) * n_chunks, half * LANES:(half + 1) * LANES]
                out = jnp.where(seg[gl], lane_move(tile, j, gl), out)
            y_ref[b, pl.ds(t, n_chunks, stride=S5_T), :] = out


def _cpow_table(ar, ai, count):
    res_r, res_i = [ar], [ai]
    for _ in range(count - 1):
        ar, ai = ar * ar - ai * ai, 2.0 * ar * ai
        res_r.append(ar)
        res_i.append(ai)
    return jnp.stack(res_r), jnp.stack(res_i)


def _s5_chunk_weights(a_re, a_im, log_dt, b_re, b_im, c_re, c_im, n_levels):
    g_n, t_n, c_n = D_GROUPS, S5_T, D_GSIZE
    taus = jnp.arange(t_n + 1, dtype=F32)[None, :, None]
    cr, ci = c_re.astype(F32)[:, None], c_im.astype(F32)[:, None]
    toe, ee, ff, lvl_r, lvl_i = [], [], [], [], []
    for direction in range(2):
        lr = jnp.minimum(a_re[direction].astype(F32), -1e-4)
        li = a_im[direction].astype(F32)
        dt = jnp.exp(log_dt[direction].astype(F32))[:, None]
        mag = jnp.exp(dt * lr)
        abr, abi = mag * jnp.cos(dt * li), mag * jnp.sin(dt * li)
        den = lr * lr + li * li
        qr, qi = lr / den, -li / den
        fr = (abr - 1.0) * qr - abi * qi
        fi = (abr - 1.0) * qi + abi * qr
        br, bi = b_re.astype(F32), b_im.astype(F32)
        btr = (fr[..., None] * br - fi[..., None] * bi).transpose(0, 2, 1)
        bti = (fr[..., None] * bi + fi[..., None] * br).transpose(0, 2, 1)
        pmag = jnp.exp(taus * (dt * lr)[:, None])
        ang = taus * (dt * li)[:, None]
        pr, pi = pmag * jnp.cos(ang), pmag * jnp.sin(ang)

        def c_times(order):
            por, poi = pr[:, order][:, :, None, :], pi[:, order][:, :, None, :]
            return cr * por - ci * poi, cr * poi + ci * por

        cpr, cpi = c_times(jnp.arange(t_n))
        k = (jnp.einsum('gin,gkn->gik', btr, cpr.reshape(g_n, t_n * c_n, D_STATE))
             - jnp.einsum('gin,gkn->gik', bti, cpi.reshape(g_n, t_n * c_n, D_STATE)))
        pad = (t_n - 1) * c_n
        if direction == 0:
            kp = jnp.pad(k, ((0, 0), (0, 0), (pad, 0)))
        else:
            krev = k.reshape(g_n, c_n, t_n, c_n)[:, :, ::-1, :].reshape(g_n, c_n, t_n * c_n)
            kp = jnp.pad(krev, ((0, 0), (0, 0), (0, pad)))
        toe.append(jnp.stack([kp[:, :, (t_n - 1 - s) * c_n:(t_n - 1 - s) * c_n + t_n * c_n]
                              for s in range(t_n)], axis=1))
        order = jnp.arange(t_n - 1, -1, -1) if direction == 0 else jnp.arange(t_n)
        por, poi = pr[:, order][:, :, None, :], pi[:, order][:, :, None, :]
        er = por * btr[:, None] - poi * bti[:, None]
        ei = por * bti[:, None] + poi * btr[:, None]
        ee.append(jnp.concatenate([er, ei], axis=3).reshape(g_n, S5_KW, S5_SW))
        order = jnp.arange(1, t_n + 1) if direction == 0 else jnp.arange(t_n, 0, -1)
        gr, gi = c_times(order)
        ff.append(jnp.concatenate([gr, -gi], axis=3).reshape(g_n, S5_KW, S5_SW))
        tr, ti = _cpow_table(pr[:, t_n], pi[:, t_n], n_levels)
        lvl_r.append(jnp.concatenate([tr, tr], axis=2))
        lvl_i.append(jnp.concatenate([-ti, ti], axis=2))
    wm = (toe[0] + toe[1]).reshape(g_n, S5_KW, S5_KW)
    we = jnp.concatenate(ee, axis=2)
    wft = jnp.concatenate(ff, axis=2)
    ar = jnp.stack(lvl_r).transpose(2, 0, 1, 3)
    ai = jnp.stack(lvl_i).transpose(2, 0, 1, 3)
    return wm.astype(BF16), we.astype(BF16), wft.astype(BF16), ar, ai


def s5_chunked(z, a_re, a_im, log_dt, b_re, b_im, c_re, c_im):
    b, s, _ = z.shape
    n_chunks = s // S5_T
    n_levels = n_chunks.bit_length() - 1
    assert n_chunks == 1 << n_levels
    wm, we, wft, ar, ai = _s5_chunk_weights(a_re, a_im, log_dt, b_re, b_im, c_re, c_im, max(n_levels, 1))
    groups = lambda shape: pl.BlockSpec((S5_GPS,) + shape, lambda k: (k,) + (0,) * len(shape))
    return pl.pallas_call(
        functools.partial(_s5_chunk_kernel, n_chunks=n_chunks, nb=b),
        grid=(D_GROUPS // S5_GPS,),
        in_specs=[pl.BlockSpec((b, s, LANES), lambda k: (0, 0, Z_D // LANES + k)),
                  groups((S5_KW, S5_KW)), groups((S5_KW, 2 * S5_SW)), groups((S5_KW, 2 * S5_SW)),
                  groups((2, ar.shape[2], S5_SW)), groups((2, ar.shape[2], S5_SW))],
        out_specs=pl.BlockSpec((b, s, LANES), lambda k: (0, 0, k)),
        out_shape=jax.ShapeDtypeStruct((b, s, MIX), F32),
        compiler_params=_cparams(("parallel",)),
        name="s5_chunked",
    )(z, wm, we, wft, ar, ai)


def s5_bidir(z, a_re, a_im, log_dt, b_re, b_im, c_re, c_im, d_skip, glu_w, glu_b, *, tm=1024):
    y = s5_chunked(z, a_re, a_im, log_dt, b_re, b_im, c_re, c_im)
    return s5_output(y, z, d_skip, glu_w, glu_b, tm=tm)


def _softplus(y):
    return jnp.maximum(y, 0.0) + jnp.log(1.0 + jnp.exp(-jnp.abs(y)))


def _head_ones(width):
    seg = np.arange(width) // B_HDIM
    return jnp.asarray(seg[:, None] == seg[None, :], BF16)


def _rwkv_pre_kernel(z_ref, zp_ref, zn_ref, mu_ref, w0_ref, a0_ref, w2_ref, a2_ref, g2_ref,
                     kk_ref, ka_ref, rk_ref, ones_ref,
                     r_o, a_o, w0_o, w1_o, k0_o, k1_o, b0_o, b1_o, v_o, g_o, bonus_o,
                     *, tm, n_tiles):
    i = pl.program_id(1)
    z = z_ref[...]
    row = lax.broadcasted_iota(jnp.int32, z.shape, 0)
    prev_row = jnp.where(i > 0, zp_ref[7:8, :], 0.0)
    next_row = jnp.where(i < n_tiles - 1, zn_ref[0:1, :], 0.0)
    zp = jnp.where(row == 0, prev_row, pltpu.roll(z, 1, axis=0))
    zn = jnp.where(row == tm - 1, next_row, pltpu.roll(z, tm - 1, axis=0))
    xs = z + mu_ref[0:1, :] * (zp - z) + mu_ref[1:2, :] * (zn - z)
    r = xs[:, 0:MIX]
    k = xs[:, MIX:2 * MIX]
    v = xs[:, 2 * MIX:3 * MIX]
    c0 = 3 * MIX
    wd = xs[:, c0:c0 + 2 * B_LORA]
    ad = xs[:, c0 + 2 * B_LORA:c0 + 4 * B_LORA]
    gd = xs[:, c0 + 4 * B_LORA:c0 + 4 * B_LORA + B_GATE_LORA]
    lw = _dot(jnp.tanh(wd).astype(BF16), w2_ref[...])
    la = _dot(ad.astype(BF16), a2_ref[...])
    g_o[...] = _dot(jax.nn.sigmoid(gd).astype(BF16), g2_ref[...])
    ones = ones_ref[...]
    kk = k * kk_ref[...]
    kk = kk * lax.rsqrt(_segsum(kk * kk, ones) + 1e-12)
    ka = ka_ref[...]
    ksum = jnp.zeros_like(k)
    for d, (w_o, k_o, b_o) in enumerate(((w0_o, k0_o, b0_o), (w1_o, k1_o, b1_o))):
        cols = slice(d * MIX, (d + 1) * MIX)
        logw = -_softplus(-(w0_ref[d:d + 1, :] + lw[:, cols])) - 0.5
        w_o[...] = -jnp.exp(logw)
        a = jax.nn.sigmoid(a0_ref[d:d + 1, :] + la[:, cols])
        kmod = k * (1.0 + (a - 1.0) * ka)
        k_o[...] = kmod
        b_o[...] = kk * a
        ksum = ksum + kmod
    r_o[...] = r
    a_o[...] = -kk
    bonus_o[...] = _segsum(r * (0.5 * ksum) * rk_ref[...], ones) * v
    v_o[...] = v


def _block_diag2(m):
    z = jnp.zeros_like(m[0])
    return jnp.concatenate([jnp.concatenate([m[0], z], axis=1),
                            jnp.concatenate([z, m[1]], axis=1)], axis=0)


def rwkv_pre(z, mu, w0, w2, a0, a2, g2, k_k, k_a, r_k, *, tm):
    b, s, _ = z.shape
    n_tiles = s // tm
    mu_p = jnp.pad(mu.astype(F32), ((0, 0), (0, IN_B_PAD - IN_B)))
    row_spec = pl.BlockSpec((None, tm, MIX), lambda bi, i: (bi, i, 0))
    vec = lambda n, w: pl.BlockSpec((n, w), lambda bi, i: (0, 0))
    rows = jax.ShapeDtypeStruct((b, s, MIX), F32)
    hb = tm // 8
    return pl.pallas_call(
        functools.partial(_rwkv_pre_kernel, tm=tm, n_tiles=n_tiles),
        grid=(b, n_tiles),
        in_specs=[pl.BlockSpec((None, tm, IN_B_PAD), lambda bi, i: (bi, i, 0)),
                  pl.BlockSpec((None, 8, IN_B_PAD), lambda bi, i: (bi, jnp.maximum(i * hb - 1, 0), 0)),
                  pl.BlockSpec((None, 8, IN_B_PAD), lambda bi, i: (bi, jnp.minimum((i + 1) * hb, s // 8 - 1), 0)),
                  vec(2, IN_B_PAD), vec(2, MIX), vec(2, MIX),
                  vec(2 * B_LORA, 2 * MIX), vec(2 * B_LORA, 2 * MIX), vec(B_GATE_LORA, MIX),
                  vec(1, MIX), vec(1, MIX), vec(1, MIX), vec(MIX, MIX)],
        out_specs=[row_spec] * 11,
        out_shape=[rows] * 11,
        compiler_params=_cparams(("parallel", "parallel")),
        name="rwkv_pre",
    )(z, z, z, mu_p, w0.astype(F32), a0.astype(F32),
      _block_diag2(w2).astype(BF16), _block_diag2(a2).astype(BF16), g2.astype(BF16),
      k_k.reshape(1, MIX), k_a.reshape(1, MIX), r_k.reshape(1, MIX), _head_ones(MIX))


def _rwkv_scan_kernel(af_ref, rf_ref, vf_ref, lwf_ref, kf_ref, bf_ref,
                      ab_ref, rb_ref, vb_ref, lwb_ref, kb_ref, bb_ref,
                      trif_ref, trib_ref, yf_ref, yb_ref, h_ref, *, tc, nb):
    @pl.when(pl.program_id(0) == 0)
    def _():
        h_ref[...] = jnp.zeros_like(h_ref)

    row = lax.broadcasted_iota(jnp.int32, (tc, tc), 0)
    col = lax.broadcasted_iota(jnp.int32, (tc, tc), 1)
    eye = (row == col).astype(F32)
    row2 = lax.broadcasted_iota(jnp.int32, (tc, 2 * tc), 0)
    col2 = lax.broadcasted_iota(jnp.int32, (tc, 2 * tc), 1) & (tc - 1)
    zeros_tv = jnp.zeros((tc, B_HDIM), BF16)
    n_sq = tc.bit_length() - 1
    tdot = lambda x, y: lax.dot_general(x, y, (((0,), (0,)), ((), ())), preferred_element_type=F32)

    seqs = []
    for reverse, (a_ref, r_ref, v_ref, lw_ref, k_ref, b_ref, tri_ref, y_ref) in (
            (False, (af_ref, rf_ref, vf_ref, lwf_ref, kf_ref, bf_ref, trif_ref, yf_ref)),
            (True, (ab_ref, rb_ref, vb_ref, lwb_ref, kb_ref, bb_ref, trib_ref, yb_ref))):
        tri = tri_ref[...]
        before = (col2 > row2) if reverse else (col2 < row2)
        upto = (col2 >= row2) if reverse else (col2 <= row2)
        last = 0 if reverse else tc - 1
        for bi in range(nb):
            lw = lw_ref[bi]
            hi, mid, lo = _split3(lw)
            cl = _dot(tri, hi) + _dot(tri, mid) + _dot(tri, lo)
            tot = cl[last:last + 1, :]
            einv = jnp.exp(-cl)
            etot = jnp.exp(tot - cl)
            b_all = b_ref[bi]
            k_all = k_ref[bi]
            seqs.append(dict(
                a_t=a_ref[bi] * jnp.exp(cl - lw), r_t=r_ref[bi] * jnp.exp(cl),
                b_t=b_all * einv, k_t=k_all * einv, b_h=b_all * etot, k_h=k_all * etot,
                g_tot=jnp.exp(tot), v=v_ref[bi], before=before, upto=upto, y_ref=y_ref, bi=bi))

    inst = [(sq, slice(hd * B_HDIM, (hd + 1) * B_HDIM)) for sq in seqs for hd in range(B_HEADS)]
    ids = range(len(inst))
    at = [sq['a_t'][:, s].astype(BF16) for sq, s in inst]
    rt = [sq['r_t'][:, s].astype(BF16) for sq, s in inst]
    bk = [jnp.concatenate([sq['b_t'][:, s], sq['k_t'][:, s]], axis=0).astype(BF16) for sq, s in inst]
    bkh = [jnp.concatenate([sq['b_h'][:, s], sq['k_h'][:, s]], axis=0).astype(BF16) for sq, s in inst]
    vv = [sq['v'][:, s].astype(BF16) for sq, s in inst]
    g = [_dot_t(jnp.concatenate([at[n], rt[n]], axis=0), bk[n]) for n in ids]
    ga = [jnp.where(inst[n][0]['before'], g[n][:tc], 0.0) for n in ids]
    gr = [jnp.where(inst[n][0]['upto'], g[n][tc:], 0.0).astype(BF16) for n in ids]
    lkv = [_dot(ga[n].astype(BF16), jnp.concatenate([zeros_tv, vv[n]], axis=0)) for n in ids]
    lb = [ga[n][:, :tc].astype(BF16) for n in ids]
    pinv = [eye + ga[n][:, :tc] for n in ids]
    lb = [_dot(x, x).astype(BF16) for x in lb]
    for _ in range(1, n_sq - 1):
        prod = [_dot(jnp.concatenate([pinv[n].astype(BF16), lb[n]], axis=0), lb[n]) for n in ids]
        pinv = [pinv[n] + prod[n][:tc] for n in ids]
        lb = [prod[n][tc:].astype(BF16) for n in ids]
    pinv = [(pinv[n] + _dot(pinv[n].astype(BF16), lb[n])).astype(BF16) for n in ids]
    ah = [_dot(pinv[n], at[n]).astype(BF16) for n in ids]
    u0 = [_dot(pinv[n], lkv[n].astype(BF16)).astype(BF16) for n in ids]
    rhs = [jnp.concatenate([jnp.concatenate([ah[n], u0[n]], axis=1),
                            jnp.concatenate([zeros_tv, vv[n]], axis=1)], axis=0) for n in ids]
    ry = [_dot(gr[n], rhs[n]) for n in ids]
    ph = [tdot(bkh[n], rhs[n]) for n in ids]
    rh = [inst[n][0]['r_t'][:, inst[n][1]] + ry[n][:, :B_HDIM] for n in ids]
    phi = [eye * inst[n][0]['g_tot'][:, inst[n][1]] + ph[n][:, :B_HDIM] for n in ids]
    h0 = [h_ref[n].astype(BF16) for n in ids]
    fin = [_dot(jnp.concatenate([rh[n], phi[n]], axis=0).astype(BF16), h0[n]) for n in ids]
    yo = [fin[n][:tc] + ry[n][:, B_HDIM:] for n in ids]
    for si, sq in enumerate(seqs):
        sq['y_ref'][sq['bi']] = jnp.concatenate(yo[si * B_HEADS:(si + 1) * B_HEADS], axis=1)
    for n in ids:
        h_ref[n] = fin[n][tc:] + ph[n][:, B_HDIM:]


def rwkv_scan(a, r, v, lwf, kf, bvf, lwb, kb, bvb, *, tc=64):
    nb, s, _ = a.shape
    nch = s // tc
    fwd = pl.BlockSpec((nb, tc, MIX), lambda c: (0, c, 0))
    bwd = pl.BlockSpec((nb, tc, MIX), lambda c: (0, nch - 1 - c, 0))
    tspec = pl.BlockSpec((tc, tc), lambda c: (0, 0))
    t_idx = np.arange(tc)
    tri_f = jnp.asarray(t_idx[None, :] <= t_idx[:, None], BF16)
    tri_b = jnp.asarray(t_idx[None, :] >= t_idx[:, None], BF16)
    out = jax.ShapeDtypeStruct((nb, s, MIX), F32)
    return pl.pallas_call(
        functools.partial(_rwkv_scan_kernel, tc=tc, nb=nb),
        grid=(nch,),
        in_specs=[fwd] * 6 + [bwd] * 6 + [tspec, tspec],
        out_specs=[fwd, bwd],
        out_shape=[out, out],
        scratch_shapes=[pltpu.VMEM((2 * nb * B_HEADS, B_HDIM, B_HDIM), F32)],
        compiler_params=_cparams(("arbitrary",)),
        name="rwkv_scan",
    )(a, r, v, lwf, kf, bvf, a, r, v, lwb, kb, bvb, tri_f, tri_b)


def _rwkv_post_kernel(yf_ref, yb_ref, bonus_ref, g_ref, lw_ref, lb_ref, ones_ref, o_ref):
    y = yf_ref[...] + yb_ref[...]
    ones = ones_ref[...]
    mean = _segsum(y, ones) * (1.0 / B_HDIM)
    yc = y - mean
    var = _segsum(yc * yc, ones) * (1.0 / B_HDIM)
    yn = yc * lax.rsqrt(var + LN_X_EPS) * lw_ref[...] + lb_ref[...]
    o_ref[...] = (yn + bonus_ref[...]) * g_ref[...]


def rwkv_post(yf, yb, bonus, g, lnx_w, lnx_b, *, tm):
    b, s, _ = yf.shape
    row_spec = pl.BlockSpec((None, tm, MIX), lambda bi, i: (bi, i, 0))
    vec = pl.BlockSpec((1, MIX), lambda bi, i: (0, 0))
    return pl.pallas_call(
        _rwkv_post_kernel,
        grid=(b, s // tm),
        in_specs=[row_spec, row_spec, row_spec, row_spec, vec, vec,
                  pl.BlockSpec((MIX, MIX), lambda bi, i: (0, 0))],
        out_specs=row_spec,
        out_shape=jax.ShapeDtypeStruct((b, s, MIX), F32),
        compiler_params=_cparams(("parallel", "parallel")),
        name="rwkv_post",
    )(yf, yb, bonus, g, lnx_w.reshape(1, MIX), lnx_b.reshape(1, MIX), _head_ones(MIX))


def rwkv7_bidir(z, mu, w0, w2, a0, a2, g2, k_k, k_a, r_k, lnx_w, lnx_b, *, tm=512, tc=64):
    r, a, wf, wb, kf, kb, bf, bb, v, g, bonus = rwkv_pre(z, mu, w0, w2, a0, a2, g2, k_k, k_a, r_k, tm=tm)
    yf, yb = rwkv_scan(a, r, v, wf, kf, bf, wb, kb, bb, tc=tc)
    return rwkv_post(yf, yb, bonus, g, lnx_w, lnx_b, tm=tm)


def _mix_out_kernel(h_ref, oa_ref, ob_ref, oc_ref, od_ref, gc_ref, gd_ref, w_ref, o_ref, mix_ref):
    @pl.when(pl.program_id(1) == 0)
    def _():
        mix_ref[...] = jnp.concatenate(
            [oa_ref[...], ob_ref[...], _rms(oc_ref[...], gc_ref[...]), _rms(od_ref[...], gd_ref[...])],
            axis=1).astype(BF16)

    o_ref[...] = h_ref[...] + _dot(mix_ref[...], w_ref[...])


def mix_out(h, oa, ob, oc, od, gc, gd, w_out, *, tm, tn):
    m, d = h.shape
    mspec = pl.BlockSpec((tm, MIX), lambda i, j: (i, 0))
    vspec = pl.BlockSpec((1, MIX), lambda i, j: (0, 0))
    hspec = pl.BlockSpec((tm, tn), lambda i, j: (i, j))
    return pl.pallas_call(
        _mix_out_kernel,
        grid=(m // tm, d // tn),
        in_specs=[hspec, mspec, mspec, mspec, mspec, vspec, vspec,
                  pl.BlockSpec((4 * MIX, tn), lambda i, j: (0, j))],
        out_specs=hspec,
        out_shape=jax.ShapeDtypeStruct((m, d), F32),
        scratch_shapes=[pltpu.VMEM((tm, 4 * MIX), BF16)],
        compiler_params=_cparams(("parallel", "arbitrary")),
        name="mix_out",
    )(h, oa, ob, oc, od, gc.reshape(1, MIX), gd.reshape(1, MIX), w_out)


def _cross_attn_kernel(h_ref, g_ref, wq_ref, kv_ref, wo_ref, o_ref):
    h = h_ref[...]
    q = _dot(_rms(h, g_ref[...]).astype(BF16), wq_ref[...]).astype(BF16)
    outs = []
    for hd in range(XA_HEADS):
        cols = slice(hd * XA_HDIM, (hd + 1) * XA_HDIM)
        kh = kv_ref[:, cols]
        vh = kv_ref[:, D_MODEL + hd * XA_HDIM:D_MODEL + (hd + 1) * XA_HDIM]
        s = _dot_t(q[:, cols], kh) * (XA_HDIM ** -0.5)
        e = jnp.exp(s - jnp.max(s, axis=-1, keepdims=True))
        p = e / jnp.sum(e, axis=-1, keepdims=True)
        outs.append(_dot(p.astype(BF16), vh).astype(BF16))
    o = jnp.concatenate(outs, axis=1)
    o_ref[...] = h + _dot(o, wo_ref[...])


def cross_attention(h, g, wq, kv, wo, *, tq):
    b, s, d = h.shape
    n_mem = kv.shape[1]
    hspec = pl.BlockSpec((None, tq, d), lambda bi, i: (bi, i, 0))
    wspec = pl.BlockSpec((d, d), lambda bi, i: (0, 0), pipeline_mode=pl.Buffered(1))
    return pl.pallas_call(
        _cross_attn_kernel,
        grid=(b, s // tq),
        in_specs=[hspec, pl.BlockSpec((1, d), lambda bi, i: (0, 0)), wspec,
                  pl.BlockSpec((None, n_mem, 2 * d), lambda bi, i: (bi, 0, 0)), wspec],
        out_specs=hspec,
        out_shape=jax.ShapeDtypeStruct((b, s, d), F32),
        compiler_params=_cparams(("parallel", "parallel")),
        name="cross_attention",
    )(h, g.reshape(1, d), wq, kv, wo)


ROUTER_W = LANES
MOE_TB = 256


def _router_kernel(h_ref, g_ref, wr_ref, lt_ref, xn_ref, eid_ref, gate_ref, rank_ref, cnt_ref, run_ref):
    @pl.when(pl.program_id(0) == 0)
    def _():
        run_ref[...] = jnp.zeros_like(run_ref)

    xn = _rms(h_ref[...], g_ref[...])
    xn_ref[...] = xn
    logits = jnp.dot(xn, wr_ref[...], preferred_element_type=F32, precision=lax.Precision.HIGHEST)
    lane = lax.broadcasted_iota(jnp.int32, logits.shape, 1)
    big = jnp.int32(ROUTER_W)

    def first_max(mask):
        m = jnp.max(jnp.where(mask, logits, NEG_INF), axis=-1, keepdims=True)
        idx = jnp.min(jnp.where(mask & (logits == m), lane, big), axis=-1, keepdims=True)
        return m, idx

    gmask = lane < N_GROUPS
    gmax, gidx = first_max(gmask)
    g_w = 1.0 / jnp.sum(jnp.where(gmask, jnp.exp(logits - gmax), 0.0), axis=-1, keepdims=True)
    e0 = N_GROUPS + gidx * EXPERTS_PER_GROUP
    emask = (lane >= e0) & (lane < e0 + EXPERTS_PER_GROUP)
    m1, i1 = first_max(emask)
    m2, i2 = first_max(emask & (lane != i1))
    e2 = jnp.exp(m2 - m1)
    w1 = 1.0 / (1.0 + e2)
    w2 = e2 / (1.0 + e2)
    e1 = i1 - N_GROUPS
    e2i = i2 - N_GROUPS
    eid_ref[...] = jnp.where(lane == 0, e1, jnp.where(lane == 1, e2i, 0))
    gate_ref[...] = jnp.where(lane == 0, g_w * w1, jnp.where(lane == 1, g_w * w2, 0.0))
    hit1 = lane == e1
    hit2 = lane == e2i
    onehot = jnp.where(hit1 | hit2, 1.0, 0.0)
    earlier = run_ref[...] + _dot(lt_ref[...], onehot.astype(BF16))
    r1 = jnp.sum(jnp.where(hit1, earlier, 0.0), axis=-1, keepdims=True)
    r2 = jnp.sum(jnp.where(hit2, earlier, 0.0), axis=-1, keepdims=True)
    rank_ref[...] = jnp.where(lane == 0, r1, jnp.where(lane == 1, r2, 0.0)).astype(jnp.int32)
    run_ref[...] = run_ref[...] + jnp.sum(onehot, axis=0, keepdims=True)
    cnt_ref[...] = run_ref[...].astype(jnp.int32)


def moe_router(h, g, wr_group, wr_expert, *, tm):
    m, d = h.shape
    wr = jnp.concatenate([wr_group, wr_expert], axis=1).astype(F32)
    wr = jnp.pad(wr, ((0, 0), (0, ROUTER_W - wr.shape[1])))
    t_idx = np.arange(tm)
    lower = jnp.asarray(t_idx[None, :] < t_idx[:, None], BF16)
    hspec = pl.BlockSpec((tm, d), lambda i: (i, 0))
    lspec = pl.BlockSpec((tm, ROUTER_W), lambda i: (i, 0))
    cspec = pl.BlockSpec((1, ROUTER_W), lambda i: (0, 0))
    lane_i = jax.ShapeDtypeStruct((m, ROUTER_W), jnp.int32)
    return pl.pallas_call(
        _router_kernel,
        grid=(m // tm,),
        in_specs=[hspec, pl.BlockSpec((1, d), lambda i: (0, 0)),
                  pl.BlockSpec((d, ROUTER_W), lambda i: (0, 0)),
                  pl.BlockSpec((tm, tm), lambda i: (0, 0))],
        out_specs=[hspec, lspec, lspec, lspec, cspec],
        out_shape=[jax.ShapeDtypeStruct((m, d), F32), lane_i,
                   jax.ShapeDtypeStruct((m, ROUTER_W), F32), lane_i,
                   jax.ShapeDtypeStruct((1, ROUTER_W), jnp.int32)],
        scratch_shapes=[pltpu.VMEM((1, ROUTER_W), F32)],
        compiler_params=_cparams(("arbitrary",)),
        name="moe_router",
    )(h, g.reshape(1, d), wr, lower)


GATHER_UNROLL = 8


def _start_row_gather(row_of, src_hbm, dst, sem, n_rows, dst_row0=0):
    def issue(r, c):
        pltpu.make_async_copy(src_hbm.at[pl.ds(row_of(r), 1)], dst.at[pl.ds(dst_row0 + r, 1)], sem).start()
        return c

    lax.fori_loop(0, n_rows, issue, 0, unroll=GATHER_UNROLL)


def _wait_row_gather(src_hbm, dst, sem, n_rows):
    pltpu.make_async_copy(src_hbm.at[pl.ds(0, n_rows)], dst, sem).wait()


def _moe_expert_kernel(pos_ref, exp_ref, nvb_ref, nxt_ref, run_ref, x_hbm, w1_hbm, w3_hbm, w2_hbm, o_ref,
                       slot_ref, xbuf, sem, w1buf, w3buf, w2buf, wsem, *, tb, n_assign, layer):
    i = pl.program_id(0)
    nvb = nvb_ref[0]

    def weight_copies(e, s):
        return (pltpu.make_async_copy(w1_hbm.at[layer, e], w1buf.at[s], wsem.at[s, 0]),
                pltpu.make_async_copy(w3_hbm.at[layer, e], w3buf.at[s], wsem.at[s, 1]),
                pltpu.make_async_copy(w2_hbm.at[layer, e], w2buf.at[s], wsem.at[s, 2]))

    e_cur = exp_ref[i]
    first = (i == 0) | (exp_ref[jnp.maximum(i - 1, 0)] != e_cur)
    wslot = run_ref[e_cur] & 1

    @pl.when(i == 0)
    def _():
        for cp in weight_copies(e_cur, wslot):
            cp.start()

    @pl.when(i == 0)
    def _():
        def clear(s, c):
            slot_ref[s] = 0
            return c

        lax.fori_loop(0, slot_ref.shape[0], clear, 0, unroll=GATHER_UNROLL)

        def place(a, c):
            slot_ref[pos_ref[a]] = lax.shift_right_logical(a, TOP_K.bit_length() - 1)
            return c

        lax.fori_loop(0, n_assign, place, 0, unroll=GATHER_UNROLL)
        _start_row_gather(lambda r: slot_ref[r], x_hbm, xbuf.at[0], sem.at[0], tb)

    @pl.when(i + 1 < nvb)
    def _():
        nxt = (i + 1) & 1
        base = (i + 1) * tb
        _start_row_gather(lambda r: slot_ref[base + r], x_hbm, xbuf.at[nxt], sem.at[nxt], tb)

    @pl.when((i < nvb) & first & (nxt_ref[e_cur] >= 0))
    def _():
        for cp in weight_copies(nxt_ref[e_cur], 1 - wslot):
            cp.start()

    @pl.when(i < nvb)
    def _():
        @pl.when(first)
        def _():
            for cp in weight_copies(e_cur, wslot):
                cp.wait()

        cur = i & 1
        _wait_row_gather(x_hbm, xbuf.at[cur], sem.at[cur], tb)
        x = xbuf[cur].astype(BF16)
        h1 = _dot(x, w1buf[wslot].astype(BF16))
        h3 = _dot(x, w3buf[wslot].astype(BF16))
        act = (jax.nn.silu(h1) * h3).astype(BF16)
        o_ref[...] = _dot(act, w2buf[wslot].astype(BF16))

    @pl.when(i >= nvb)
    def _():
        o_ref[...] = jnp.zeros_like(o_ref)


def moe_experts(xn, pos, blk_exp, n_valid, nxt_exp, run_idx, w1, w3, w2, layer, *, tb):
    n_assign = pos.shape[0]
    cap = n_assign + N_EXPERTS * tb
    d = xn.shape[1]
    nblk = cap // tb
    hbm = pl.BlockSpec(memory_space=pl.ANY)
    grid_spec = pltpu.PrefetchScalarGridSpec(
        num_scalar_prefetch=5,
        grid=(nblk,),
        in_specs=[hbm, hbm, hbm, hbm],
        out_specs=pl.BlockSpec((tb, d), lambda i, *_: (i, 0)),
        scratch_shapes=[pltpu.SMEM((cap,), jnp.int32), pltpu.VMEM((2, tb, d), F32),
                        pltpu.SemaphoreType.DMA((2,)),
                        pltpu.VMEM((2, d, D_EXPERT), F32), pltpu.VMEM((2, d, D_EXPERT), F32),
                        pltpu.VMEM((2, D_EXPERT, d), F32), pltpu.SemaphoreType.DMA((2, 3))],
    )
    return pl.pallas_call(
        functools.partial(_moe_expert_kernel, tb=tb, n_assign=n_assign, layer=layer),
        grid_spec=grid_spec,
        out_shape=jax.ShapeDtypeStruct((cap, d), F32),
        compiler_params=_cparams(("arbitrary",)),
        name="moe_experts",
    )(pos, blk_exp, n_valid, nxt_exp, run_idx, xn, w1, w3, w2)


def _moe_combine_kernel(pos_ref, ys_hbm, h_ref, gate_ref, g_ref, o_ref, buf, sem, *, tm, n_tiles, normalize):
    i = pl.program_id(0)

    def start(tile, slot):
        for choice in range(TOP_K):
            def row_of(r, choice=choice):
                return pos_ref[(tile * tm + r) * TOP_K + choice]

            _start_row_gather(row_of, ys_hbm, buf.at[slot], sem.at[slot], tm, dst_row0=choice * tm)

    @pl.when(i == 0)
    def _():
        start(0, 0)

    @pl.when(i + 1 < n_tiles)
    def _():
        start(i + 1, (i + 1) & 1)

    cur = i & 1
    _wait_row_gather(ys_hbm, buf.at[cur], sem.at[cur], TOP_K * tm)
    gate = gate_ref[...]
    out = h_ref[...] + gate[:, 0:1] * buf[cur, 0:tm, :] + gate[:, 1:2] * buf[cur, tm:2 * tm, :]
    o_ref[...] = _rms(out, g_ref[...]) if normalize else out


def moe_combine(h, ys, gate, pos, out_gain, *, tm, normalize):
    m, d = h.shape
    hspec = pl.BlockSpec((tm, d), lambda i, p: (i, 0))
    grid_spec = pltpu.PrefetchScalarGridSpec(
        num_scalar_prefetch=1,
        grid=(m // tm,),
        in_specs=[pl.BlockSpec(memory_space=pl.ANY), hspec,
                  pl.BlockSpec((tm, ROUTER_W), lambda i, p: (i, 0)),
                  pl.BlockSpec((1, d), lambda i, p: (0, 0))],
        out_specs=hspec,
        scratch_shapes=[pltpu.VMEM((2, TOP_K * tm, d), F32), pltpu.SemaphoreType.DMA((2,))],
    )
    return pl.pallas_call(
        functools.partial(_moe_combine_kernel, tm=tm, n_tiles=m // tm, normalize=normalize),
        grid_spec=grid_spec,
        out_shape=jax.ShapeDtypeStruct((m, d), F32),
        compiler_params=_cparams(("arbitrary",)),
        name="moe_combine",
    )(pos, ys, h, gate, out_gain.reshape(1, d))


def _moe_blocks(counts, n_assign, tb):
    padded = ((counts + tb - 1) // tb) * tb
    pad_end = jnp.cumsum(padded)
    pad_start = (pad_end - padded).astype(jnp.int32)
    nblk = (n_assign + N_EXPERTS * tb) // tb
    n_valid = (pad_end[-1] // tb).astype(jnp.int32)
    blk = jnp.minimum(jnp.arange(nblk, dtype=jnp.int32), n_valid - 1) * tb
    blk_exp = jnp.sum(blk[:, None] >= pad_end[None, :], axis=1).astype(jnp.int32)
    used = counts > 0
    run_idx = (jnp.cumsum(used) - 1).astype(jnp.int32)
    ids = jnp.where(used, jnp.arange(N_EXPERTS, dtype=jnp.int32), N_EXPERTS)
    later = lax.cummin(ids[::-1])[::-1]
    nxt = jnp.concatenate([later[1:], jnp.full((1,), N_EXPERTS, jnp.int32)])
    nxt_exp = jnp.where(nxt < N_EXPERTS, nxt, -1).astype(jnp.int32)
    return pad_start, jnp.minimum(blk_exp, N_EXPERTS - 1), n_valid.reshape(1), nxt_exp, run_idx


def _moe_positions_kernel(eid_ref, rank_ref, pstart_ref, pos_ref):
    eid = eid_ref[...]
    lane = lax.broadcasted_iota(jnp.int32, eid.shape, 1)
    pstart = pstart_ref[...]
    first = [jnp.sum(jnp.where(lane == eid[:, c:c + 1], pstart, 0), axis=-1, keepdims=True)
             for c in range(TOP_K)]
    pos_ref[...] = rank_ref[...] + jnp.where(lane == 0, first[0], jnp.where(lane == 1, first[1], 0))


def moe_positions(eid, rank, pad_start, *, tm):
    m = eid.shape[0]
    spec = pl.BlockSpec((tm, ROUTER_W), lambda i: (i, 0))
    pstart = jnp.pad(pad_start, (0, ROUTER_W - pad_start.shape[0])).reshape(1, ROUTER_W)
    return pl.pallas_call(
        _moe_positions_kernel,
        grid=(m // tm,),
        in_specs=[spec, spec, pl.BlockSpec((1, ROUTER_W), lambda i: (0, 0))],
        out_specs=spec,
        out_shape=jax.ShapeDtypeStruct((m, ROUTER_W), jnp.int32),
        compiler_params=_cparams(("parallel",)),
        name="moe_positions",
    )(eid, rank, pstart)


def hier_moe(h, g, wr_group, wr_expert, w1, w3, w2, layer, out_gain, normalize_out,
             *, tm_router=1024, tm_combine=128, tb=MOE_TB):
    m, d = h.shape
    xn, eid, gate, rank, counts = moe_router(h, g, wr_group, wr_expert, tm=tm_router)
    pad_start, blk_exp, n_valid, nxt_exp, run_idx = _moe_blocks(counts[0, :N_EXPERTS], m * TOP_K, tb)
    pos = moe_positions(eid, rank, pad_start, tm=1024)[:, :TOP_K].reshape(-1)
    ys = moe_experts(xn, pos, blk_exp, n_valid, nxt_exp, run_idx, w1, w3, w2, layer, tb=tb)
    return moe_combine(h, ys, gate, pos, out_gain, tm=tm_combine, normalize=normalize_out)


def _w_in_layout(w_in):
    wa = w_in[:, :IN_A]
    wb = w_in[:, IN_A:IN_A + IN_B]
    wc = w_in[:, IN_A + IN_B:IN_A + IN_B + IN_C]
    wd = w_in[:, IN_A + IN_B + IN_C:]
    wb = jnp.pad(wb, ((0, 0), (0, IN_B_PAD - IN_B)))
    return jnp.concatenate([wb, wa, wc, wd], axis=1).astype(BF16)


def kernel(x, mem, norm_mix, w_in, w_out, diff_lambda, diff_subln, rwkv_mu, rwkv_w0, rwkv_w2,
           rwkv_a0, rwkv_a2, rwkv_g2, rwkv_kk, rwkv_ka, rwkv_rk, rwkv_lnx_w, rwkv_lnx_b,
           s5_a_re, s5_a_im, s5_log_dt, s5_b_re, s5_b_im, s5_c_re, s5_c_im, s5_d, s5_glu_w,
           s5_glu_b, mix_out_norm, norm_cross, norm_mem, xa_wq, xa_wkv, xa_wo, norm_moe,
           router_group, router_expert, moe_w1, moe_w3, moe_w2, norm_final):
    b, s, d = x.shape
    m = b * s
    n_mem = mem.shape[1]
    depth = w_in.shape[0]
    h = x.reshape(m, d)
    mem2 = mem.reshape(b * n_mem, d)
    rope_a = _rope_tables(s, A_QKDIM, MIX)
    rope_c = _rope_tables(s, C_HDIM, MIX)
    for l in range(depth):
        z = norm_matmul(h, norm_mix[l], _w_in_layout(w_in[l]), tm=1024, tn=Z_W // 4, out_dtype=F32)
        z = z.reshape(b, s, Z_W)
        qa, ka, v1a = rope_qkv(z, Z_A, A_QKDIM, tm=1024, ones_cols=A_VDIM, tables=rope_a)
        oa = diff_attention(qa, ka, v1a, diff_lambda[l], diff_subln[l], l, tq=512)
        ob = rwkv7_bidir(z, rwkv_mu[l], rwkv_w0[l], rwkv_w2[l], rwkv_a0[l], rwkv_a2[l], rwkv_g2[l],
                         rwkv_kk[l], rwkv_ka[l], rwkv_rk[l], rwkv_lnx_w[l], rwkv_lnx_b[l])
        qc, kc, vc = rope_qkv(z, Z_C, C_HDIM, tm=1024, tables=rope_c)
        oc = dilated_attention(qc, kc, vc, tq=256)
        od = s5_bidir(z, s5_a_re[l], s5_a_im[l], s5_log_dt[l], s5_b_re[l], s5_b_im[l],
                      s5_c_re[l], s5_c_im[l], s5_d[l], s5_glu_w[l], s5_glu_b[l])
        h = mix_out(h, oa.reshape(m, MIX), ob.reshape(m, MIX), oc.reshape(m, MIX), od.reshape(m, MIX),
                    mix_out_norm[l, 0], mix_out_norm[l, 1], w_out[l].astype(BF16), tm=1024, tn=1024)
        kv = norm_matmul(mem2, norm_mem[l], xa_wkv, tm=b * n_mem, tn=512, out_dtype=BF16, layer=l)
        h = cross_attention(h.reshape(b, s, d), norm_cross[l], xa_wq[l].astype(BF16),
                            kv.reshape(b, n_mem, 2 * d), xa_wo[l].astype(BF16), tq=256).reshape(m, d)
        h = hier_moe(h, norm_moe[l], router_group[l], router_expert[l], moe_w1, moe_w3, moe_w2, l,
                     norm_final, l == depth - 1)
    return h.reshape(b, s, d)
```

```python
import functools
import math

import numpy as np
import jax
import jax.numpy as jnp
from jax import lax
from jax.experimental import pallas as pl
from jax.experimental.pallas import tpu as pltpu

F32 = jnp.float32
BF16 = jnp.bfloat16

D_MODEL = 2048
MIX = D_MODEL // 4
A_HEADS = 4
A_VDIM = MIX // A_HEADS
A_QKDIM = A_VDIM // 2
B_HDIM = 64
B_HEADS = MIX // B_HDIM
B_LORA = 64
B_GATE_LORA = 128
LN_X_EPS = 64e-5
C_HEADS = 4
C_HDIM = MIX // C_HEADS
C_PATTERNS = ((128, 1), (512, 4), (2048, 16))
D_GSIZE = 16
D_GROUPS = MIX // D_GSIZE
D_STATE = 64
IN_A = 3 * MIX
IN_B = 3 * MIX + 4 * B_LORA + B_GATE_LORA
IN_B_PAD = 2048
IN_C = 3 * MIX
IN_D = MIX
XA_HEADS = 4
XA_HDIM = D_MODEL // XA_HEADS
N_GROUPS = 4
EXPERTS_PER_GROUP = 8
N_EXPERTS = N_GROUPS * EXPERTS_PER_GROUP
TOP_K = 2
D_EXPERT = D_MODEL // 4
ROPE_THETA = 10000.0
RMS_EPS = 1e-6
NEG_INF = -1e30

Z_B = 0
Z_A = IN_B_PAD
Z_C = Z_A + IN_A
Z_D = Z_C + IN_C
Z_W = Z_D + IN_D

LANES = 128
VMEM_LIMIT = 56 * 1024 * 1024


def _cparams(sem, vmem=VMEM_LIMIT):
    return pltpu.CompilerParams(dimension_semantics=sem, vmem_limit_bytes=vmem)


def _rms(x, g):
    return x * lax.rsqrt(jnp.mean(x * x, axis=-1, keepdims=True) + RMS_EPS) * g


def _dot(a, b):
    return jnp.dot(a, b, preferred_element_type=F32)


def _dot_t(a, b):
    return lax.dot_general(a, b, (((1,), (1,)), ((), ())), preferred_element_type=F32)


def _split3(x):
    hi = x.astype(BF16)
    r1 = x - hi.astype(F32)
    mid = r1.astype(BF16)
    lo = (r1 - mid.astype(F32)).astype(BF16)
    return hi, mid, lo


def _segsum(x, ones):
    hi, mid, lo = _split3(x)
    return _dot(hi, ones) + _dot(mid, ones) + _dot(lo, ones)


def _norm_matmul_kernel(x_ref, g_ref, w_ref, o_ref, xn_ref):
    @pl.when(pl.program_id(1) == 0)
    def _():
        xn_ref[...] = _rms(x_ref[...], g_ref[...]).astype(BF16)

    o_ref[...] = _dot(xn_ref[...], w_ref[...].astype(BF16)).astype(o_ref.dtype)


def norm_matmul(x, g, w, *, tm, tn, out_dtype, layer=None):
    m, k = x.shape
    n = w.shape[-1]
    if layer is None:
        wspec = pl.BlockSpec((k, tn), lambda i, j: (0, j))
    else:
        wspec = pl.BlockSpec((None, k, tn), lambda i, j: (layer, 0, j))
    return pl.pallas_call(
        _norm_matmul_kernel,
        grid=(m // tm, n // tn),
        in_specs=[pl.BlockSpec((tm, k), lambda i, j: (i, 0)),
                  pl.BlockSpec((1, k), lambda i, j: (0, 0)),
                  wspec],
        out_specs=pl.BlockSpec((tm, tn), lambda i, j: (i, j)),
        out_shape=jax.ShapeDtypeStruct((m, n), out_dtype),
        scratch_shapes=[pltpu.VMEM((tm, k), BF16)],
        compiler_params=_cparams(("parallel", "arbitrary")),
        name="norm_matmul",
    )(x, g.reshape(1, k), w)


def _rope_tables(seq, dim, width):
    inv = (1.0 / (np.float32(ROPE_THETA) ** (np.arange(0, dim, 2, dtype=np.float32) / np.float32(dim))))
    ang = np.arange(seq, dtype=np.float32)[:, None] * inv.astype(np.float32)[None, :]
    cos, sin = np.cos(ang), np.sin(ang)
    cos = np.concatenate([cos, cos], axis=-1)
    sin = np.concatenate([-sin, sin], axis=-1)
    reps = width // dim
    return jnp.asarray(np.tile(cos, (1, reps)), F32), jnp.asarray(np.tile(sin, (1, reps)), F32)


def _rope_kernel(q_ref, k_ref, v_ref, cos_ref, sin_ref, qo_ref, ko_ref, vo_ref, *, half, scale, ones_cols):
    cos = cos_ref[...]
    sin = sin_ref[...]
    width = cos.shape[1]
    lane = lax.broadcasted_iota(jnp.int32, cos.shape, 1)
    first = (lane % (2 * half)) < half

    def rot(x):
        ahead = pltpu.roll(x, width - half, axis=1)
        behind = pltpu.roll(x, half, axis=1)
        return x * cos + jnp.where(first, ahead, behind) * sin

    qo_ref[...] = (rot(q_ref[...]) * scale).astype(BF16)
    ko_ref[...] = rot(k_ref[...]).astype(BF16)
    v = v_ref[...].astype(BF16)
    if ones_cols is None:
        vo_ref[...] = v
    else:
        ones = jnp.ones((v.shape[0], ones_cols), BF16)
        parts = []
        for hd in range(v.shape[1] // ones_cols):
            parts += [v[:, hd * ones_cols:(hd + 1) * ones_cols], ones]
        vo_ref[...] = jnp.concatenate(parts, axis=1)


def rope_qkv(z, col0, head_dim, *, tm, ones_cols=None, tables=None):
    b, s, _ = z.shape
    cos, sin = tables if tables is not None else _rope_tables(s, head_dim, MIX)
    cb = col0 // MIX
    zspec = lambda off: pl.BlockSpec((None, tm, MIX), lambda bi, i, off=off: (bi, i, cb + off))
    tspec = pl.BlockSpec((tm, MIX), lambda bi, i: (i, 0))
    ospec = pl.BlockSpec((None, tm, MIX), lambda bi, i: (bi, i, 0))
    oshape = jax.ShapeDtypeStruct((b, s, MIX), BF16)
    if ones_cols is None:
        vspec, vshape = ospec, oshape
    else:
        vspec = pl.BlockSpec((None, tm, 2 * MIX), lambda bi, i: (bi, i, 0))
        vshape = jax.ShapeDtypeStruct((b, s, 2 * MIX), BF16)
    return pl.pallas_call(
        functools.partial(_rope_kernel, half=head_dim // 2, scale=head_dim ** -0.5, ones_cols=ones_cols),
        grid=(b, s // tm),
        in_specs=[zspec(0), zspec(1), zspec(2), tspec, tspec],
        out_specs=[ospec, ospec, vspec],
        out_shape=[oshape, oshape, vshape],
        compiler_params=_cparams(("parallel", "parallel")),
        name="rope_qkv",
    )(z, z, z, cos, sin)


def _diff_attn_kernel(lam_ref, g_ref, q_ref, k_ref, v1_ref, o_ref, *, lam_init):
    lv = lam_ref[...]
    lam = (jnp.exp(jnp.sum(lv[0:1] * lv[1:2], axis=-1, keepdims=True))
           - jnp.exp(jnp.sum(lv[2:3] * lv[3:4], axis=-1, keepdims=True)) + lam_init)
    q = q_ref[...]
    k = k_ref[...]
    v1 = v1_ref[...]
    lane = lax.broadcasted_iota(jnp.int32, q.shape, 1)
    zero = jnp.zeros_like(q)

    def branch(qm):
        s = _dot_t(qm, k)
        p = jnp.exp((s - jnp.max(s, axis=-1, keepdims=True)).astype(BF16))
        acc = _dot(p, v1)
        return acc[:, :A_VDIM] / acc[:, A_VDIM:]

    o = branch(jnp.where(lane < A_QKDIM, q, zero)) - lam * branch(jnp.where(lane >= A_QKDIM, q, zero))
    o_ref[...] = _rms(o, g_ref[...]) * (1.0 - lam_init)


def diff_attention(q, k, v1, lam_vecs, subln_g, layer_idx, *, tq):
    b, s, _ = q.shape
    lam_init = 0.8 - 0.6 * math.exp(-0.3 * layer_idx)
    qspec = pl.BlockSpec((None, tq, A_VDIM), lambda bi, h, i: (bi, i, h))
    return pl.pallas_call(
        functools.partial(_diff_attn_kernel, lam_init=lam_init),
        grid=(b, A_HEADS, s // tq),
        in_specs=[pl.BlockSpec((4, A_QKDIM), lambda bi, h, i: (0, 0)),
                  pl.BlockSpec((1, A_VDIM), lambda bi, h, i: (0, 0)),
                  qspec,
                  pl.BlockSpec((None, s, A_VDIM), lambda bi, h, i: (bi, 0, h)),
                  pl.BlockSpec((None, s, 2 * A_VDIM), lambda bi, h, i: (bi, 0, h))],
        out_specs=qspec,
        out_shape=jax.ShapeDtypeStruct((b, s, MIX), F32),
        compiler_params=_cparams(("parallel", "parallel", "arbitrary")),
        name="diff_attention",
    )(lam_vecs, subln_g.reshape(1, A_VDIM), q, k, v1)


C_REACH = max(w // 2 for w, _ in C_PATTERNS)


def _dilated_bias_table(tq, window):
    n_delta = (window - tq) // tq + 1
    i = np.arange(tq)[None, :, None]
    j = np.arange(window)[None, None, :]
    n = np.arange(n_delta)[:, None, None]
    d = j - i - n * tq
    count = np.zeros(d.shape, np.int32)
    for w, dil in C_PATTERNS:
        count += ((np.abs(d) <= w // 2) & (d % dil == 0)).astype(np.int32)
    bias = np.where(count > 0, np.log(np.maximum(count, 1)), NEG_INF)
    return jnp.asarray(bias, F32)


def _dilated_attn_kernel(bias_ref, q_ref, k_ref, v_ref, o_ref, *, tq, window, seq):
    start = pl.program_id(2) * tq
    ws = pl.multiple_of(jnp.clip(start - C_REACH, 0, seq - window), tq)
    kw = k_ref[pl.ds(ws, window), :]
    vw = v_ref[pl.ds(ws, window), :]
    s = _dot_t(q_ref[...], kw) + bias_ref[...]
    e = jnp.exp(s - jnp.max(s, axis=-1, keepdims=True))
    den = jnp.sum(e, axis=-1, keepdims=True)
    o_ref[...] = _dot(e.astype(BF16), vw) / den


def dilated_attention(q, k, v, *, tq=128):
    b, s, _ = q.shape
    window = tq + 2 * C_REACH
    assert s >= window and s % tq == 0 and C_REACH % tq == 0
    bias = _dilated_bias_table(tq, window)

    def bias_map(bi, h, i):
        start = i * tq
        ws = jnp.clip(start - C_REACH, 0, s - window)
        return ((start - ws) // tq, 0, 0)

    qspec = pl.BlockSpec((None, tq, C_HDIM), lambda bi, h, i: (bi, i, h))
    kspec = pl.BlockSpec((None, s, C_HDIM), lambda bi, h, i: (bi, 0, h))
    return pl.pallas_call(
        functools.partial(_dilated_attn_kernel, tq=tq, window=window, seq=s),
        grid=(b, C_HEADS, s // tq),
        in_specs=[pl.BlockSpec((None, tq, window), bias_map), qspec, kspec, kspec],
        out_specs=qspec,
        out_shape=jax.ShapeDtypeStruct((b, s, MIX), F32),
        compiler_params=_cparams(("parallel", "parallel", "arbitrary")),
        name="dilated_attention",
    )(bias, q, k, v)


def _s5_out_kernel(y_ref, u_ref, d_ref, w_ref, b_ref, o_ref):
    y = y_ref[...] + d_ref[...] * u_ref[...]
    gl = jax.nn.gelu(y)
    gate = jax.nn.sigmoid(_dot(gl.astype(BF16), w_ref[...]) + b_ref[...])
    o_ref[...] = gl * gate


def s5_output(y, z, d_skip, glu_w, glu_b, *, tm):
    b, s, _ = z.shape
    yspec = pl.BlockSpec((None, tm, MIX), lambda bi, i: (bi, i, 0))
    vspec = pl.BlockSpec((1, MIX), lambda bi, i: (0, 0))
    return pl.pallas_call(
        _s5_out_kernel,
        grid=(b, s // tm),
        in_specs=[yspec,
                  pl.BlockSpec((None, tm, MIX), lambda bi, i: (bi, i, Z_D // MIX)),
                  vspec, pl.BlockSpec((MIX, MIX), lambda bi, i: (0, 0)), vspec],
        out_specs=yspec,
        out_shape=jax.ShapeDtypeStruct((b, s, MIX), F32),
        compiler_params=_cparams(("parallel", "parallel")),
        name="s5_output",
    )(y, z, d_skip.reshape(1, MIX), glu_w.astype(BF16), glu_b.reshape(1, MIX))


S5_T = 16
S5_KW = S5_T * D_GSIZE
S5_SW = 2 * D_STATE


S5_GPS = LANES // D_GSIZE


def _s5_chunk_kernel(u_ref, wm_ref, we_ref, wft_ref, ar_ref, ai_ref, y_ref, *, n_chunks, nb):
    rows = nb * n_chunks
    lane = lax.broadcasted_iota(jnp.int32, (n_chunks, LANES), 1)
    seg = [(lane >= D_GSIZE * j) & (lane < D_GSIZE * (j + 1)) for j in range(S5_GPS)]
    per_tile = LANES // D_GSIZE
    u_t = [[u_ref[b, pl.ds(t, n_chunks, stride=S5_T), :] for t in range(S5_T)] for b in range(nb)]
    row = lax.broadcasted_iota(jnp.int32, (rows, S5_SW), 0) & (n_chunks - 1)

    def lane_move(x, src, dst):
        shift = (D_GSIZE * (dst - src)) % LANES
        return pltpu.roll(x, shift, axis=1) if shift else x

    y_groups = []
    for gl in range(S5_GPS):
        packed = []
        for b in range(nb):
            tiles = []
            for half in range(S5_T // per_tile):
                acc = jnp.zeros((n_chunks, LANES), F32)
                for j in range(per_tile):
                    acc = jnp.where(seg[j], lane_move(u_t[b][half * per_tile + j], gl, j), acc)
                tiles.append(acc)
            packed.append(jnp.concatenate(tiles, axis=1))
        u = jnp.concatenate(packed, axis=0).astype(BF16)
        y = _dot(u, wm_ref[gl])
        xinc = _dot(u, we_ref[gl])

        def cmul(t, lvl, d, gl=gl):
            ar = ar_ref[gl, d, lvl:lvl + 1, :]
            ai = ai_ref[gl, d, lvl:lvl + 1, :]
            return t * ar + pltpu.roll(t, D_STATE, axis=1) * ai

        carries = []
        for d in range(2):
            x = xinc[:, d * S5_SW:(d + 1) * S5_SW]
            for lvl in range(n_chunks.bit_length() - 1):
                sh = 1 << lvl
                if d == 0:
                    prev = jnp.where(row >= sh, pltpu.roll(x, sh, axis=0), 0.0)
                else:
                    prev = jnp.where(row < n_chunks - sh, pltpu.roll(x, rows - sh, axis=0), 0.0)
                x = x + cmul(prev, lvl, d)
            if d == 0:
                carries.append(jnp.where(row >= 1, pltpu.roll(x, 1, axis=0), 0.0))
            else:
                carries.append(jnp.where(row < n_chunks - 1, pltpu.roll(x, rows - 1, axis=0), 0.0))
        cin = jnp.concatenate(carries, axis=1).astype(BF16)
        y_groups.append(y + _dot_t(cin, wft_ref[gl]))

    for b in range(nb):
        for t in range(S5_T):
            half, j = divmod(t, per_tile)
            out = jnp.zeros((n_chunks, LANES), F32)
            for gl in range(S5_GPS):
                tile = y_groups[gl][b * n_chunks:(b + 1) * n_chunks, half * LANES:(half + 1) * LANES]
                out = jnp.where(seg[gl], lane_move(tile, j, gl), out)
            y_ref[b, pl.ds(t, n_chunks, stride=S5_T), :] = out


def _cpow_table(ar, ai, count):
    res_r, res_i = [ar], [ai]
    for _ in range(count - 1):
        ar, ai = ar * ar - ai * ai, 2.0 * ar * ai
        res_r.append(ar)
        res_i.append(ai)
    return jnp.stack(res_r), jnp.stack(res_i)


def _s5_chunk_weights(a_re, a_im, log_dt, b_re, b_im, c_re, c_im, n_levels):
    g_n, t_n, c_n = D_GROUPS, S5_T, D_GSIZE
    taus = jnp.arange(t_n + 1, dtype=F32)[None, :, None]
    cr, ci = c_re.astype(F32)[:, None], c_im.astype(F32)[:, None]
    toe, ee, ff, lvl_r, lvl_i = [], [], [], [], []
    for direction in range(2):
        lr = jnp.minimum(a_re[direction].astype(F32), -1e-4)
        li = a_im[direction].astype(F32)
        dt = jnp.exp(log_dt[direction].astype(F32))[:, None]
        mag = jnp.exp(dt * lr)
        abr, abi = mag * jnp.cos(dt * li), mag * jnp.sin(dt * li)
        den = lr * lr + li * li
        qr, qi = lr / den, -li / den
        fr = (abr - 1.0) * qr - abi * qi
        fi = (abr - 1.0) * qi + abi * qr
        br, bi = b_re.astype(F32), b_im.astype(F32)
        btr = (fr[..., None] * br - fi[..., None] * bi).transpose(0, 2, 1)
        bti = (fr[..., None] * bi + fi[..., None] * br).transpose(0, 2, 1)
        pmag = jnp.exp(taus * (dt * lr)[:, None])
        ang = taus * (dt * li)[:, None]
        pr, pi = pmag * jnp.cos(ang), pmag * jnp.sin(ang)

        def c_times(order):
            por, poi = pr[:, order][:, :, None, :], pi[:, order][:, :, None, :]
            return cr * por - ci * poi, cr * poi + ci * por

        cpr, cpi = c_times(jnp.arange(t_n))
        k = (jnp.einsum('gin,gkn->gik', btr, cpr.reshape(g_n, t_n * c_n, D_STATE))
             - jnp.einsum('gin,gkn->gik', bti, cpi.reshape(g_n, t_n * c_n, D_STATE)))
        pad = (t_n - 1) * c_n
        if direction == 0:
            kp = jnp.pad(k, ((0, 0), (0, 0), (pad, 0)))
        else:
            krev = k.reshape(g_n, c_n, t_n, c_n)[:, :, ::-1, :].reshape(g_n, c_n, t_n * c_n)
            kp = jnp.pad(krev, ((0, 0), (0, 0), (0, pad)))
        toe.append(jnp.stack([kp[:, :, (t_n - 1 - s) * c_n:(t_n - 1 - s) * c_n + t_n * c_n]
                              for s in range(t_n)], axis=1))
        order = jnp.arange(t_n - 1, -1, -1) if direction == 0 else jnp.arange(t_n)
        por, poi = pr[:, order][:, :, None, :], pi[:, order][:, :, None, :]
        er = por * btr[:, None] - poi * bti[:, None]
        ei = por * bti[:, None] + poi * btr[:, None]
        ee.append(jnp.concatenate([er, ei], axis=3).reshape(g_n, S5_KW, S5_SW))
        order = jnp.arange(1, t_n + 1) if direction == 0 else jnp.arange(t_n, 0, -1)
        gr, gi = c_times(order)
        ff.append(jnp.concatenate([gr, -gi], axis=3).reshape(g_n, S5_KW, S5_SW))
        tr, ti = _cpow_table(pr[:, t_n], pi[:, t_n], n_levels)
        lvl_r.append(jnp.concatenate([tr, tr], axis=2))
        lvl_i.append(jnp.concatenate([-ti, ti], axis=2))
    wm = (toe[0] + toe[1]).reshape(g_n, S5_KW, S5_KW)
    we = jnp.concatenate(ee, axis=2)
    wft = jnp.concatenate(ff, axis=2)
    ar = jnp.stack(lvl_r).transpose(2, 0, 1, 3)
    ai = jnp.stack(lvl_i).transpose(2, 0, 1, 3)
    return wm.astype(BF16), we.astype(BF16), wft.astype(BF16), ar, ai


def s5_chunked(z, a_re, a_im, log_dt, b_re, b_im, c_re, c_im):
    b, s, _ = z.shape
    n_chunks = s // S5_T
    n_levels = n_chunks.bit_length() - 1
    assert n_chunks == 1 << n_levels
    wm, we, wft, ar, ai = _s5_chunk_weights(a_re, a_im, log_dt, b_re, b_im, c_re, c_im, max(n_levels, 1))
    groups = lambda shape: pl.BlockSpec((S5_GPS,) + shape, lambda k: (k,) + (0,) * len(shape))
    return pl.pallas_call(
        functools.partial(_s5_chunk_kernel, n_chunks=n_chunks, nb=b),
        grid=(D_GROUPS // S5_GPS,),
        in_specs=[pl.BlockSpec((b, s, LANES), lambda k: (0, 0, Z_D // LANES + k)),
                  groups((S5_KW, S5_KW)), groups((S5_KW, 2 * S5_SW)), groups((S5_KW, 2 * S5_SW)),
                  groups((2, ar.shape[2], S5_SW)), groups((2, ar.shape[2], S5_SW))],
        out_specs=pl.BlockSpec((b, s, LANES), lambda k: (0, 0, k)),
        out_shape=jax.ShapeDtypeStruct((b, s, MIX), F32),
        compiler_params=_cparams(("parallel",)),
        name="s5_chunked",
    )(z, wm, we, wft, ar, ai)


def s5_bidir(z, a_re, a_im, log_dt, b_re, b_im, c_re, c_im, d_skip, glu_w, glu_b, *, tm=1024):
    y = s5_chunked(z, a_re, a_im, log_dt, b_re, b_im, c_re, c_im)
    return s5_output(y, z, d_skip, glu_w, glu_b, tm=tm)


def _softplus(y):
    return jnp.maximum(y, 0.0) + jnp.log(1.0 + jnp.exp(-jnp.abs(y)))


def _head_ones(width):
    seg = np.arange(width) // B_HDIM
    return jnp.asarray(seg[:, None] == seg[None, :], BF16)


def _rwkv_pre_kernel(z_ref, zp_ref, zn_ref, mu_ref, w0_ref, a0_ref, w2_ref, a2_ref, g2_ref,
                     kk_ref, ka_ref, rk_ref, ones_ref,
                     r_o, a_o, w0_o, w1_o, k0_o, k1_o, b0_o, b1_o, v_o, g_o, bonus_o,
                     *, tm, n_tiles):
    i = pl.program_id(1)
    z = z_ref[...]
    row = lax.broadcasted_iota(jnp.int32, z.shape, 0)
    prev_row = jnp.where(i > 0, zp_ref[7:8, :], 0.0)
    next_row = jnp.where(i < n_tiles - 1, zn_ref[0:1, :], 0.0)
    zp = jnp.where(row == 0, prev_row, pltpu.roll(z, 1, axis=0))
    zn = jnp.where(row == tm - 1, next_row, pltpu.roll(z, tm - 1, axis=0))
    xs = z + mu_ref[0:1, :] * (zp - z) + mu_ref[1:2, :] * (zn - z)
    r = xs[:, 0:MIX]
    k = xs[:, MIX:2 * MIX]
    v = xs[:, 2 * MIX:3 * MIX]
    c0 = 3 * MIX
    wd = xs[:, c0:c0 + 2 * B_LORA]
    ad = xs[:, c0 + 2 * B_LORA:c0 + 4 * B_LORA]
    gd = xs[:, c0 + 4 * B_LORA:c0 + 4 * B_LORA + B_GATE_LORA]
    lw = _dot(jnp.tanh(wd).astype(BF16), w2_ref[...])
    la = _dot(ad.astype(BF16), a2_ref[...])
    g_o[...] = _dot(jax.nn.sigmoid(gd).astype(BF16), g2_ref[...])
    ones = ones_ref[...]
    kk = k * kk_ref[...]
    kk = kk * lax.rsqrt(_segsum(kk * kk, ones) + 1e-12)
    ka = ka_ref[...]
    ksum = jnp.zeros_like(k)
    for d, (w_o, k_o, b_o) in enumerate(((w0_o, k0_o, b0_o), (w1_o, k1_o, b1_o))):
        cols = slice(d * MIX, (d + 1) * MIX)
        logw = -_softplus(-(w0_ref[d:d + 1, :] + lw[:, cols])) - 0.5
        w_o[...] = -jnp.exp(logw)
        a = jax.nn.sigmoid(a0_ref[d:d + 1, :] + la[:, cols])
        kmod = k * (1.0 + (a - 1.0) * ka)
        k_o[...] = kmod
        b_o[...] = kk * a
        ksum = ksum + kmod
    r_o[...] = r
    a_o[...] = -kk
    bonus_o[...] = _segsum(r * (0.5 * ksum) * rk_ref[...], ones) * v
    v_o[...] = v


def _block_diag2(m):
    z = jnp.zeros_like(m[0])
    return jnp.concatenate([jnp.concatenate([m[0], z], axis=1),
                            jnp.concatenate([z, m[1]], axis=1)], axis=0)


def rwkv_pre(z, mu, w0, w2, a0, a2, g2, k_k, k_a, r_k, *, tm):
    b, s, _ = z.shape
    n_tiles = s // tm
    mu_p = jnp.pad(mu.astype(F32), ((0, 0), (0, IN_B_PAD - IN_B)))
    row_spec = pl.BlockSpec((None, tm, MIX), lambda bi, i: (bi, i, 0))
    vec = lambda n, w: pl.BlockSpec((n, w), lambda bi, i: (0, 0))
    rows = jax.ShapeDtypeStruct((b, s, MIX), F32)
    hb = tm // 8
    return pl.pallas_call(
        functools.partial(_rwkv_pre_kernel, tm=tm, n_tiles=n_tiles),
        grid=(b, n_tiles),
        in_specs=[pl.BlockSpec((None, tm, IN_B_PAD), lambda bi, i: (bi, i, 0)),
                  pl.BlockSpec((None, 8, IN_B_PAD), lambda bi, i: (bi, jnp.maximum(i * hb - 1, 0), 0)),
                  pl.BlockSpec((None, 8, IN_B_PAD), lambda bi, i: (bi, jnp.minimum((i + 1) * hb, s // 8 - 1), 0)),
                  vec(2, IN_B_PAD), vec(2, MIX), vec(2, MIX),
                  vec(2 * B_LORA, 2 * MIX), vec(2 * B_LORA, 2 * MIX), vec(B_GATE_LORA, MIX),
                  vec(1, MIX), vec(1, MIX), vec(1, MIX), vec(MIX, MIX)],
        out_specs=[row_spec] * 11,
        out_shape=[rows] * 11,
        compiler_params=_cparams(("parallel", "parallel")),
        name="rwkv_pre",
    )(z, z, z, mu_p, w0.astype(F32), a0.astype(F32),
      _block_diag2(w2).astype(BF16), _block_diag2(a2).astype(BF16), g2.astype(BF16),
      k_k.reshape(1, MIX), k_a.reshape(1, MIX), r_k.reshape(1, MIX), _head_ones(MIX))


def _rwkv_scan_kernel(af_ref, rf_ref, vf_ref, lwf_ref, kf_ref, bf_ref,
                      ab_ref, rb_ref, vb_ref, lwb_ref, kb_ref, bb_ref,
                      trif_ref, trib_ref, yf_ref, yb_ref, h_ref, *, tc, nb):
    @pl.when(pl.program_id(0) == 0)
    def _():
        h_ref[...] = jnp.zeros_like(h_ref)

    row = lax.broadcasted_iota(jnp.int32, (tc, tc), 0)
    col = lax.broadcasted_iota(jnp.int32, (tc, tc), 1)
    eye = (row == col).astype(F32)
    row2 = lax.broadcasted_iota(jnp.int32, (tc, 2 * tc), 0)
    col2 = lax.broadcasted_iota(jnp.int32, (tc, 2 * tc), 1) & (tc - 1)
    zeros_tv = jnp.zeros((tc, B_HDIM), BF16)
    n_sq = tc.bit_length() - 1
    tdot = lambda x, y: lax.dot_general(x, y, (((0,), (0,)), ((), ())), preferred_element_type=F32)

    seqs = []
    for reverse, (a_ref, r_ref, v_ref, lw_ref, k_ref, b_ref, tri_ref, y_ref) in (
            (False, (af_ref, rf_ref, vf_ref, lwf_ref, kf_ref, bf_ref, trif_ref, yf_ref)),
            (True, (ab_ref, rb_ref, vb_ref, lwb_ref, kb_ref, bb_ref, trib_ref, yb_ref))):
        tri = tri_ref[...]
        before = (col2 > row2) if reverse else (col2 < row2)
        upto = (col2 >= row2) if reverse else (col2 <= row2)
        last = 0 if reverse else tc - 1
        for bi in range(nb):
            lw = lw_ref[bi]
            hi, mid, lo = _split3(lw)
            cl = _dot(tri, hi) + _dot(tri, mid) + _dot(tri, lo)
            tot = cl[last:last + 1, :]
            einv = jnp.exp(-cl)
            etot = jnp.exp(tot - cl)
            b_all = b_ref[bi]
            k_all = k_ref[bi]
            seqs.append(dict(
                a_t=a_ref[bi] * jnp.exp(cl - lw), r_t=r_ref[bi] * jnp.exp(cl),
                b_t=b_all * einv, k_t=k_all * einv, b_h=b_all * etot, k_h=k_all * etot,
                g_tot=jnp.exp(tot), v=v_ref[bi], before=before, upto=upto, y_ref=y_ref, bi=bi))

    inst = [(sq, slice(hd * B_HDIM, (hd + 1) * B_HDIM)) for sq in seqs for hd in range(B_HEADS)]
    ids = range(len(inst))
    at = [sq['a_t'][:, s].astype(BF16) for sq, s in inst]
    rt = [sq['r_t'][:, s].astype(BF16) for sq, s in inst]
    bk = [jnp.concatenate([sq['b_t'][:, s], sq['k_t'][:, s]], axis=0).astype(BF16) for sq, s in inst]
    bkh = [jnp.concatenate([sq['b_h'][:, s], sq['k_h'][:, s]], axis=0).astype(BF16) for sq, s in inst]
    vv = [sq['v'][:, s].astype(BF16) for sq, s in inst]
    g = [_dot_t(jnp.concatenate([at[n], rt[n]], axis=0), bk[n]) for n in ids]
    ga = [jnp.where(inst[n][0]['before'], g[n][:tc], 0.0) for n in ids]
    gr = [jnp.where(inst[n][0]['upto'], g[n][tc:], 0.0).astype(BF16) for n in ids]
    lkv = [_dot(ga[n].astype(BF16), jnp.concatenate([zeros_tv, vv[n]], axis=0)) for n in ids]
    lb = [ga[n][:, :tc].astype(BF16) for n in ids]
    pinv = [eye + ga[n][:, :tc] for n in ids]
    lb = [_dot(x, x).astype(BF16) for x in lb]
    for _ in range(1, n_sq - 1):
        prod = [_dot(jnp.concatenate([pinv[n].astype(BF16), lb[n]], axis=0), lb[n]) for n in ids]
        pinv = [pinv[n] + prod[n][:tc] for n in ids]
        lb = [prod[n][tc:].astype(BF16) for n in ids]
    pinv = [(pinv[n] + _dot(pinv[n].astype(BF16), lb[n])).astype(BF16) for n in ids]
    ah = [_dot(pinv[n], at[n]).astype(BF16) for n in ids]
    u0 = [_dot(pinv[n], lkv[n].astype(BF16)).astype(BF16) for n in ids]
    rhs = [jnp.concatenate([jnp.concatenate([ah[n], u0[n]], axis=1),
                            jnp.concatenate([zeros_tv, vv[n]], axis=1)], axis=0) for n in ids]
    ry = [_dot(gr[n], rhs[n]) for n in ids]
    ph = [tdot(bkh[n], rhs[n]) for n in ids]
    rh = [inst[n][0]['r_t'][:, inst[n][1]] + ry[n][:, :B_HDIM] for n in ids]
    phi = [eye * inst[n][0]['g_tot'][:, inst[n][1]] + ph[n][:, :B_HDIM] for n in ids]
    h0 = [h_ref[n].astype(BF16) for n in ids]
    fin = [_dot(jnp.concatenate([rh[n], phi[n]], axis=0).astype(BF16), h0[n]) for n in ids]
    yo = [fin[n][:tc] + ry[n][:, B_HDIM:] for n in ids]
    for si, sq in enumerate(seqs):
        sq['y_ref'][sq['bi']] = jnp.concatenate(yo[si * B_HEADS:(si + 1) * B_HEADS], axis=1)
    for n in ids:
        h_ref[n] = fin[n][tc:] + ph[n][:, B_HDIM:]


def rwkv_scan(a, r, v, lwf, kf, bvf, lwb, kb, bvb, *, tc=64):
    nb, s, _ = a.shape
    nch = s // tc
    fwd = pl.BlockSpec((nb, tc, MIX), lambda c: (0, c, 0))
    bwd = pl.BlockSpec((nb, tc, MIX), lambda c: (0, nch - 1 - c, 0))
    tspec = pl.BlockSpec((tc, tc), lambda c: (0, 0))
    t_idx = np.arange(tc)
    tri_f = jnp.asarray(t_idx[None, :] <= t_idx[:, None], BF16)
    tri_b = jnp.asarray(t_idx[None, :] >= t_idx[:, None], BF16)
    out = jax.ShapeDtypeStruct((nb, s, MIX), F32)
    return pl.pallas_call(
        functools.partial(_rwkv_scan_kernel, tc=tc, nb=nb),
        grid=(nch,),
        in_specs=[fwd] * 6 + [bwd] * 6 + [tspec, tspec],
        out_specs=[fwd, bwd],
        out_shape=[out, out],
        scratch_shapes=[pltpu.VMEM((2 * nb * B_HEADS, B_HDIM, B_HDIM), F32)],
        compiler_params=_cparams(("arbitrary",)),
        name="rwkv_scan",
    )(a, r, v, lwf, kf, bvf, a, r, v, lwb, kb, bvb, tri_f, tri_b)


def _rwkv_post_kernel(yf_ref, yb_ref, bonus_ref, g_ref, lw_ref, lb_ref, ones_ref, o_ref):
    y = yf_ref[...] + yb_ref[...]
    ones = ones_ref[...]
    mean = _segsum(y, ones) * (1.0 / B_HDIM)
    yc = y - mean
    var = _segsum(yc * yc, ones) * (1.0 / B_HDIM)
    yn = yc * lax.rsqrt(var + LN_X_EPS) * lw_ref[...] + lb_ref[...]
    o_ref[...] = (yn + bonus_ref[...]) * g_ref[...]


def rwkv_post(yf, yb, bonus, g, lnx_w, lnx_b, *, tm):
    b, s, _ = yf.shape
    row_spec = pl.BlockSpec((None, tm, MIX), lambda bi, i: (bi, i, 0))
    vec = pl.BlockSpec((1, MIX), lambda bi, i: (0, 0))
    return pl.pallas_call(
        _rwkv_post_kernel,
        grid=(b, s // tm),
        in_specs=[row_spec, row_spec, row_spec, row_spec, vec, vec,
                  pl.BlockSpec((MIX, MIX), lambda bi, i: (0, 0))],
        out_specs=row_spec,
        out_shape=jax.ShapeDtypeStruct((b, s, MIX), F32),
        compiler_params=_cparams(("parallel", "parallel")),
        name="rwkv_post",
    )(yf, yb, bonus, g, lnx_w.reshape(1, MIX), lnx_b.reshape(1, MIX), _head_ones(MIX))


def rwkv7_bidir(z, mu, w0, w2, a0, a2, g2, k_k, k_a, r_k, lnx_w, lnx_b, *, tm=512, tc=64):
    r, a, wf, wb, kf, kb, bf, bb, v, g, bonus = rwkv_pre(z, mu, w0, w2, a0, a2, g2, k_k, k_a, r_k, tm=tm)
    yf, yb = rwkv_scan(a, r, v, wf, kf, bf, wb, kb, bb, tc=tc)
    return rwkv_post(yf, yb, bonus, g, lnx_w, lnx_b, tm=tm)


def _mix_out_kernel(h_ref, oa_ref, ob_ref, oc_ref, od_ref, gc_ref, gd_ref, w_ref, o_ref, mix_ref):
    @pl.when(pl.program_id(1) == 0)
    def _():
        mix_ref[...] = jnp.concatenate(
            [oa_ref[...], ob_ref[...], _rms(oc_ref[...], gc_ref[...]), _rms(od_ref[...], gd_ref[...])],
            axis=1).astype(BF16)

    o_ref[...] = h_ref[...] + _dot(mix_ref[...], w_ref[...])


def mix_out(h, oa, ob, oc, od, gc, gd, w_out, *, tm, tn):
    m, d = h.shape
    mspec = pl.BlockSpec((tm, MIX), lambda i, j: (i, 0))
    vspec = pl.BlockSpec((1, MIX), lambda i, j: (0, 0))
    hspec = pl.BlockSpec((tm, tn), lambda i, j: (i, j))
    return pl.pallas_call(
        _mix_out_kernel,
        grid=(m // tm, d // tn),
        in_specs=[hspec, mspec, mspec, mspec, mspec, vspec, vspec,
                  pl.BlockSpec((4 * MIX, tn), lambda i, j: (0, j),
                               pipeline_mode=pl.Buffered(1 if tn == d else 2))],
        out_specs=hspec,
        out_shape=jax.ShapeDtypeStruct((m, d), F32),
        scratch_shapes=[pltpu.VMEM((tm, 4 * MIX), BF16)],
        compiler_params=_cparams(("parallel", "arbitrary")),
        name="mix_out",
    )(h, oa, ob, oc, od, gc.reshape(1, MIX), gd.reshape(1, MIX), w_out)


def _cross_attn_kernel(h_ref, g_ref, wq_ref, kv_ref, wo_ref, o_ref):
    h = h_ref[...]
    q = _dot(_rms(h, g_ref[...]).astype(BF16), wq_ref[...]).astype(BF16)
    outs = []
    for hd in range(XA_HEADS):
        cols = slice(hd * XA_HDIM, (hd + 1) * XA_HDIM)
        kh = kv_ref[:, cols]
        vh = kv_ref[:, D_MODEL + hd * XA_HDIM:D_MODEL + (hd + 1) * XA_HDIM]
        s = _dot_t(q[:, cols], kh) * (XA_HDIM ** -0.5)
        e = jnp.exp(s - jnp.max(s, axis=-1, keepdims=True))
        p = e / jnp.sum(e, axis=-1, keepdims=True)
        outs.append(_dot(p.astype(BF16), vh).astype(BF16))
    o = jnp.concatenate(outs, axis=1)
    o_ref[...] = h + _dot(o, wo_ref[...])


def cross_attention(h, g, wq, kv, wo, *, tq):
    b, s, d = h.shape
    n_mem = kv.shape[1]
    hspec = pl.BlockSpec((None, tq, d), lambda bi, i: (bi, i, 0))
    wspec = pl.BlockSpec((d, d), lambda bi, i: (0, 0), pipeline_mode=pl.Buffered(1))
    return pl.pallas_call(
        _cross_attn_kernel,
        grid=(b, s // tq),
        in_specs=[hspec, pl.BlockSpec((1, d), lambda bi, i: (0, 0)), wspec,
                  pl.BlockSpec((None, n_mem, 2 * d), lambda bi, i: (bi, 0, 0)), wspec],
        out_specs=hspec,
        out_shape=jax.ShapeDtypeStruct((b, s, d), F32),
        compiler_params=_cparams(("parallel", "parallel")),
        name="cross_attention",
    )(h, g.reshape(1, d), wq, kv, wo)


ROUTER_W = LANES
MOE_TB = 256


def _router_kernel(h_ref, g_ref, wr_ref, lt_ref, xn_ref, eid_ref, gate_ref, rank_ref, cnt_ref, run_ref):
    @pl.when(pl.program_id(0) == 0)
    def _():
        run_ref[...] = jnp.zeros_like(run_ref)

    xn = _rms(h_ref[...], g_ref[...])
    xn_ref[...] = xn
    logits = jnp.dot(xn, wr_ref[...], preferred_element_type=F32, precision=lax.Precision.HIGHEST)
    lane = lax.broadcasted_iota(jnp.int32, logits.shape, 1)
    big = jnp.int32(ROUTER_W)

    def first_max(mask):
        m = jnp.max(jnp.where(mask, logits, NEG_INF), axis=-1, keepdims=True)
        idx = jnp.min(jnp.where(mask & (logits == m), lane, big), axis=-1, keepdims=True)
        return m, idx

    gmask = lane < N_GROUPS
    gmax, gidx = first_max(gmask)
    g_w = 1.0 / jnp.sum(jnp.where(gmask, jnp.exp(logits - gmax), 0.0), axis=-1, keepdims=True)
    e0 = N_GROUPS + gidx * EXPERTS_PER_GROUP
    emask = (lane >= e0) & (lane < e0 + EXPERTS_PER_GROUP)
    m1, i1 = first_max(emask)
    m2, i2 = first_max(emask & (lane != i1))
    e2 = jnp.exp(m2 - m1)
    w1 = 1.0 / (1.0 + e2)
    w2 = e2 / (1.0 + e2)
    e1 = i1 - N_GROUPS
    e2i = i2 - N_GROUPS
    eid_ref[...] = jnp.where(lane == 0, e1, jnp.where(lane == 1, e2i, 0))
    gate_ref[...] = jnp.where(lane == 0, g_w * w1, jnp.where(lane == 1, g_w * w2, 0.0))
    hit1 = lane == e1
    hit2 = lane == e2i
    onehot = jnp.where(hit1 | hit2, 1.0, 0.0)
    earlier = run_ref[...] + _dot(lt_ref[...], onehot.astype(BF16))
    r1 = jnp.sum(jnp.where(hit1, earlier, 0.0), axis=-1, keepdims=True)
    r2 = jnp.sum(jnp.where(hit2, earlier, 0.0), axis=-1, keepdims=True)
    rank_ref[...] = jnp.where(lane == 0, r1, jnp.where(lane == 1, r2, 0.0)).astype(jnp.int32)
    run_ref[...] = run_ref[...] + jnp.sum(onehot, axis=0, keepdims=True)
    cnt_ref[...] = run_ref[...].astype(jnp.int32)


def moe_router(h, g, wr_group, wr_expert, *, tm):
    m, d = h.shape
    wr = jnp.concatenate([wr_group, wr_expert], axis=1).astype(F32)
    wr = jnp.pad(wr, ((0, 0), (0, ROUTER_W - wr.shape[1])))
    t_idx = np.arange(tm)
    lower = jnp.asarray(t_idx[None, :] < t_idx[:, None], BF16)
    hspec = pl.BlockSpec((tm, d), lambda i: (i, 0))
    lspec = pl.BlockSpec((tm, ROUTER_W), lambda i: (i, 0))
    cspec = pl.BlockSpec((1, ROUTER_W), lambda i: (0, 0))
    lane_i = jax.ShapeDtypeStruct((m, ROUTER_W), jnp.int32)
    return pl.pallas_call(
        _router_kernel,
        grid=(m // tm,),
        in_specs=[hspec, pl.BlockSpec((1, d), lambda i: (0, 0)),
                  pl.BlockSpec((d, ROUTER_W), lambda i: (0, 0)),
                  pl.BlockSpec((tm, tm), lambda i: (0, 0))],
        out_specs=[hspec, lspec, lspec, lspec, cspec],
        out_shape=[jax.ShapeDtypeStruct((m, d), F32), lane_i,
                   jax.ShapeDtypeStruct((m, ROUTER_W), F32), lane_i,
                   jax.ShapeDtypeStruct((1, ROUTER_W), jnp.int32)],
        scratch_shapes=[pltpu.VMEM((1, ROUTER_W), F32)],
        compiler_params=_cparams(("arbitrary",)),
        name="moe_router",
    )(h, g.reshape(1, d), wr, lower)


GATHER_UNROLL = 8


def _start_row_gather(row_of, src_hbm, dst, sem, n_rows, dst_row0=0):
    def issue(r, c):
        pltpu.make_async_copy(src_hbm.at[pl.ds(row_of(r), 1)], dst.at[pl.ds(dst_row0 + r, 1)], sem).start()
        return c

    lax.fori_loop(0, n_rows, issue, 0, unroll=GATHER_UNROLL)


def _wait_row_gather(src_hbm, dst, sem, n_rows):
    pltpu.make_async_copy(src_hbm.at[pl.ds(0, n_rows)], dst, sem).wait()


def _moe_expert_kernel(pos_ref, exp_ref, nvb_ref, nxt_ref, run_ref, x_hbm, w1_hbm, w3_hbm, w2_hbm, o_ref,
                       slot_ref, xbuf, sem, w1buf, w3buf, w2buf, wsem, *, tb, n_assign, layer):
    i = pl.program_id(0)
    nvb = nvb_ref[0]

    def weight_copies(e, s):
        return (pltpu.make_async_copy(w1_hbm.at[layer, e], w1buf.at[s], wsem.at[s, 0]),
                pltpu.make_async_copy(w3_hbm.at[layer, e], w3buf.at[s], wsem.at[s, 1]),
                pltpu.make_async_copy(w2_hbm.at[layer, e], w2buf.at[s], wsem.at[s, 2]))

    e_cur = exp_ref[i]
    first = (i == 0) | (exp_ref[jnp.maximum(i - 1, 0)] != e_cur)
    wslot = run_ref[e_cur] & 1

    @pl.when(i == 0)
    def _():
        for cp in weight_copies(e_cur, wslot):
            cp.start()

    @pl.when(i == 0)
    def _():
        def clear(s, c):
            slot_ref[s] = 0
            return c

        lax.fori_loop(0, slot_ref.shape[0], clear, 0, unroll=GATHER_UNROLL)

        def place(a, c):
            slot_ref[pos_ref[a]] = lax.shift_right_logical(a, TOP_K.bit_length() - 1)
            return c

        lax.fori_loop(0, n_assign, place, 0, unroll=GATHER_UNROLL)
        _start_row_gather(lambda r: slot_ref[r], x_hbm, xbuf.at[0], sem.at[0], tb)

    @pl.when(i + 1 < nvb)
    def _():
        nxt = (i + 1) & 1
        base = (i + 1) * tb
        _start_row_gather(lambda r: slot_ref[base + r], x_hbm, xbuf.at[nxt], sem.at[nxt], tb)

    @pl.when((i < nvb) & first & (nxt_ref[e_cur] >= 0))
    def _():
        for cp in weight_copies(nxt_ref[e_cur], 1 - wslot):
            cp.start()

    @pl.when(i < nvb)
    def _():
        @pl.when(first)
        def _():
            for cp in weight_copies(e_cur, wslot):
                cp.wait()

        cur = i & 1
        _wait_row_gather(x_hbm, xbuf.at[cur], sem.at[cur], tb)
        x = xbuf[cur].astype(BF16)
        h1 = _dot(x, w1buf[wslot].astype(BF16))
        h3 = _dot(x, w3buf[wslot].astype(BF16))
        act = (jax.nn.silu(h1) * h3).astype(BF16)
        o_ref[...] = _dot(act, w2buf[wslot].astype(BF16))

    @pl.when(i >= nvb)
    def _():
        o_ref[...] = jnp.zeros_like(o_ref)


def moe_experts(xn, pos, blk_exp, n_valid, nxt_exp, run_idx, w1, w3, w2, layer, *, tb):
    n_assign = pos.shape[0]
    cap = n_assign + N_EXPERTS * tb
    d = xn.shape[1]
    nblk = cap // tb
    hbm = pl.BlockSpec(memory_space=pl.ANY)
    grid_spec = pltpu.PrefetchScalarGridSpec(
        num_scalar_prefetch=5,
        grid=(nblk,),
        in_specs=[hbm, hbm, hbm, hbm],
        out_specs=pl.BlockSpec((tb, d), lambda i, *_: (i, 0)),
        scratch_shapes=[pltpu.SMEM((cap,), jnp.int32), pltpu.VMEM((2, tb, d), F32),
                        pltpu.SemaphoreType.DMA((2,)),
                        pltpu.VMEM((2, d, D_EXPERT), F32), pltpu.VMEM((2, d, D_EXPERT), F32),
                        pltpu.VMEM((2, D_EXPERT, d), F32), pltpu.SemaphoreType.DMA((2, 3))],
    )
    return pl.pallas_call(
        functools.partial(_moe_expert_kernel, tb=tb, n_assign=n_assign, layer=layer),
        grid_spec=grid_spec,
        out_shape=jax.ShapeDtypeStruct((cap, d), F32),
        compiler_params=_cparams(("arbitrary",)),
        name="moe_experts",
    )(pos, blk_exp, n_valid, nxt_exp, run_idx, xn, w1, w3, w2)


def _moe_combine_kernel(pos_ref, ys_hbm, h_ref, gate_ref, g_ref, o_ref, buf, sem, *, tm, n_tiles, normalize):
    i = pl.program_id(0)

    def start(tile, slot):
        for choice in range(TOP_K):
            def row_of(r, choice=choice):
                return pos_ref[(tile * tm + r) * TOP_K + choice]

            _start_row_gather(row_of, ys_hbm, buf.at[slot], sem.at[slot], tm, dst_row0=choice * tm)

    @pl.when(i == 0)
    def _():
        start(0, 0)

    @pl.when(i + 1 < n_tiles)
    def _():
        start(i + 1, (i + 1) & 1)

    cur = i & 1
    _wait_row_gather(ys_hbm, buf.at[cur], sem.at[cur], TOP_K * tm)
    gate = gate_ref[...]
    out = h_ref[...] + gate[:, 0:1] * buf[cur, 0:tm, :] + gate[:, 1:2] * buf[cur, tm:2 * tm, :]
    o_ref[...] = _rms(out, g_ref[...]) if normalize else out


def moe_combine(h, ys, gate, pos, out_gain, *, tm, normalize):
    m, d = h.shape
    hspec = pl.BlockSpec((tm, d), lambda i, p: (i, 0))
    grid_spec = pltpu.PrefetchScalarGridSpec(
        num_scalar_prefetch=1,
        grid=(m // tm,),
        in_specs=[pl.BlockSpec(memory_space=pl.ANY), hspec,
                  pl.BlockSpec((tm, ROUTER_W), lambda i, p: (i, 0)),
                  pl.BlockSpec((1, d), lambda i, p: (0, 0))],
        out_specs=hspec,
        scratch_shapes=[pltpu.VMEM((2, TOP_K * tm, d), F32), pltpu.SemaphoreType.DMA((2,))],
    )
    return pl.pallas_call(
        functools.partial(_moe_combine_kernel, tm=tm, n_tiles=m // tm, normalize=normalize),
        grid_spec=grid_spec,
        out_shape=jax.ShapeDtypeStruct((m, d), F32),
        compiler_params=_cparams(("arbitrary",)),
        name="moe_combine",
    )(pos, ys, h, gate, out_gain.reshape(1, d))


def _moe_blocks(counts, n_assign, tb):
    padded = ((counts + tb - 1) // tb) * tb
    pad_end = jnp.cumsum(padded)
    pad_start = (pad_end - padded).astype(jnp.int32)
    nblk = (n_assign + N_EXPERTS * tb) // tb
    n_valid = (pad_end[-1] // tb).astype(jnp.int32)
    blk = jnp.minimum(jnp.arange(nblk, dtype=jnp.int32), n_valid - 1) * tb
    blk_exp = jnp.sum(blk[:, None] >= pad_end[None, :], axis=1).astype(jnp.int32)
    used = counts > 0
    run_idx = (jnp.cumsum(used) - 1).astype(jnp.int32)
    ids = jnp.where(used, jnp.arange(N_EXPERTS, dtype=jnp.int32), N_EXPERTS)
    later = lax.cummin(ids[::-1])[::-1]
    nxt = jnp.concatenate([later[1:], jnp.full((1,), N_EXPERTS, jnp.int32)])
    nxt_exp = jnp.where(nxt < N_EXPERTS, nxt, -1).astype(jnp.int32)
    return pad_start, jnp.minimum(blk_exp, N_EXPERTS - 1), n_valid.reshape(1), nxt_exp, run_idx


def _moe_positions_kernel(eid_ref, rank_ref, pstart_ref, pos_ref):
    eid = eid_ref[...]
    lane = lax.broadcasted_iota(jnp.int32, eid.shape, 1)
    pstart = pstart_ref[...]
    first = [jnp.sum(jnp.where(lane == eid[:, c:c + 1], pstart, 0), axis=-1, keepdims=True)
             for c in range(TOP_K)]
    pos_ref[...] = rank_ref[...] + jnp.where(lane == 0, first[0], jnp.where(lane == 1, first[1], 0))


def moe_positions(eid, rank, pad_start, *, tm):
    m = eid.shape[0]
    spec = pl.BlockSpec((tm, ROUTER_W), lambda i: (i, 0))
    pstart = jnp.pad(pad_start, (0, ROUTER_W - pad_start.shape[0])).reshape(1, ROUTER_W)
    return pl.pallas_call(
        _moe_positions_kernel,
        grid=(m // tm,),
        in_specs=[spec, spec, pl.BlockSpec((1, ROUTER_W), lambda i: (0, 0))],
        out_specs=spec,
        out_shape=jax.ShapeDtypeStruct((m, ROUTER_W), jnp.int32),
        compiler_params=_cparams(("parallel",)),
        name="moe_positions",
    )(eid, rank, pstart)


def hier_moe(h, g, wr_group, wr_expert, w1, w3, w2, layer, out_gain, normalize_out,
             *, tm_router=1024, tm_combine=128, tb=MOE_TB):
    m, d = h.shape
    xn, eid, gate, rank, counts = moe_router(h, g, wr_group, wr_expert, tm=tm_router)
    pad_start, blk_exp, n_valid, nxt_exp, run_idx = _moe_blocks(counts[0, :N_EXPERTS], m * TOP_K, tb)
    pos = moe_positions(eid, rank, pad_start, tm=1024)[:, :TOP_K].reshape(-1)
    ys = moe_experts(xn, pos, blk_exp, n_valid, nxt_exp, run_idx, w1, w3, w2, layer, tb=tb)
    return moe_combine(h, ys, gate, pos, out_gain, tm=tm_combine, normalize=normalize_out)


def _w_in_layout(w_in):
    wa = w_in[:, :IN_A]
    wb = w_in[:, IN_A:IN_A + IN_B]
    wc = w_in[:, IN_A + IN_B:IN_A + IN_B + IN_C]
    wd = w_in[:, IN_A + IN_B + IN_C:]
    wb = jnp.pad(wb, ((0, 0), (0, IN_B_PAD - IN_B)))
    return jnp.concatenate([wb, wa, wc, wd], axis=1).astype(BF16)


def kernel(x, mem, norm_mix, w_in, w_out, diff_lambda, diff_subln, rwkv_mu, rwkv_w0, rwkv_w2,
           rwkv_a0, rwkv_a2, rwkv_g2, rwkv_kk, rwkv_ka, rwkv_rk, rwkv_lnx_w, rwkv_lnx_b,
           s5_a_re, s5_a_im, s5_log_dt, s5_b_re, s5_b_im, s5_c_re, s5_c_im, s5_d, s5_glu_w,
           s5_glu_b, mix_out_norm, norm_cross, norm_mem, xa_wq, xa_wkv, xa_wo, norm_moe,
           router_group, router_expert, moe_w1, moe_w3, moe_w2, norm_final):
    b, s, d = x.shape
    m = b * s
    n_mem = mem.shape[1]
    depth = w_in.shape[0]
    h = x.reshape(m, d)
    mem2 = mem.reshape(b * n_mem, d)
    rope_a = _rope_tables(s, A_QKDIM, MIX)
    rope_c = _rope_tables(s, C_HDIM, MIX)
    for l in range(depth):
        z = norm_matmul(h, norm_mix[l], _w_in_layout(w_in[l]), tm=1024, tn=Z_W // 4, out_dtype=F32)
        z = z.reshape(b, s, Z_W)
        qa, ka, v1a = rope_qkv(z, Z_A, A_QKDIM, tm=1024, ones_cols=A_VDIM, tables=rope_a)
        oa = diff_attention(qa, ka, v1a, diff_lambda[l], diff_subln[l], l, tq=512)
        ob = rwkv7_bidir(z, rwkv_mu[l], rwkv_w0[l], rwkv_w2[l], rwkv_a0[l], rwkv_a2[l], rwkv_g2[l],
                         rwkv_kk[l], rwkv_ka[l], rwkv_rk[l], rwkv_lnx_w[l], rwkv_lnx_b[l])
        qc, kc, vc = rope_qkv(z, Z_C, C_HDIM, tm=1024, tables=rope_c)
        oc = dilated_attention(qc, kc, vc, tq=256)
        od = s5_bidir(z, s5_a_re[l], s5_a_im[l], s5_log_dt[l], s5_b_re[l], s5_b_im[l],
                      s5_c_re[l], s5_c_im[l], s5_d[l], s5_glu_w[l], s5_glu_b[l])
        h = mix_out(h, oa.reshape(m, MIX), ob.reshape(m, MIX), oc.reshape(m, MIX), od.reshape(m, MIX),
                    mix_out_norm[l, 0], mix_out_norm[l, 1], w_out[l].astype(BF16), tm=512, tn=D_MODEL)
        kv = norm_matmul(mem2, norm_mem[l], xa_wkv, tm=b * n_mem, tn=512, out_dtype=BF16, layer=l)
        h = cross_attention(h.reshape(b, s, d), norm_cross[l], xa_wq[l].astype(BF16),
                            kv.reshape(b, n_mem, 2 * d), xa_wo[l].astype(BF16), tq=256).reshape(m, d)
        h = hier_moe(h, norm_moe[l], router_group[l], router_expert[l], moe_w1, moe_w3, moe_w2, l,
                     norm_final, l == depth - 1)
    return h.reshape(b, s, d)
```
